```python
import jax
import jax.numpy as jnp
from jax import lax
import numpy as np

D_MODEL = 1024
BATCH = 1
SEQ = 16384
DEPTH = 4

GRID_W = 64
CTX_LEN = 256
N_MIXERS = 3
RMS_EPS = 1e-6
ADA_CHUNKS = 6

D_RNN = D_MODEL
RG_BLOCK_W = 256
RG_BLOCKS = D_RNN // RG_BLOCK_W
CONV_WIDTH = 4
CONV_PAD = (2, 1)
LRU_C = 8.0

NA_HEAD_DIM = 64
NA_HEADS = D_MODEL // NA_HEAD_DIM
NA_ROWS_MAX = 8
NA_COLS = 16

FT_GROUPS = 4
FT_GROUP_W = D_MODEL // FT_GROUPS

D_FF = 7 * D_MODEL // 2
N_EXPERTS = 8
TOP_K = 2

kernel_name = "hybrid_rglru_natten_fnet_moe_dit"


def _layer_counts(depth):
    n_mix = [sum(1 for i in range(depth) if i % N_MIXERS == m) for m in range(N_MIXERS)]
    n_dense = sum(1 for i in range(depth) if i % 2 == 0)
    return n_mix, n_dense, depth - n_dense


def rmsnorm(x, g):
    xf = x.astype(jnp.float32)
    y = xf * lax.rsqrt(jnp.mean(xf * xf, axis=-1, keepdims=True) + RMS_EPS)
    return (y * g.astype(jnp.float32)).astype(x.dtype)


def dwconv_centred(z, w, b):
    y = lax.conv_general_dilated(z, w[:, None, :].astype(z.dtype), window_strides=(1,), padding=[CONV_PAD], dimension_numbers=("NWC", "WIO", "NWC"), feature_group_count=z.shape[-1])
    return y + b


def block_diag(z, w):
    bsz, length, _ = z.shape
    zb = z.reshape(bsz, length, RG_BLOCKS, RG_BLOCK_W)
    return jnp.einsum("blnc,ncd->blnd", zb, w).reshape(bsz, length, D_RNN)


def linear_scan(a, b, h0):
    def combine(p, q):
        return p[0] * q[0], q[0] * p[1] + q[1]
    a_cum, b_cum = lax.associative_scan(combine, (a, b), axis=1)
    return a_cum * h0[:, None, :] + b_cum


def rglru_coeffs(z, wa, ba, wi, bi, lam):
    zf = z.astype(jnp.float32)
    r = jax.nn.sigmoid(block_diag(zf, wa) + ba)
    i = jax.nn.sigmoid(block_diag(zf, wi) + bi)
    log_a = -LRU_C * r * jax.nn.softplus(-lam.astype(jnp.float32))
    a = jnp.exp(log_a)
    b = jnp.sqrt(-jnp.expm1(2.0 * log_a)) * (i * zf)
    return a, b


def rglru_mixer(h, hc, w_in, conv_w, conv_b, wa, ba, wi, bi, lam, w_out, need_ctx):
    def branches(z):
        xz, gz = jnp.split(z @ w_in, 2, axis=-1)
        return dwconv_centred(xz, conv_w, conv_b), gz
    xl, gl = branches(h)
    xc, gc = branches(hc)
    zeros = jnp.zeros((hc.shape[0], D_RNN), jnp.float32)
    hcf = linear_scan(*rglru_coeffs(xc, wa[0], ba[0], wi[0], bi[0], lam[0]), zeros)
    hlf = linear_scan(*rglru_coeffs(xl, wa[0], ba[0], wi[0], bi[0], lam[0]), hcf[:, -1])
    hcb = linear_scan(*rglru_coeffs(jnp.flip(xc, 1), wa[1], ba[1], wi[1], bi[1], lam[1]), zeros)
    hlb = jnp.flip(linear_scan(*rglru_coeffs(jnp.flip(xl, 1), wa[1], ba[1], wi[1], bi[1], lam[1]), hcb[:, -1]), 1)
    y = ((hlf + hlb).astype(h.dtype) * jax.nn.gelu(gl)) @ w_out
    yc = None
    if need_ctx:
        yc = ((hcf + jnp.flip(hcb, 1)).astype(h.dtype) * jax.nn.gelu(gc)) @ w_out
    return y, yc


def na_mixer(h, hc, w_qkv, q_g, k_g, rpb, w_o, need_ctx):
    bsz, seq, _ = h.shape
    rows = seq // GRID_W
    kr = min(NA_ROWS_MAX, rows)
    n_loc = kr * NA_COLS
    scale = NA_HEAD_DIM ** -0.5

    def proj(z):
        q, k, v = jnp.split(z @ w_qkv, 3, axis=-1)
        shp = z.shape[:2] + (NA_HEADS, NA_HEAD_DIM)
        return rmsnorm(q.reshape(shp), q_g), rmsnorm(k.reshape(shp), k_g), v.reshape(shp)

    q, k, v = proj(h)
    qc, kc, vc = proj(hc)
    grid = (bsz, rows, GRID_W, NA_HEADS, NA_HEAD_DIM)
    qg, kg, vg = q.reshape(grid), k.reshape(grid), v.reshape(grid)

    cols = np.arange(GRID_W)
    col_start = np.clip(cols - NA_COLS // 2, 0, GRID_W - NA_COLS)
    col_idx = col_start[:, None] + np.arange(NA_COLS)[None, :]
    col_bias_idx = col_idx - cols[:, None] + (NA_COLS - 1)

    def row_block(r):
        rs = jnp.clip(r - kr // 2, 0, rows - kr)
        qr = lax.dynamic_index_in_dim(qg, r, axis=1, keepdims=False)
        def gather(t):
            tw = lax.dynamic_slice_in_dim(t, rs, kr, axis=1)[:, :, col_idx]
            return tw.transpose(0, 2, 1, 3, 4, 5).reshape(bsz, GRID_W, n_loc, NA_HEADS, NA_HEAD_DIM)
        kw, vw = gather(kg), gather(vg)
        row_bias_idx = rs + jnp.arange(kr) - r + (NA_ROWS_MAX - 1)
        bias = rpb[:, row_bias_idx[:, None, None], col_bias_idx[None, :, :]]
        bias = bias.transpose(0, 2, 1, 3).reshape(NA_HEADS, GRID_W, n_loc).astype(jnp.float32)
        s_loc = jnp.einsum("bqhd,bqkhd->bhqk", qr, kw).astype(jnp.float32) * scale + bias
        s_ctx = jnp.einsum("bqhd,bkhd->bhqk", qr, kc).astype(jnp.float32) * scale
        p = jax.nn.softmax(jnp.concatenate([s_loc, s_ctx], axis=-1), axis=-1).astype(v.dtype)
        return (jnp.einsum("bhqk,bqkhd->bqhd", p[..., :n_loc], vw)
                + jnp.einsum("bhqk,bkhd->bqhd", p[..., n_loc:], vc))

    o = lax.map(row_block, jnp.arange(rows))
    y = o.transpose(1, 0, 2, 3, 4).reshape(bsz, seq, D_MODEL) @ w_o
    yc = None
    if need_ctx:
        s = jnp.einsum("bqhd,bkhd->bhqk", qc, kc).astype(jnp.float32) * scale
        p = jax.nn.softmax(s, axis=-1).astype(vc.dtype)
        oc = jnp.einsum("bhqk,bkhd->bqhd", p, vc)
        yc = oc.reshape(hc.shape[0], hc.shape[1], D_MODEL) @ w_o
    return y, yc


def fourier_mix(z, w_f):
    bsz, length, _ = z.shape
    zg = z.astype(jnp.float32).reshape(bsz, length, FT_GROUPS, FT_GROUP_W)
    f = jnp.fft.fft2(zg, axes=(1, 3), norm="ortho").real
    return f.reshape(bsz, length, D_MODEL).astype(z.dtype) @ w_f


def fourier_mixer(h, hc, w_f, need_ctx):
    return fourier_mix(h, w_f), (fourier_mix(hc, w_f) if need_ctx else None)


def swiglu(z, w_gu, w_down):
    g, u = jnp.split(z @ w_gu, 2, axis=-1)
    return (jax.nn.silu(g) * u) @ w_down


def moe_swiglu(z, router, w_gu, w_down):
    logits = (z @ router).astype(jnp.float32)
    top_v, top_i = lax.top_k(logits, TOP_K)
    wts = jax.nn.softmax(top_v, axis=-1)
    gates = jnp.sum(jax.nn.one_hot(top_i, N_EXPERTS, dtype=jnp.float32) * wts[..., None], axis=-2)
    out = jnp.zeros_like(z)
    for e in range(N_EXPERTS):
        out = out + gates[..., e:e + 1].astype(z.dtype) * swiglu(z, w_gu[e], w_down[e])
    return out


def setup_inputs(seed: int = 0) -> dict:
    key = jax.random.key(seed)
    ks = jax.random.split(key, 32)
    (n_a, n_b, n_c), n_dense, n_moe = _layer_counts(DEPTH)
    d = D_MODEL

    def nrm(k, shape, s):
        return jax.random.normal(k, shape, jnp.float32) * s

    u = jax.random.uniform(ks[11], (n_a, 2, D_RNN), jnp.float32, 0.9, 0.999)
    a0 = u ** (1.0 / LRU_C)
    rg_lambda = jnp.log(a0) - jnp.log1p(-a0)
    return {
        "x": nrm(ks[0], (BATCH, SEQ, d), 1.0),
        "c": nrm(ks[1], (BATCH, d), 1.0),
        "ctx": nrm(ks[2], (BATCH, CTX_LEN, d), 1.0),
        "c_ctx": nrm(ks[3], (d,), 1.0),
        "ada_w": nrm(ks[4], (DEPTH, d, ADA_CHUNKS * d), 0.5 * d ** -0.5),
        "ada_b": nrm(ks[5], (DEPTH, ADA_CHUNKS * d), 0.02),
        "norm_g": 1.0 + nrm(ks[6], (DEPTH, 2, d), 0.1),
        "rg_w_in": nrm(ks[7], (n_a, d, 2 * D_RNN), d ** -0.5),
        "rg_conv_w": nrm(ks[8], (n_a, CONV_WIDTH, D_RNN), CONV_WIDTH ** -0.5),
        "rg_conv_b": nrm(ks[9], (n_a, D_RNN), 0.02),
        "rg_wa": nrm(ks[10], (n_a, 2, RG_BLOCKS, RG_BLOCK_W, RG_BLOCK_W), RG_BLOCK_W ** -0.5),
        "rg_ba": nrm(ks[12], (n_a, 2, D_RNN), 0.02),
        "rg_wi": nrm(ks[13], (n_a, 2, RG_BLOCKS, RG_BLOCK_W, RG_BLOCK_W), RG_BLOCK_W ** -0.5),
        "rg_bi": nrm(ks[14], (n_a, 2, D_RNN), 0.02),
        "rg_lambda": rg_lambda,
        "rg_w_out": nrm(ks[15], (n_a, D_RNN, d), D_RNN ** -0.5),
        "na_w_qkv": nrm(ks[16], (n_b, d, 3 * d), d ** -0.5),
        "na_q_g": 1.0 + nrm(ks[17], (n_b, NA_HEAD_DIM), 0.1),
        "na_k_g": 1.0 + nrm(ks[18], (n_b, NA_HEAD_DIM), 0.1),
        "na_rpb": nrm(ks[19], (n_b, NA_HEADS, 2 * NA_ROWS_MAX - 1, 2 * NA_COLS - 1), 0.1),
        "na_w_o": nrm(ks[20], (n_b, d, d), d ** -0.5),
        "ft_w_out": nrm(ks[21], (n_c, d, d), d ** -0.5),
        "ffn_w_gu": nrm(ks[22], (n_dense, d, 2 * D_FF), d ** -0.5),
        "ffn_w_down": nrm(ks[23], (n_dense, D_FF, d), D_FF ** -0.5),
        "moe_router": nrm(ks[24], (n_moe, d, N_EXPERTS), d ** -0.5),
        "moe_w_gu": nrm(ks[25], (n_moe, N_EXPERTS, d, 2 * D_FF), d ** -0.5),
        "moe_w_down": nrm(ks[26], (n_moe, N_EXPERTS, D_FF, d), D_FF ** -0.5),
    }


def reference(x, c, ctx, c_ctx, ada_w, ada_b, norm_g, rg_w_in, rg_conv_w, rg_conv_b, rg_wa, rg_ba, rg_wi, rg_bi, rg_lambda, rg_w_out, na_w_qkv, na_q_g, na_k_g, na_rpb, na_w_o, ft_w_out, ffn_w_gu, ffn_w_down, moe_router, moe_w_gu, moe_w_down):
    xc = ctx
    silu_c = jax.nn.silu(c)
    silu_cc = jax.nn.silu(c_ctx)
    mix_idx = [0] * N_MIXERS
    dense_idx = 0
    moe_idx = 0
    for layer in range(DEPTH):
        last = layer == DEPTH - 1
        need_ctx = not last
        m = [t[:, None, :] for t in jnp.split(silu_c @ ada_w[layer] + ada_b[layer], ADA_CHUNKS, axis=-1)]
        mc = jnp.split(silu_cc @ ada_w[layer] + ada_b[layer], ADA_CHUNKS, axis=-1)
        h = rmsnorm(x, norm_g[layer, 0]) * (1.0 + m[1]) + m[0]
        hc = rmsnorm(xc, norm_g[layer, 0]) * (1.0 + mc[1]) + mc[0]
        kind = layer % N_MIXERS
        j = mix_idx[kind]
        mix_idx[kind] += 1
        if kind == 0:
            y, yc = rglru_mixer(h, hc, rg_w_in[j], rg_conv_w[j], rg_conv_b[j], rg_wa[j], rg_ba[j], rg_wi[j], rg_bi[j], rg_lambda[j], rg_w_out[j], need_ctx)
        elif kind == 1:
            y, yc = na_mixer(h, hc, na_w_qkv[j], na_q_g[j], na_k_g[j], na_rpb[j], na_w_o[j], need_ctx)
        else:
            y, yc = fourier_mixer(h, hc, ft_w_out[j], need_ctx)
        x = x + m[2] * y
        if need_ctx:
            xc = xc + mc[2] * yc
        h = rmsnorm(x, norm_g[layer, 1]) * (1.0 + m[4]) + m[3]
        if layer % 2 == 0:
            f = swiglu(h, ffn_w_gu[dense_idx], ffn_w_down[dense_idx])
            if need_ctx:
                hc = rmsnorm(xc, norm_g[layer, 1]) * (1.0 + mc[4]) + mc[3]
                xc = xc + mc[5] * swiglu(hc, ffn_w_gu[dense_idx], ffn_w_down[dense_idx])
            dense_idx += 1
        else:
            f = moe_swiglu(h, moe_router[moe_idx], moe_w_gu[moe_idx], moe_w_down[moe_idx])
            if need_ctx:
                hc = rmsnorm(xc, norm_g[layer, 1]) * (1.0 + mc[4]) + mc[3]
                xc = xc + mc[5] * moe_swiglu(hc, moe_router[moe_idx], moe_w_gu[moe_idx], moe_w_down[moe_idx])
            moe_idx += 1
        x = x + m[5] * f
    return x
```

```python
import functools
import math

import numpy as np
import jax
import jax.numpy as jnp
from jax import lax
from jax.experimental import pallas as pl
from jax.experimental.pallas import tpu as pltpu

F32 = jnp.float32
BF16 = jnp.bfloat16

D = 1024
D_FF = 3584
N_EXPERTS = 8
GRID_W = 64
NA_HEADS = 16
NA_HEAD_DIM = 64
NA_ROWS = 8
NA_COLS = 16
FT_GROUP_W = 256
RG_BLOCK_W = 256
RMS_EPS = 1e-6
LRU_C = 8.0
N_MIXERS = 3
ADA_CHUNKS = 6

LANES = 128
SUBLANES = 8
VMEM_LIMIT = 56 * 1024 * 1024
NEG_BIG = -1e30


def _cparams(sem):
    return pltpu.CompilerParams(dimension_semantics=sem, vmem_limit_bytes=VMEM_LIMIT)


def _full(shape):
    nd = len(shape)
    return pl.BlockSpec(shape, lambda *_: (0,) * nd)


def _normmod(x, g, scale, shift):
    ms = jnp.mean(x * x, axis=-1, keepdims=True)
    y = x * lax.rsqrt(ms + RMS_EPS)
    return (y * g) * (1.0 + scale) + shift


def _lane_tile(v):
    return jnp.concatenate([v] * (D // LANES), axis=1)


def _sigmoid(v):
    return 1.0 / (1.0 + jnp.exp(-v))


def _gelu_tanh(v):
    c = math.sqrt(2.0 / math.pi)
    return v * (0.5 * (1.0 + jnp.tanh(c * (v + 0.044715 * (v * v * v)))))


def _ada_body(cin_ref, w_ref, b_ref, o_ref):
    v = cin_ref[...]
    s = v * _sigmoid(v)
    w = w_ref[0]
    r0 = jnp.sum(s[:, 0:1] * w, axis=0, keepdims=True)
    r1 = jnp.sum(s[:, 1:2] * w, axis=0, keepdims=True)
    o_ref[0] = jnp.concatenate([r0, r1], axis=0) + b_ref[0]


def ada_modulation(c, c_ctx, ada_w, ada_b):
    depth = ada_w.shape[0]
    n = ada_w.shape[2]
    nc = n // 4
    cin = jnp.stack([c[0], c_ctx], axis=1)
    out = pl.pallas_call(
        _ada_body,
        grid=(depth, n // nc),
        in_specs=[
            pl.BlockSpec((D, 2), lambda l, j: (0, 0)),
            pl.BlockSpec((1, D, nc), lambda l, j: (l, 0, j)),
            pl.BlockSpec((1, 1, nc), lambda l, j: (l, 0, j)),
        ],
        out_specs=pl.BlockSpec((1, 2, nc), lambda l, j: (l, 0, j)),
        out_shape=jax.ShapeDtypeStruct((depth, 2, n), F32),
        compiler_params=_cparams(("arbitrary", "arbitrary")),
        name="ada_mod",
    )(cin, ada_w, ada_b.reshape(depth, 1, n))
    return out.reshape(depth, 2, ADA_CHUNKS, D)


def _ffn_body(x_ref, mod_ref, g_ref, wg_ref, wu_ref, wd_ref, o_ref, h_scr, acc_scr):
    j = pl.program_id(1)

    @pl.when(j == 0)
    def _():
        h = _normmod(x_ref[...], g_ref[...], mod_ref[4:5, :], mod_ref[3:4, :])
        h_scr[...] = h.astype(BF16)
        acc_scr[...] = jnp.zeros_like(acc_scr)

    h = h_scr[...]
    gg = jnp.dot(h, wg_ref[...], preferred_element_type=F32)
    uu = jnp.dot(h, wu_ref[...], preferred_element_type=F32)
    a = (gg * _sigmoid(gg)) * uu
    acc_scr[...] += jnp.dot(a.astype(BF16), wd_ref[...], preferred_element_type=F32)

    @pl.when(j == pl.num_programs(1) - 1)
    def _():
        o_ref[...] = x_ref[...] + mod_ref[5:6, :] * acc_scr[...]


def ffn_dense(x, mod, g, w_gu, w_down, tm, fc=512):
    t = x.shape[0]
    nf = D_FF // fc
    return pl.pallas_call(
        _ffn_body,
        grid=(t // tm, nf),
        in_specs=[
            pl.BlockSpec((tm, D), lambda i, j: (i, 0)),
            _full((ADA_CHUNKS, D)),
            _full((1, D)),
            pl.BlockSpec((D, fc), lambda i, j: (0, j)),
            pl.BlockSpec((D, fc), lambda i, j: (0, nf + j)),
            pl.BlockSpec((fc, D), lambda i, j: (j, 0)),
        ],
        out_specs=pl.BlockSpec((tm, D), lambda i, j: (i, 0)),
        out_shape=jax.ShapeDtypeStruct((t, D), F32),
        scratch_shapes=[pltpu.VMEM((tm, D), BF16), pltpu.VMEM((tm, D), F32)],
        compiler_params=_cparams(("arbitrary", "arbitrary")),
        name="ffn_dense",
    )(x, mod, g, w_gu, w_gu, w_down)


def _proj_body(gate_row, a_ref, w_ref, x_ref, mod_ref, o_ref):
    y = jnp.dot(a_ref[...], w_ref[...], preferred_element_type=F32)
    o_ref[...] = x_ref[...] + mod_ref[gate_row:gate_row + 1, :] * y


def proj_residual(a, w, x, mod, gate_row, tm):
    t, k = a.shape
    return pl.pallas_call(
        functools.partial(_proj_body, gate_row),
        grid=(t // tm,),
        in_specs=[
            pl.BlockSpec((tm, k), lambda i: (i, 0)),
            _full((k, D)),
            pl.BlockSpec((tm, D), lambda i: (i, 0)),
            _full((ADA_CHUNKS, D)),
        ],
        out_specs=pl.BlockSpec((tm, D), lambda i: (i, 0)),
        out_shape=jax.ShapeDtypeStruct((t, D), F32),
        compiler_params=_cparams(("arbitrary",)),
        name="proj_residual",
    )(a, w, x, mod)


HALO = SUBLANES


def _rg_in_body(tm, xp_ref, x_ref, xn_ref, mod_ref, g_ref, w_ref, cw_ref, cb_ref, xc_ref, gg_ref):
    i = pl.program_id(0)
    last = pl.num_programs(0) - 1
    xa = jnp.concatenate([xp_ref[...], x_ref[...], xn_ref[...]], axis=0)
    h = _normmod(xa, g_ref[...], mod_ref[1:2, :], mod_ref[0:1, :]).astype(BF16)
    z = jnp.dot(h, w_ref[...], preferred_element_type=F32)
    row = lax.broadcasted_iota(jnp.int32, (tm + 2 * HALO, 1), 0)
    valid = jnp.logical_and(jnp.logical_or(row >= HALO, i > 0),
                            jnp.logical_or(row < tm + HALO, i < last))
    xz = jnp.where(valid, z[:, :D], 0.0)
    y = cb_ref[...] + cw_ref[2:3, :] * xz[HALO:HALO + tm]
    y = y + cw_ref[0:1, :] * xz[HALO - 2:HALO - 2 + tm]
    y = y + cw_ref[1:2, :] * xz[HALO - 1:HALO - 1 + tm]
    y = y + cw_ref[3:4, :] * xz[HALO + 1:HALO + 1 + tm]
    xc_ref[...] = y
    gg_ref[...] = _gelu_tanh(z[HALO:HALO + tm, D:])


def rg_in(x, mod, g, w_in, conv_w, conv_b, tm):
    t = x.shape[0]
    nb = tm // HALO
    nblk = t // HALO
    return pl.pallas_call(
        functools.partial(_rg_in_body, tm),
        grid=(t // tm,),
        in_specs=[
            pl.BlockSpec((HALO, D), lambda i: (jnp.maximum(i * nb - 1, 0), 0)),
            pl.BlockSpec((tm, D), lambda i: (i, 0)),
            pl.BlockSpec((HALO, D), lambda i: (jnp.minimum((i + 1) * nb, nblk - 1), 0)),
            _full((ADA_CHUNKS, D)),
            _full((1, D)),
            _full((D, 2 * D)),
            _full((4, D)),
            _full((1, D)),
        ],
        out_specs=[pl.BlockSpec((tm, D), lambda i: (i, 0)), pl.BlockSpec((tm, D), lambda i: (i, 0))],
        out_shape=[jax.ShapeDtypeStruct((t, D), F32), jax.ShapeDtypeStruct((t, D), F32)],
        compiler_params=_cparams(("arbitrary",)),
        name="rg_in",
    )(x, x, x, mod, g, w_in, conv_w, conv_b)


def _rg_gates(xc, wa_ref, wi_ref, ba, bi, lam):
    xb = xc.astype(BF16)
    nblk = D // RG_BLOCK_W
    r = jnp.concatenate([jnp.dot(xb[:, n * RG_BLOCK_W:(n + 1) * RG_BLOCK_W], wa_ref[n],
                                 preferred_element_type=F32) for n in range(nblk)], axis=1)
    ig = jnp.concatenate([jnp.dot(xb[:, n * RG_BLOCK_W:(n + 1) * RG_BLOCK_W], wi_ref[n],
                                  preferred_element_type=F32) for n in range(nblk)], axis=1)
    r = _sigmoid(r + ba)
    ig = _sigmoid(ig + bi)
    nl = -lam
    softplus = jnp.maximum(nl, 0.0) + jnp.log1p(jnp.exp(-jnp.abs(nl)))
    log_a = (-LRU_C * r) * softplus
    a = jnp.exp(log_a)
    th = jnp.tanh(log_a)
    b = jnp.sqrt((-2.0 * th) / (1.0 - th)) * (ig * xc)
    return a, b


def _rg_scan_body(reverse, epilogue, tc, *refs):
    if epilogue:
        (xc_ref, wa_ref, wi_ref, ba_ref, bi_ref, lam_ref, h0_ref, hf_ref, gg_ref, wo_ref, x_ref, mod_ref,
         h_ref, o_ref, a_scr, b_scr, carry_scr) = refs
    else:
        (xc_ref, wa_ref, wi_ref, ba_ref, bi_ref, lam_ref, h0_ref, h_ref, a_scr, b_scr, carry_scr) = refs
    c = pl.program_id(0)

    @pl.when(c == 0)
    def _():
        carry_scr[...] = jnp.broadcast_to(h0_ref[...], (SUBLANES, D))

    a, b = _rg_gates(xc_ref[...], wa_ref, wi_ref, ba_ref[...], bi_ref[...], lam_ref[...])
    a_scr[...] = a
    b_scr[...] = b
    nblk = tc // SUBLANES
    row = lax.broadcasted_iota(jnp.int32, (SUBLANES, D), 0)

    def block(n, carry):
        blk = (nblk - 1 - n) if reverse else n
        off = pl.multiple_of(blk * SUBLANES, SUBLANES)
        av = a_scr[pl.ds(off, SUBLANES), :]
        bv = b_scr[pl.ds(off, SUBLANES), :]
        for k in (1, 2, 4):
            shift = (SUBLANES - k) if reverse else k
            a_s = pltpu.roll(av, shift, 0)
            b_s = pltpu.roll(bv, shift, 0)
            m = (row < SUBLANES - k) if reverse else (row >= k)
            bv = jnp.where(m, av * b_s + bv, bv)
            av = jnp.where(m, av * a_s, av)
        hv = av * carry + bv
        h_ref[pl.ds(off, SUBLANES), :] = hv
        edge = hv[0:1, :] if reverse else hv[SUBLANES - 1:SUBLANES, :]
        return jnp.broadcast_to(edge, (SUBLANES, D))

    carry_scr[...] = lax.fori_loop(0, nblk, block, carry_scr[...], unroll=2)

    if epilogue:
        y = ((hf_ref[...] + h_ref[...]) * gg_ref[...]).astype(BF16)
        o_ref[...] = x_ref[...] + mod_ref[2:3, :] * jnp.dot(y, wo_ref[...], preferred_element_type=F32)


def rg_scan(xconv, wa, wi, ba, bi, lam, h0, tc, reverse, epi=None):
    t = xconv.shape[0]
    nchunks = t // tc
    idx = (lambda c: (nchunks - 1 - c, 0)) if reverse else (lambda c: (c, 0))
    nb = D // RG_BLOCK_W
    in_specs = [
        pl.BlockSpec((tc, D), idx),
        _full((nb, RG_BLOCK_W, RG_BLOCK_W)),
        _full((nb, RG_BLOCK_W, RG_BLOCK_W)),
        _full((1, D)), _full((1, D)), _full((1, D)), _full((1, D)),
    ]
    args = [xconv, wa, wi, ba, bi, lam, h0]
    out_specs = [pl.BlockSpec((tc, D), idx)]
    out_shape = [jax.ShapeDtypeStruct((t, D), F32)]
    if epi is not None:
        hf, gg, w_out, x, mod = epi
        in_specs += [pl.BlockSpec((tc, D), idx), pl.BlockSpec((tc, D), idx), _full((D, D)),
                     pl.BlockSpec((tc, D), idx), _full((ADA_CHUNKS, D))]
        args += [hf, gg, w_out, x, mod]
        out_specs.append(pl.BlockSpec((tc, D), idx))
        out_shape.append(jax.ShapeDtypeStruct((t, D), F32))
    res = pl.pallas_call(
        functools.partial(_rg_scan_body, reverse, epi is not None, tc),
        grid=(nchunks,),
        in_specs=in_specs,
        out_specs=out_specs,
        out_shape=out_shape,
        scratch_shapes=[pltpu.VMEM((tc, D), F32), pltpu.VMEM((tc, D), F32), pltpu.VMEM((SUBLANES, D), F32)],
        compiler_params=_cparams(("arbitrary",)),
        name="rg_scan_bwd" if reverse else "rg_scan_fwd",
    )(*args)
    return res


def rglru_layer(x, xc, mod, modc, g, w_in, conv_w, conv_b, wa, wi, ba, bi, lam, w_out, need_ctx):
    w_in_b = w_in.astype(BF16)
    wa_b = wa.astype(BF16)
    wi_b = wi.astype(BF16)
    w_out_b = w_out.astype(BF16)
    cb = conv_b[None]
    tcx = xc.shape[0]
    xcl, ggl = rg_in(x, mod, g, w_in_b, conv_w, cb, 512)
    xcc, ggc = rg_in(xc, modc, g, w_in_b, conv_w, cb, tcx)
    zeros = jnp.zeros((1, D), F32)
    p = lambda d: (wa_b[d], wi_b[d], ba[d][None], bi[d][None], lam[d][None])
    (hcf,) = rg_scan(xcc, *p(0), zeros, tcx, False)
    (hlf,) = rg_scan(xcl, *p(0), hcf[tcx - 1:tcx], 512, False)
    if need_ctx:
        hcb, xc_new = rg_scan(xcc, *p(1), zeros, tcx, True, epi=(hcf, ggc, w_out_b, xc, modc))
    else:
        (hcb,) = rg_scan(xcc, *p(1), zeros, tcx, True)
        xc_new = None
    _, x_new = rg_scan(xcl, *p(1), hcb[0:1], 512, True, epi=(hlf, ggl, w_out_b, x, mod))
    return x_new, xc_new


def _qkv_body(x_ref, mod_ref, g_ref, w_ref, gm_ref, qg_ref, kg_ref, q_ref, k_ref, v_ref):
    h = _normmod(x_ref[...], g_ref[...], mod_ref[1:2, :], mod_ref[0:1, :]).astype(BF16)
    z = jnp.dot(h, w_ref[...], preferred_element_type=F32)

    def headnorm(v, gain):
        sq = v * v
        hi = sq.astype(BF16)
        lo = (sq - hi.astype(F32)).astype(BF16)
        ms = (jnp.dot(hi, gm_ref[...], preferred_element_type=F32)
              + jnp.dot(lo, gm_ref[...], preferred_element_type=F32))
        return (v * lax.rsqrt(ms + RMS_EPS)) * gain

    q_ref[...] = headnorm(z[:, :D], qg_ref[...]).astype(BF16)
    k_ref[...] = headnorm(z[:, D:2 * D], kg_ref[...]).astype(BF16)
    v_ref[...] = z[:, 2 * D:].astype(BF16)


def qkv_proj(x, mod, g, w_qkv, gmean, qg, kg, tm):
    t = x.shape[0]
    spec = pl.BlockSpec((tm, D), lambda i: (i, 0))
    return pl.pallas_call(
        _qkv_body,
        grid=(t // tm,),
        in_specs=[spec, _full((ADA_CHUNKS, D)), _full((1, D)), _full((D, 3 * D)), _full((D, D)),
                  _full((1, D)), _full((1, D))],
        out_specs=[spec, spec, spec],
        out_shape=[jax.ShapeDtypeStruct((t, D), BF16)] * 3,
        compiler_params=_cparams(("arbitrary",)),
        name="qkv_proj",
    )(x, mod, g, w_qkv, gmean, qg, kg)


def _attend_pair(q2, keys, vals, biases):
    lane = lax.broadcasted_iota(jnp.int32, q2.shape, 1)
    outs = []
    for half in range(2):
        sel = (lane < NA_HEAD_DIM) if half == 0 else (lane >= NA_HEAD_DIM)
        qh = jnp.where(sel, q2, jnp.zeros_like(q2))
        ss = []
        for kseg, bseg in zip(keys, biases):
            s = lax.dot_general(qh, kseg, (((1,), (1,)), ((), ())), preferred_element_type=F32)
            if bseg is not None:
                s = s + bseg[half]
            ss.append(s)
        m = ss[0].max(axis=-1, keepdims=True)
        for s in ss[1:]:
            m = jnp.maximum(m, s.max(axis=-1, keepdims=True))
        den = None
        acc = None
        for s, vseg in zip(ss, vals):
            p = jnp.exp(s - m)
            d = jnp.sum(p, axis=-1, keepdims=True)
            o = jnp.dot(p.astype(BF16), vseg, preferred_element_type=F32)
            den = d if den is None else den + d
            acc = o if acc is None else acc + o
        outs.append(acc / den)
    return jnp.where(lane < NA_HEAD_DIM, outs[0], outs[1])


def _na_body(q_ref, *refs):
    k_refs = refs[0:NA_ROWS]
    v_refs = refs[NA_ROWS:2 * NA_ROWS]
    kc_ref, vc_ref, bias_ref, o_ref = refs[2 * NA_ROWS:]
    kl = jnp.concatenate([r[...] for r in k_refs], axis=0)
    vl = jnp.concatenate([r[...] for r in v_refs], axis=0)
    for pr in range(NA_HEADS // 2):
        sl = slice(pr * LANES, (pr + 1) * LANES)
        o_ref[:, sl] = _attend_pair(
            q_ref[:, sl], [kl[:, sl], kc_ref[:, sl]], [vl[:, sl], vc_ref[:, sl]],
            [(bias_ref[0, 2 * pr], bias_ref[0, 2 * pr + 1]), None]).astype(BF16)


def na_attention(q, k, v, kc, vc, bias_tab):
    t = q.shape[0]
    rows = t // GRID_W
    kr = NA_ROWS
    nctx = kc.shape[0]

    def rstart(r):
        return jnp.clip(r - kr // 2, 0, rows - kr)

    kspecs = [pl.BlockSpec((GRID_W, D), functools.partial(lambda j, r: (rstart(r) + j, 0), j)) for j in range(kr)]
    nloc = kr * GRID_W
    return pl.pallas_call(
        _na_body,
        grid=(rows,),
        in_specs=[pl.BlockSpec((GRID_W, D), lambda r: (r, 0))] + kspecs + kspecs + [
            _full((nctx, D)), _full((nctx, D)),
            pl.BlockSpec((1, NA_HEADS, GRID_W, nloc), lambda r: (rstart(r) - r + (kr - 1), 0, 0, 0)),
        ],
        out_specs=pl.BlockSpec((GRID_W, D), lambda r: (r, 0)),
        out_shape=jax.ShapeDtypeStruct((t, D), BF16),
        compiler_params=_cparams(("arbitrary",)),
        name="na_attention",
    )(q, *([k] * kr), *([v] * kr), kc, vc, bias_tab)


def _ctx_attn_body(q_ref, k_ref, v_ref, o_ref):
    for pr in range(NA_HEADS // 2):
        sl = slice(pr * LANES, (pr + 1) * LANES)
        o_ref[:, sl] = _attend_pair(q_ref[:, sl], [k_ref[:, sl]], [v_ref[:, sl]], [None]).astype(BF16)


def ctx_attention(q, k, v):
    t = q.shape[0]
    return pl.pallas_call(
        _ctx_attn_body,
        grid=(1,),
        in_specs=[_full((t, D))] * 3,
        out_specs=_full((t, D)),
        out_shape=jax.ShapeDtypeStruct((t, D), BF16),
        compiler_params=_cparams(("arbitrary",)),
        name="ctx_attention",
    )(q, k, v)


def _na_bias_table(rpb):
    kr = NA_ROWS
    cols = np.arange(GRID_W)
    cstart = np.clip(cols - NA_COLS // 2, 0, GRID_W - NA_COLS)
    kcol = np.arange(GRID_W)
    inwin = (kcol[None, :] >= cstart[:, None]) & (kcol[None, :] < cstart[:, None] + NA_COLS)
    cidx = np.clip(kcol[None, :] - cols[:, None] + (NA_COLS - 1), 0, 2 * NA_COLS - 2)
    dq = np.arange(kr)
    ridx = (dq[:, None] - (kr - 1)) + np.arange(kr)[None, :] + (NA_ROWS - 1)
    tab = rpb[:, ridx]
    tab = tab[:, :, :, cidx]
    tab = jnp.where(jnp.asarray(inwin)[None, None, None], tab, NEG_BIG)
    tab = tab.transpose(1, 0, 3, 2, 4)
    return tab.reshape(kr, NA_HEADS, GRID_W, kr * GRID_W).astype(F32)


def na_layer(x, xc, mod, modc, g, w_qkv, q_g, k_g, rpb, w_o, need_ctx):
    w_qkv_b = w_qkv.astype(BF16)
    w_o_b = w_o.astype(BF16)
    gm = np.kron(np.eye(NA_HEADS), np.full((NA_HEAD_DIM, NA_HEAD_DIM), 1.0 / NA_HEAD_DIM))
    gmean = jnp.asarray(gm, dtype=BF16)
    qg = jnp.tile(q_g, NA_HEADS)[None] * (NA_HEAD_DIM ** -0.5)
    kg = jnp.tile(k_g, NA_HEADS)[None]
    q, k, v = qkv_proj(x, mod, g, w_qkv_b, gmean, qg, kg, 512)
    qc, kc, vc = qkv_proj(xc, modc, g, w_qkv_b, gmean, qg, kg, xc.shape[0])
    o = na_attention(q, k, v, kc, vc, _na_bias_table(rpb))
    x_new = proj_residual(o, w_o_b, x, mod, 2, 1024)
    xc_new = None
    if need_ctx:
        oc = ctx_attention(qc, kc, vc)
        xc_new = proj_residual(oc, w_o_b, xc, modc, 2, xc.shape[0])
    return x_new, xc_new


def _dft_mats(n):
    ang = 2.0 * np.pi * np.outer(np.arange(n), np.arange(n)) / n
    return np.cos(ang), np.sin(ang)


def _ft_chan_body(x_ref, mod_ref, g_ref, w_ref, ur_ref, ui_ref):
    h = _normmod(x_ref[...], g_ref[...], mod_ref[1:2, :], mod_ref[0:1, :]).astype(BF16)
    u = jnp.dot(h, w_ref[...], preferred_element_type=F32)
    ur_ref[...] = u[:, :D].astype(BF16)
    ui_ref[...] = u[:, D:].astype(BF16)


def ft_chan(x, mod, g, wc, tm):
    t = x.shape[0]
    spec = pl.BlockSpec((tm, D), lambda i: (i, 0))
    return pl.pallas_call(
        _ft_chan_body,
        grid=(t // tm,),
        in_specs=[spec, _full((ADA_CHUNKS, D)), _full((1, D)), _full((D, 2 * D))],
        out_specs=[spec, spec],
        out_shape=[jax.ShapeDtypeStruct((t, D), BF16)] * 2,
        compiler_params=_cparams(("arbitrary",)),
        name="ft_chan",
    )(x, mod, g, wc)


def _ft_a_body(n, nj, ur_ref, ui_ref, ma_ref, tc_ref, ts_ref, yr_ref, yi_ref):
    u = jnp.concatenate([ur_ref[...], ui_ref[...]], axis=0)
    y = jnp.dot(ma_ref[...], u, preferred_element_type=F32)
    for j in range(nj):
        yr = y[:n, j * D:(j + 1) * D]
        yi = y[n:, j * D:(j + 1) * D]
        tc = _lane_tile(tc_ref[0, :, j * LANES:(j + 1) * LANES])
        ts = _lane_tile(ts_ref[0, :, j * LANES:(j + 1) * LANES])
        yr_ref[j] = (yr * tc + yi * ts).astype(BF16)
        yi_ref[j] = (yi * tc - yr * ts).astype(BF16)


def _ft_c_body(n, nj, yr_ref, yi_ref, mc_ref, wf_ref, x_ref, mod_ref, o_ref):
    y = jnp.concatenate([yr_ref[...], yi_ref[...]], axis=0)
    f = jnp.dot(mc_ref[...], y, preferred_element_type=F32).astype(BF16)
    fs = jnp.concatenate([f[:, j * D:(j + 1) * D] for j in range(nj)], axis=0)
    z = jnp.dot(fs, wf_ref[...], preferred_element_type=F32)
    gate = mod_ref[2:3, :]
    for j in range(nj):
        o_ref[:, j * D:(j + 1) * D] = x_ref[:, j * D:(j + 1) * D] + gate * z[j * n:(j + 1) * n]


def _ft_ctx_body(x_ref, mod_ref, g_ref, wc_ref, ml_ref, wf_ref, o_ref):
    x = x_ref[...]
    h = _normmod(x, g_ref[...], mod_ref[1:2, :], mod_ref[0:1, :]).astype(BF16)
    u = jnp.dot(h, wc_ref[...], preferred_element_type=F32).astype(BF16)
    us = jnp.concatenate([u[:, :D], u[:, D:]], axis=0)
    f = jnp.dot(ml_ref[...], us, preferred_element_type=F32).astype(BF16)
    o_ref[...] = x + mod_ref[2:3, :] * jnp.dot(f, wf_ref[...], preferred_element_type=F32)


def fourier_layer(x, xc, mod, modc, g, w_f, need_ctx):
    t = x.shape[0]
    n = math.isqrt(t)
    assert n * n == t and n % 16 == 0
    w_f_b = w_f.astype(BF16)
    cw, sw = _dft_mats(FT_GROUP_W)
    eye = np.eye(D // FT_GROUP_W)
    wc = np.concatenate([np.kron(eye, cw), -np.kron(eye, sw)], axis=1) / math.sqrt(FT_GROUP_W)
    wc = jnp.asarray(wc, dtype=F32).astype(BF16)
    ur, ui = ft_chan(x, mod, g, wc, 512)
    cn, sn = _dft_mats(n)
    ma = jnp.asarray(np.block([[cn, sn], [-sn, cn]]) / math.sqrt(n), dtype=F32).astype(BF16)
    mc = jnp.asarray(np.concatenate([cn, sn], axis=1) / math.sqrt(n), dtype=F32).astype(BF16)
    nj = 8
    ang = 2.0 * np.pi * np.outer(np.arange(n), np.arange(n)) / t
    def expand(tab):
        a = jnp.asarray(tab, dtype=F32).reshape(n // nj, nj, n).transpose(0, 2, 1)
        return jnp.repeat(a, LANES, axis=2)
    twc, tws = expand(np.cos(ang)), expand(np.sin(ang))
    ublk = pl.BlockSpec((n, nj * D), lambda b: (0, b))
    yblk = pl.BlockSpec((nj, n, D), lambda b: (b, 0, 0))
    tblk = pl.BlockSpec((1, n, nj * LANES), lambda b: (b, 0, 0))
    yr, yi = pl.pallas_call(
        functools.partial(_ft_a_body, n, nj),
        grid=(n // nj,),
        in_specs=[ublk, ublk, _full((2 * n, 2 * n)), tblk, tblk],
        out_specs=[yblk, yblk],
        out_shape=[jax.ShapeDtypeStruct((n, n, D), BF16)] * 2,
        compiler_params=_cparams(("arbitrary",)),
        name="ft_stage_a",
    )(ur.reshape(n, n * D), ui.reshape(n, n * D), ma, twc, tws)
    x_new = pl.pallas_call(
        functools.partial(_ft_c_body, n, nj),
        grid=(n // nj,),
        in_specs=[ublk, ublk, _full((n, 2 * n)), _full((D, D)), ublk, _full((ADA_CHUNKS, D))],
        out_specs=ublk,
        out_shape=jax.ShapeDtypeStruct((n, n * D), F32),
        compiler_params=_cparams(("arbitrary",)),
        name="ft_stage_c",
    )(yr.reshape(n, n * D), yi.reshape(n, n * D), mc, w_f_b, x.reshape(n, n * D), mod).reshape(t, D)
    xc_new = None
    if need_ctx:
        lc = xc.shape[0]
        cl, sl = _dft_mats(lc)
        ml = jnp.asarray(np.concatenate([cl, sl], axis=1) / math.sqrt(lc), dtype=F32).astype(BF16)
        xc_new = pl.pallas_call(
            _ft_ctx_body,
            grid=(1,),
            in_specs=[_full((lc, D)), _full((ADA_CHUNKS, D)), _full((1, D)), _full((D, 2 * D)),
                      _full((lc, 2 * lc)), _full((D, D))],
            out_specs=_full((lc, D)),
            out_shape=jax.ShapeDtypeStruct((lc, D), F32),
            compiler_params=_cparams(("arbitrary",)),
            name="ft_ctx",
        )(xc, modc, g, wc, ml, w_f_b)
    return x_new, xc_new


def _router_body(x_ref, mod_ref, g_ref, r_ref, h_ref, info_ref, w0_ref, w1_ref):
    h = _normmod(x_ref[...], g_ref[...], mod_ref[4:5, :], mod_ref[3:4, :])
    h_ref[...] = h
    hh = h.astype(BF16)
    hl = (h - hh.astype(F32)).astype(BF16)
    r = r_ref[...]
    rh = r.astype(BF16)
    rl = (r - rh.astype(F32)).astype(BF16)
    logits = (jnp.dot(hh, rh, preferred_element_type=F32) + jnp.dot(hh, rl, preferred_element_type=F32)
              + jnp.dot(hl, rh, preferred_element_type=F32))
    lane = lax.broadcasted_iota(jnp.int32, logits.shape, 1)
    logits = jnp.where(lane < N_EXPERTS, logits, NEG_BIG)
    v0 = jnp.max(logits, axis=-1, keepdims=True)
    i0 = jnp.min(jnp.where(logits == v0, lane, LANES), axis=-1, keepdims=True)
    rest = jnp.where(lane == i0, NEG_BIG, logits)
    v1 = jnp.max(rest, axis=-1, keepdims=True)
    i1 = jnp.min(jnp.where(rest == v1, lane, LANES), axis=-1, keepdims=True)
    e = jnp.exp(v1 - v0)
    w0 = 1.0 / (1.0 + e)
    w1 = e / (1.0 + e)
    info_ref[...] = jnp.where(lane == 0, i0, jnp.where(lane == 1, i1, 0))
    w0_ref[...] = jnp.broadcast_to(w0, logits.shape)
    w1_ref[...] = jnp.broadcast_to(w1, logits.shape)


def moe_router(x, mod, g, router_pad, tm):
    t = x.shape[0]
    spec = pl.BlockSpec((tm, D), lambda i: (i, 0))
    lspec = pl.BlockSpec((tm, LANES), lambda i: (i, 0))
    return pl.pallas_call(
        _router_body,
        grid=(t // tm,),
        in_specs=[spec, _full((ADA_CHUNKS, D)), _full((1, D)), _full((D, LANES))],
        out_specs=[spec, lspec, lspec, lspec],
        out_shape=[jax.ShapeDtypeStruct((t, D), F32), jax.ShapeDtypeStruct((t, LANES), jnp.int32),
                   jax.ShapeDtypeStruct((t, LANES), F32), jax.ShapeDtypeStruct((t, LANES), F32)],
        compiler_params=_cparams(("arbitrary",)),
        name="moe_router",
    )(x, mod, g, router_pad)


SCATTER_TOKENS = 256


def _row_scatter_body(ntok_steps, didx_ref, h_ref, dst_ref, zero_scr, sem):
    i = pl.program_id(0)
    nrow = 2 * SCATTER_TOKENS

    def run(src_ref, mask):
        def issue(n, c):
            pltpu.make_async_copy(src_ref.at[pl.ds(n & mask, 1), :],
                                  dst_ref.at[pl.ds(didx_ref[0, 0, n], 1), :], sem).start()
            return c
        lax.fori_loop(0, nrow, issue, 0, unroll=8)

        def drain(n, c):
            pltpu.make_async_copy(src_ref.at[pl.ds(0, 1), :], dst_ref.at[pl.ds(0, 1), :], sem).wait()
            return c
        lax.fori_loop(0, nrow, drain, 0, unroll=8)

    @pl.when(i == 0)
    def _():
        zero_scr[...] = jnp.zeros_like(zero_scr)

    @pl.when(i < ntok_steps)
    def _():
        run(h_ref, SCATTER_TOKENS - 1)

    @pl.when(i >= ntok_steps)
    def _():
        run(zero_scr, SUBLANES - 1)


def row_scatter(h, dest, pad_pos):
    t = h.shape[0]
    ts = SCATTER_TOKENS
    ntok = t // ts
    npad = pad_pos.shape[0] // (2 * ts)
    didx = jnp.concatenate([dest.reshape(ntok, ts, 2).transpose(0, 2, 1).reshape(ntok, 1, 2 * ts),
                            pad_pos.reshape(npad, 1, 2 * ts)], axis=0)
    return pl.pallas_call(
        functools.partial(_row_scatter_body, ntok),
        grid=(ntok + npad,),
        in_specs=[pl.BlockSpec((1, 1, 2 * ts), lambda i: (i, 0, 0), memory_space=pltpu.SMEM),
                  pl.BlockSpec((ts, D), lambda i: (jnp.minimum(i, ntok - 1), 0))],
        out_specs=pl.BlockSpec(memory_space=pl.ANY),
        out_shape=jax.ShapeDtypeStruct((2 * t + pad_pos.shape[0], D), h.dtype),
        scratch_shapes=[pltpu.VMEM((SUBLANES, D), h.dtype), pltpu.SemaphoreType.DMA(())],
        compiler_params=_cparams(("arbitrary",)),
        name="moe_row_scatter",
    )(didx, h)


def _moe_ffn_body(te_ref, tv_ref, xg_ref, wg_ref, wu_ref, wd_ref, o_ref, h_scr, acc_scr):
    i = pl.program_id(0)
    j = pl.program_id(1)

    @pl.when(tv_ref[i] > 0)
    def _():
        @pl.when(j == 0)
        def _():
            h_scr[...] = xg_ref[...].astype(BF16)
            acc_scr[...] = jnp.zeros_like(acc_scr)

        h = h_scr[...]
        gg = jnp.dot(h, wg_ref[0], preferred_element_type=F32)
        uu = jnp.dot(h, wu_ref[0], preferred_element_type=F32)
        a = (gg * _sigmoid(gg)) * uu
        acc_scr[...] += jnp.dot(a.astype(BF16), wd_ref[0], preferred_element_type=F32)

        @pl.when(j == pl.num_programs(1) - 1)
        def _():
            o_ref[...] = acc_scr[...]

    @pl.when(jnp.logical_and(tv_ref[i] == 0, j == pl.num_programs(1) - 1))
    def _():
        o_ref[...] = jnp.zeros_like(o_ref)


def moe_ffn(xg, tile_e, tile_v, w_gu, w_down, tm, fc=512):
    p = xg.shape[0]
    nf = D_FF // fc

    def jeff(i, j, tv):
        return jnp.where(tv[i] > 0, j, nf - 1)

    grid_spec = pltpu.PrefetchScalarGridSpec(
        num_scalar_prefetch=2,
        grid=(p // tm, nf),
        in_specs=[
            pl.BlockSpec((tm, D), lambda i, j, te, tv: (i, 0)),
            pl.BlockSpec((1, D, fc), lambda i, j, te, tv: (te[i], 0, jeff(i, j, tv))),
            pl.BlockSpec((1, D, fc), lambda i, j, te, tv: (te[i], 0, nf + jeff(i, j, tv))),
            pl.BlockSpec((1, fc, D), lambda i, j, te, tv: (te[i], jeff(i, j, tv), 0)),
        ],
        out_specs=pl.BlockSpec((tm, D), lambda i, j, te, tv: (i, 0)),
        scratch_shapes=[pltpu.VMEM((tm, D), BF16), pltpu.VMEM((tm, D), F32)],
    )
    return pl.pallas_call(
        _moe_ffn_body,
        grid_spec=grid_spec,
        out_shape=jax.ShapeDtypeStruct((p, D), F32),
        compiler_params=_cparams(("arbitrary", "arbitrary")),
        name="moe_ffn",
    )(tile_e, tile_v, xg, w_gu, w_gu, w_down)


def _combine_body(tt, d0_ref, d1_ref, yp_ref, x_ref, mod_ref, w0_ref, w1_ref, o_ref, a_scr, b_scr, sem):
    def issue(n, c):
        pltpu.make_async_copy(yp_ref.at[pl.ds(d0_ref[0, 0, n], 1), :], a_scr.at[pl.ds(n, 1), :], sem).start()
        pltpu.make_async_copy(yp_ref.at[pl.ds(d1_ref[0, 0, n], 1), :], b_scr.at[pl.ds(n, 1), :], sem).start()
        return c
    lax.fori_loop(0, tt, issue, 0, unroll=8)

    def drain(n, c):
        pltpu.make_async_copy(yp_ref.at[pl.ds(0, 1), :], a_scr.at[pl.ds(0, 1), :], sem).wait()
        pltpu.make_async_copy(yp_ref.at[pl.ds(0, 1), :], b_scr.at[pl.ds(0, 1), :], sem).wait()
        return c
    lax.fori_loop(0, tt, drain, 0, unroll=8)
    w0 = _lane_tile(w0_ref[...])
    w1 = _lane_tile(w1_ref[...])
    o_ref[...] = x_ref[...] + mod_ref[5:6, :] * (w0 * a_scr[...] + w1 * b_scr[...])


def moe_combine(yp, d0, d1, x, mod, w0b, w1b, tt):
    t = x.shape[0]
    nt = t // tt
    ispec = pl.BlockSpec((1, 1, tt), lambda i: (i, 0, 0), memory_space=pltpu.SMEM)
    spec = pl.BlockSpec((tt, D), lambda i: (i, 0))
    lspec = pl.BlockSpec((tt, LANES), lambda i: (i, 0))
    return pl.pallas_call(
        functools.partial(_combine_body, tt),
        grid=(nt,),
        in_specs=[ispec, ispec, pl.BlockSpec(memory_space=pl.ANY), spec, _full((ADA_CHUNKS, D)), lspec, lspec],
        out_specs=spec,
        out_shape=jax.ShapeDtypeStruct((t, D), F32),
        scratch_shapes=[pltpu.VMEM((tt, D), F32), pltpu.VMEM((tt, D), F32), pltpu.SemaphoreType.DMA(())],
        compiler_params=_cparams(("arbitrary",)),
        name="moe_combine",
    )(d0.reshape(nt, 1, tt), d1.reshape(nt, 1, tt), yp, x, mod, w0b, w1b)


def _route_plan(idx, tm):
    t = idx.shape[0]
    n = 2 * t
    e_flat = idx.reshape(n)
    ex = jnp.arange(N_EXPERTS, dtype=jnp.int32)
    onehot = (e_flat[:, None] == ex[None, :]).astype(jnp.int32)
    csum = jnp.cumsum(onehot, axis=0)
    rank = jnp.sum((csum - onehot) * onehot, axis=1)
    counts = csum[-1]
    padded = ((counts + tm - 1) // tm) * tm
    pad_end = jnp.cumsum(padded)
    pad_off = pad_end - padded
    total = pad_end[-1]
    dest = jnp.sum(onehot * pad_off[None, :], axis=1) + rank
    gap = padded - counts
    tail_off = jnp.cumsum(tm - gap) - (tm - gap)
    r = jnp.arange(tm, dtype=jnp.int32)[None, :]
    pad_pos = jnp.where(r < gap[:, None], (pad_off + counts)[:, None] + r,
                        total + tail_off[:, None] + (r - gap[:, None])).reshape(-1)
    ntiles = (n + N_EXPERTS * tm) // tm
    tstart = jnp.arange(ntiles, dtype=jnp.int32) * tm
    tile_v = (tstart < total).astype(jnp.int32)
    tile_e = jnp.sum((jnp.minimum(tstart, total - 1)[:, None] >= pad_end[None, :]).astype(jnp.int32), axis=1)
    return dest.reshape(t, 2).astype(jnp.int32), pad_pos.astype(jnp.int32), tile_e.astype(jnp.int32), tile_v


def moe_layer(x, xc, mod, modc, g, router, w_gu, w_down, need_ctx, tm=512):
    w_gu_b = w_gu.astype(BF16)
    w_down_b = w_down.astype(BF16)
    router_pad = jnp.pad(router, ((0, 0), (0, LANES - N_EXPERTS)))
    s = x.shape[0]
    h, info, w0b, w1b = moe_router(x, mod, g, router_pad, 1024)
    if need_ctx:
        sc = xc.shape[0]
        hc, infoc, w0c, w1c = moe_router(xc, modc, g, router_pad, sc)
        h = jnp.concatenate([h, hc], axis=0)
        info = jnp.concatenate([info, infoc], axis=0)
    dest, pad_pos, tile_e, tile_v = _route_plan(info[:, :2], tm)
    xg = row_scatter(h, dest, pad_pos)
    yp = moe_ffn(xg, tile_e, tile_v, w_gu_b, w_down_b, tm)
    x_new = moe_combine(yp, dest[:s, 0], dest[:s, 1], x, mod, w0b, w1b, 512)
    xc_new = None
    if need_ctx:
        xc_new = moe_combine(yp, dest[s:, 0], dest[s:, 1], xc, modc, w0c, w1c, sc)
    return x_new, xc_new


def kernel(x, c, ctx, c_ctx, ada_w, ada_b, norm_g, rg_w_in, rg_conv_w, rg_conv_b, rg_wa, rg_ba, rg_wi, rg_bi,
           rg_lambda, rg_w_out, na_w_qkv, na_q_g, na_k_g, na_rpb, na_w_o, ft_w_out, ffn_w_gu, ffn_w_down,
           moe_router, moe_w_gu, moe_w_down):
    depth = ada_w.shape[0]
    assert x.shape[0] == 1 and x.shape[2] == D
    xs = x[0]
    xc = ctx[0]
    mods = ada_modulation(c, c_ctx, ada_w, ada_b)
    mix_idx = [0] * N_MIXERS
    dense_idx = 0
    moe_idx = 0
    for layer in range(depth):
        need_ctx = layer != depth - 1
        mod, modc = mods[layer, 0], mods[layer, 1]
        g0 = norm_g[layer, 0][None]
        g1 = norm_g[layer, 1][None]
        kind = layer % N_MIXERS
        j = mix_idx[kind]
        mix_idx[kind] += 1
        if kind == 0:
            xs, xcn = rglru_layer(xs, xc, mod, modc, g0, rg_w_in[j], rg_conv_w[j], rg_conv_b[j], rg_wa[j], rg_wi[j],
                                  rg_ba[j], rg_bi[j], rg_lambda[j], rg_w_out[j], need_ctx)
        elif kind == 1:
            xs, xcn = na_layer(xs, xc, mod, modc, g0, na_w_qkv[j], na_q_g[j], na_k_g[j], na_rpb[j], na_w_o[j],
                               need_ctx)
        else:
            xs, xcn = fourier_layer(xs, xc, mod, modc, g0, ft_w_out[j], need_ctx)
        if need_ctx:
            xc = xcn
        if layer % 2 == 0:
            w_gu = ffn_w_gu[dense_idx].astype(BF16)
            w_dn = ffn_w_down[dense_idx].astype(BF16)
            dense_idx += 1
            if need_ctx:
                xc = ffn_dense(xc, modc, g1, w_gu, w_dn, xc.shape[0])
            xs = ffn_dense(xs, mod, g1, w_gu, w_dn, 1024)
        else:
            xs, xcn = moe_layer(xs, xc, mod, modc, g1, moe_router[moe_idx], moe_w_gu[moe_idx], moe_w_down[moe_idx],
                                need_ctx)
            moe_idx += 1
            if need_ctx:
                xc = xcn
    return xs[None]
```

```python
import functools
import math

import numpy as np
import jax
import jax.numpy as jnp
from jax import lax
from jax.experimental import pallas as pl
from jax.experimental.pallas import tpu as pltpu

F32 = jnp.float32
BF16 = jnp.bfloat16

D = 1024
D_FF = 3584
N_EXPERTS = 8
GRID_W = 64
NA_HEADS = 16
NA_HEAD_DIM = 64
NA_ROWS = 8
NA_COLS = 16
FT_GROUP_W = 256
RG_BLOCK_W = 256
RMS_EPS = 1e-6
LRU_C = 8.0
N_MIXERS = 3
ADA_CHUNKS = 6

LANES = 128
SUBLANES = 8
VMEM_LIMIT = 56 * 1024 * 1024
NEG_BIG = -1e30


def _cparams(sem):
    return pltpu.CompilerParams(dimension_semantics=sem, vmem_limit_bytes=VMEM_LIMIT)


def _full(shape):
    nd = len(shape)
    return pl.BlockSpec(shape, lambda *_: (0,) * nd)


def _normmod(x, g, scale, shift):
    ms = jnp.mean(x * x, axis=-1, keepdims=True)
    y = x * lax.rsqrt(ms + RMS_EPS)
    return (y * g) * (1.0 + scale) + shift


def _lane_tile(v):
    return jnp.concatenate([v] * (D // LANES), axis=1)


def _sigmoid(v):
    return 1.0 / (1.0 + jnp.exp(-v))


def _gelu_tanh(v):
    c = math.sqrt(2.0 / math.pi)
    return v * (0.5 * (1.0 + jnp.tanh(c * (v + 0.044715 * (v * v * v)))))


def _ada_body(cin_ref, w_ref, b_ref, o_ref):
    v = cin_ref[...]
    s = v * _sigmoid(v)
    w = w_ref[0]
    r0 = jnp.sum(s[:, 0:1] * w, axis=0, keepdims=True)
    r1 = jnp.sum(s[:, 1:2] * w, axis=0, keepdims=True)
    o_ref[0] = jnp.concatenate([r0, r1], axis=0) + b_ref[0]


def ada_modulation(c, c_ctx, ada_w, ada_b):
    depth = ada_w.shape[0]
    n = ada_w.shape[2]
    nc = n // 4
    cin = jnp.stack([c[0], c_ctx], axis=1)
    out = pl.pallas_call(
        _ada_body,
        grid=(depth, n // nc),
        in_specs=[
            pl.BlockSpec((D, 2), lambda l, j: (0, 0)),
            pl.BlockSpec((1, D, nc), lambda l, j: (l, 0, j)),
            pl.BlockSpec((1, 1, nc), lambda l, j: (l, 0, j)),
        ],
        out_specs=pl.BlockSpec((1, 2, nc), lambda l, j: (l, 0, j)),
        out_shape=jax.ShapeDtypeStruct((depth, 2, n), F32),
        compiler_params=_cparams(("arbitrary", "arbitrary")),
        name="ada_mod",
    )(cin, ada_w, ada_b.reshape(depth, 1, n))
    return out.reshape(depth, 2, ADA_CHUNKS, D)


def _ffn_body(x_ref, mod_ref, g_ref, wg_ref, wu_ref, wd_ref, o_ref, h_scr, acc_scr):
    j = pl.program_id(1)

    @pl.when(j == 0)
    def _():
        h = _normmod(x_ref[...], g_ref[...], mod_ref[4:5, :], mod_ref[3:4, :])
        h_scr[...] = h.astype(BF16)
        acc_scr[...] = jnp.zeros_like(acc_scr)

    h = h_scr[...]
    gg = jnp.dot(h, wg_ref[...], preferred_element_type=F32)
    uu = jnp.dot(h, wu_ref[...], preferred_element_type=F32)
    a = (gg * _sigmoid(gg)) * uu
    acc_scr[...] += jnp.dot(a.astype(BF16), wd_ref[...], preferred_element_type=F32)

    @pl.when(j == pl.num_programs(1) - 1)
    def _():
        o_ref[...] = x_ref[...] + mod_ref[5:6, :] * acc_scr[...]


def ffn_dense(x, mod, g, w_gu, w_down, tm, fc=1792):
    t = x.shape[0]
    nf = D_FF // fc
    return pl.pallas_call(
        _ffn_body,
        grid=(t // tm, nf),
        in_specs=[
            pl.BlockSpec((tm, D), lambda i, j: (i, 0)),
            _full((ADA_CHUNKS, D)),
            _full((1, D)),
            pl.BlockSpec((D, fc), lambda i, j: (0, j)),
            pl.BlockSpec((D, fc), lambda i, j: (0, nf + j)),
            pl.BlockSpec((fc, D), lambda i, j: (j, 0)),
        ],
        out_specs=pl.BlockSpec((tm, D), lambda i, j: (i, 0)),
        out_shape=jax.ShapeDtypeStruct((t, D), F32),
        scratch_shapes=[pltpu.VMEM((tm, D), BF16), pltpu.VMEM((tm, D), F32)],
        compiler_params=_cparams(("arbitrary", "arbitrary")),
        name="ffn_dense",
    )(x, mod, g, w_gu, w_gu, w_down)


def _proj_body(gate_row, a_ref, w_ref, x_ref, mod_ref, o_ref):
    y = jnp.dot(a_ref[...], w_ref[...], preferred_element_type=F32)
    o_ref[...] = x_ref[...] + mod_ref[gate_row:gate_row + 1, :] * y


def proj_residual(a, w, x, mod, gate_row, tm):
    t, k = a.shape
    return pl.pallas_call(
        functools.partial(_proj_body, gate_row),
        grid=(t // tm,),
        in_specs=[
            pl.BlockSpec((tm, k), lambda i: (i, 0)),
            _full((k, D)),
            pl.BlockSpec((tm, D), lambda i: (i, 0)),
            _full((ADA_CHUNKS, D)),
        ],
        out_specs=pl.BlockSpec((tm, D), lambda i: (i, 0)),
        out_shape=jax.ShapeDtypeStruct((t, D), F32),
        compiler_params=_cparams(("arbitrary",)),
        name="proj_residual",
    )(a, w, x, mod)


HALO = SUBLANES


def _rg_in_body(tm, xp_ref, x_ref, xn_ref, mod_ref, g_ref, w_ref, cw_ref, cb_ref, xc_ref, gg_ref):
    i = pl.program_id(0)
    last = pl.num_programs(0) - 1
    xa = jnp.concatenate([xp_ref[...], x_ref[...], xn_ref[...]], axis=0)
    h = _normmod(xa, g_ref[...], mod_ref[1:2, :], mod_ref[0:1, :]).astype(BF16)
    z = jnp.dot(h, w_ref[...], preferred_element_type=F32)
    row = lax.broadcasted_iota(jnp.int32, (tm + 2 * HALO, 1), 0)
    valid = jnp.logical_and(jnp.logical_or(row >= HALO, i > 0),
                            jnp.logical_or(row < tm + HALO, i < last))
    xz = jnp.where(valid, z[:, :D], 0.0)
    y = cb_ref[...] + cw_ref[2:3, :] * xz[HALO:HALO + tm]
    y = y + cw_ref[0:1, :] * xz[HALO - 2:HALO - 2 + tm]
    y = y + cw_ref[1:2, :] * xz[HALO - 1:HALO - 1 + tm]
    y = y + cw_ref[3:4, :] * xz[HALO + 1:HALO + 1 + tm]
    xc_ref[...] = y
    gg_ref[...] = _gelu_tanh(z[HALO:HALO + tm, D:])


def rg_in(x, mod, g, w_in, conv_w, conv_b, tm):
    t = x.shape[0]
    nb = tm // HALO
    nblk = t // HALO
    return pl.pallas_call(
        functools.partial(_rg_in_body, tm),
        grid=(t // tm,),
        in_specs=[
            pl.BlockSpec((HALO, D), lambda i: (jnp.maximum(i * nb - 1, 0), 0)),
            pl.BlockSpec((tm, D), lambda i: (i, 0)),
            pl.BlockSpec((HALO, D), lambda i: (jnp.minimum((i + 1) * nb, nblk - 1), 0)),
            _full((ADA_CHUNKS, D)),
            _full((1, D)),
            _full((D, 2 * D)),
            _full((4, D)),
            _full((1, D)),
        ],
        out_specs=[pl.BlockSpec((tm, D), lambda i: (i, 0)), pl.BlockSpec((tm, D), lambda i: (i, 0))],
        out_shape=[jax.ShapeDtypeStruct((t, D), F32), jax.ShapeDtypeStruct((t, D), F32)],
        compiler_params=_cparams(("arbitrary",)),
        name="rg_in",
    )(x, x, x, mod, g, w_in, conv_w, conv_b)


def _rg_gates(xc, wa_ref, wi_ref, ba, bi, lam):
    xb = xc.astype(BF16)
    nblk = D // RG_BLOCK_W
    r = jnp.concatenate([jnp.dot(xb[:, n * RG_BLOCK_W:(n + 1) * RG_BLOCK_W], wa_ref[n],
                                 preferred_element_type=F32) for n in range(nblk)], axis=1)
    ig = jnp.concatenate([jnp.dot(xb[:, n * RG_BLOCK_W:(n + 1) * RG_BLOCK_W], wi_ref[n],
                                  preferred_element_type=F32) for n in range(nblk)], axis=1)
    r = 0.5 + 0.5 * jnp.tanh(0.5 * (r + ba))
    ig = 0.5 + 0.5 * jnp.tanh(0.5 * (ig + bi))
    nl = -lam
    softplus = jnp.maximum(nl, 0.0) + jnp.log1p(jnp.exp(-jnp.abs(nl)))
    log_a = (-LRU_C * r) * softplus
    a = jnp.exp(log_a)
    b = jnp.sqrt(1.0 - a * a) * (ig * xc)
    return a, b


def _rg_scan_body(reverse, epilogue, tc, *refs):
    if epilogue:
        (xc_ref, wa_ref, wi_ref, ba_ref, bi_ref, lam_ref, h0_ref, hf_ref, gg_ref, wo_ref, x_ref, mod_ref,
         h_ref, o_ref, a_scr, b_scr, carry_scr) = refs
    else:
        (xc_ref, wa_ref, wi_ref, ba_ref, bi_ref, lam_ref, h0_ref, h_ref, a_scr, b_scr, carry_scr) = refs
    c = pl.program_id(0)

    @pl.when(c == 0)
    def _():
        carry_scr[...] = jnp.broadcast_to(h0_ref[...], (SUBLANES, D))

    a, b = _rg_gates(xc_ref[...], wa_ref, wi_ref, ba_ref[...], bi_ref[...], lam_ref[...])
    a_scr[...] = a
    b_scr[...] = b
    nblk = tc // SUBLANES
    row = lax.broadcasted_iota(jnp.int32, (SUBLANES, D), 0)

    def block(n, carry):
        blk = (nblk - 1 - n) if reverse else n
        off = pl.multiple_of(blk * SUBLANES, SUBLANES)
        av = a_scr[pl.ds(off, SUBLANES), :]
        bv = b_scr[pl.ds(off, SUBLANES), :]
        for k in (1, 2, 4):
            shift = (SUBLANES - k) if reverse else k
            a_s = pltpu.roll(av, shift, 0)
            b_s = pltpu.roll(bv, shift, 0)
            m = (row < SUBLANES - k) if reverse else (row >= k)
            bv = jnp.where(m, av * b_s + bv, bv)
            av = jnp.where(m, av * a_s, av)
        hv = av * carry + bv
        h_ref[pl.ds(off, SUBLANES), :] = hv
        edge = hv[0:1, :] if reverse else hv[SUBLANES - 1:SUBLANES, :]
        return jnp.broadcast_to(edge, (SUBLANES, D))

    carry_scr[...] = lax.fori_loop(0, nblk, block, carry_scr[...], unroll=2)

    if epilogue:
        y = ((hf_ref[...] + h_ref[...]) * gg_ref[...]).astype(BF16)
        o_ref[...] = x_ref[...] + mod_ref[2:3, :] * jnp.dot(y, wo_ref[...], preferred_element_type=F32)


def rg_scan(xconv, wa, wi, ba, bi, lam, h0, tc, reverse, epi=None):
    t = xconv.shape[0]
    nchunks = t // tc
    idx = (lambda c: (nchunks - 1 - c, 0)) if reverse else (lambda c: (c, 0))
    nb = D // RG_BLOCK_W
    in_specs = [
        pl.BlockSpec((tc, D), idx),
        _full((nb, RG_BLOCK_W, RG_BLOCK_W)),
        _full((nb, RG_BLOCK_W, RG_BLOCK_W)),
        _full((1, D)), _full((1, D)), _full((1, D)), _full((1, D)),
    ]
    args = [xconv, wa, wi, ba, bi, lam, h0]
    out_specs = [pl.BlockSpec((tc, D), idx)]
    out_shape = [jax.ShapeDtypeStruct((t, D), F32)]
    if epi is not None:
        hf, gg, w_out, x, mod = epi
        in_specs += [pl.BlockSpec((tc, D), idx), pl.BlockSpec((tc, D), idx), _full((D, D)),
                     pl.BlockSpec((tc, D), idx), _full((ADA_CHUNKS, D))]
        args += [hf, gg, w_out, x, mod]
        out_specs.append(pl.BlockSpec((tc, D), idx))
        out_shape.append(jax.ShapeDtypeStruct((t, D), F32))
    res = pl.pallas_call(
        functools.partial(_rg_scan_body, reverse, epi is not None, tc),
        grid=(nchunks,),
        in_specs=in_specs,
        out_specs=out_specs,
        out_shape=out_shape,
        scratch_shapes=[pltpu.VMEM((tc, D), F32), pltpu.VMEM((tc, D), F32), pltpu.VMEM((SUBLANES, D), F32)],
        compiler_params=_cparams(("arbitrary",)),
        name="rg_scan_bwd" if reverse else "rg_scan_fwd",
    )(*args)
    return res


def rglru_layer(x, xc, mod, modc, g, w_in, conv_w, conv_b, wa, wi, ba, bi, lam, w_out, need_ctx):
    w_in_b = w_in.astype(BF16)
    wa_b = wa.astype(BF16)
    wi_b = wi.astype(BF16)
    w_out_b = w_out.astype(BF16)
    cb = conv_b[None]
    tcx = xc.shape[0]
    xcl, ggl = rg_in(x, mod, g, w_in_b, conv_w, cb, 512)
    xcc, ggc = rg_in(xc, modc, g, w_in_b, conv_w, cb, tcx)
    zeros = jnp.zeros((1, D), F32)
    p = lambda d: (wa_b[d], wi_b[d], ba[d][None], bi[d][None], lam[d][None])
    (hcf,) = rg_scan(xcc, *p(0), zeros, tcx, False)
    (hlf,) = rg_scan(xcl, *p(0), hcf[tcx - 1:tcx], 512, False)
    if need_ctx:
        hcb, xc_new = rg_scan(xcc, *p(1), zeros, tcx, True, epi=(hcf, ggc, w_out_b, xc, modc))
    else:
        (hcb,) = rg_scan(xcc, *p(1), zeros, tcx, True)
        xc_new = None
    _, x_new = rg_scan(xcl, *p(1), hcb[0:1], 512, True, epi=(hlf, ggl, w_out_b, x, mod))
    return x_new, xc_new


def _qkv_body(x_ref, mod_ref, g_ref, w_ref, gm_ref, qg_ref, kg_ref, q_ref, k_ref, v_ref):
    h = _normmod(x_ref[...], g_ref[...], mod_ref[1:2, :], mod_ref[0:1, :]).astype(BF16)
    z = jnp.dot(h, w_ref[...], preferred_element_type=F32)

    def headnorm(v, gain):
        sq = v * v
        hi = sq.astype(BF16)
        lo = (sq - hi.astype(F32)).astype(BF16)
        ms = (jnp.dot(hi, gm_ref[...], preferred_element_type=F32)
              + jnp.dot(lo, gm_ref[...], preferred_element_type=F32))
        return (v * lax.rsqrt(ms + RMS_EPS)) * gain

    q_ref[...] = headnorm(z[:, :D], qg_ref[...]).astype(BF16)
    k_ref[...] = headnorm(z[:, D:2 * D], kg_ref[...]).astype(BF16)
    v_ref[...] = z[:, 2 * D:].astype(BF16)


def qkv_proj(x, mod, g, w_qkv, gmean, qg, kg, tm):
    t = x.shape[0]
    spec = pl.BlockSpec((tm, D), lambda i: (i, 0))
    return pl.pallas_call(
        _qkv_body,
        grid=(t // tm,),
        in_specs=[spec, _full((ADA_CHUNKS, D)), _full((1, D)), _full((D, 3 * D)), _full((D, D)),
                  _full((1, D)), _full((1, D))],
        out_specs=[spec, spec, spec],
        out_shape=[jax.ShapeDtypeStruct((t, D), BF16)] * 3,
        compiler_params=_cparams(("arbitrary",)),
        name="qkv_proj",
    )(x, mod, g, w_qkv, gmean, qg, kg)


def _attend_pair(q2, keys, vals, biases):
    m_rows = q2.shape[0]
    lane = lax.broadcasted_iota(jnp.int32, q2.shape, 1)
    zero = jnp.zeros_like(q2)
    qs = jnp.concatenate([jnp.where(lane < NA_HEAD_DIM, q2, zero), jnp.where(lane >= NA_HEAD_DIM, q2, zero)], axis=0)
    ss = []
    for kseg, bseg in zip(keys, biases):
        s = lax.dot_general(qs, kseg, (((1,), (1,)), ((), ())), preferred_element_type=F32)
        if bseg is not None:
            s = s + jnp.concatenate([bseg[0], bseg[1]], axis=0)
        ss.append(s)
    m = ss[0].max(axis=-1, keepdims=True)
    for s in ss[1:]:
        m = jnp.maximum(m, s.max(axis=-1, keepdims=True))
    den = None
    acc = None
    for s, vseg in zip(ss, vals):
        p = jnp.exp(s - m)
        d = jnp.sum(p, axis=-1, keepdims=True)
        o = jnp.dot(p.astype(BF16), vseg, preferred_element_type=F32)
        den = d if den is None else den + d
        acc = o if acc is None else acc + o
    out = acc / den
    return jnp.where(lane < NA_HEAD_DIM, out[:m_rows], out[m_rows:])


NA_QROWS = 2
NA_UNION = NA_ROWS + NA_QROWS - 1


def _na_body(var_ref, q_ref, *refs):
    k_refs = refs[0:NA_UNION]
    v_refs = refs[NA_UNION:2 * NA_UNION]
    kc_ref, vc_ref, bias_ref, o_ref = refs[2 * NA_UNION:]
    kl = jnp.concatenate([r[...] for r in k_refs], axis=0)
    vl = jnp.concatenate([r[...] for r in v_refs], axis=0)
    for pr in range(NA_HEADS // 2):
        sl = slice(pr * LANES, (pr + 1) * LANES)
        o_ref[:, sl] = _attend_pair(
            q_ref[:, sl], [kl[:, sl], kc_ref[:, sl]], [vl[:, sl], vc_ref[:, sl]],
            [(bias_ref[0, 2 * pr], bias_ref[0, 2 * pr + 1]), None]).astype(BF16)


def _na_geometry(rows):
    steps = rows // NA_QROWS
    g = np.arange(steps)
    base = np.clip(NA_QROWS * g - NA_ROWS // 2, 0, rows - NA_UNION)
    r = NA_QROWS * g[:, None] + np.arange(NA_QROWS)[None, :]
    rs = np.clip(r - NA_ROWS // 2, 0, rows - NA_ROWS)
    key = np.concatenate([(base - NA_QROWS * g)[:, None], rs - r], axis=1)
    uniq, first, var = np.unique(key, axis=0, return_index=True, return_inverse=True)
    return base, var.reshape(-1).astype(np.int32), g[first]


def na_attention(q, k, v, kc, vc, bias_tab, var):
    t = q.shape[0]
    rows = t // GRID_W
    nctx = kc.shape[0]
    steps = rows // NA_QROWS

    def kbase(g):
        return jnp.clip(NA_QROWS * g - NA_ROWS // 2, 0, rows - NA_UNION)

    kspecs = [pl.BlockSpec((GRID_W, D), functools.partial(lambda j, g, var: (kbase(g) + j, 0), j))
              for j in range(NA_UNION)]
    qrows = NA_QROWS * GRID_W
    nloc = NA_UNION * GRID_W
    grid_spec = pltpu.PrefetchScalarGridSpec(
        num_scalar_prefetch=1,
        grid=(steps,),
        in_specs=[pl.BlockSpec((qrows, D), lambda g, var: (g, 0))] + kspecs + kspecs + [
            pl.BlockSpec((nctx, D), lambda g, var: (0, 0)), pl.BlockSpec((nctx, D), lambda g, var: (0, 0)),
            pl.BlockSpec((1, NA_HEADS, qrows, nloc), lambda g, var: (var[g], 0, 0, 0)),
        ],
        out_specs=pl.BlockSpec((qrows, D), lambda g, var: (g, 0)),
    )
    return pl.pallas_call(
        _na_body,
        grid_spec=grid_spec,
        out_shape=jax.ShapeDtypeStruct((t, D), BF16),
        compiler_params=_cparams(("arbitrary",)),
        name="na_attention",
    )(var, q, *([k] * NA_UNION), *([v] * NA_UNION), kc, vc, bias_tab)


def _ctx_attn_body(q_ref, k_ref, v_ref, o_ref):
    for pr in range(NA_HEADS // 2):
        sl = slice(pr * LANES, (pr + 1) * LANES)
        o_ref[:, sl] = _attend_pair(q_ref[:, sl], [k_ref[:, sl]], [v_ref[:, sl]], [None]).astype(BF16)


def ctx_attention(q, k, v):
    t = q.shape[0]
    return pl.pallas_call(
        _ctx_attn_body,
        grid=(1,),
        in_specs=[_full((t, D))] * 3,
        out_specs=_full((t, D)),
        out_shape=jax.ShapeDtypeStruct((t, D), BF16),
        compiler_params=_cparams(("arbitrary",)),
        name="ctx_attention",
    )(q, k, v)


def _na_bias_table(rpb, rows):
    base, var, reps = _na_geometry(rows)
    cols = np.arange(GRID_W)
    cstart = np.clip(cols - NA_COLS // 2, 0, GRID_W - NA_COLS)
    kcol = np.arange(GRID_W)
    inwin = (kcol[None, :] >= cstart[:, None]) & (kcol[None, :] < cstart[:, None] + NA_COLS)
    cidx = np.clip(kcol[None, :] - cols[:, None] + (NA_COLS - 1), 0, 2 * NA_COLS - 2)
    r = NA_QROWS * reps[:, None] + np.arange(NA_QROWS)[None, :]
    rs = np.clip(r - NA_ROWS // 2, 0, rows - NA_ROWS)
    krow = base[reps][:, None] + np.arange(NA_UNION)[None, :]
    rvalid = (krow[:, None, :] >= rs[:, :, None]) & (krow[:, None, :] < rs[:, :, None] + NA_ROWS)
    ridx = np.clip(krow[:, None, :] - r[:, :, None] + (NA_ROWS - 1), 0, 2 * NA_ROWS - 2)
    tab = rpb[:, ridx]
    tab = tab[..., cidx]
    mask = rvalid[:, :, :, None, None] & inwin[None, None, None]
    tab = jnp.where(jnp.asarray(mask)[None], tab, NEG_BIG)
    tab = tab.transpose(1, 0, 2, 4, 3, 5)
    tab = tab.reshape(len(reps), NA_HEADS, NA_QROWS * GRID_W, NA_UNION * GRID_W).astype(F32)
    return tab, jnp.asarray(var)


def na_layer(x, xc, mod, modc, g, w_qkv, q_g, k_g, rpb, w_o, need_ctx):
    w_qkv_b = w_qkv.astype(BF16)
    w_o_b = w_o.astype(BF16)
    gm = np.kron(np.eye(NA_HEADS), np.full((NA_HEAD_DIM, NA_HEAD_DIM), 1.0 / NA_HEAD_DIM))
    gmean = jnp.asarray(gm, dtype=BF16)
    qg = jnp.tile(q_g, NA_HEADS)[None] * (NA_HEAD_DIM ** -0.5)
    kg = jnp.tile(k_g, NA_HEADS)[None]
    q, k, v = qkv_proj(x, mod, g, w_qkv_b, gmean, qg, kg, 512)
    qc, kc, vc = qkv_proj(xc, modc, g, w_qkv_b, gmean, qg, kg, xc.shape[0])
    bias_tab, var = _na_bias_table(rpb, x.shape[0] // GRID_W)
    o = na_attention(q, k, v, kc, vc, bias_tab, var)
    x_new = proj_residual(o, w_o_b, x, mod, 2, 1024)
    xc_new = None
    if need_ctx:
        oc = ctx_attention(qc, kc, vc)
        xc_new = proj_residual(oc, w_o_b, xc, modc, 2, xc.shape[0])
    return x_new, xc_new


def _dft_mats(n):
    ang = 2.0 * np.pi * np.outer(np.arange(n), np.arange(n)) / n
    return np.cos(ang), np.sin(ang)


def _ft_a_body(n, nj, x_ref, mod_ref, g_ref, wc_ref, ma_ref, tc_ref, ts_ref, yr_ref, yi_ref):
    g, scale, shift = g_ref[...], mod_ref[1:2, :], mod_ref[0:1, :]
    h = jnp.concatenate([_normmod(x_ref[:, j, :], g, scale, shift).astype(BF16) for j in range(nj)], axis=0)
    u = jnp.dot(h, wc_ref[...], preferred_element_type=F32).astype(BF16)
    for j in range(nj):
        uj = u[j * n:(j + 1) * n]
        y = jnp.dot(ma_ref[...], jnp.concatenate([uj[:, :D], uj[:, D:]], axis=0), preferred_element_type=F32)
        yr, yi = y[:n], y[n:]
        tc = _lane_tile(tc_ref[0, :, j * LANES:(j + 1) * LANES])
        ts = _lane_tile(ts_ref[0, :, j * LANES:(j + 1) * LANES])
        yr_ref[:, j, :] = yr * tc + yi * ts
        yi_ref[:, j, :] = yi * tc - yr * ts


def _ft_c_body(n, nj, yr_ref, yi_ref, mc_ref, wf_ref, x_ref, mod_ref, o_ref):
    fs = []
    for j in range(nj):
        ys = jnp.concatenate([yr_ref[j].astype(BF16), yi_ref[j].astype(BF16)], axis=0)
        fs.append(jnp.dot(mc_ref[...], ys, preferred_element_type=F32).astype(BF16))
    z = jnp.dot(jnp.concatenate(fs, axis=0), wf_ref[...], preferred_element_type=F32)
    gate = mod_ref[2:3, :]
    for j in range(nj):
        o_ref[:, j, :] = x_ref[:, j, :] + gate * z[j * n:(j + 1) * n]


def _ft_ctx_body(x_ref, mod_ref, g_ref, wc_ref, ml_ref, wf_ref, o_ref):
    x = x_ref[...]
    h = _normmod(x, g_ref[...], mod_ref[1:2, :], mod_ref[0:1, :]).astype(BF16)
    u = jnp.dot(h, wc_ref[...], preferred_element_type=F32).astype(BF16)
    us = jnp.concatenate([u[:, :D], u[:, D:]], axis=0)
    f = jnp.dot(ml_ref[...], us, preferred_element_type=F32).astype(BF16)
    o_ref[...] = x + mod_ref[2:3, :] * jnp.dot(f, wf_ref[...], preferred_element_type=F32)


def fourier_layer(x, xc, mod, modc, g, w_f, need_ctx):
    t = x.shape[0]
    n = math.isqrt(t)
    assert n * n == t and n % 16 == 0
    w_f_b = w_f.astype(BF16)
    cw, sw = _dft_mats(FT_GROUP_W)
    eye = np.eye(D // FT_GROUP_W)
    wc = np.concatenate([np.kron(eye, cw), -np.kron(eye, sw)], axis=1) / math.sqrt(FT_GROUP_W)
    wc = jnp.asarray(wc, dtype=F32).astype(BF16)
    cn, sn = _dft_mats(n)
    ma = jnp.asarray(np.block([[cn, sn], [-sn, cn]]) / math.sqrt(n), dtype=F32).astype(BF16)
    mc = jnp.asarray(np.concatenate([cn, sn], axis=1) / math.sqrt(n), dtype=F32).astype(BF16)
    nj = 8
    ang = 2.0 * np.pi * np.outer(np.arange(n), np.arange(n)) / t
    def expand(tab):
        a = jnp.asarray(tab, dtype=F32).reshape(n // nj, nj, n).transpose(0, 2, 1)
        return jnp.repeat(a, LANES, axis=2)
    twc, tws = expand(np.cos(ang)), expand(np.sin(ang))
    xblk = pl.BlockSpec((n, nj, D), lambda b: (0, b, 0))
    yblk = pl.BlockSpec((nj, n, D), lambda b: (b, 0, 0))
    tblk = pl.BlockSpec((1, n, nj * LANES), lambda b: (b, 0, 0))
    x3 = x.reshape(n, n, D)
    yr, yi = pl.pallas_call(
        functools.partial(_ft_a_body, n, nj),
        grid=(n // nj,),
        in_specs=[xblk, _full((ADA_CHUNKS, D)), _full((1, D)), _full((D, 2 * D)), _full((2 * n, 2 * n)), tblk, tblk],
        out_specs=[xblk, xblk],
        out_shape=[jax.ShapeDtypeStruct((n, n, D), F32)] * 2,
        compiler_params=_cparams(("arbitrary",)),
        name="ft_stage_a",
    )(x3, mod, g, wc, ma, twc, tws)
    x_new = pl.pallas_call(
        functools.partial(_ft_c_body, n, nj),
        grid=(n // nj,),
        in_specs=[yblk, yblk, _full((n, 2 * n)), _full((D, D)), xblk, _full((ADA_CHUNKS, D))],
        out_specs=xblk,
        out_shape=jax.ShapeDtypeStruct((n, n, D), F32),
        compiler_params=_cparams(("arbitrary",)),
        name="ft_stage_c",
    )(yr, yi, mc, w_f_b, x3, mod).reshape(t, D)
    xc_new = None
    if need_ctx:
        lc = xc.shape[0]
        cl, sl = _dft_mats(lc)
        ml = jnp.asarray(np.concatenate([cl, sl], axis=1) / math.sqrt(lc), dtype=F32).astype(BF16)
        xc_new = pl.pallas_call(
            _ft_ctx_body,
            grid=(1,),
            in_specs=[_full((lc, D)), _full((ADA_CHUNKS, D)), _full((1, D)), _full((D, 2 * D)),
                      _full((lc, 2 * lc)), _full((D, D))],
            out_specs=_full((lc, D)),
            out_shape=jax.ShapeDtypeStruct((lc, D), F32),
            compiler_params=_cparams(("arbitrary",)),
            name="ft_ctx",
        )(xc, modc, g, wc, ml, w_f_b)
    return x_new, xc_new


def _router_body(x_ref, mod_ref, g_ref, r_ref, h_ref, info_ref, w0_ref, w1_ref):
    h = _normmod(x_ref[...], g_ref[...], mod_ref[4:5, :], mod_ref[3:4, :])
    h_ref[...] = h
    hh = h.astype(BF16)
    hl = (h - hh.astype(F32)).astype(BF16)
    r = r_ref[...]
    rh = r.astype(BF16)
    rl = (r - rh.astype(F32)).astype(BF16)
    logits = (jnp.dot(hh, rh, preferred_element_type=F32) + jnp.dot(hh, rl, preferred_element_type=F32)
              + jnp.dot(hl, rh, preferred_element_type=F32))
    lane = lax.broadcasted_iota(jnp.int32, logits.shape, 1)
    logits = jnp.where(lane < N_EXPERTS, logits, NEG_BIG)
    v0 = jnp.max(logits, axis=-1, keepdims=True)
    i0 = jnp.min(jnp.where(logits == v0, lane, LANES), axis=-1, keepdims=True)
    rest = jnp.where(lane == i0, NEG_BIG, logits)
    v1 = jnp.max(rest, axis=-1, keepdims=True)
    i1 = jnp.min(jnp.where(rest == v1, lane, LANES), axis=-1, keepdims=True)
    e = jnp.exp(v1 - v0)
    w0 = 1.0 / (1.0 + e)
    w1 = e / (1.0 + e)
    info_ref[...] = jnp.where(lane == 0, i0, jnp.where(lane == 1, i1, 0))
    w0_ref[...] = jnp.broadcast_to(w0, logits.shape)
    w1_ref[...] = jnp.broadcast_to(w1, logits.shape)


def moe_router(x, mod, g, router_pad, tm):
    t = x.shape[0]
    spec = pl.BlockSpec((tm, D), lambda i: (i, 0))
    lspec = pl.BlockSpec((tm, LANES), lambda i: (i, 0))
    return pl.pallas_call(
        _router_body,
        grid=(t // tm,),
        in_specs=[spec, _full((ADA_CHUNKS, D)), _full((1, D)), _full((D, LANES))],
        out_specs=[spec, lspec, lspec, lspec],
        out_shape=[jax.ShapeDtypeStruct((t, D), F32), jax.ShapeDtypeStruct((t, LANES), jnp.int32),
                   jax.ShapeDtypeStruct((t, LANES), F32), jax.ShapeDtypeStruct((t, LANES), F32)],
        compiler_params=_cparams(("arbitrary",)),
        name="moe_router",
    )(x, mod, g, router_pad)


SCATTER_TOKENS = 256


def _row_scatter_body(ntok_steps, didx_ref, h_ref, dst_ref, zero_scr, sem):
    i = pl.program_id(0)
    nrow = 2 * SCATTER_TOKENS

    def run(src_ref, mask):
        def issue(n, c):
            for half in range(2):
                m = n + half * SCATTER_TOKENS
                pltpu.make_async_copy(src_ref.at[pl.ds(m & mask, 1), :],
                                      dst_ref.at[pl.ds(didx_ref[0, 0, m], 1), :], sem).start(priority=half)
            return c
        lax.fori_loop(0, SCATTER_TOKENS, issue, 0, unroll=8)

        def drain(n, c):
            pltpu.make_async_copy(src_ref.at[pl.ds(0, 1), :], dst_ref.at[pl.ds(0, 1), :], sem).wait()
            return c
        lax.fori_loop(0, nrow, drain, 0, unroll=8)

    @pl.when(i == 0)
    def _():
        zero_scr[...] = jnp.zeros_like(zero_scr)

    @pl.when(i < ntok_steps)
    def _():
        run(h_ref, SCATTER_TOKENS - 1)

    @pl.when(i >= ntok_steps)
    def _():
        run(zero_scr, SUBLANES - 1)


def row_scatter(h, dest, pad_pos):
    t = h.shape[0]
    ts = SCATTER_TOKENS
    ntok = t // ts
    npad = pad_pos.shape[0] // (2 * ts)
    didx = jnp.concatenate([dest.reshape(ntok, ts, 2).transpose(0, 2, 1).reshape(ntok, 1, 2 * ts),
                            pad_pos.reshape(npad, 1, 2 * ts)], axis=0)
    return pl.pallas_call(
        functools.partial(_row_scatter_body, ntok),
        grid=(ntok + npad,),
        in_specs=[pl.BlockSpec((1, 1, 2 * ts), lambda i: (i, 0, 0), memory_space=pltpu.SMEM),
                  pl.BlockSpec((ts, D), lambda i: (jnp.minimum(i, ntok - 1), 0))],
        out_specs=pl.BlockSpec(memory_space=pl.ANY),
        out_shape=jax.ShapeDtypeStruct((2 * t + pad_pos.shape[0], D), h.dtype),
        scratch_shapes=[pltpu.VMEM((SUBLANES, D), h.dtype), pltpu.SemaphoreType.DMA(())],
        compiler_params=_cparams(("arbitrary",)),
        name="moe_row_scatter",
    )(didx, h)


def _moe_ffn_body(te_ref, tv_ref, xg_ref, wg_ref, wu_ref, wd_ref, o_ref, h_scr, acc_scr):
    i = pl.program_id(0)
    j = pl.program_id(1)

    @pl.when(tv_ref[i] > 0)
    def _():
        @pl.when(j == 0)
        def _():
            h_scr[...] = xg_ref[...].astype(BF16)
            acc_scr[...] = jnp.zeros_like(acc_scr)

        h = h_scr[...]
        gg = jnp.dot(h, wg_ref[0], preferred_element_type=F32)
        uu = jnp.dot(h, wu_ref[0], preferred_element_type=F32)
        a = (gg * _sigmoid(gg)) * uu
        acc_scr[...] += jnp.dot(a.astype(BF16), wd_ref[0], preferred_element_type=F32)

        @pl.when(j == pl.num_programs(1) - 1)
        def _():
            o_ref[...] = acc_scr[...]

    @pl.when(jnp.logical_and(tv_ref[i] == 0, j == pl.num_programs(1) - 1))
    def _():
        o_ref[...] = jnp.zeros_like(o_ref)


def moe_ffn(xg, tile_e, tile_v, w_gu, w_down, tm, fc=1792):
    p = xg.shape[0]
    nf = D_FF // fc

    def jeff(i, j, tv):
        return jnp.where(tv[i] > 0, j, nf - 1)

    grid_spec = pltpu.PrefetchScalarGridSpec(
        num_scalar_prefetch=2,
        grid=(p // tm, nf),
        in_specs=[
            pl.BlockSpec((tm, D), lambda i, j, te, tv: (i, 0)),
            pl.BlockSpec((1, D, fc), lambda i, j, te, tv: (te[i], 0, jeff(i, j, tv))),
            pl.BlockSpec((1, D, fc), lambda i, j, te, tv: (te[i], 0, nf + jeff(i, j, tv))),
            pl.BlockSpec((1, fc, D), lambda i, j, te, tv: (te[i], jeff(i, j, tv), 0)),
        ],
        out_specs=pl.BlockSpec((tm, D), lambda i, j, te, tv: (i, 0)),
        scratch_shapes=[pltpu.VMEM((tm, D), BF16), pltpu.VMEM((tm, D), F32)],
    )
    return pl.pallas_call(
        _moe_ffn_body,
        grid_spec=grid_spec,
        out_shape=jax.ShapeDtypeStruct((p, D), F32),
        compiler_params=_cparams(("arbitrary", "arbitrary")),
        name="moe_ffn",
    )(tile_e, tile_v, xg, w_gu, w_gu, w_down)


def _combine_body(tt, d0_ref, d1_ref, yp_ref, x_ref, mod_ref, w0_ref, w1_ref, o_ref, a_scr, b_scr, sem):
    def issue(n, c):
        pltpu.make_async_copy(yp_ref.at[pl.ds(d0_ref[0, 0, n], 1), :], a_scr.at[pl.ds(n, 1), :], sem).start(priority=0)
        pltpu.make_async_copy(yp_ref.at[pl.ds(d1_ref[0, 0, n], 1), :], b_scr.at[pl.ds(n, 1), :], sem).start(priority=1)
        return c
    lax.fori_loop(0, tt, issue, 0, unroll=8)

    def drain(n, c):
        pltpu.make_async_copy(yp_ref.at[pl.ds(0, 1), :], a_scr.at[pl.ds(0, 1), :], sem).wait()
        pltpu.make_async_copy(yp_ref.at[pl.ds(0, 1), :], b_scr.at[pl.ds(0, 1), :], sem).wait()
        return c
    lax.fori_loop(0, tt, drain, 0, unroll=8)
    w0 = _lane_tile(w0_ref[...])
    w1 = _lane_tile(w1_ref[...])
    o_ref[...] = x_ref[...] + mod_ref[5:6, :] * (w0 * a_scr[...] + w1 * b_scr[...])


def moe_combine(yp, d0, d1, x, mod, w0b, w1b, tt):
    t = x.shape[0]
    nt = t // tt
    ispec = pl.BlockSpec((1, 1, tt), lambda i: (i, 0, 0), memory_space=pltpu.SMEM)
    spec = pl.BlockSpec((tt, D), lambda i: (i, 0))
    lspec = pl.BlockSpec((tt, LANES), lambda i: (i, 0))
    return pl.pallas_call(
        functools.partial(_combine_body, tt),
        grid=(nt,),
        in_specs=[ispec, ispec, pl.BlockSpec(memory_space=pl.ANY), spec, _full((ADA_CHUNKS, D)), lspec, lspec],
        out_specs=spec,
        out_shape=jax.ShapeDtypeStruct((t, D), F32),
        scratch_shapes=[pltpu.VMEM((tt, D), F32), pltpu.VMEM((tt, D), F32), pltpu.SemaphoreType.DMA(())],
        compiler_params=_cparams(("arbitrary",)),
        name="moe_combine",
    )(d0.reshape(nt, 1, tt), d1.reshape(nt, 1, tt), yp, x, mod, w0b, w1b)


def _route_plan(idx, tm):
    t = idx.shape[0]
    n = 2 * t
    e_flat = idx.reshape(n)
    ex = jnp.arange(N_EXPERTS, dtype=jnp.int32)
    onehot = (e_flat[:, None] == ex[None, :]).astype(jnp.int32)
    csum = jnp.cumsum(onehot, axis=0)
    rank = jnp.sum((csum - onehot) * onehot, axis=1)
    counts = csum[-1]
    padded = ((counts + tm - 1) // tm) * tm
    pad_end = jnp.cumsum(padded)
    pad_off = pad_end - padded
    total = pad_end[-1]
    dest = jnp.sum(onehot * pad_off[None, :], axis=1) + rank
    gap = padded - counts
    tail_off = jnp.cumsum(tm - gap) - (tm - gap)
    r = jnp.arange(tm, dtype=jnp.int32)[None, :]
    pad_pos = jnp.where(r < gap[:, None], (pad_off + counts)[:, None] + r,
                        total + tail_off[:, None] + (r - gap[:, None])).reshape(-1)
    ntiles = (n + N_EXPERTS * tm) // tm
    tstart = jnp.arange(ntiles, dtype=jnp.int32) * tm
    tile_v = (tstart < total).astype(jnp.int32)
    tile_e = jnp.sum((jnp.minimum(tstart, total - 1)[:, None] >= pad_end[None, :]).astype(jnp.int32), axis=1)
    return dest.reshape(t, 2).astype(jnp.int32), pad_pos.astype(jnp.int32), tile_e.astype(jnp.int32), tile_v


def moe_layer(x, xc, mod, modc, g, router, w_gu, w_down, need_ctx, tm=512):
    w_gu_b = w_gu.astype(BF16)
    w_down_b = w_down.astype(BF16)
    router_pad = jnp.pad(router, ((0, 0), (0, LANES - N_EXPERTS)))
    s = x.shape[0]
    h, info, w0b, w1b = moe_router(x, mod, g, router_pad, 1024)
    if need_ctx:
        sc = xc.shape[0]
        hc, infoc, w0c, w1c = moe_router(xc, modc, g, router_pad, sc)
        h = jnp.concatenate([h, hc], axis=0)
        info = jnp.concatenate([info, infoc], axis=0)
    dest, pad_pos, tile_e, tile_v = _route_plan(info[:, :2], tm)
    xg = row_scatter(h, dest, pad_pos)
    yp = moe_ffn(xg, tile_e, tile_v, w_gu_b, w_down_b, tm)
    x_new = moe_combine(yp, dest[:s, 0], dest[:s, 1], x, mod, w0b, w1b, 512)
    xc_new = None
    if need_ctx:
        xc_new = moe_combine(yp, dest[s:, 0], dest[s:, 1], xc, modc, w0c, w1c, sc)
    return x_new, xc_new


def kernel(x, c, ctx, c_ctx, ada_w, ada_b, norm_g, rg_w_in, rg_conv_w, rg_conv_b, rg_wa, rg_ba, rg_wi, rg_bi,
           rg_lambda, rg_w_out, na_w_qkv, na_q_g, na_k_g, na_rpb, na_w_o, ft_w_out, ffn_w_gu, ffn_w_down,
           moe_router, moe_w_gu, moe_w_down):
    depth = ada_w.shape[0]
    assert x.shape[0] == 1 and x.shape[2] == D
    xs = x[0]
    xc = ctx[0]
    mods = ada_modulation(c, c_ctx, ada_w, ada_b)
    mix_idx = [0] * N_MIXERS
    dense_idx = 0
    moe_idx = 0
    for layer in range(depth):
        need_ctx = layer != depth - 1
        mod, modc = mods[layer, 0], mods[layer, 1]
        g0 = norm_g[layer, 0][None]
        g1 = norm_g[layer, 1][None]
        kind = layer % N_MIXERS
        j = mix_idx[kind]
        mix_idx[kind] += 1
        if kind == 0:
            xs, xcn = rglru_layer(xs, xc, mod, modc, g0, rg_w_in[j], rg_conv_w[j], rg_conv_b[j], rg_wa[j], rg_wi[j],
                                  rg_ba[j], rg_bi[j], rg_lambda[j], rg_w_out[j], need_ctx)
        elif kind == 1:
            xs, xcn = na_layer(xs, xc, mod, modc, g0, na_w_qkv[j], na_q_g[j], na_k_g[j], na_rpb[j], na_w_o[j],
                               need_ctx)
        else:
            xs, xcn = fourier_layer(xs, xc, mod, modc, g0, ft_w_out[j], need_ctx)
        if need_ctx:
            xc = xcn
        if layer % 2 == 0:
            w_gu = ffn_w_gu[dense_idx].astype(BF16)
            w_dn = ffn_w_down[dense_idx].astype(BF16)
            dense_idx += 1
            if need_ctx:
                xc = ffn_dense(xc, modc, g1, w_gu, w_dn, xc.shape[0])
            xs = ffn_dense(xs, mod, g1, w_gu, w_dn, 512)
        else:
            xs, xcn = moe_layer(xs, xc, mod, modc, g1, moe_router[moe_idx], moe_w_gu[moe_idx], moe_w_down[moe_idx],
                                need_ctx)
            moe_idx += 1
            if need_ctx:
                xc = xcn
    return xs[None]
```

```python
import functools
import math

import numpy as np
import jax
import jax.numpy as jnp
from jax import lax
from jax.experimental import pallas as pl
from jax.experimental.pallas import tpu as pltpu

F32 = jnp.float32
BF16 = jnp.bfloat16

D = 1024
D_FF = 3584
N_EXPERTS = 8
GRID_W = 64
NA_HEADS = 16
NA_HEAD_DIM = 64
NA_ROWS = 8
NA_COLS = 16
FT_GROUP_W = 256
RG_BLOCK_W = 256
RMS_EPS = 1e-6
LRU_C = 8.0
N_MIXERS = 3
ADA_CHUNKS = 6

LANES = 128
SUBLANES = 8
VMEM_LIMIT = 56 * 1024 * 1024
NEG_BIG = -1e30


def _cparams(sem):
    return pltpu.CompilerParams(dimension_semantics=sem, vmem_limit_bytes=VMEM_LIMIT)


def _full(shape):
    nd = len(shape)
    return pl.BlockSpec(shape, lambda *_: (0,) * nd)


def _normmod(x, g, scale, shift):
    ms = jnp.mean(x * x, axis=-1, keepdims=True)
    y = x * lax.rsqrt(ms + RMS_EPS)
    return (y * g) * (1.0 + scale) + shift


def _lane_tile(v):
    return jnp.concatenate([v] * (D // LANES), axis=1)


def _sigmoid(v):
    return 1.0 / (1.0 + jnp.exp(-v))


def _gelu_tanh(v):
    c = math.sqrt(2.0 / math.pi)
    return v * (0.5 * (1.0 + jnp.tanh(c * (v + 0.044715 * (v * v * v)))))


def _ada_body(cin_ref, w_ref, b_ref, o_ref):
    v = cin_ref[...]
    s = v * _sigmoid(v)
    w = w_ref[0]
    r0 = jnp.sum(s[:, 0:1] * w, axis=0, keepdims=True)
    r1 = jnp.sum(s[:, 1:2] * w, axis=0, keepdims=True)
    o_ref[0] = jnp.concatenate([r0, r1], axis=0) + b_ref[0]


def ada_modulation(c, c_ctx, ada_w, ada_b):
    depth = ada_w.shape[0]
    n = ada_w.shape[2]
    nc = n // 4
    cin = jnp.stack([c[0], c_ctx], axis=1)
    out = pl.pallas_call(
        _ada_body,
        grid=(depth, n // nc),
        in_specs=[
            pl.BlockSpec((D, 2), lambda l, j: (0, 0)),
            pl.BlockSpec((1, D, nc), lambda l, j: (l, 0, j)),
            pl.BlockSpec((1, 1, nc), lambda l, j: (l, 0, j)),
        ],
        out_specs=pl.BlockSpec((1, 2, nc), lambda l, j: (l, 0, j)),
        out_shape=jax.ShapeDtypeStruct((depth, 2, n), F32),
        compiler_params=_cparams(("arbitrary", "arbitrary")),
        name="ada_mod",
    )(cin, ada_w, ada_b.reshape(depth, 1, n))
    return out.reshape(depth, 2, ADA_CHUNKS, D)


def _ffn_body(x_ref, mod_ref, g_ref, wg_ref, wu_ref, wd_ref, o_ref, h_scr, acc_scr):
    j = pl.program_id(1)

    @pl.when(j == 0)
    def _():
        h = _normmod(x_ref[...], g_ref[...], mod_ref[4:5, :], mod_ref[3:4, :])
        h_scr[...] = h.astype(BF16)
        acc_scr[...] = jnp.zeros_like(acc_scr)

    h = h_scr[...]
    gg = jnp.dot(h, wg_ref[...], preferred_element_type=F32)
    uu = jnp.dot(h, wu_ref[...], preferred_element_type=F32)
    a = (gg * _sigmoid(gg)) * uu
    acc_scr[...] += jnp.dot(a.astype(BF16), wd_ref[...], preferred_element_type=F32)

    @pl.when(j == pl.num_programs(1) - 1)
    def _():
        o_ref[...] = x_ref[...] + mod_ref[5:6, :] * acc_scr[...]


def ffn_dense(x, mod, g, w_gu, w_down, li, tm, fc=1792):
    t = x.shape[0]
    nf = D_FF // fc
    return pl.pallas_call(
        _ffn_body,
        grid=(t // tm, nf),
        in_specs=[
            pl.BlockSpec((tm, D), lambda i, j: (i, 0)),
            _full((ADA_CHUNKS, D)),
            _full((1, D)),
            pl.BlockSpec((None, D, fc), lambda i, j: (li, 0, j)),
            pl.BlockSpec((None, D, fc), lambda i, j: (li, 0, nf + j)),
            pl.BlockSpec((None, fc, D), lambda i, j: (li, j, 0)),
        ],
        out_specs=pl.BlockSpec((tm, D), lambda i, j: (i, 0)),
        out_shape=jax.ShapeDtypeStruct((t, D), F32),
        scratch_shapes=[pltpu.VMEM((tm, D), BF16), pltpu.VMEM((tm, D), F32)],
        compiler_params=_cparams(("arbitrary", "arbitrary")),
        name="ffn_dense",
    )(x, mod, g, w_gu, w_gu, w_down)


def _proj_body(gate_row, a_ref, w_ref, x_ref, mod_ref, o_ref):
    y = jnp.dot(a_ref[...], w_ref[...], preferred_element_type=F32)
    o_ref[...] = x_ref[...] + mod_ref[gate_row:gate_row + 1, :] * y


def proj_residual(a, w, x, mod, gate_row, tm):
    t, k = a.shape
    return pl.pallas_call(
        functools.partial(_proj_body, gate_row),
        grid=(t // tm,),
        in_specs=[
            pl.BlockSpec((tm, k), lambda i: (i, 0)),
            _full((k, D)),
            pl.BlockSpec((tm, D), lambda i: (i, 0)),
            _full((ADA_CHUNKS, D)),
        ],
        out_specs=pl.BlockSpec((tm, D), lambda i: (i, 0)),
        out_shape=jax.ShapeDtypeStruct((t, D), F32),
        compiler_params=_cparams(("arbitrary",)),
        name="proj_residual",
    )(a, w, x, mod)


HALO = SUBLANES


def _rg_in_body(tm, xp_ref, x_ref, xn_ref, mod_ref, g_ref, w_ref, cw_ref, cb_ref, xc_ref, gg_ref):
    i = pl.program_id(0)
    last = pl.num_programs(0) - 1
    xa = jnp.concatenate([xp_ref[...], x_ref[...], xn_ref[...]], axis=0)
    h = _normmod(xa, g_ref[...], mod_ref[1:2, :], mod_ref[0:1, :]).astype(BF16)
    z = jnp.dot(h, w_ref[...], preferred_element_type=F32)
    row = lax.broadcasted_iota(jnp.int32, (tm + 2 * HALO, 1), 0)
    valid = jnp.logical_and(jnp.logical_or(row >= HALO, i > 0),
                            jnp.logical_or(row < tm + HALO, i < last))
    xz = jnp.where(valid, z[:, :D], 0.0)
    y = cb_ref[...] + cw_ref[2:3, :] * xz[HALO:HALO + tm]
    y = y + cw_ref[0:1, :] * xz[HALO - 2:HALO - 2 + tm]
    y = y + cw_ref[1:2, :] * xz[HALO - 1:HALO - 1 + tm]
    y = y + cw_ref[3:4, :] * xz[HALO + 1:HALO + 1 + tm]
    xc_ref[...] = y
    gg_ref[...] = _gelu_tanh(z[HALO:HALO + tm, D:])


def rg_in(x, mod, g, w_in, conv_w, conv_b, tm):
    t = x.shape[0]
    nb = tm // HALO
    nblk = t // HALO
    return pl.pallas_call(
        functools.partial(_rg_in_body, tm),
        grid=(t // tm,),
        in_specs=[
            pl.BlockSpec((HALO, D), lambda i: (jnp.maximum(i * nb - 1, 0), 0)),
            pl.BlockSpec((tm, D), lambda i: (i, 0)),
            pl.BlockSpec((HALO, D), lambda i: (jnp.minimum((i + 1) * nb, nblk - 1), 0)),
            _full((ADA_CHUNKS, D)),
            _full((1, D)),
            _full((D, 2 * D)),
            _full((4, D)),
            _full((1, D)),
        ],
        out_specs=[pl.BlockSpec((tm, D), lambda i: (i, 0)), pl.BlockSpec((tm, D), lambda i: (i, 0))],
        out_shape=[jax.ShapeDtypeStruct((t, D), F32), jax.ShapeDtypeStruct((t, D), F32)],
        compiler_params=_cparams(("arbitrary",)),
        name="rg_in",
    )(x, x, x, mod, g, w_in, conv_w, conv_b)


def _rg_gates(xc, wa_ref, wi_ref, ba, bi, lam):
    xb = xc.astype(BF16)
    nblk = D // RG_BLOCK_W
    r = jnp.concatenate([jnp.dot(xb[:, n * RG_BLOCK_W:(n + 1) * RG_BLOCK_W], wa_ref[n],
                                 preferred_element_type=F32) for n in range(nblk)], axis=1)
    ig = jnp.concatenate([jnp.dot(xb[:, n * RG_BLOCK_W:(n + 1) * RG_BLOCK_W], wi_ref[n],
                                  preferred_element_type=F32) for n in range(nblk)], axis=1)
    r = 0.5 + 0.5 * jnp.tanh(0.5 * (r + ba))
    ig = 0.5 + 0.5 * jnp.tanh(0.5 * (ig + bi))
    nl = -lam
    softplus = jnp.maximum(nl, 0.0) + jnp.log1p(jnp.exp(-jnp.abs(nl)))
    log_a = (-LRU_C * r) * softplus
    a = jnp.exp(log_a)
    b = jnp.sqrt(1.0 - a * a) * (ig * xc)
    return a, b


def _rg_scan_body(reverse, epilogue, tc, *refs):
    if epilogue:
        (xc_ref, wa_ref, wi_ref, ba_ref, bi_ref, lam_ref, h0_ref, hf_ref, gg_ref, wo_ref, x_ref, mod_ref,
         h_ref, o_ref, a_scr, b_scr, carry_scr) = refs
    else:
        (xc_ref, wa_ref, wi_ref, ba_ref, bi_ref, lam_ref, h0_ref, h_ref, a_scr, b_scr, carry_scr) = refs
    c = pl.program_id(0)

    @pl.when(c == 0)
    def _():
        carry_scr[...] = jnp.broadcast_to(h0_ref[...], (SUBLANES, D))

    a, b = _rg_gates(xc_ref[...], wa_ref, wi_ref, ba_ref[...], bi_ref[...], lam_ref[...])
    a_scr[...] = a
    b_scr[...] = b
    nblk = tc // SUBLANES
    row = lax.broadcasted_iota(jnp.int32, (SUBLANES, D), 0)

    def block(n, carry):
        blk = (nblk - 1 - n) if reverse else n
        off = pl.multiple_of(blk * SUBLANES, SUBLANES)
        av = a_scr[pl.ds(off, SUBLANES), :]
        bv = b_scr[pl.ds(off, SUBLANES), :]
        for k in (1, 2, 4):
            shift = (SUBLANES - k) if reverse else k
            a_s = pltpu.roll(av, shift, 0)
            b_s = pltpu.roll(bv, shift, 0)
            m = (row < SUBLANES - k) if reverse else (row >= k)
            bv = jnp.where(m, av * b_s + bv, bv)
            av = jnp.where(m, av * a_s, av)
        hv = av * carry + bv
        h_ref[pl.ds(off, SUBLANES), :] = hv
        edge = hv[0:1, :] if reverse else hv[SUBLANES - 1:SUBLANES, :]
        return jnp.broadcast_to(edge, (SUBLANES, D))

    carry_scr[...] = lax.fori_loop(0, nblk, block, carry_scr[...], unroll=2)

    if epilogue:
        y = ((hf_ref[...] + h_ref[...]) * gg_ref[...]).astype(BF16)
        o_ref[...] = x_ref[...] + mod_ref[2:3, :] * jnp.dot(y, wo_ref[...], preferred_element_type=F32)


def rg_scan(xconv, wa, wi, ba, bi, lam, h0, tc, reverse, epi=None):
    t = xconv.shape[0]
    nchunks = t // tc
    idx = (lambda c: (nchunks - 1 - c, 0)) if reverse else (lambda c: (c, 0))
    nb = D // RG_BLOCK_W
    in_specs = [
        pl.BlockSpec((tc, D), idx),
        _full((nb, RG_BLOCK_W, RG_BLOCK_W)),
        _full((nb, RG_BLOCK_W, RG_BLOCK_W)),
        _full((1, D)), _full((1, D)), _full((1, D)), _full((1, D)),
    ]
    args = [xconv, wa, wi, ba, bi, lam, h0]
    out_specs = [pl.BlockSpec((tc, D), idx)]
    out_shape = [jax.ShapeDtypeStruct((t, D), F32)]
    if epi is not None:
        hf, gg, w_out, x, mod = epi
        in_specs += [pl.BlockSpec((tc, D), idx), pl.BlockSpec((tc, D), idx), _full((D, D)),
                     pl.BlockSpec((tc, D), idx), _full((ADA_CHUNKS, D))]
        args += [hf, gg, w_out, x, mod]
        out_specs.append(pl.BlockSpec((tc, D), idx))
        out_shape.append(jax.ShapeDtypeStruct((t, D), F32))
    res = pl.pallas_call(
        functools.partial(_rg_scan_body, reverse, epi is not None, tc),
        grid=(nchunks,),
        in_specs=in_specs,
        out_specs=out_specs,
        out_shape=out_shape,
        scratch_shapes=[pltpu.VMEM((tc, D), F32), pltpu.VMEM((tc, D), F32), pltpu.VMEM((SUBLANES, D), F32)],
        compiler_params=_cparams(("arbitrary",)),
        name="rg_scan_bwd" if reverse else "rg_scan_fwd",
    )(*args)
    return res


def rglru_layer(x, xc, mod, modc, g, w_in, conv_w, conv_b, wa, wi, ba, bi, lam, w_out, need_ctx):
    w_in_b = w_in.astype(BF16)
    wa_b = wa.astype(BF16)
    wi_b = wi.astype(BF16)
    w_out_b = w_out.astype(BF16)
    cb = conv_b[None]
    tcx = xc.shape[0]
    xcl, ggl = rg_in(x, mod, g, w_in_b, conv_w, cb, 512)
    xcc, ggc = rg_in(xc, modc, g, w_in_b, conv_w, cb, tcx)
    zeros = jnp.zeros((1, D), F32)
    p = lambda d: (wa_b[d], wi_b[d], ba[d][None], bi[d][None], lam[d][None])
    (hcf,) = rg_scan(xcc, *p(0), zeros, tcx, False)
    (hlf,) = rg_scan(xcl, *p(0), hcf[tcx - 1:tcx], 512, False)
    if need_ctx:
        hcb, xc_new = rg_scan(xcc, *p(1), zeros, tcx, True, epi=(hcf, ggc, w_out_b, xc, modc))
    else:
        (hcb,) = rg_scan(xcc, *p(1), zeros, tcx, True)
        xc_new = None
    _, x_new = rg_scan(xcl, *p(1), hcb[0:1], 512, True, epi=(hlf, ggl, w_out_b, x, mod))
    return x_new, xc_new


def _qkv_body(x_ref, mod_ref, g_ref, w_ref, gm_ref, qg_ref, kg_ref, q_ref, k_ref, v_ref):
    h = _normmod(x_ref[...], g_ref[...], mod_ref[1:2, :], mod_ref[0:1, :]).astype(BF16)
    z = jnp.dot(h, w_ref[...], preferred_element_type=F32)

    def headnorm(v, gain):
        ms = jnp.dot((v * v).astype(BF16), gm_ref[...], preferred_element_type=F32)
        return (v * lax.rsqrt(ms + RMS_EPS)) * gain

    q_ref[...] = headnorm(z[:, :D], qg_ref[...]).astype(BF16)
    k_ref[...] = headnorm(z[:, D:2 * D], kg_ref[...]).astype(BF16)
    v_ref[...] = z[:, 2 * D:].astype(BF16)


def qkv_proj(x, mod, g, w_qkv, gmean, qg, kg, tm):
    t = x.shape[0]
    spec = pl.BlockSpec((tm, D), lambda i: (i, 0))
    return pl.pallas_call(
        _qkv_body,
        grid=(t // tm,),
        in_specs=[spec, _full((ADA_CHUNKS, D)), _full((1, D)), _full((D, 3 * D)), _full((D, D)),
                  _full((1, D)), _full((1, D))],
        out_specs=[spec, spec, spec],
        out_shape=[jax.ShapeDtypeStruct((t, D), BF16)] * 3,
        compiler_params=_cparams(("arbitrary",)),
        name="qkv_proj",
    )(x, mod, g, w_qkv, gmean, qg, kg)


def _attend_pair(q2, keys, vals, biases):
    m_rows = q2.shape[0]
    lane = lax.broadcasted_iota(jnp.int32, q2.shape, 1)
    zero = jnp.zeros_like(q2)
    qs = jnp.concatenate([jnp.where(lane < NA_HEAD_DIM, q2, zero), jnp.where(lane >= NA_HEAD_DIM, q2, zero)], axis=0)
    ss = []
    for kseg, bseg in zip(keys, biases):
        s = lax.dot_general(qs, kseg, (((1,), (1,)), ((), ())), preferred_element_type=F32)
        if bseg is not None:
            s = s + jnp.concatenate([bseg[0], bseg[1]], axis=0)
        ss.append(s)
    m = ss[0].max(axis=-1, keepdims=True)
    for s in ss[1:]:
        m = jnp.maximum(m, s.max(axis=-1, keepdims=True))
    den = None
    acc = None
    for s, vseg in zip(ss, vals):
        p = jnp.exp(s - m)
        d = jnp.sum(p, axis=-1, keepdims=True)
        o = jnp.dot(p.astype(BF16), vseg, preferred_element_type=F32)
        den = d if den is None else den + d
        acc = o if acc is None else acc + o
    out = acc / den
    return jnp.where(lane < NA_HEAD_DIM, out[:m_rows], out[m_rows:])


NA_QROWS = 2
NA_UNION = NA_ROWS + NA_QROWS - 1


def _na_body(var_ref, q_ref, *refs):
    k_refs = refs[0:NA_UNION]
    v_refs = refs[NA_UNION:2 * NA_UNION]
    kc_ref, vc_ref, bias_ref, o_ref = refs[2 * NA_UNION:]
    kl = jnp.concatenate([r[...] for r in k_refs], axis=0)
    vl = jnp.concatenate([r[...] for r in v_refs], axis=0)
    for pr in range(NA_HEADS // 2):
        sl = slice(pr * LANES, (pr + 1) * LANES)
        o_ref[:, sl] = _attend_pair(
            q_ref[:, sl], [kl[:, sl], kc_ref[:, sl]], [vl[:, sl], vc_ref[:, sl]],
            [(bias_ref[0, 2 * pr], bias_ref[0, 2 * pr + 1]), None]).astype(BF16)


def _na_geometry(rows):
    steps = rows // NA_QROWS
    g = np.arange(steps)
    base = np.clip(NA_QROWS * g - NA_ROWS // 2, 0, rows - NA_UNION)
    r = NA_QROWS * g[:, None] + np.arange(NA_QROWS)[None, :]
    rs = np.clip(r - NA_ROWS // 2, 0, rows - NA_ROWS)
    key = np.concatenate([(base - NA_QROWS * g)[:, None], rs - r], axis=1)
    uniq, first, var = np.unique(key, axis=0, return_index=True, return_inverse=True)
    return base, var.reshape(-1).astype(np.int32), g[first]


def na_attention(q, k, v, kc, vc, bias_tab, var):
    t = q.shape[0]
    rows = t // GRID_W
    nctx = kc.shape[0]
    steps = rows // NA_QROWS

    def kbase(g):
        return jnp.clip(NA_QROWS * g - NA_ROWS // 2, 0, rows - NA_UNION)

    kspecs = [pl.BlockSpec((GRID_W, D), functools.partial(lambda j, g, var: (kbase(g) + j, 0), j))
              for j in range(NA_UNION)]
    qrows = NA_QROWS * GRID_W
    nloc = NA_UNION * GRID_W
    grid_spec = pltpu.PrefetchScalarGridSpec(
        num_scalar_prefetch=1,
        grid=(steps,),
        in_specs=[pl.BlockSpec((qrows, D), lambda g, var: (g, 0))] + kspecs + kspecs + [
            pl.BlockSpec((nctx, D), lambda g, var: (0, 0)), pl.BlockSpec((nctx, D), lambda g, var: (0, 0)),
            pl.BlockSpec((1, NA_HEADS, qrows, nloc), lambda g, var: (var[g], 0, 0, 0)),
        ],
        out_specs=pl.BlockSpec((qrows, D), lambda g, var: (g, 0)),
    )
    return pl.pallas_call(
        _na_body,
        grid_spec=grid_spec,
        out_shape=jax.ShapeDtypeStruct((t, D), BF16),
        compiler_params=_cparams(("arbitrary",)),
        name="na_attention",
    )(var, q, *([k] * NA_UNION), *([v] * NA_UNION), kc, vc, bias_tab)


def _ctx_attn_body(q_ref, k_ref, v_ref, o_ref):
    for pr in range(NA_HEADS // 2):
        sl = slice(pr * LANES, (pr + 1) * LANES)
        o_ref[:, sl] = _attend_pair(q_ref[:, sl], [k_ref[:, sl]], [v_ref[:, sl]], [None]).astype(BF16)


def ctx_attention(q, k, v):
    t = q.shape[0]
    return pl.pallas_call(
        _ctx_attn_body,
        grid=(1,),
        in_specs=[_full((t, D))] * 3,
        out_specs=_full((t, D)),
        out_shape=jax.ShapeDtypeStruct((t, D), BF16),
        compiler_params=_cparams(("arbitrary",)),
        name="ctx_attention",
    )(q, k, v)


def _na_bias_table(rpb, rows):
    base, var, reps = _na_geometry(rows)
    cols = np.arange(GRID_W)
    cstart = np.clip(cols - NA_COLS // 2, 0, GRID_W - NA_COLS)
    kcol = np.arange(GRID_W)
    inwin = (kcol[None, :] >= cstart[:, None]) & (kcol[None, :] < cstart[:, None] + NA_COLS)
    cidx = np.clip(kcol[None, :] - cols[:, None] + (NA_COLS - 1), 0, 2 * NA_COLS - 2)
    r = NA_QROWS * reps[:, None] + np.arange(NA_QROWS)[None, :]
    rs = np.clip(r - NA_ROWS // 2, 0, rows - NA_ROWS)
    krow = base[reps][:, None] + np.arange(NA_UNION)[None, :]
    rvalid = (krow[:, None, :] >= rs[:, :, None]) & (krow[:, None, :] < rs[:, :, None] + NA_ROWS)
    ridx = np.clip(krow[:, None, :] - r[:, :, None] + (NA_ROWS - 1), 0, 2 * NA_ROWS - 2)
    nd = 2 * NA_COLS - 1
    w = jnp.pad(rpb.astype(F32), ((0, 0), (0, 0), (GRID_W - NA_COLS, 2 * GRID_W - (GRID_W - NA_COLS) - nd)))
    flat = jnp.tile(w, (1, 1, GRID_W))[:, :, :GRID_W * (2 * GRID_W - 1)]
    blk = flat.reshape(NA_HEADS, 2 * NA_ROWS - 1, GRID_W, 2 * GRID_W - 1)[..., GRID_W - 1:]
    blk = jnp.where(jnp.asarray(inwin)[None, None], blk, NEG_BIG)
    neg = jnp.full((NA_HEADS, GRID_W, GRID_W), NEG_BIG, F32)
    variants = []
    for v in range(len(reps)):
        strips = [jnp.concatenate([blk[:, ridx[v, a, j]] if rvalid[v, a, j] else neg for j in range(NA_UNION)], axis=2)
                  for a in range(NA_QROWS)]
        variants.append(jnp.concatenate(strips, axis=1))
    return jnp.stack(variants, axis=0), jnp.asarray(var)


def na_layer(x, xc, mod, modc, g, w_qkv, q_g, k_g, rpb, w_o, need_ctx):
    w_qkv_b = w_qkv.astype(BF16)
    w_o_b = w_o.astype(BF16)
    gm = np.kron(np.eye(NA_HEADS), np.full((NA_HEAD_DIM, NA_HEAD_DIM), 1.0 / NA_HEAD_DIM))
    gmean = jnp.asarray(gm, dtype=BF16)
    qg = jnp.tile(q_g, NA_HEADS)[None] * (NA_HEAD_DIM ** -0.5)
    kg = jnp.tile(k_g, NA_HEADS)[None]
    q, k, v = qkv_proj(x, mod, g, w_qkv_b, gmean, qg, kg, 512)
    qc, kc, vc = qkv_proj(xc, modc, g, w_qkv_b, gmean, qg, kg, xc.shape[0])
    bias_tab, var = _na_bias_table(rpb, x.shape[0] // GRID_W)
    o = na_attention(q, k, v, kc, vc, bias_tab, var)
    x_new = proj_residual(o, w_o_b, x, mod, 2, 1024)
    xc_new = None
    if need_ctx:
        oc = ctx_attention(qc, kc, vc)
        xc_new = proj_residual(oc, w_o_b, xc, modc, 2, xc.shape[0])
    return x_new, xc_new


def _dft_mats(n):
    ang = 2.0 * np.pi * np.outer(np.arange(n), np.arange(n)) / n
    return np.cos(ang), np.sin(ang)


def _ft_a_body(n, nj, x_ref, mod_ref, g_ref, wc_ref, ma_ref, tc_ref, ts_ref, yr_ref, yi_ref):
    g, scale, shift = g_ref[...], mod_ref[1:2, :], mod_ref[0:1, :]
    h = jnp.concatenate([_normmod(x_ref[:, j, :], g, scale, shift).astype(BF16) for j in range(nj)], axis=0)
    u = jnp.dot(h, wc_ref[...], preferred_element_type=F32).astype(BF16)
    for j in range(nj):
        uj = u[j * n:(j + 1) * n]
        y = jnp.dot(ma_ref[...], jnp.concatenate([uj[:, :D], uj[:, D:]], axis=0), preferred_element_type=F32)
        yr, yi = y[:n], y[n:]
        tc = _lane_tile(tc_ref[0, :, j * LANES:(j + 1) * LANES])
        ts = _lane_tile(ts_ref[0, :, j * LANES:(j + 1) * LANES])
        yr_ref[:, j, :] = yr * tc + yi * ts
        yi_ref[:, j, :] = yi * tc - yr * ts


def _ft_c_body(n, nj, yr_ref, yi_ref, mc_ref, wf_ref, x_ref, mod_ref, o_ref):
    fs = []
    for j in range(nj):
        ys = jnp.concatenate([yr_ref[j].astype(BF16), yi_ref[j].astype(BF16)], axis=0)
        fs.append(jnp.dot(mc_ref[...], ys, preferred_element_type=F32).astype(BF16))
    z = jnp.dot(jnp.concatenate(fs, axis=0), wf_ref[...], preferred_element_type=F32)
    gate = mod_ref[2:3, :]
    for j in range(nj):
        o_ref[:, j, :] = x_ref[:, j, :] + gate * z[j * n:(j + 1) * n]


def _ft_ctx_body(x_ref, mod_ref, g_ref, wc_ref, ml_ref, wf_ref, o_ref):
    x = x_ref[...]
    h = _normmod(x, g_ref[...], mod_ref[1:2, :], mod_ref[0:1, :]).astype(BF16)
    u = jnp.dot(h, wc_ref[...], preferred_element_type=F32).astype(BF16)
    us = jnp.concatenate([u[:, :D], u[:, D:]], axis=0)
    f = jnp.dot(ml_ref[...], us, preferred_element_type=F32).astype(BF16)
    o_ref[...] = x + mod_ref[2:3, :] * jnp.dot(f, wf_ref[...], preferred_element_type=F32)


def fourier_layer(x, xc, mod, modc, g, w_f, need_ctx):
    t = x.shape[0]
    n = math.isqrt(t)
    assert n * n == t and n % 16 == 0
    w_f_b = w_f.astype(BF16)
    cw, sw = _dft_mats(FT_GROUP_W)
    eye = np.eye(D // FT_GROUP_W)
    wc = np.concatenate([np.kron(eye, cw), -np.kron(eye, sw)], axis=1) / math.sqrt(FT_GROUP_W)
    wc = jnp.asarray(wc, dtype=F32).astype(BF16)
    cn, sn = _dft_mats(n)
    ma = jnp.asarray(np.block([[cn, sn], [-sn, cn]]) / math.sqrt(n), dtype=F32).astype(BF16)
    mc = jnp.asarray(np.concatenate([cn, sn], axis=1) / math.sqrt(n), dtype=F32).astype(BF16)
    nj = 8
    ang = 2.0 * np.pi * np.outer(np.arange(n), np.arange(n)) / t
    def expand(tab):
        a = jnp.asarray(tab, dtype=F32).reshape(n // nj, nj, n).transpose(0, 2, 1)
        return jnp.repeat(a, LANES, axis=2)
    twc, tws = expand(np.cos(ang)), expand(np.sin(ang))
    xblk = pl.BlockSpec((n, nj, D), lambda b: (0, b, 0))
    yblk = pl.BlockSpec((nj, n, D), lambda b: (b, 0, 0))
    tblk = pl.BlockSpec((1, n, nj * LANES), lambda b: (b, 0, 0))
    x3 = x.reshape(n, n, D)
    yr, yi = pl.pallas_call(
        functools.partial(_ft_a_body, n, nj),
        grid=(n // nj,),
        in_specs=[xblk, _full((ADA_CHUNKS, D)), _full((1, D)), _full((D, 2 * D)), _full((2 * n, 2 * n)), tblk, tblk],
        out_specs=[xblk, xblk],
        out_shape=[jax.ShapeDtypeStruct((n, n, D), F32)] * 2,
        compiler_params=_cparams(("arbitrary",)),
        name="ft_stage_a",
    )(x3, mod, g, wc, ma, twc, tws)
    x_new = pl.pallas_call(
        functools.partial(_ft_c_body, n, nj),
        grid=(n // nj,),
        in_specs=[yblk, yblk, _full((n, 2 * n)), _full((D, D)), xblk, _full((ADA_CHUNKS, D))],
        out_specs=xblk,
        out_shape=jax.ShapeDtypeStruct((n, n, D), F32),
        compiler_params=_cparams(("arbitrary",)),
        name="ft_stage_c",
    )(yr, yi, mc, w_f_b, x3, mod).reshape(t, D)
    xc_new = None
    if need_ctx:
        lc = xc.shape[0]
        cl, sl = _dft_mats(lc)
        ml = jnp.asarray(np.concatenate([cl, sl], axis=1) / math.sqrt(lc), dtype=F32).astype(BF16)
        xc_new = pl.pallas_call(
            _ft_ctx_body,
            grid=(1,),
            in_specs=[_full((lc, D)), _full((ADA_CHUNKS, D)), _full((1, D)), _full((D, 2 * D)),
                      _full((lc, 2 * lc)), _full((D, D))],
            out_specs=_full((lc, D)),
            out_shape=jax.ShapeDtypeStruct((lc, D), F32),
            compiler_params=_cparams(("arbitrary",)),
            name="ft_ctx",
        )(xc, modc, g, wc, ml, w_f_b)
    return x_new, xc_new


def _router_body(x_ref, mod_ref, g_ref, r_ref, h_ref, info_ref, w0_ref, w1_ref):
    h = _normmod(x_ref[...], g_ref[...], mod_ref[4:5, :], mod_ref[3:4, :])
    h_ref[...] = h
    hh = h.astype(BF16)
    hl = (h - hh.astype(F32)).astype(BF16)
    r = r_ref[...]
    rh = r.astype(BF16)
    rl = (r - rh.astype(F32)).astype(BF16)
    logits = (jnp.dot(hh, rh, preferred_element_type=F32) + jnp.dot(hh, rl, preferred_element_type=F32)
              + jnp.dot(hl, rh, preferred_element_type=F32))
    lane = lax.broadcasted_iota(jnp.int32, logits.shape, 1)
    logits = jnp.where(lane < N_EXPERTS, logits, NEG_BIG)
    v0 = jnp.max(logits, axis=-1, keepdims=True)
    i0 = jnp.min(jnp.where(logits == v0, lane, LANES), axis=-1, keepdims=True)
    rest = jnp.where(lane == i0, NEG_BIG, logits)
    v1 = jnp.max(rest, axis=-1, keepdims=True)
    i1 = jnp.min(jnp.where(rest == v1, lane, LANES), axis=-1, keepdims=True)
    e = jnp.exp(v1 - v0)
    w0 = 1.0 / (1.0 + e)
    w1 = e / (1.0 + e)
    info_ref[...] = jnp.where(lane == 0, i0, jnp.where(lane == 1, i1, 0))
    w0_ref[...] = jnp.broadcast_to(w0, logits.shape)
    w1_ref[...] = jnp.broadcast_to(w1, logits.shape)


def moe_router(x, mod, g, router_pad, tm):
    t = x.shape[0]
    spec = pl.BlockSpec((tm, D), lambda i: (i, 0))
    lspec = pl.BlockSpec((tm, LANES), lambda i: (i, 0))
    return pl.pallas_call(
        _router_body,
        grid=(t // tm,),
        in_specs=[spec, _full((ADA_CHUNKS, D)), _full((1, D)), _full((D, LANES))],
        out_specs=[spec, lspec, lspec, lspec],
        out_shape=[jax.ShapeDtypeStruct((t, D), F32), jax.ShapeDtypeStruct((t, LANES), jnp.int32),
                   jax.ShapeDtypeStruct((t, LANES), F32), jax.ShapeDtypeStruct((t, LANES), F32)],
        compiler_params=_cparams(("arbitrary",)),
        name="moe_router",
    )(x, mod, g, router_pad)


SCATTER_TOKENS = 256


def _row_scatter_body(ntok_steps, didx_ref, h_ref, dst_ref, zero_scr, sem):
    i = pl.program_id(0)
    nrow = 2 * SCATTER_TOKENS

    def run(src_ref, mask):
        def issue(n, c):
            for half in range(2):
                m = n + half * SCATTER_TOKENS
                pltpu.make_async_copy(src_ref.at[pl.ds(m & mask, 1), :],
                                      dst_ref.at[pl.ds(didx_ref[0, 0, m], 1), :], sem).start(priority=half)
            return c
        lax.fori_loop(0, SCATTER_TOKENS, issue, 0, unroll=8)

        def drain(n, c):
            pltpu.make_async_copy(src_ref.at[pl.ds(0, 1), :], dst_ref.at[pl.ds(0, 1), :], sem).wait()
            return c
        lax.fori_loop(0, nrow, drain, 0, unroll=8)

    @pl.when(i == 0)
    def _():
        zero_scr[...] = jnp.zeros_like(zero_scr)

    @pl.when(i < ntok_steps)
    def _():
        run(h_ref, SCATTER_TOKENS - 1)

    @pl.when(i >= ntok_steps)
    def _():
        run(zero_scr, SUBLANES - 1)


def row_scatter(h, dest, pad_pos):
    t = h.shape[0]
    ts = SCATTER_TOKENS
    ntok = t // ts
    npad = pad_pos.shape[0] // (2 * ts)
    didx = jnp.concatenate([dest.reshape(ntok, ts, 2).transpose(0, 2, 1).reshape(ntok, 1, 2 * ts),
                            pad_pos.reshape(npad, 1, 2 * ts)], axis=0)
    return pl.pallas_call(
        functools.partial(_row_scatter_body, ntok),
        grid=(ntok + npad,),
        in_specs=[pl.BlockSpec((1, 1, 2 * ts), lambda i: (i, 0, 0), memory_space=pltpu.SMEM),
                  pl.BlockSpec((ts, D), lambda i: (jnp.minimum(i, ntok - 1), 0))],
        out_specs=pl.BlockSpec(memory_space=pl.ANY),
        out_shape=jax.ShapeDtypeStruct((2 * t + pad_pos.shape[0], D), h.dtype),
        scratch_shapes=[pltpu.VMEM((SUBLANES, D), h.dtype), pltpu.SemaphoreType.DMA(())],
        compiler_params=_cparams(("arbitrary",)),
        name="moe_row_scatter",
    )(didx, h)


def _moe_ffn_body(te_ref, tv_ref, xg_ref, wg_ref, wu_ref, wd_ref, o_ref, h_scr, acc_scr):
    i = pl.program_id(0)
    j = pl.program_id(1)

    @pl.when(tv_ref[i] > 0)
    def _():
        @pl.when(j == 0)
        def _():
            h_scr[...] = xg_ref[...].astype(BF16)
            acc_scr[...] = jnp.zeros_like(acc_scr)

        h = h_scr[...]
        gg = jnp.dot(h, wg_ref[0], preferred_element_type=F32)
        uu = jnp.dot(h, wu_ref[0], preferred_element_type=F32)
        a = (gg * _sigmoid(gg)) * uu
        acc_scr[...] += jnp.dot(a.astype(BF16), wd_ref[0], preferred_element_type=F32)

        @pl.when(j == pl.num_programs(1) - 1)
        def _():
            o_ref[...] = acc_scr[...]

    @pl.when(jnp.logical_and(tv_ref[i] == 0, j == pl.num_programs(1) - 1))
    def _():
        o_ref[...] = jnp.zeros_like(o_ref)


def moe_ffn(xg, tile_e, tile_v, w_gu, w_down, li, tm, fc=1792):
    p = xg.shape[0]
    nf = D_FF // fc

    def jeff(i, j, tv):
        return jnp.where(tv[i] > 0, j, nf - 1)

    grid_spec = pltpu.PrefetchScalarGridSpec(
        num_scalar_prefetch=2,
        grid=(p // tm, nf),
        in_specs=[
            pl.BlockSpec((tm, D), lambda i, j, te, tv: (i, 0)),
            pl.BlockSpec((None, 1, D, fc), lambda i, j, te, tv: (li, te[i], 0, jeff(i, j, tv))),
            pl.BlockSpec((None, 1, D, fc), lambda i, j, te, tv: (li, te[i], 0, nf + jeff(i, j, tv))),
            pl.BlockSpec((None, 1, fc, D), lambda i, j, te, tv: (li, te[i], jeff(i, j, tv), 0)),
        ],
        out_specs=pl.BlockSpec((tm, D), lambda i, j, te, tv: (i, 0)),
        scratch_shapes=[pltpu.VMEM((tm, D), BF16), pltpu.VMEM((tm, D), F32)],
    )
    return pl.pallas_call(
        _moe_ffn_body,
        grid_spec=grid_spec,
        out_shape=jax.ShapeDtypeStruct((p, D), F32),
        compiler_params=_cparams(("arbitrary", "arbitrary")),
        name="moe_ffn",
    )(tile_e, tile_v, xg, w_gu, w_gu, w_down)


def _combine_body(tt, d0_ref, d1_ref, yp_ref, x_ref, mod_ref, w0_ref, w1_ref, o_ref, a_scr, b_scr, sem):
    def issue(n, c):
        pltpu.make_async_copy(yp_ref.at[pl.ds(d0_ref[0, 0, n], 1), :], a_scr.at[pl.ds(n, 1), :], sem).start(priority=0)
        pltpu.make_async_copy(yp_ref.at[pl.ds(d1_ref[0, 0, n], 1), :], b_scr.at[pl.ds(n, 1), :], sem).start(priority=1)
        return c
    lax.fori_loop(0, tt, issue, 0, unroll=8)

    def drain(n, c):
        pltpu.make_async_copy(yp_ref.at[pl.ds(0, 1), :], a_scr.at[pl.ds(0, 1), :], sem).wait()
        pltpu.make_async_copy(yp_ref.at[pl.ds(0, 1), :], b_scr.at[pl.ds(0, 1), :], sem).wait()
        return c
    lax.fori_loop(0, tt, drain, 0, unroll=8)
    w0 = _lane_tile(w0_ref[...])
    w1 = _lane_tile(w1_ref[...])
    o_ref[...] = x_ref[...] + mod_ref[5:6, :] * (w0 * a_scr[...] + w1 * b_scr[...])


def moe_combine(yp, d0, d1, x, mod, w0b, w1b, tt):
    t = x.shape[0]
    nt = t // tt
    ispec = pl.BlockSpec((1, 1, tt), lambda i: (i, 0, 0), memory_space=pltpu.SMEM)
    spec = pl.BlockSpec((tt, D), lambda i: (i, 0))
    lspec = pl.BlockSpec((tt, LANES), lambda i: (i, 0))
    return pl.pallas_call(
        functools.partial(_combine_body, tt),
        grid=(nt,),
        in_specs=[ispec, ispec, pl.BlockSpec(memory_space=pl.ANY), spec, _full((ADA_CHUNKS, D)), lspec, lspec],
        out_specs=spec,
        out_shape=jax.ShapeDtypeStruct((t, D), F32),
        scratch_shapes=[pltpu.VMEM((tt, D), F32), pltpu.VMEM((tt, D), F32), pltpu.SemaphoreType.DMA(())],
        compiler_params=_cparams(("arbitrary",)),
        name="moe_combine",
    )(d0.reshape(nt, 1, tt), d1.reshape(nt, 1, tt), yp, x, mod, w0b, w1b)


def _route_plan(idx, tm):
    t = idx.shape[0]
    n = 2 * t
    e_flat = idx.reshape(n)
    ex = jnp.arange(N_EXPERTS, dtype=jnp.int32)
    onehot = (e_flat[:, None] == ex[None, :]).astype(jnp.int32)
    csum = jnp.cumsum(onehot, axis=0)
    rank = jnp.sum((csum - onehot) * onehot, axis=1)
    counts = csum[-1]
    padded = ((counts + tm - 1) // tm) * tm
    pad_end = jnp.cumsum(padded)
    pad_off = pad_end - padded
    total = pad_end[-1]
    dest = jnp.sum(onehot * pad_off[None, :], axis=1) + rank
    gap = padded - counts
    tail_off = jnp.cumsum(tm - gap) - (tm - gap)
    r = jnp.arange(tm, dtype=jnp.int32)[None, :]
    pad_pos = jnp.where(r < gap[:, None], (pad_off + counts)[:, None] + r,
                        total + tail_off[:, None] + (r - gap[:, None])).reshape(-1)
    ntiles = (n + N_EXPERTS * tm) // tm
    tstart = jnp.arange(ntiles, dtype=jnp.int32) * tm
    tile_v = (tstart < total).astype(jnp.int32)
    tile_e = jnp.sum((jnp.minimum(tstart, total - 1)[:, None] >= pad_end[None, :]).astype(jnp.int32), axis=1)
    return dest.reshape(t, 2).astype(jnp.int32), pad_pos.astype(jnp.int32), tile_e.astype(jnp.int32), tile_v


def moe_layer(x, xc, mod, modc, g, router, w_gu_b, w_down_b, li, need_ctx, tm=512):
    router_pad = jnp.pad(router, ((0, 0), (0, LANES - N_EXPERTS)))
    s = x.shape[0]
    h, info, w0b, w1b = moe_router(x, mod, g, router_pad, 1024)
    if need_ctx:
        sc = xc.shape[0]
        hc, infoc, w0c, w1c = moe_router(xc, modc, g, router_pad, sc)
        h = jnp.concatenate([h, hc], axis=0)
        info = jnp.concatenate([info, infoc], axis=0)
    dest, pad_pos, tile_e, tile_v = _route_plan(info[:, :2], tm)
    xg = row_scatter(h, dest, pad_pos)
    yp = moe_ffn(xg, tile_e, tile_v, w_gu_b, w_down_b, li, tm)
    x_new = moe_combine(yp, dest[:s, 0], dest[:s, 1], x, mod, w0b, w1b, 512)
    xc_new = None
    if need_ctx:
        xc_new = moe_combine(yp, dest[s:, 0], dest[s:, 1], xc, modc, w0c, w1c, sc)
    return x_new, xc_new


def kernel(x, c, ctx, c_ctx, ada_w, ada_b, norm_g, rg_w_in, rg_conv_w, rg_conv_b, rg_wa, rg_ba, rg_wi, rg_bi,
           rg_lambda, rg_w_out, na_w_qkv, na_q_g, na_k_g, na_rpb, na_w_o, ft_w_out, ffn_w_gu, ffn_w_down,
           moe_router, moe_w_gu, moe_w_down):
    depth = ada_w.shape[0]
    assert x.shape[0] == 1 and x.shape[2] == D
    xs = x[0]
    xc = ctx[0]
    mods = ada_modulation(c, c_ctx, ada_w, ada_b)
    ffn_gu_b, ffn_dn_b = ffn_w_gu.astype(BF16), ffn_w_down.astype(BF16)
    moe_gu_b, moe_dn_b = moe_w_gu.astype(BF16), moe_w_down.astype(BF16)
    mix_idx = [0] * N_MIXERS
    dense_idx = 0
    moe_idx = 0
    for layer in range(depth):
        need_ctx = layer != depth - 1
        mod, modc = mods[layer, 0], mods[layer, 1]
        g0 = norm_g[layer, 0][None]
        g1 = norm_g[layer, 1][None]
        kind = layer % N_MIXERS
        j = mix_idx[kind]
        mix_idx[kind] += 1
        if kind == 0:
            xs, xcn = rglru_layer(xs, xc, mod, modc, g0, rg_w_in[j], rg_conv_w[j], rg_conv_b[j], rg_wa[j], rg_wi[j],
                                  rg_ba[j], rg_bi[j], rg_lambda[j], rg_w_out[j], need_ctx)
        elif kind == 1:
            xs, xcn = na_layer(xs, xc, mod, modc, g0, na_w_qkv[j], na_q_g[j], na_k_g[j], na_rpb[j], na_w_o[j],
                               need_ctx)
        else:
            xs, xcn = fourier_layer(xs, xc, mod, modc, g0, ft_w_out[j], need_ctx)
        if need_ctx:
            xc = xcn
        if layer % 2 == 0:
            if need_ctx:
                xc = ffn_dense(xc, modc, g1, ffn_gu_b, ffn_dn_b, dense_idx, xc.shape[0])
            xs = ffn_dense(xs, mod, g1, ffn_gu_b, ffn_dn_b, dense_idx, 512)
            dense_idx += 1
        else:
            xs, xcn = moe_layer(xs, xc, mod, modc, g1, moe_router[moe_idx], moe_gu_b, moe_dn_b, moe_idx, need_ctx)
            moe_idx += 1
            if need_ctx:
                xc = xcn
    return xs[None]
```

```python
import functools
import math

import numpy as np
import jax
import jax.numpy as jnp
from jax import lax
from jax.experimental import pallas as pl
from jax.experimental.pallas import tpu as pltpu

F32 = jnp.float32
BF16 = jnp.bfloat16

D = 1024
D_FF = 3584
N_EXPERTS = 8
GRID_W = 64
NA_HEADS = 16
NA_HEAD_DIM = 64
NA_ROWS = 8
NA_COLS = 16
FT_GROUP_W = 256
RG_BLOCK_W = 256
RMS_EPS = 1e-6
LRU_C = 8.0
N_MIXERS = 3
ADA_CHUNKS = 6

LANES = 128
SUBLANES = 8
VMEM_LIMIT = 56 * 1024 * 1024
NEG_BIG = -1e30
LOG2E = math.log2(math.e)


def _cparams(sem):
    return pltpu.CompilerParams(dimension_semantics=sem, vmem_limit_bytes=VMEM_LIMIT)


def _full(shape):
    nd = len(shape)
    return pl.BlockSpec(shape, lambda *_: (0,) * nd)


def _normmod(x, g, scale, shift):
    ms = jnp.mean(x * x, axis=-1, keepdims=True)
    y = x * lax.rsqrt(ms + RMS_EPS)
    return (y * g) * (1.0 + scale) + shift


def _lane_tile(v):
    return jnp.concatenate([v] * (D // LANES), axis=1)


def _sigmoid(v):
    return 1.0 / (1.0 + jnp.exp(-v))


def _gelu_tanh(v):
    c = math.sqrt(2.0 / math.pi)
    return v * (0.5 * (1.0 + jnp.tanh(c * (v + 0.044715 * (v * v * v)))))


def _ada_body(cin_ref, w_ref, b_ref, o_ref):
    v = cin_ref[...]
    s = v * _sigmoid(v)
    w = w_ref[0]
    r0 = jnp.sum(s[:, 0:1] * w, axis=0, keepdims=True)
    r1 = jnp.sum(s[:, 1:2] * w, axis=0, keepdims=True)
    o_ref[0] = jnp.concatenate([r0, r1], axis=0) + b_ref[0]


def ada_modulation(c, c_ctx, ada_w, ada_b):
    depth = ada_w.shape[0]
    n = ada_w.shape[2]
    nc = n // 4
    cin = jnp.stack([c[0], c_ctx], axis=1)
    out = pl.pallas_call(
        _ada_body,
        grid=(depth, n // nc),
        in_specs=[
            pl.BlockSpec((D, 2), lambda l, j: (0, 0)),
            pl.BlockSpec((1, D, nc), lambda l, j: (l, 0, j)),
            pl.BlockSpec((1, 1, nc), lambda l, j: (l, 0, j)),
        ],
        out_specs=pl.BlockSpec((1, 2, nc), lambda l, j: (l, 0, j)),
        out_shape=jax.ShapeDtypeStruct((depth, 2, n), F32),
        compiler_params=_cparams(("arbitrary", "arbitrary")),
        name="ada_mod",
    )(cin, ada_w, ada_b.reshape(depth, 1, n))
    return out.reshape(depth, 2, ADA_CHUNKS, D)


def _ffn_body(x_ref, mod_ref, g_ref, wg_ref, wu_ref, wd_ref, o_ref, h_scr, acc_scr):
    j = pl.program_id(1)

    @pl.when(j == 0)
    def _():
        h = _normmod(x_ref[...], g_ref[...], mod_ref[4:5, :], mod_ref[3:4, :])
        h_scr[...] = h.astype(BF16)
        acc_scr[...] = jnp.zeros_like(acc_scr)

    h = h_scr[...]
    gg = jnp.dot(h, wg_ref[...], preferred_element_type=F32)
    uu = jnp.dot(h, wu_ref[...], preferred_element_type=F32)
    a = (gg * _sigmoid(gg)) * uu
    acc_scr[...] += jnp.dot(a.astype(BF16), wd_ref[...], preferred_element_type=F32)

    @pl.when(j == pl.num_programs(1) - 1)
    def _():
        o_ref[...] = x_ref[...] + mod_ref[5:6, :] * acc_scr[...]


def ffn_dense(x, mod, g, w_gu, w_down, li, tm, fc=1792):
    t = x.shape[0]
    nf = D_FF // fc
    return pl.pallas_call(
        _ffn_body,
        grid=(t // tm, nf),
        in_specs=[
            pl.BlockSpec((tm, D), lambda i, j: (i, 0)),
            _full((ADA_CHUNKS, D)),
            _full((1, D)),
            pl.BlockSpec((None, D, fc), lambda i, j: (li, 0, j)),
            pl.BlockSpec((None, D, fc), lambda i, j: (li, 0, nf + j)),
            pl.BlockSpec((None, fc, D), lambda i, j: (li, j, 0)),
        ],
        out_specs=pl.BlockSpec((tm, D), lambda i, j: (i, 0)),
        out_shape=jax.ShapeDtypeStruct((t, D), F32),
        scratch_shapes=[pltpu.VMEM((tm, D), BF16), pltpu.VMEM((tm, D), F32)],
        compiler_params=_cparams(("arbitrary", "arbitrary")),
        name="ffn_dense",
    )(x, mod, g, w_gu, w_gu, w_down)


def _proj_body(gate_row, a_ref, w_ref, x_ref, mod_ref, o_ref):
    y = jnp.dot(a_ref[...], w_ref[...], preferred_element_type=F32)
    o_ref[...] = x_ref[...] + mod_ref[gate_row:gate_row + 1, :] * y


def proj_residual(a, w, x, mod, gate_row, tm):
    t, k = a.shape
    return pl.pallas_call(
        functools.partial(_proj_body, gate_row),
        grid=(t // tm,),
        in_specs=[
            pl.BlockSpec((tm, k), lambda i: (i, 0)),
            _full((k, D)),
            pl.BlockSpec((tm, D), lambda i: (i, 0)),
            _full((ADA_CHUNKS, D)),
        ],
        out_specs=pl.BlockSpec((tm, D), lambda i: (i, 0)),
        out_shape=jax.ShapeDtypeStruct((t, D), F32),
        compiler_params=_cparams(("arbitrary",)),
        name="proj_residual",
    )(a, w, x, mod)


HALO = SUBLANES


def _rg_in_body(tm, xp_ref, x_ref, xn_ref, mod_ref, g_ref, w_ref, cw_ref, cb_ref, xc_ref, gg_ref):
    i = pl.program_id(0)
    last = pl.num_programs(0) - 1
    xa = jnp.concatenate([xp_ref[...], x_ref[...], xn_ref[...]], axis=0)
    h = _normmod(xa, g_ref[...], mod_ref[1:2, :], mod_ref[0:1, :]).astype(BF16)
    z = jnp.dot(h, w_ref[...], preferred_element_type=F32)
    row = lax.broadcasted_iota(jnp.int32, (tm + 2 * HALO, 1), 0)
    valid = jnp.logical_and(jnp.logical_or(row >= HALO, i > 0),
                            jnp.logical_or(row < tm + HALO, i < last))
    xz = jnp.where(valid, z[:, :D], 0.0)
    y = cb_ref[...] + cw_ref[2:3, :] * xz[HALO:HALO + tm]
    y = y + cw_ref[0:1, :] * xz[HALO - 2:HALO - 2 + tm]
    y = y + cw_ref[1:2, :] * xz[HALO - 1:HALO - 1 + tm]
    y = y + cw_ref[3:4, :] * xz[HALO + 1:HALO + 1 + tm]
    xc_ref[...] = y
    gg_ref[...] = _gelu_tanh(z[HALO:HALO + tm, D:]).astype(BF16)


def rg_in(x, mod, g, w_in, conv_w, conv_b, tm):
    t = x.shape[0]
    nb = tm // HALO
    nblk = t // HALO
    return pl.pallas_call(
        functools.partial(_rg_in_body, tm),
        grid=(t // tm,),
        in_specs=[
            pl.BlockSpec((HALO, D), lambda i: (jnp.maximum(i * nb - 1, 0), 0)),
            pl.BlockSpec((tm, D), lambda i: (i, 0)),
            pl.BlockSpec((HALO, D), lambda i: (jnp.minimum((i + 1) * nb, nblk - 1), 0)),
            _full((ADA_CHUNKS, D)),
            _full((1, D)),
            _full((D, 2 * D)),
            _full((4, D)),
            _full((1, D)),
        ],
        out_specs=[pl.BlockSpec((tm, D), lambda i: (i, 0)), pl.BlockSpec((tm, D), lambda i: (i, 0))],
        out_shape=[jax.ShapeDtypeStruct((t, D), F32), jax.ShapeDtypeStruct((t, D), BF16)],
        compiler_params=_cparams(("arbitrary",)),
        name="rg_in",
    )(x, x, x, mod, g, w_in, conv_w, conv_b)


def _rg_gates(xc, wa_ref, wi_ref, ba, bi, lam):
    xb = xc.astype(BF16)
    nblk = D // RG_BLOCK_W
    r = jnp.concatenate([jnp.dot(xb[:, n * RG_BLOCK_W:(n + 1) * RG_BLOCK_W], wa_ref[n],
                                 preferred_element_type=F32) for n in range(nblk)], axis=1)
    ig = jnp.concatenate([jnp.dot(xb[:, n * RG_BLOCK_W:(n + 1) * RG_BLOCK_W], wi_ref[n],
                                  preferred_element_type=F32) for n in range(nblk)], axis=1)
    r = 0.5 + 0.5 * jnp.tanh(0.5 * (r + ba))
    ig = 0.5 + 0.5 * jnp.tanh(0.5 * (ig + bi))
    nl = -lam
    softplus = jnp.maximum(nl, 0.0) + jnp.log1p(jnp.exp(-jnp.abs(nl)))
    log_a = (-LRU_C * r) * softplus
    a = jnp.exp(log_a)
    b = jnp.sqrt(1.0 - a * a) * (ig * xc)
    return a, b


def _rg_scan_body(reverse, epilogue, emit_h, tc, *refs):
    xc_ref, wa_ref, wi_ref, ba_ref, bi_ref, lam_ref, h0_ref = refs[:7]
    refs = refs[7:]
    if epilogue:
        hf_ref, gg_ref, wo_ref, x_ref, mod_ref = refs[:5]
        refs = refs[5:]
    if emit_h:
        h_ref = refs[0]
        refs = refs[1:]
    if epilogue:
        o_ref = refs[0]
        refs = refs[1:]
    a_scr, b_scr, h_scr, carry_scr = refs
    c = pl.program_id(0)

    @pl.when(c == 0)
    def _():
        carry_scr[...] = jnp.broadcast_to(h0_ref[...], (SUBLANES, D))

    a, b = _rg_gates(xc_ref[...], wa_ref, wi_ref, ba_ref[...], bi_ref[...], lam_ref[...])
    a_scr[...] = a
    b_scr[...] = b
    nblk = tc // SUBLANES
    row = lax.broadcasted_iota(jnp.int32, (SUBLANES, D), 0)
    first = (row == SUBLANES - 1) if reverse else (row == 0)

    def block(n, carry):
        blk = (nblk - 1 - n) if reverse else n
        off = pl.multiple_of(blk * SUBLANES, SUBLANES)
        av = a_scr[pl.ds(off, SUBLANES), :]
        bv = b_scr[pl.ds(off, SUBLANES), :]
        bv = jnp.where(first, av * carry + bv, bv)
        av = jnp.where(first, 0.0, av)
        for k in (1, 2, 4):
            shift = (SUBLANES - k) if reverse else k
            bv = av * pltpu.roll(bv, shift, 0) + bv
            if k != 4:
                av = av * pltpu.roll(av, shift, 0)
        h_scr[pl.ds(off, SUBLANES), :] = bv
        edge = bv[0:1, :] if reverse else bv[SUBLANES - 1:SUBLANES, :]
        return jnp.broadcast_to(edge, (SUBLANES, D))

    carry_scr[...] = lax.fori_loop(0, nblk, block, carry_scr[...], unroll=2)

    if emit_h:
        h_ref[...] = h_scr[...].astype(h_ref.dtype)
    if epilogue:
        y = ((hf_ref[...].astype(F32) + h_scr[...]) * gg_ref[...].astype(F32)).astype(BF16)
        o_ref[...] = x_ref[...] + mod_ref[2:3, :] * jnp.dot(y, wo_ref[...], preferred_element_type=F32)


def rg_scan(xconv, wa, wi, ba, bi, lam, h0, tc, reverse, epi=None, h_dtype=F32):
    t = xconv.shape[0]
    nchunks = t // tc
    idx = (lambda c: (nchunks - 1 - c, 0)) if reverse else (lambda c: (c, 0))
    nb = D // RG_BLOCK_W
    blk = pl.BlockSpec((tc, D), idx)
    in_specs = [
        blk,
        _full((nb, RG_BLOCK_W, RG_BLOCK_W)),
        _full((nb, RG_BLOCK_W, RG_BLOCK_W)),
        _full((1, D)), _full((1, D)), _full((1, D)), _full((1, D)),
    ]
    args = [xconv, wa, wi, ba, bi, lam, h0]
    out_specs = []
    out_shape = []
    if epi is not None:
        hf, gg, w_out, x, mod = epi
        in_specs += [blk, blk, _full((D, D)), blk, _full((ADA_CHUNKS, D))]
        args += [hf, gg, w_out, x, mod]
    if h_dtype is not None:
        out_specs.append(blk)
        out_shape.append(jax.ShapeDtypeStruct((t, D), h_dtype))
    if epi is not None:
        out_specs.append(blk)
        out_shape.append(jax.ShapeDtypeStruct((t, D), F32))
    return pl.pallas_call(
        functools.partial(_rg_scan_body, reverse, epi is not None, h_dtype is not None, tc),
        grid=(nchunks,),
        in_specs=in_specs,
        out_specs=out_specs,
        out_shape=out_shape,
        scratch_shapes=[pltpu.VMEM((tc, D), F32), pltpu.VMEM((tc, D), F32), pltpu.VMEM((tc, D), F32),
                        pltpu.VMEM((SUBLANES, D), F32)],
        compiler_params=_cparams(("arbitrary",)),
        name="rg_scan_bwd" if reverse else "rg_scan_fwd",
    )(*args)


def rglru_layer(x, xc, mod, modc, g, w_in, conv_w, conv_b, wa, wi, ba, bi, lam, w_out, need_ctx):
    w_in_b = w_in.astype(BF16)
    wa_b = wa.astype(BF16)
    wi_b = wi.astype(BF16)
    w_out_b = w_out.astype(BF16)
    cb = conv_b[None]
    tcx = xc.shape[0]
    xcl, ggl = rg_in(x, mod, g, w_in_b, conv_w, cb, 512)
    xcc, ggc = rg_in(xc, modc, g, w_in_b, conv_w, cb, tcx)
    zeros = jnp.zeros((1, D), F32)
    p = lambda d: (wa_b[d], wi_b[d], ba[d][None], bi[d][None], lam[d][None])
    (hcf,) = rg_scan(xcc, *p(0), zeros, tcx, False)
    (hlf,) = rg_scan(xcl, *p(0), hcf[tcx - 1:tcx], 512, False, h_dtype=BF16)
    if need_ctx:
        hcb, xc_new = rg_scan(xcc, *p(1), zeros, tcx, True, epi=(hcf, ggc, w_out_b, xc, modc))
    else:
        (hcb,) = rg_scan(xcc, *p(1), zeros, tcx, True)
        xc_new = None
    (x_new,) = rg_scan(xcl, *p(1), hcb[0:1], 512, True, epi=(hlf, ggl, w_out_b, x, mod), h_dtype=None)
    return x_new, xc_new


def _qkv_body(x_ref, mod_ref, g_ref, w_ref, gm_ref, qg_ref, kg_ref, q_ref, k_ref, v_ref):
    h = _normmod(x_ref[...], g_ref[...], mod_ref[1:2, :], mod_ref[0:1, :]).astype(BF16)
    z = jnp.dot(h, w_ref[...], preferred_element_type=F32)

    def headnorm(v, gain):
        ms = jnp.dot((v * v).astype(BF16), gm_ref[...], preferred_element_type=F32)
        return (v * lax.rsqrt(ms + RMS_EPS)) * gain

    q_ref[...] = headnorm(z[:, :D], qg_ref[...]).astype(BF16)
    k_ref[...] = headnorm(z[:, D:2 * D], kg_ref[...]).astype(BF16)
    v_ref[...] = z[:, 2 * D:].astype(BF16)


def qkv_proj(x, mod, g, w_qkv, gmean, qg, kg, tm):
    t = x.shape[0]
    spec = pl.BlockSpec((tm, D), lambda i: (i, 0))
    return pl.pallas_call(
        _qkv_body,
        grid=(t // tm,),
        in_specs=[spec, _full((ADA_CHUNKS, D)), _full((1, D)), _full((D, 3 * D)), _full((D, D)),
                  _full((1, D)), _full((1, D))],
        out_specs=[spec, spec, spec],
        out_shape=[jax.ShapeDtypeStruct((t, D), BF16)] * 3,
        compiler_params=_cparams(("arbitrary",)),
        name="qkv_proj",
    )(x, mod, g, w_qkv, gmean, qg, kg)


def _attend_pair(q2, keys, vals, biases):
    m_rows = q2.shape[0]
    lane = lax.broadcasted_iota(jnp.int32, q2.shape, 1)
    zero = jnp.zeros_like(q2)
    qs = jnp.concatenate([jnp.where(lane < NA_HEAD_DIM, q2, zero), jnp.where(lane >= NA_HEAD_DIM, q2, zero)], axis=0)
    ss = []
    for kseg, bseg in zip(keys, biases):
        s = lax.dot_general(qs, kseg, (((1,), (1,)), ((), ())), preferred_element_type=F32)
        if bseg is not None:
            s = s + jnp.concatenate([bseg[0], bseg[1]], axis=0)
        ss.append(s)
    m = ss[0].max(axis=-1, keepdims=True)
    for s in ss[1:]:
        m = jnp.maximum(m, s.max(axis=-1, keepdims=True))
    den = None
    acc = None
    for s, vseg in zip(ss, vals):
        p = jnp.exp2(s - m)
        d = jnp.sum(p, axis=-1, keepdims=True)
        o = jnp.dot(p.astype(BF16), vseg, preferred_element_type=F32)
        den = d if den is None else den + d
        acc = o if acc is None else acc + o
    out = acc / den
    return jnp.where(lane < NA_HEAD_DIM, out[:m_rows], out[m_rows:])


NA_QROWS = 2
NA_UNION = NA_ROWS + NA_QROWS - 1


def _na_body(var_ref, q_ref, *refs):
    k_refs = refs[0:NA_UNION]
    v_refs = refs[NA_UNION:2 * NA_UNION]
    kc_ref, vc_ref, bias_ref, o_ref = refs[2 * NA_UNION:]
    kl = jnp.concatenate([r[...] for r in k_refs], axis=0)
    vl = jnp.concatenate([r[...] for r in v_refs], axis=0)
    for pr in range(NA_HEADS // 2):
        sl = slice(pr * LANES, (pr + 1) * LANES)
        o_ref[:, sl] = _attend_pair(
            q_ref[:, sl], [kl[:, sl], kc_ref[:, sl]], [vl[:, sl], vc_ref[:, sl]],
            [(bias_ref[0, 2 * pr], bias_ref[0, 2 * pr + 1]), None]).astype(BF16)


def _na_geometry(rows):
    steps = rows // NA_QROWS
    g = np.arange(steps)
    base = np.clip(NA_QROWS * g - NA_ROWS // 2, 0, rows - NA_UNION)
    r = NA_QROWS * g[:, None] + np.arange(NA_QROWS)[None, :]
    rs = np.clip(r - NA_ROWS // 2, 0, rows - NA_ROWS)
    key = np.concatenate([(base - NA_QROWS * g)[:, None], rs - r], axis=1)
    uniq, first, var = np.unique(key, axis=0, return_index=True, return_inverse=True)
    return base, var.reshape(-1).astype(np.int32), g[first]


def na_attention(q, k, v, kc, vc, bias_tab, var):
    t = q.shape[0]
    rows = t // GRID_W
    nctx = kc.shape[0]
    steps = rows // NA_QROWS

    def kbase(g):
        return jnp.clip(NA_QROWS * g - NA_ROWS // 2, 0, rows - NA_UNION)

    kspecs = [pl.BlockSpec((GRID_W, D), functools.partial(lambda j, g, var: (kbase(g) + j, 0), j))
              for j in range(NA_UNION)]
    qrows = NA_QROWS * GRID_W
    nloc = NA_UNION * GRID_W
    grid_spec = pltpu.PrefetchScalarGridSpec(
        num_scalar_prefetch=1,
        grid=(steps,),
        in_specs=[pl.BlockSpec((qrows, D), lambda g, var: (g, 0))] + kspecs + kspecs + [
            pl.BlockSpec((nctx, D), lambda g, var: (0, 0)), pl.BlockSpec((nctx, D), lambda g, var: (0, 0)),
            pl.BlockSpec((1, NA_HEADS, qrows, nloc), lambda g, var: (var[g], 0, 0, 0)),
        ],
        out_specs=pl.BlockSpec((qrows, D), lambda g, var: (g, 0)),
    )
    return pl.pallas_call(
        _na_body,
        grid_spec=grid_spec,
        out_shape=jax.ShapeDtypeStruct((t, D), BF16),
        compiler_params=_cparams(("arbitrary",)),
        name="na_attention",
    )(var, q, *([k] * NA_UNION), *([v] * NA_UNION), kc, vc, bias_tab)


def _ctx_attn_body(q_ref, k_ref, v_ref, o_ref):
    for pr in range(NA_HEADS // 2):
        sl = slice(pr * LANES, (pr + 1) * LANES)
        o_ref[:, sl] = _attend_pair(q_ref[:, sl], [k_ref[:, sl]], [v_ref[:, sl]], [None]).astype(BF16)


def ctx_attention(q, k, v):
    t = q.shape[0]
    return pl.pallas_call(
        _ctx_attn_body,
        grid=(1,),
        in_specs=[_full((t, D))] * 3,
        out_specs=_full((t, D)),
        out_shape=jax.ShapeDtypeStruct((t, D), BF16),
        compiler_params=_cparams(("arbitrary",)),
        name="ctx_attention",
    )(q, k, v)


def _na_bias_table(rpb, rows):
    base, var, reps = _na_geometry(rows)
    cols = np.arange(GRID_W)
    cstart = np.clip(cols - NA_COLS // 2, 0, GRID_W - NA_COLS)
    kcol = np.arange(GRID_W)
    inwin = (kcol[None, :] >= cstart[:, None]) & (kcol[None, :] < cstart[:, None] + NA_COLS)
    r = NA_QROWS * reps[:, None] + np.arange(NA_QROWS)[None, :]
    rs = np.clip(r - NA_ROWS // 2, 0, rows - NA_ROWS)
    krow = base[reps][:, None] + np.arange(NA_UNION)[None, :]
    rvalid = (krow[:, None, :] >= rs[:, :, None]) & (krow[:, None, :] < rs[:, :, None] + NA_ROWS)
    ridx = np.clip(krow[:, None, :] - r[:, :, None] + (NA_ROWS - 1), 0, 2 * NA_ROWS - 2)
    nd = 2 * NA_COLS - 1
    w = jnp.pad(rpb.astype(F32), ((0, 0), (0, 0), (GRID_W - NA_COLS, 2 * GRID_W - (GRID_W - NA_COLS) - nd)))
    flat = jnp.tile(w, (1, 1, GRID_W))[:, :, :GRID_W * (2 * GRID_W - 1)]
    blk = flat.reshape(NA_HEADS, 2 * NA_ROWS - 1, GRID_W, 2 * GRID_W - 1)[..., GRID_W - 1:]
    blk = jnp.where(jnp.asarray(inwin)[None, None], blk, NEG_BIG)
    neg = jnp.full((NA_HEADS, GRID_W, GRID_W), NEG_BIG, F32)
    variants = []
    for v in range(len(reps)):
        strips = [jnp.concatenate([blk[:, ridx[v, a, j]] if rvalid[v, a, j] else neg for j in range(NA_UNION)], axis=2)
                  for a in range(NA_QROWS)]
        variants.append(jnp.concatenate(strips, axis=1))
    return jnp.stack(variants, axis=0), jnp.asarray(var)


def na_layer(x, xc, mod, modc, g, w_qkv, q_g, k_g, rpb, w_o, need_ctx):
    w_qkv_b = w_qkv.astype(BF16)
    w_o_b = w_o.astype(BF16)
    gm = np.kron(np.eye(NA_HEADS), np.full((NA_HEAD_DIM, NA_HEAD_DIM), 1.0 / NA_HEAD_DIM))
    gmean = jnp.asarray(gm, dtype=BF16)
    qg = jnp.tile(q_g, NA_HEADS)[None] * (NA_HEAD_DIM ** -0.5 * LOG2E)
    kg = jnp.tile(k_g, NA_HEADS)[None]
    q, k, v = qkv_proj(x, mod, g, w_qkv_b, gmean, qg, kg, 512)
    qc, kc, vc = qkv_proj(xc, modc, g, w_qkv_b, gmean, qg, kg, xc.shape[0])
    bias_tab, var = _na_bias_table(rpb * LOG2E, x.shape[0] // GRID_W)
    o = na_attention(q, k, v, kc, vc, bias_tab, var)
    x_new = proj_residual(o, w_o_b, x, mod, 2, 1024)
    xc_new = None
    if need_ctx:
        oc = ctx_attention(qc, kc, vc)
        xc_new = proj_residual(oc, w_o_b, xc, modc, 2, xc.shape[0])
    return x_new, xc_new


def _dft_mats(n):
    ang = 2.0 * np.pi * np.outer(np.arange(n), np.arange(n)) / n
    return np.cos(ang), np.sin(ang)


def _channel_dft(h, wc):
    us = [jnp.dot(h[:, gi * FT_GROUP_W:(gi + 1) * FT_GROUP_W], wc, preferred_element_type=F32).astype(BF16)
          for gi in range(D // FT_GROUP_W)]
    return jnp.concatenate([u[:, :FT_GROUP_W] for u in us] + [u[:, FT_GROUP_W:] for u in us], axis=1)


def _ft_a_body(n, nj, x_ref, mod_ref, g_ref, wc_ref, ma_ref, tc_ref, ts_ref, yr_ref, yi_ref):
    g, scale, shift = g_ref[...], mod_ref[1:2, :], mod_ref[0:1, :]
    h = jnp.concatenate([_normmod(x_ref[:, j, :], g, scale, shift).astype(BF16) for j in range(nj)], axis=0)
    u = _channel_dft(h, wc_ref[...])
    for j in range(nj):
        uj = u[j * n:(j + 1) * n]
        y = jnp.dot(ma_ref[...], jnp.concatenate([uj[:, :D], uj[:, D:]], axis=0), preferred_element_type=F32)
        yr, yi = y[:n], y[n:]
        tc = _lane_tile(tc_ref[0, :, j * LANES:(j + 1) * LANES])
        ts = _lane_tile(ts_ref[0, :, j * LANES:(j + 1) * LANES])
        yr_ref[:, j, :] = yr * tc + yi * ts
        yi_ref[:, j, :] = yi * tc - yr * ts


def _ft_c_body(n, nj, yr_ref, yi_ref, mc_ref, wf_ref, x_ref, mod_ref, o_ref):
    fs = []
    for j in range(nj):
        ys = jnp.concatenate([yr_ref[j].astype(BF16), yi_ref[j].astype(BF16)], axis=0)
        fs.append(jnp.dot(mc_ref[...], ys, preferred_element_type=F32).astype(BF16))
    z = jnp.dot(jnp.concatenate(fs, axis=0), wf_ref[...], preferred_element_type=F32)
    gate = mod_ref[2:3, :]
    for j in range(nj):
        o_ref[:, j, :] = x_ref[:, j, :] + gate * z[j * n:(j + 1) * n]


def _ft_ctx_body(x_ref, mod_ref, g_ref, wc_ref, ml_ref, wf_ref, o_ref):
    x = x_ref[...]
    h = _normmod(x, g_ref[...], mod_ref[1:2, :], mod_ref[0:1, :]).astype(BF16)
    u = _channel_dft(h, wc_ref[...])
    us = jnp.concatenate([u[:, :D], u[:, D:]], axis=0)
    f = jnp.dot(ml_ref[...], us, preferred_element_type=F32).astype(BF16)
    o_ref[...] = x + mod_ref[2:3, :] * jnp.dot(f, wf_ref[...], preferred_element_type=F32)


def fourier_layer(x, xc, mod, modc, g, w_f, need_ctx):
    t = x.shape[0]
    n = math.isqrt(t)
    assert n * n == t and n % 16 == 0
    w_f_b = w_f.astype(BF16)
    cw, sw = _dft_mats(FT_GROUP_W)
    wc = jnp.asarray(np.concatenate([cw, -sw], axis=1) / math.sqrt(FT_GROUP_W), dtype=F32).astype(BF16)
    wcspec = _full((FT_GROUP_W, 2 * FT_GROUP_W))
    cn, sn = _dft_mats(n)
    ma = jnp.asarray(np.block([[cn, sn], [-sn, cn]]) / math.sqrt(n), dtype=F32).astype(BF16)
    mc = jnp.asarray(np.concatenate([cn, sn], axis=1) / math.sqrt(n), dtype=F32).astype(BF16)
    nj = 8
    ang = 2.0 * np.pi * np.outer(np.arange(n), np.arange(n)) / t
    def expand(tab):
        a = jnp.asarray(tab, dtype=F32).reshape(n // nj, nj, n).transpose(0, 2, 1)
        return jnp.repeat(a, LANES, axis=2)
    twc, tws = expand(np.cos(ang)), expand(np.sin(ang))
    xblk = pl.BlockSpec((n, nj, D), lambda b: (0, b, 0))
    yblk = pl.BlockSpec((nj, n, D), lambda b: (b, 0, 0))
    tblk = pl.BlockSpec((1, n, nj * LANES), lambda b: (b, 0, 0))
    x3 = x.reshape(n, n, D)
    yr, yi = pl.pallas_call(
        functools.partial(_ft_a_body, n, nj),
        grid=(n // nj,),
        in_specs=[xblk, _full((ADA_CHUNKS, D)), _full((1, D)), wcspec, _full((2 * n, 2 * n)), tblk, tblk],
        out_specs=[xblk, xblk],
        out_shape=[jax.ShapeDtypeStruct((n, n, D), F32)] * 2,
        compiler_params=_cparams(("arbitrary",)),
        name="ft_stage_a",
    )(x3, mod, g, wc, ma, twc, tws)
    x_new = pl.pallas_call(
        functools.partial(_ft_c_body, n, nj),
        grid=(n // nj,),
        in_specs=[yblk, yblk, _full((n, 2 * n)), _full((D, D)), xblk, _full((ADA_CHUNKS, D))],
        out_specs=xblk,
        out_shape=jax.ShapeDtypeStruct((n, n, D), F32),
        compiler_params=_cparams(("arbitrary",)),
        name="ft_stage_c",
    )(yr, yi, mc, w_f_b, x3, mod).reshape(t, D)
    xc_new = None
    if need_ctx:
        lc = xc.shape[0]
        cl, sl = _dft_mats(lc)
        ml = jnp.asarray(np.concatenate([cl, sl], axis=1) / math.sqrt(lc), dtype=F32).astype(BF16)
        xc_new = pl.pallas_call(
            _ft_ctx_body,
            grid=(1,),
            in_specs=[_full((lc, D)), _full((ADA_CHUNKS, D)), _full((1, D)), wcspec,
                      _full((lc, 2 * lc)), _full((D, D))],
            out_specs=_full((lc, D)),
            out_shape=jax.ShapeDtypeStruct((lc, D), F32),
            compiler_params=_cparams(("arbitrary",)),
            name="ft_ctx",
        )(xc, modc, g, wc, ml, w_f_b)
    return x_new, xc_new


def _router_body(x_ref, mod_ref, g_ref, r_ref, h_ref, info_ref, w0_ref, w1_ref):
    h = _normmod(x_ref[...], g_ref[...], mod_ref[4:5, :], mod_ref[3:4, :])
    h_ref[...] = h
    hh = h.astype(BF16)
    hl = (h - hh.astype(F32)).astype(BF16)
    r = r_ref[...]
    rh = r.astype(BF16)
    rl = (r - rh.astype(F32)).astype(BF16)
    logits = (jnp.dot(hh, rh, preferred_element_type=F32) + jnp.dot(hh, rl, preferred_element_type=F32)
              + jnp.dot(hl, rh, preferred_element_type=F32))
    lane = lax.broadcasted_iota(jnp.int32, logits.shape, 1)
    logits = jnp.where(lane < N_EXPERTS, logits, NEG_BIG)
    v0 = jnp.max(logits, axis=-1, keepdims=True)
    i0 = jnp.min(jnp.where(logits == v0, lane, LANES), axis=-1, keepdims=True)
    rest = jnp.where(lane == i0, NEG_BIG, logits)
    v1 = jnp.max(rest, axis=-1, keepdims=True)
    i1 = jnp.min(jnp.where(rest == v1, lane, LANES), axis=-1, keepdims=True)
    e = jnp.exp(v1 - v0)
    w0 = 1.0 / (1.0 + e)
    w1 = e / (1.0 + e)
    info_ref[...] = jnp.where(lane == 0, i0, jnp.where(lane == 1, i1, 0))
    w0_ref[...] = jnp.broadcast_to(w0, logits.shape)
    w1_ref[...] = jnp.broadcast_to(w1, logits.shape)


def moe_router(x, mod, g, router_pad, tm):
    t = x.shape[0]
    spec = pl.BlockSpec((tm, D), lambda i: (i, 0))
    lspec = pl.BlockSpec((tm, LANES), lambda i: (i, 0))
    return pl.pallas_call(
        _router_body,
        grid=(t // tm,),
        in_specs=[spec, _full((ADA_CHUNKS, D)), _full((1, D)), _full((D, LANES))],
        out_specs=[spec, lspec, lspec, lspec],
        out_shape=[jax.ShapeDtypeStruct((t, D), F32), jax.ShapeDtypeStruct((t, LANES), jnp.int32),
                   jax.ShapeDtypeStruct((t, LANES), F32), jax.ShapeDtypeStruct((t, LANES), F32)],
        compiler_params=_cparams(("arbitrary",)),
        name="moe_router",
    )(x, mod, g, router_pad)


SCATTER_TOKENS = 256


def _row_scatter_body(nlat, nctx, didx_ref, h_ref, *rest):
    if nctx:
        hc_ref, dst_ref, zero_scr, sem = rest
    else:
        dst_ref, zero_scr, sem = rest
    i = pl.program_id(0)
    ts = SCATTER_TOKENS

    def run(src_ref, src_is_zero_rows):
        def issue(grp, c):
            base = pl.multiple_of(grp * SUBLANES, SUBLANES)
            for r in range(SUBLANES):
                src = src_ref.at[pl.ds(r if src_is_zero_rows else base + r, 1), :]
                for half in range(2):
                    d = didx_ref[0, 0, base + r + half * ts]
                    pltpu.make_async_copy(src, dst_ref.at[pl.ds(d, 1), :], sem).start(priority=half)
            return c
        lax.fori_loop(0, ts // SUBLANES, issue, 0)

        def drain(n, c):
            pltpu.make_async_copy(src_ref.at[pl.ds(0, 1), :], dst_ref.at[pl.ds(0, 1), :], sem).wait()
            return c
        lax.fori_loop(0, 2 * ts, drain, 0, unroll=8)

    @pl.when(i == 0)
    def _():
        zero_scr[...] = jnp.zeros_like(zero_scr)

    @pl.when(i < nlat)
    def _():
        run(h_ref, False)

    if nctx:
        @pl.when(jnp.logical_and(i >= nlat, i < nlat + nctx))
        def _():
            run(hc_ref, False)

    @pl.when(i >= nlat + nctx)
    def _():
        run(zero_scr, True)


def row_scatter(h, hc, dest, pad_pos):
    ts = SCATTER_TOKENS
    nlat = h.shape[0] // ts
    nctx = 0 if hc is None else 1
    assert hc is None or hc.shape[0] == ts
    ntok = nlat + nctx
    npad = pad_pos.shape[0] // (2 * ts)
    didx = jnp.concatenate([dest.reshape(ntok, ts, 2).transpose(0, 2, 1).reshape(ntok, 1, 2 * ts),
                            pad_pos.reshape(npad, 1, 2 * ts)], axis=0)
    in_specs = [pl.BlockSpec((1, 1, 2 * ts), lambda i: (i, 0, 0), memory_space=pltpu.SMEM),
                pl.BlockSpec((ts, D), lambda i: (jnp.minimum(i, nlat - 1), 0))]
    args = [didx, h]
    if nctx:
        in_specs.append(_full((ts, D)))
        args.append(hc)
    return pl.pallas_call(
        functools.partial(_row_scatter_body, nlat, nctx),
        grid=(ntok + npad,),
        in_specs=in_specs,
        out_specs=pl.BlockSpec(memory_space=pl.ANY),
        out_shape=jax.ShapeDtypeStruct((2 * ntok * ts + pad_pos.shape[0], D), h.dtype),
        scratch_shapes=[pltpu.VMEM((SUBLANES, D), h.dtype), pltpu.SemaphoreType.DMA(())],
        compiler_params=_cparams(("arbitrary",)),
        name="moe_row_scatter",
    )(*args)


def _moe_ffn_body(te_ref, tv_ref, xg_ref, wg_ref, wu_ref, wd_ref, o_ref, h_scr, acc_scr):
    i = pl.program_id(0)
    j = pl.program_id(1)

    @pl.when(tv_ref[i] > 0)
    def _():
        @pl.when(j == 0)
        def _():
            h_scr[...] = xg_ref[...].astype(BF16)
            acc_scr[...] = jnp.zeros_like(acc_scr)

        h = h_scr[...]
        gg = jnp.dot(h, wg_ref[0], preferred_element_type=F32)
        uu = jnp.dot(h, wu_ref[0], preferred_element_type=F32)
        a = (gg * _sigmoid(gg)) * uu
        acc_scr[...] += jnp.dot(a.astype(BF16), wd_ref[0], preferred_element_type=F32)

        @pl.when(j == pl.num_programs(1) - 1)
        def _():
            o_ref[...] = acc_scr[...]

    @pl.when(jnp.logical_and(tv_ref[i] == 0, j == pl.num_programs(1) - 1))
    def _():
        o_ref[...] = jnp.zeros_like(o_ref)


def moe_ffn(xg, tile_e, tile_v, w_gu, w_down, li, tm, fc=1792):
    p = xg.shape[0]
    nf = D_FF // fc

    def jeff(i, j, tv):
        return jnp.where(tv[i] > 0, j, nf - 1)

    grid_spec = pltpu.PrefetchScalarGridSpec(
        num_scalar_prefetch=2,
        grid=(p // tm, nf),
        in_specs=[
            pl.BlockSpec((tm, D), lambda i, j, te, tv: (i, 0)),
            pl.BlockSpec((None, 1, D, fc), lambda i, j, te, tv: (li, te[i], 0, jeff(i, j, tv))),
            pl.BlockSpec((None, 1, D, fc), lambda i, j, te, tv: (li, te[i], 0, nf + jeff(i, j, tv))),
            pl.BlockSpec((None, 1, fc, D), lambda i, j, te, tv: (li, te[i], jeff(i, j, tv), 0)),
        ],
        out_specs=pl.BlockSpec((tm, D), lambda i, j, te, tv: (i, 0)),
        scratch_shapes=[pltpu.VMEM((tm, D), BF16), pltpu.VMEM((tm, D), F32)],
    )
    return pl.pallas_call(
        _moe_ffn_body,
        grid_spec=grid_spec,
        out_shape=jax.ShapeDtypeStruct((p, D), F32),
        compiler_params=_cparams(("arbitrary", "arbitrary")),
        name="moe_ffn",
    )(tile_e, tile_v, xg, w_gu, w_gu, w_down)


def _combine_body(tt, d0_ref, d1_ref, yp_ref, x_ref, mod_ref, w0_ref, w1_ref, o_ref, a_scr, b_scr, sem):
    def issue(grp, c):
        base = pl.multiple_of(grp * SUBLANES, SUBLANES)
        for r in range(SUBLANES):
            n = base + r
            pltpu.make_async_copy(yp_ref.at[pl.ds(d0_ref[0, 0, n], 1), :], a_scr.at[pl.ds(n, 1), :],
                                  sem).start(priority=0)
            pltpu.make_async_copy(yp_ref.at[pl.ds(d1_ref[0, 0, n], 1), :], b_scr.at[pl.ds(n, 1), :],
                                  sem).start(priority=1)
        return c
    lax.fori_loop(0, tt // SUBLANES, issue, 0)

    def drain(n, c):
        pltpu.make_async_copy(yp_ref.at[pl.ds(0, 1), :], a_scr.at[pl.ds(0, 1), :], sem).wait()
        pltpu.make_async_copy(yp_ref.at[pl.ds(0, 1), :], b_scr.at[pl.ds(0, 1), :], sem).wait()
        return c
    lax.fori_loop(0, tt, drain, 0, unroll=8)
    w0 = _lane_tile(w0_ref[...])
    w1 = _lane_tile(w1_ref[...])
    o_ref[...] = x_ref[...] + mod_ref[5:6, :] * (w0 * a_scr[...] + w1 * b_scr[...])


def moe_combine(yp, d0, d1, x, mod, w0b, w1b, tt):
    t = x.shape[0]
    nt = t // tt
    ispec = pl.BlockSpec((1, 1, tt), lambda i: (i, 0, 0), memory_space=pltpu.SMEM)
    spec = pl.BlockSpec((tt, D), lambda i: (i, 0))
    lspec = pl.BlockSpec((tt, LANES), lambda i: (i, 0))
    return pl.pallas_call(
        functools.partial(_combine_body, tt),
        grid=(nt,),
        in_specs=[ispec, ispec, pl.BlockSpec(memory_space=pl.ANY), spec, _full((ADA_CHUNKS, D)), lspec, lspec],
        out_specs=spec,
        out_shape=jax.ShapeDtypeStruct((t, D), F32),
        scratch_shapes=[pltpu.VMEM((tt, D), F32), pltpu.VMEM((tt, D), F32), pltpu.SemaphoreType.DMA(())],
        compiler_params=_cparams(("arbitrary",)),
        name="moe_combine",
    )(d0.reshape(nt, 1, tt), d1.reshape(nt, 1, tt), yp, x, mod, w0b, w1b)


def _route_plan(idx, tm):
    t = idx.shape[0]
    n = 2 * t
    e_flat = idx.reshape(n)
    ex = jnp.arange(N_EXPERTS, dtype=jnp.int32)
    onehot = (e_flat[:, None] == ex[None, :]).astype(jnp.int32)
    csum = jnp.cumsum(onehot, axis=0)
    rank = jnp.sum((csum - onehot) * onehot, axis=1)
    counts = csum[-1]
    padded = ((counts + tm - 1) // tm) * tm
    pad_end = jnp.cumsum(padded)
    pad_off = pad_end - padded
    total = pad_end[-1]
    dest = jnp.sum(onehot * pad_off[None, :], axis=1) + rank
    gap = padded - counts
    tail_off = jnp.cumsum(tm - gap) - (tm - gap)
    r = jnp.arange(tm, dtype=jnp.int32)[None, :]
    pad_pos = jnp.where(r < gap[:, None], (pad_off + counts)[:, None] + r,
                        total + tail_off[:, None] + (r - gap[:, None])).reshape(-1)
    ntiles = (n + N_EXPERTS * tm) // tm
    tstart = jnp.arange(ntiles, dtype=jnp.int32) * tm
    tile_v = (tstart < total).astype(jnp.int32)
    tile_e = jnp.sum((jnp.minimum(tstart, total - 1)[:, None] >= pad_end[None, :]).astype(jnp.int32), axis=1)
    return dest.reshape(t, 2).astype(jnp.int32), pad_pos.astype(jnp.int32), tile_e.astype(jnp.int32), tile_v


def moe_layer(x, xc, mod, modc, g, router, w_gu_b, w_down_b, li, need_ctx, tm=512):
    router_pad = jnp.pad(router, ((0, 0), (0, LANES - N_EXPERTS)))
    s = x.shape[0]
    h, info, w0b, w1b = moe_router(x, mod, g, router_pad, 1024)
    if need_ctx:
        sc = xc.shape[0]
        hc, infoc, w0c, w1c = moe_router(xc, modc, g, router_pad, sc)
        top2 = jnp.concatenate([info[:, :2], infoc[:, :2]], axis=0)
    else:
        hc, top2 = None, info[:, :2]
    dest, pad_pos, tile_e, tile_v = _route_plan(top2, tm)
    xg = row_scatter(h, hc, dest, pad_pos)
    yp = moe_ffn(xg, tile_e, tile_v, w_gu_b, w_down_b, li, tm)
    x_new = moe_combine(yp, dest[:s, 0], dest[:s, 1], x, mod, w0b, w1b, 512)
    xc_new = None
    if need_ctx:
        xc_new = moe_combine(yp, dest[s:, 0], dest[s:, 1], xc, modc, w0c, w1c, sc)
    return x_new, xc_new


def kernel(x, c, ctx, c_ctx, ada_w, ada_b, norm_g, rg_w_in, rg_conv_w, rg_conv_b, rg_wa, rg_ba, rg_wi, rg_bi,
           rg_lambda, rg_w_out, na_w_qkv, na_q_g, na_k_g, na_rpb, na_w_o, ft_w_out, ffn_w_gu, ffn_w_down,
           moe_router, moe_w_gu, moe_w_down):
    depth = ada_w.shape[0]
    assert x.shape[0] == 1 and x.shape[2] == D
    xs = x[0]
    xc = ctx[0]
    mods = ada_modulation(c, c_ctx, ada_w, ada_b)
    ffn_gu_b, ffn_dn_b = ffn_w_gu.astype(BF16), ffn_w_down.astype(BF16)
    moe_gu_b, moe_dn_b = moe_w_gu.astype(BF16), moe_w_down.astype(BF16)
    mix_idx = [0] * N_MIXERS
    dense_idx = 0
    moe_idx = 0
    for layer in range(depth):
        need_ctx = layer != depth - 1
        mod, modc = mods[layer, 0], mods[layer, 1]
        g0 = norm_g[layer, 0][None]
        g1 = norm_g[layer, 1][None]
        kind = layer % N_MIXERS
        j = mix_idx[kind]
        mix_idx[kind] += 1
        if kind == 0:
            xs, xcn = rglru_layer(xs, xc, mod, modc, g0, rg_w_in[j], rg_conv_w[j], rg_conv_b[j], rg_wa[j], rg_wi[j],
                                  rg_ba[j], rg_bi[j], rg_lambda[j], rg_w_out[j], need_ctx)
        elif kind == 1:
            xs, xcn = na_layer(xs, xc, mod, modc, g0, na_w_qkv[j], na_q_g[j], na_k_g[j], na_rpb[j], na_w_o[j],
                               need_ctx)
        else:
            xs, xcn = fourier_layer(xs, xc, mod, modc, g0, ft_w_out[j], need_ctx)
        if need_ctx:
            xc = xcn
        if layer % 2 == 0:
            if need_ctx:
                xc = ffn_dense(xc, modc, g1, ffn_gu_b, ffn_dn_b, dense_idx, xc.shape[0])
            xs = ffn_dense(xs, mod, g1, ffn_gu_b, ffn_dn_b, dense_idx, 512)
            dense_idx += 1
        else:
            xs, xcn = moe_layer(xs, xc, mod, modc, g1, moe_router[moe_idx], moe_gu_b, moe_dn_b, moe_idx, need_ctx)
            moe_idx += 1
            if need_ctx:
                xc = xcn
    return xs[None]
```

```python
import functools
import math

import numpy as np
import jax
import jax.numpy as jnp
from jax import lax
from jax.experimental import pallas as pl
from jax.experimental.pallas import tpu as pltpu

F32 = jnp.float32
BF16 = jnp.bfloat16

D = 1024
D_FF = 3584
N_EXPERTS = 8
GRID_W = 64
NA_HEADS = 16
NA_HEAD_DIM = 64
NA_ROWS = 8
NA_COLS = 16
FT_GROUP_W = 256
RG_BLOCK_W = 256
RMS_EPS = 1e-6
LRU_C = 8.0
N_MIXERS = 3
ADA_CHUNKS = 6

LANES = 128
SUBLANES = 8
VMEM_LIMIT = 56 * 1024 * 1024
NEG_BIG = -1e30
LOG2E = math.log2(math.e)


def _cparams(sem):
    return pltpu.CompilerParams(dimension_semantics=sem, vmem_limit_bytes=VMEM_LIMIT)


def _full(shape):
    nd = len(shape)
    return pl.BlockSpec(shape, lambda *_: (0,) * nd)


def _normmod(x, g, scale, shift):
    ms = jnp.mean(x * x, axis=-1, keepdims=True)
    y = x * lax.rsqrt(ms + RMS_EPS)
    return (y * g) * (1.0 + scale) + shift


def _lane_tile(v):
    return jnp.concatenate([v] * (D // LANES), axis=1)


def _sigmoid(v):
    return 1.0 / (1.0 + jnp.exp(-v))


def _gelu_tanh(v):
    c = math.sqrt(2.0 / math.pi)
    return v * (0.5 * (1.0 + jnp.tanh(c * (v + 0.044715 * (v * v * v)))))


def _ada_body(cin_ref, w_ref, b_ref, o_ref):
    v = cin_ref[...]
    s = v * _sigmoid(v)
    w = w_ref[0]
    r0 = jnp.sum(s[:, 0:1] * w, axis=0, keepdims=True)
    r1 = jnp.sum(s[:, 1:2] * w, axis=0, keepdims=True)
    o_ref[0] = jnp.concatenate([r0, r1], axis=0) + b_ref[0]


def ada_modulation(c, c_ctx, ada_w, ada_b):
    depth = ada_w.shape[0]
    n = ada_w.shape[2]
    nc = n // 4
    cin = jnp.stack([c[0], c_ctx], axis=1)
    out = pl.pallas_call(
        _ada_body,
        grid=(depth, n // nc),
        in_specs=[
            pl.BlockSpec((D, 2), lambda l, j: (0, 0)),
            pl.BlockSpec((1, D, nc), lambda l, j: (l, 0, j)),
            pl.BlockSpec((1, 1, nc), lambda l, j: (l, 0, j)),
        ],
        out_specs=pl.BlockSpec((1, 2, nc), lambda l, j: (l, 0, j)),
        out_shape=jax.ShapeDtypeStruct((depth, 2, n), F32),
        compiler_params=_cparams(("arbitrary", "arbitrary")),
        name="ada_mod",
    )(cin, ada_w, ada_b.reshape(depth, 1, n))
    return out.reshape(depth, 2, ADA_CHUNKS, D)


def _ffn_body(x_ref, mod_ref, g_ref, wg_ref, wu_ref, wd_ref, o_ref, h_scr, acc_scr):
    j = pl.program_id(1)

    @pl.when(j == 0)
    def _():
        h = _normmod(x_ref[...], g_ref[...], mod_ref[4:5, :], mod_ref[3:4, :])
        h_scr[...] = h.astype(BF16)
        acc_scr[...] = jnp.zeros_like(acc_scr)

    h = h_scr[...]
    gg = jnp.dot(h, wg_ref[...], preferred_element_type=F32)
    uu = jnp.dot(h, wu_ref[...], preferred_element_type=F32)
    a = (gg * _sigmoid(gg)) * uu
    acc_scr[...] += jnp.dot(a.astype(BF16), wd_ref[...], preferred_element_type=F32)

    @pl.when(j == pl.num_programs(1) - 1)
    def _():
        o_ref[...] = x_ref[...] + mod_ref[5:6, :] * acc_scr[...]


def ffn_dense(x, mod, g, w_gu, w_down, li, tm, fc=1792):
    t = x.shape[0]
    nf = D_FF // fc
    return pl.pallas_call(
        _ffn_body,
        grid=(t // tm, nf),
        in_specs=[
            pl.BlockSpec((tm, D), lambda i, j: (i, 0)),
            _full((ADA_CHUNKS, D)),
            _full((1, D)),
            pl.BlockSpec((None, D, fc), lambda i, j: (li, 0, j)),
            pl.BlockSpec((None, D, fc), lambda i, j: (li, 0, nf + j)),
            pl.BlockSpec((None, fc, D), lambda i, j: (li, j, 0)),
        ],
        out_specs=pl.BlockSpec((tm, D), lambda i, j: (i, 0)),
        out_shape=jax.ShapeDtypeStruct((t, D), F32),
        scratch_shapes=[pltpu.VMEM((tm, D), BF16), pltpu.VMEM((tm, D), F32)],
        compiler_params=_cparams(("arbitrary", "arbitrary")),
        name="ffn_dense",
    )(x, mod, g, w_gu, w_gu, w_down)


def _proj_body(gate_row, a_ref, w_ref, x_ref, mod_ref, o_ref):
    y = jnp.dot(a_ref[...], w_ref[...], preferred_element_type=F32)
    o_ref[...] = x_ref[...] + mod_ref[gate_row:gate_row + 1, :] * y


def proj_residual(a, w, x, mod, gate_row, tm):
    t, k = a.shape
    return pl.pallas_call(
        functools.partial(_proj_body, gate_row),
        grid=(t // tm,),
        in_specs=[
            pl.BlockSpec((tm, k), lambda i: (i, 0)),
            _full((k, D)),
            pl.BlockSpec((tm, D), lambda i: (i, 0)),
            _full((ADA_CHUNKS, D)),
        ],
        out_specs=pl.BlockSpec((tm, D), lambda i: (i, 0)),
        out_shape=jax.ShapeDtypeStruct((t, D), F32),
        compiler_params=_cparams(("arbitrary",)),
        name="proj_residual",
    )(a, w, x, mod)


HALO = SUBLANES


def _rg_in_body(tm, xp_ref, x_ref, xn_ref, mod_ref, g_ref, w_ref, cw_ref, cb_ref, xc_ref, gg_ref):
    i = pl.program_id(0)
    last = pl.num_programs(0) - 1
    xa = jnp.concatenate([xp_ref[...], x_ref[...], xn_ref[...]], axis=0)
    h = _normmod(xa, g_ref[...], mod_ref[1:2, :], mod_ref[0:1, :]).astype(BF16)
    z = jnp.dot(h, w_ref[...], preferred_element_type=F32)
    row = lax.broadcasted_iota(jnp.int32, (tm + 2 * HALO, 1), 0)
    valid = jnp.logical_and(jnp.logical_or(row >= HALO, i > 0),
                            jnp.logical_or(row < tm + HALO, i < last))
    xz = jnp.where(valid, z[:, :D], 0.0)
    y = cb_ref[...] + cw_ref[2:3, :] * xz[HALO:HALO + tm]
    y = y + cw_ref[0:1, :] * xz[HALO - 2:HALO - 2 + tm]
    y = y + cw_ref[1:2, :] * xz[HALO - 1:HALO - 1 + tm]
    y = y + cw_ref[3:4, :] * xz[HALO + 1:HALO + 1 + tm]
    xc_ref[...] = y
    gg_ref[...] = _gelu_tanh(z[HALO:HALO + tm, D:]).astype(BF16)


def rg_in(x, mod, g, w_in, conv_w, conv_b, tm):
    t = x.shape[0]
    nb = tm // HALO
    nblk = t // HALO
    return pl.pallas_call(
        functools.partial(_rg_in_body, tm),
        grid=(t // tm,),
        in_specs=[
            pl.BlockSpec((HALO, D), lambda i: (jnp.maximum(i * nb - 1, 0), 0)),
            pl.BlockSpec((tm, D), lambda i: (i, 0)),
            pl.BlockSpec((HALO, D), lambda i: (jnp.minimum((i + 1) * nb, nblk - 1), 0)),
            _full((ADA_CHUNKS, D)),
            _full((1, D)),
            _full((D, 2 * D)),
            _full((4, D)),
            _full((1, D)),
        ],
        out_specs=[pl.BlockSpec((tm, D), lambda i: (i, 0)), pl.BlockSpec((tm, D), lambda i: (i, 0))],
        out_shape=[jax.ShapeDtypeStruct((t, D), F32), jax.ShapeDtypeStruct((t, D), BF16)],
        compiler_params=_cparams(("arbitrary",)),
        name="rg_in",
    )(x, x, x, mod, g, w_in, conv_w, conv_b)


def _rg_gates(xc, wa_ref, wi_ref, ba, bi, lam):
    xb = xc.astype(BF16)
    nblk = D // RG_BLOCK_W
    r = jnp.concatenate([jnp.dot(xb[:, n * RG_BLOCK_W:(n + 1) * RG_BLOCK_W], wa_ref[n],
                                 preferred_element_type=F32) for n in range(nblk)], axis=1)
    ig = jnp.concatenate([jnp.dot(xb[:, n * RG_BLOCK_W:(n + 1) * RG_BLOCK_W], wi_ref[n],
                                  preferred_element_type=F32) for n in range(nblk)], axis=1)
    r = 0.5 + 0.5 * jnp.tanh(0.5 * (r + ba))
    ig = 0.5 + 0.5 * jnp.tanh(0.5 * (ig + bi))
    nl = -lam
    softplus = jnp.maximum(nl, 0.0) + jnp.log1p(jnp.exp(-jnp.abs(nl)))
    log_a = (-LRU_C * r) * softplus
    a = jnp.exp(log_a)
    b = jnp.sqrt(1.0 - a * a) * (ig * xc)
    return a, b


def _rg_scan_body(reverse, epilogue, emit_h, tc, *refs):
    xc_ref, wa_ref, wi_ref, ba_ref, bi_ref, lam_ref, h0_ref = refs[:7]
    refs = refs[7:]
    if epilogue:
        hf_ref, gg_ref, wo_ref, x_ref, mod_ref = refs[:5]
        refs = refs[5:]
    if emit_h:
        h_ref = refs[0]
        refs = refs[1:]
    if epilogue:
        o_ref = refs[0]
        refs = refs[1:]
    a_scr, b_scr, h_scr, carry_scr = refs
    c = pl.program_id(0)

    @pl.when(c == 0)
    def _():
        carry_scr[...] = jnp.broadcast_to(h0_ref[...], (SUBLANES, D))

    a, b = _rg_gates(xc_ref[...], wa_ref, wi_ref, ba_ref[...], bi_ref[...], lam_ref[...])
    a_scr[...] = a
    b_scr[...] = b
    nblk = tc // SUBLANES
    row = lax.broadcasted_iota(jnp.int32, (SUBLANES, D), 0)
    first = (row == SUBLANES - 1) if reverse else (row == 0)

    def block(n, carry):
        blk = (nblk - 1 - n) if reverse else n
        off = pl.multiple_of(blk * SUBLANES, SUBLANES)
        av = a_scr[pl.ds(off, SUBLANES), :]
        bv = b_scr[pl.ds(off, SUBLANES), :]
        bv = jnp.where(first, av * carry + bv, bv)
        av = jnp.where(first, 0.0, av)
        for k in (1, 2, 4):
            shift = (SUBLANES - k) if reverse else k
            bv = av * pltpu.roll(bv, shift, 0) + bv
            if k != 4:
                av = av * pltpu.roll(av, shift, 0)
        h_scr[pl.ds(off, SUBLANES), :] = bv
        edge = bv[0:1, :] if reverse else bv[SUBLANES - 1:SUBLANES, :]
        return jnp.broadcast_to(edge, (SUBLANES, D))

    carry_scr[...] = lax.fori_loop(0, nblk, block, carry_scr[...], unroll=2)

    if emit_h:
        h_ref[...] = h_scr[...].astype(h_ref.dtype)
    if epilogue:
        y = ((hf_ref[...].astype(F32) + h_scr[...]) * gg_ref[...].astype(F32)).astype(BF16)
        o_ref[...] = x_ref[...] + mod_ref[2:3, :] * jnp.dot(y, wo_ref[...], preferred_element_type=F32)


def rg_scan(xconv, wa, wi, ba, bi, lam, h0, tc, reverse, epi=None, h_dtype=F32):
    t = xconv.shape[0]
    nchunks = t // tc
    idx = (lambda c: (nchunks - 1 - c, 0)) if reverse else (lambda c: (c, 0))
    nb = D // RG_BLOCK_W
    blk = pl.BlockSpec((tc, D), idx)
    in_specs = [
        blk,
        _full((nb, RG_BLOCK_W, RG_BLOCK_W)),
        _full((nb, RG_BLOCK_W, RG_BLOCK_W)),
        _full((1, D)), _full((1, D)), _full((1, D)), _full((1, D)),
    ]
    args = [xconv, wa, wi, ba, bi, lam, h0]
    out_specs = []
    out_shape = []
    if epi is not None:
        hf, gg, w_out, x, mod = epi
        in_specs += [blk, blk, _full((D, D)), blk, _full((ADA_CHUNKS, D))]
        args += [hf, gg, w_out, x, mod]
    if h_dtype is not None:
        out_specs.append(blk)
        out_shape.append(jax.ShapeDtypeStruct((t, D), h_dtype))
    if epi is not None:
        out_specs.append(blk)
        out_shape.append(jax.ShapeDtypeStruct((t, D), F32))
    return pl.pallas_call(
        functools.partial(_rg_scan_body, reverse, epi is not None, h_dtype is not None, tc),
        grid=(nchunks,),
        in_specs=in_specs,
        out_specs=out_specs,
        out_shape=out_shape,
        scratch_shapes=[pltpu.VMEM((tc, D), F32), pltpu.VMEM((tc, D), F32), pltpu.VMEM((tc, D), F32),
                        pltpu.VMEM((SUBLANES, D), F32)],
        compiler_params=_cparams(("arbitrary",)),
        name="rg_scan_bwd" if reverse else "rg_scan_fwd",
    )(*args)


def rglru_layer(x, xc, mod, modc, g, w_in, conv_w, conv_b, wa, wi, ba, bi, lam, w_out, need_ctx):
    w_in_b = w_in.astype(BF16)
    wa_b = wa.astype(BF16)
    wi_b = wi.astype(BF16)
    w_out_b = w_out.astype(BF16)
    cb = conv_b[None]
    tcx = xc.shape[0]
    xcl, ggl = rg_in(x, mod, g, w_in_b, conv_w, cb, 512)
    xcc, ggc = rg_in(xc, modc, g, w_in_b, conv_w, cb, tcx)
    zeros = jnp.zeros((1, D), F32)
    p = lambda d: (wa_b[d], wi_b[d], ba[d][None], bi[d][None], lam[d][None])
    (hcf,) = rg_scan(xcc, *p(0), zeros, tcx, False)
    (hlf,) = rg_scan(xcl, *p(0), hcf[tcx - 1:tcx], 512, False, h_dtype=BF16)
    if need_ctx:
        hcb, xc_new = rg_scan(xcc, *p(1), zeros, tcx, True, epi=(hcf, ggc, w_out_b, xc, modc))
    else:
        (hcb,) = rg_scan(xcc, *p(1), zeros, tcx, True)
        xc_new = None
    (x_new,) = rg_scan(xcl, *p(1), hcb[0:1], 512, True, epi=(hlf, ggl, w_out_b, x, mod), h_dtype=None)
    return x_new, xc_new


def _qkv_body(x_ref, mod_ref, g_ref, w_ref, gm_ref, qg_ref, kg_ref, q_ref, k_ref, v_ref):
    h = _normmod(x_ref[...], g_ref[...], mod_ref[1:2, :], mod_ref[0:1, :]).astype(BF16)
    z = jnp.dot(h, w_ref[...], preferred_element_type=F32)

    def headnorm(v, gain):
        ms = jnp.dot((v * v).astype(BF16), gm_ref[...], preferred_element_type=F32)
        return (v * lax.rsqrt(ms + RMS_EPS)) * gain

    q_ref[...] = headnorm(z[:, :D], qg_ref[...]).astype(BF16)
    k_ref[...] = headnorm(z[:, D:2 * D], kg_ref[...]).astype(BF16)
    v_ref[...] = z[:, 2 * D:].astype(BF16)


def qkv_proj(x, mod, g, w_qkv, gmean, qg, kg, tm):
    t = x.shape[0]
    spec = pl.BlockSpec((tm, D), lambda i: (i, 0))
    return pl.pallas_call(
        _qkv_body,
        grid=(t // tm,),
        in_specs=[spec, _full((ADA_CHUNKS, D)), _full((1, D)), _full((D, 3 * D)), _full((D, D)),
                  _full((1, D)), _full((1, D))],
        out_specs=[spec, spec, spec],
        out_shape=[jax.ShapeDtypeStruct((t, D), BF16)] * 3,
        compiler_params=_cparams(("arbitrary",)),
        name="qkv_proj",
    )(x, mod, g, w_qkv, gmean, qg, kg)


def _attend_pair(q2, keys, vals, biases):
    m_rows = q2.shape[0]
    lane = lax.broadcasted_iota(jnp.int32, q2.shape, 1)
    zero = jnp.zeros_like(q2)
    qs = jnp.concatenate([jnp.where(lane < NA_HEAD_DIM, q2, zero), jnp.where(lane >= NA_HEAD_DIM, q2, zero)], axis=0)
    ss = []
    for kseg, bseg in zip(keys, biases):
        s = lax.dot_general(qs, kseg, (((1,), (1,)), ((), ())), preferred_element_type=F32)
        if bseg is not None:
            s = s + jnp.concatenate([bseg[0], bseg[1]], axis=0)
        ss.append(s)
    m = ss[0].max(axis=-1, keepdims=True)
    for s in ss[1:]:
        m = jnp.maximum(m, s.max(axis=-1, keepdims=True))
    den = None
    acc = None
    for s, vseg in zip(ss, vals):
        p = jnp.exp2(s - m)
        d = jnp.sum(p, axis=-1, keepdims=True)
        o = jnp.dot(p.astype(BF16), vseg, preferred_element_type=F32)
        den = d if den is None else den + d
        acc = o if acc is None else acc + o
    out = acc / den
    return jnp.where(lane < NA_HEAD_DIM, out[:m_rows], out[m_rows:])


NA_QROWS = 2
NA_UNION = NA_ROWS + NA_QROWS - 1


def _na_body(var_ref, q_ref, kl_ref, vl_ref, kc_ref, vc_ref, bias_ref, o_ref):
    for pr in range(NA_HEADS // 2):
        sl = slice(pr * LANES, (pr + 1) * LANES)
        o_ref[:, sl] = _attend_pair(
            q_ref[:, sl], [kl_ref[:, sl], kc_ref[:, sl]], [vl_ref[:, sl], vc_ref[:, sl]],
            [(bias_ref[0, 2 * pr], bias_ref[0, 2 * pr + 1]), None]).astype(BF16)


def _na_geometry(rows):
    steps = rows // NA_QROWS
    g = np.arange(steps)
    base = np.clip(NA_QROWS * g - NA_ROWS // 2, 0, rows - NA_UNION)
    r = NA_QROWS * g[:, None] + np.arange(NA_QROWS)[None, :]
    rs = np.clip(r - NA_ROWS // 2, 0, rows - NA_ROWS)
    key = np.concatenate([(base - NA_QROWS * g)[:, None], rs - r], axis=1)
    uniq, first, var = np.unique(key, axis=0, return_index=True, return_inverse=True)
    return base, var.reshape(-1).astype(np.int32), g[first]


def na_attention(q, k, v, kc, vc, bias_tab, var):
    t = q.shape[0]
    rows = t // GRID_W
    nctx = kc.shape[0]
    steps = rows // NA_QROWS

    def kbase(g):
        return jnp.clip(NA_QROWS * g - NA_ROWS // 2, 0, rows - NA_UNION)

    qrows = NA_QROWS * GRID_W
    nloc = NA_UNION * GRID_W
    kspec = pl.BlockSpec((pl.Element(nloc), pl.Element(D)), lambda g, var: (kbase(g) * GRID_W, 0))
    grid_spec = pltpu.PrefetchScalarGridSpec(
        num_scalar_prefetch=1,
        grid=(steps,),
        in_specs=[pl.BlockSpec((qrows, D), lambda g, var: (g, 0)), kspec, kspec] + [
            pl.BlockSpec((nctx, D), lambda g, var: (0, 0)), pl.BlockSpec((nctx, D), lambda g, var: (0, 0)),
            pl.BlockSpec((1, NA_HEADS, qrows, nloc), lambda g, var: (var[g], 0, 0, 0)),
        ],
        out_specs=pl.BlockSpec((qrows, D), lambda g, var: (g, 0)),
    )
    return pl.pallas_call(
        _na_body,
        grid_spec=grid_spec,
        out_shape=jax.ShapeDtypeStruct((t, D), BF16),
        compiler_params=_cparams(("arbitrary",)),
        name="na_attention",
    )(var, q, k, v, kc, vc, bias_tab)


def _ctx_attn_body(q_ref, k_ref, v_ref, o_ref):
    for pr in range(NA_HEADS // 2):
        sl = slice(pr * LANES, (pr + 1) * LANES)
        o_ref[:, sl] = _attend_pair(q_ref[:, sl], [k_ref[:, sl]], [v_ref[:, sl]], [None]).astype(BF16)


def ctx_attention(q, k, v):
    t = q.shape[0]
    return pl.pallas_call(
        _ctx_attn_body,
        grid=(1,),
        in_specs=[_full((t, D))] * 3,
        out_specs=_full((t, D)),
        out_shape=jax.ShapeDtypeStruct((t, D), BF16),
        compiler_params=_cparams(("arbitrary",)),
        name="ctx_attention",
    )(q, k, v)


def _na_bias_table(rpb, rows):
    base, var, reps = _na_geometry(rows)
    cols = np.arange(GRID_W)
    cstart = np.clip(cols - NA_COLS // 2, 0, GRID_W - NA_COLS)
    kcol = np.arange(GRID_W)
    inwin = (kcol[None, :] >= cstart[:, None]) & (kcol[None, :] < cstart[:, None] + NA_COLS)
    r = NA_QROWS * reps[:, None] + np.arange(NA_QROWS)[None, :]
    rs = np.clip(r - NA_ROWS // 2, 0, rows - NA_ROWS)
    krow = base[reps][:, None] + np.arange(NA_UNION)[None, :]
    rvalid = (krow[:, None, :] >= rs[:, :, None]) & (krow[:, None, :] < rs[:, :, None] + NA_ROWS)
    ridx = np.clip(krow[:, None, :] - r[:, :, None] + (NA_ROWS - 1), 0, 2 * NA_ROWS - 2)
    nd = 2 * NA_COLS - 1
    w = jnp.pad(rpb.astype(F32), ((0, 0), (0, 0), (GRID_W - NA_COLS, 2 * GRID_W - (GRID_W - NA_COLS) - nd)))
    flat = jnp.tile(w, (1, 1, GRID_W))[:, :, :GRID_W * (2 * GRID_W - 1)]
    blk = flat.reshape(NA_HEADS, 2 * NA_ROWS - 1, GRID_W, 2 * GRID_W - 1)[..., GRID_W - 1:]
    blk = jnp.where(jnp.asarray(inwin)[None, None], blk, NEG_BIG)
    neg = jnp.full((NA_HEADS, GRID_W, GRID_W), NEG_BIG, F32)
    variants = []
    for v in range(len(reps)):
        strips = [jnp.concatenate([blk[:, ridx[v, a, j]] if rvalid[v, a, j] else neg for j in range(NA_UNION)], axis=2)
                  for a in range(NA_QROWS)]
        variants.append(jnp.concatenate(strips, axis=1))
    return jnp.stack(variants, axis=0), jnp.asarray(var)


def na_layer(x, xc, mod, modc, g, w_qkv, q_g, k_g, rpb, w_o, need_ctx):
    w_qkv_b = w_qkv.astype(BF16)
    w_o_b = w_o.astype(BF16)
    gm = np.kron(np.eye(NA_HEADS), np.full((NA_HEAD_DIM, NA_HEAD_DIM), 1.0 / NA_HEAD_DIM))
    gmean = jnp.asarray(gm, dtype=BF16)
    qg = jnp.tile(q_g, NA_HEADS)[None] * (NA_HEAD_DIM ** -0.5 * LOG2E)
    kg = jnp.tile(k_g, NA_HEADS)[None]
    q, k, v = qkv_proj(x, mod, g, w_qkv_b, gmean, qg, kg, 512)
    qc, kc, vc = qkv_proj(xc, modc, g, w_qkv_b, gmean, qg, kg, xc.shape[0])
    bias_tab, var = _na_bias_table(rpb * LOG2E, x.shape[0] // GRID_W)
    o = na_attention(q, k, v, kc, vc, bias_tab, var)
    x_new = proj_residual(o, w_o_b, x, mod, 2, 1024)
    xc_new = None
    if need_ctx:
        oc = ctx_attention(qc, kc, vc)
        xc_new = proj_residual(oc, w_o_b, xc, modc, 2, xc.shape[0])
    return x_new, xc_new


def _dft_mats(n):
    ang = 2.0 * np.pi * np.outer(np.arange(n), np.arange(n)) / n
    return np.cos(ang), np.sin(ang)


def _channel_dft(h, wc):
    us = [jnp.dot(h[:, gi * FT_GROUP_W:(gi + 1) * FT_GROUP_W], wc, preferred_element_type=F32).astype(BF16)
          for gi in range(D // FT_GROUP_W)]
    return jnp.concatenate([u[:, :FT_GROUP_W] for u in us] + [u[:, FT_GROUP_W:] for u in us], axis=1)


def _ft_a_body(n, nj, x_ref, mod_ref, g_ref, perm_ref, wc_ref, ma_ref, tc_ref, ts_ref, yr_ref, yi_ref):
    h3 = _normmod(x_ref[...], g_ref[...], mod_ref[1:2, :], mod_ref[0:1, :])
    h = jnp.dot(perm_ref[...], h3.reshape(n * nj, D).astype(BF16), preferred_element_type=F32).astype(BF16)
    u = _channel_dft(h, wc_ref[...])
    for j in range(nj):
        uj = u[j * n:(j + 1) * n]
        y = jnp.dot(ma_ref[...], jnp.concatenate([uj[:, :D], uj[:, D:]], axis=0), preferred_element_type=F32)
        yr, yi = y[:n], y[n:]
        tc = _lane_tile(tc_ref[0, :, j * LANES:(j + 1) * LANES])
        ts = _lane_tile(ts_ref[0, :, j * LANES:(j + 1) * LANES])
        yr_ref[:, j, :] = yr * tc + yi * ts
        yi_ref[:, j, :] = yi * tc - yr * ts


def _ft_c_body(n, nj, yr_ref, yi_ref, mc_ref, wf_ref, x_ref, mod_ref, o_ref):
    fs = []
    for j in range(nj):
        ys = jnp.concatenate([yr_ref[j].astype(BF16), yi_ref[j].astype(BF16)], axis=0)
        fs.append(jnp.dot(mc_ref[...], ys, preferred_element_type=F32).astype(BF16))
    z = jnp.dot(jnp.concatenate(fs, axis=0), wf_ref[...], preferred_element_type=F32)
    gate = mod_ref[2:3, :]
    for j in range(nj):
        o_ref[:, j, :] = x_ref[:, j, :] + gate * z[j * n:(j + 1) * n]


def _ft_ctx_body(x_ref, mod_ref, g_ref, wc_ref, ml_ref, wf_ref, o_ref):
    x = x_ref[...]
    h = _normmod(x, g_ref[...], mod_ref[1:2, :], mod_ref[0:1, :]).astype(BF16)
    u = _channel_dft(h, wc_ref[...])
    us = jnp.concatenate([u[:, :D], u[:, D:]], axis=0)
    f = jnp.dot(ml_ref[...], us, preferred_element_type=F32).astype(BF16)
    o_ref[...] = x + mod_ref[2:3, :] * jnp.dot(f, wf_ref[...], preferred_element_type=F32)


def fourier_layer(x, xc, mod, modc, g, w_f, need_ctx):
    t = x.shape[0]
    n = math.isqrt(t)
    assert n * n == t and n % 16 == 0
    w_f_b = w_f.astype(BF16)
    cw, sw = _dft_mats(FT_GROUP_W)
    wc = jnp.asarray(np.concatenate([cw, -sw], axis=1) / math.sqrt(FT_GROUP_W), dtype=F32).astype(BF16)
    wcspec = _full((FT_GROUP_W, 2 * FT_GROUP_W))
    cn, sn = _dft_mats(n)
    ma = jnp.asarray(np.block([[cn, sn], [-sn, cn]]) / math.sqrt(n), dtype=F32).astype(BF16)
    mc = jnp.asarray(np.concatenate([cn, sn], axis=1) / math.sqrt(n), dtype=F32).astype(BF16)
    nj = 8
    ang = 2.0 * np.pi * np.outer(np.arange(n), np.arange(n)) / t
    def expand(tab):
        a = jnp.asarray(tab, dtype=F32).reshape(n // nj, nj, n).transpose(0, 2, 1)
        return jnp.repeat(a, LANES, axis=2)
    twc, tws = expand(np.cos(ang)), expand(np.sin(ang))
    xblk = pl.BlockSpec((n, nj, D), lambda b: (0, b, 0))
    yblk = pl.BlockSpec((nj, n, D), lambda b: (b, 0, 0))
    tblk = pl.BlockSpec((1, n, nj * LANES), lambda b: (b, 0, 0))
    x3 = x.reshape(n, n, D)
    src = (np.arange(n)[None, :] * nj + np.arange(nj)[:, None]).reshape(-1)
    perm = jnp.asarray(np.eye(n * nj)[src], dtype=BF16)
    yr, yi = pl.pallas_call(
        functools.partial(_ft_a_body, n, nj),
        grid=(n // nj,),
        in_specs=[xblk, _full((ADA_CHUNKS, D)), _full((1, D)), _full((n * nj, n * nj)), wcspec,
                  _full((2 * n, 2 * n)), tblk, tblk],
        out_specs=[xblk, xblk],
        out_shape=[jax.ShapeDtypeStruct((n, n, D), F32)] * 2,
        compiler_params=_cparams(("arbitrary",)),
        name="ft_stage_a",
    )(x3, mod, g, perm, wc, ma, twc, tws)
    x_new = pl.pallas_call(
        functools.partial(_ft_c_body, n, nj),
        grid=(n // nj,),
        in_specs=[yblk, yblk, _full((n, 2 * n)), _full((D, D)), xblk, _full((ADA_CHUNKS, D))],
        out_specs=xblk,
        out_shape=jax.ShapeDtypeStruct((n, n, D), F32),
        compiler_params=_cparams(("arbitrary",)),
        name="ft_stage_c",
    )(yr, yi, mc, w_f_b, x3, mod).reshape(t, D)
    xc_new = None
    if need_ctx:
        lc = xc.shape[0]
        cl, sl = _dft_mats(lc)
        ml = jnp.asarray(np.concatenate([cl, sl], axis=1) / math.sqrt(lc), dtype=F32).astype(BF16)
        xc_new = pl.pallas_call(
            _ft_ctx_body,
            grid=(1,),
            in_specs=[_full((lc, D)), _full((ADA_CHUNKS, D)), _full((1, D)), wcspec,
                      _full((lc, 2 * lc)), _full((D, D))],
            out_specs=_full((lc, D)),
            out_shape=jax.ShapeDtypeStruct((lc, D), F32),
            compiler_params=_cparams(("arbitrary",)),
            name="ft_ctx",
        )(xc, modc, g, wc, ml, w_f_b)
    return x_new, xc_new


def _router_body(x_ref, mod_ref, g_ref, r_ref, info_ref, w0_ref, w1_ref):
    h = _normmod(x_ref[...], g_ref[...], mod_ref[4:5, :], mod_ref[3:4, :])
    hh = h.astype(BF16)
    hl = (h - hh.astype(F32)).astype(BF16)
    r = r_ref[...]
    rh = r.astype(BF16)
    rl = (r - rh.astype(F32)).astype(BF16)
    logits = (jnp.dot(hh, rh, preferred_element_type=F32) + jnp.dot(hh, rl, preferred_element_type=F32)
              + jnp.dot(hl, rh, preferred_element_type=F32))
    lane = lax.broadcasted_iota(jnp.int32, logits.shape, 1)
    logits = jnp.where(lane < N_EXPERTS, logits, NEG_BIG)
    v0 = jnp.max(logits, axis=-1, keepdims=True)
    i0 = jnp.min(jnp.where(logits == v0, lane, LANES), axis=-1, keepdims=True)
    rest = jnp.where(lane == i0, NEG_BIG, logits)
    v1 = jnp.max(rest, axis=-1, keepdims=True)
    i1 = jnp.min(jnp.where(rest == v1, lane, LANES), axis=-1, keepdims=True)
    e = jnp.exp(v1 - v0)
    w0 = 1.0 / (1.0 + e)
    w1 = e / (1.0 + e)
    info_ref[...] = jnp.where(lane == 0, i0, jnp.where(lane == 1, i1, 0))
    w0_ref[...] = jnp.broadcast_to(w0, logits.shape)
    w1_ref[...] = jnp.broadcast_to(w1, logits.shape)


def moe_router(x, mod, g, router_pad, tm):
    t = x.shape[0]
    spec = pl.BlockSpec((tm, D), lambda i: (i, 0))
    lspec = pl.BlockSpec((tm, LANES), lambda i: (i, 0))
    return pl.pallas_call(
        _router_body,
        grid=(t // tm,),
        in_specs=[spec, _full((ADA_CHUNKS, D)), _full((1, D)), _full((D, LANES))],
        out_specs=[lspec, lspec, lspec],
        out_shape=[jax.ShapeDtypeStruct((t, LANES), jnp.int32),
                   jax.ShapeDtypeStruct((t, LANES), F32), jax.ShapeDtypeStruct((t, LANES), F32)],
        compiler_params=_cparams(("arbitrary",)),
        name="moe_router",
    )(x, mod, g, router_pad)


SCATTER_TOKENS = 256


def _row_scatter_body(nlat, nctx, didx_ref, g_ref, x_ref, mod_ref, *rest):
    if nctx:
        xc_ref, modc_ref, dst_ref, h_scr, zero_scr, sem = rest
    else:
        dst_ref, h_scr, zero_scr, sem = rest
    i = pl.program_id(0)
    ts = SCATTER_TOKENS

    def run(src_ref, src_is_zero_rows):
        def issue(grp, c):
            base = pl.multiple_of(grp * SUBLANES, SUBLANES)
            for r in range(SUBLANES):
                src = src_ref.at[pl.ds(r if src_is_zero_rows else base + r, 1), :]
                for half in range(2):
                    d = didx_ref[0, 0, base + r + half * ts]
                    pltpu.make_async_copy(src, dst_ref.at[pl.ds(d, 1), :], sem).start(priority=half)
            return c
        lax.fori_loop(0, ts // SUBLANES, issue, 0)

        def drain(n, c):
            pltpu.make_async_copy(src_ref.at[pl.ds(0, 1), :], dst_ref.at[pl.ds(0, 1), :], sem).wait()
            return c
        lax.fori_loop(0, 2 * ts, drain, 0, unroll=8)

    @pl.when(i == 0)
    def _():
        zero_scr[...] = jnp.zeros_like(zero_scr)

    def stage_and_run(src_ref, m_ref):
        h_scr[...] = _normmod(src_ref[...], g_ref[...], m_ref[4:5, :], m_ref[3:4, :])
        run(h_scr, False)

    @pl.when(i < nlat)
    def _():
        stage_and_run(x_ref, mod_ref)

    if nctx:
        @pl.when(jnp.logical_and(i >= nlat, i < nlat + nctx))
        def _():
            stage_and_run(xc_ref, modc_ref)

    @pl.when(i >= nlat + nctx)
    def _():
        run(zero_scr, True)


def row_scatter(x, xc, mod, modc, g, d0, d1, pad_pos):
    ts = SCATTER_TOKENS
    nlat = x.shape[0] // ts
    nctx = 0 if xc is None else 1
    assert xc is None or xc.shape[0] == ts
    ntok = nlat + nctx
    npad = pad_pos.shape[0] // (2 * ts)
    didx = jnp.concatenate([jnp.concatenate([d0.reshape(ntok, 1, ts), d1.reshape(ntok, 1, ts)], axis=2),
                            pad_pos.reshape(npad, 1, 2 * ts)], axis=0)
    in_specs = [pl.BlockSpec((1, 1, 2 * ts), lambda i: (i, 0, 0), memory_space=pltpu.SMEM),
                _full((1, D)),
                pl.BlockSpec((ts, D), lambda i: (jnp.minimum(i, nlat - 1), 0)),
                _full((ADA_CHUNKS, D))]
    args = [didx, g, x, mod]
    if nctx:
        in_specs += [_full((ts, D)), _full((ADA_CHUNKS, D))]
        args += [xc, modc]
    return pl.pallas_call(
        functools.partial(_row_scatter_body, nlat, nctx),
        grid=(ntok + npad,),
        in_specs=in_specs,
        out_specs=pl.BlockSpec(memory_space=pl.ANY),
        out_shape=jax.ShapeDtypeStruct((2 * ntok * ts + pad_pos.shape[0], D), F32),
        scratch_shapes=[pltpu.VMEM((ts, D), F32), pltpu.VMEM((SUBLANES, D), F32), pltpu.SemaphoreType.DMA(())],
        compiler_params=_cparams(("arbitrary",)),
        name="moe_row_scatter",
    )(*args)


def _moe_ffn_body(te_ref, tv_ref, xg_ref, wg_ref, wu_ref, wd_ref, o_ref, h_scr, acc_scr):
    i = pl.program_id(0)
    j = pl.program_id(1)

    @pl.when(tv_ref[i] > 0)
    def _():
        @pl.when(j == 0)
        def _():
            h_scr[...] = xg_ref[...].astype(BF16)
            acc_scr[...] = jnp.zeros_like(acc_scr)

        h = h_scr[...]
        gg = jnp.dot(h, wg_ref[0], preferred_element_type=F32)
        uu = jnp.dot(h, wu_ref[0], preferred_element_type=F32)
        a = (gg * _sigmoid(gg)) * uu
        acc_scr[...] += jnp.dot(a.astype(BF16), wd_ref[0], preferred_element_type=F32)

        @pl.when(j == pl.num_programs(1) - 1)
        def _():
            o_ref[...] = acc_scr[...]

    @pl.when(jnp.logical_and(tv_ref[i] == 0, j == pl.num_programs(1) - 1))
    def _():
        o_ref[...] = jnp.zeros_like(o_ref)


def moe_ffn(xg, tile_e, tile_v, w_gu, w_down, li, tm, fc=1792):
    p = xg.shape[0]
    nf = D_FF // fc

    def jeff(i, j, tv):
        return jnp.where(tv[i] > 0, j, nf - 1)

    grid_spec = pltpu.PrefetchScalarGridSpec(
        num_scalar_prefetch=2,
        grid=(p // tm, nf),
        in_specs=[
            pl.BlockSpec((tm, D), lambda i, j, te, tv: (i, 0)),
            pl.BlockSpec((None, 1, D, fc), lambda i, j, te, tv: (li, te[i], 0, jeff(i, j, tv))),
            pl.BlockSpec((None, 1, D, fc), lambda i, j, te, tv: (li, te[i], 0, nf + jeff(i, j, tv))),
            pl.BlockSpec((None, 1, fc, D), lambda i, j, te, tv: (li, te[i], jeff(i, j, tv), 0)),
        ],
        out_specs=pl.BlockSpec((tm, D), lambda i, j, te, tv: (i, 0)),
        scratch_shapes=[pltpu.VMEM((tm, D), BF16), pltpu.VMEM((tm, D), F32)],
    )
    return pl.pallas_call(
        _moe_ffn_body,
        grid_spec=grid_spec,
        out_shape=jax.ShapeDtypeStruct((p, D), F32),
        compiler_params=_cparams(("arbitrary", "arbitrary")),
        name="moe_ffn",
    )(tile_e, tile_v, xg, w_gu, w_gu, w_down)


def _combine_body(tt, d0_ref, d1_ref, yp_ref, x_ref, mod_ref, w0_ref, w1_ref, o_ref, a_scr, b_scr, sem):
    def issue(grp, c):
        base = pl.multiple_of(grp * SUBLANES, SUBLANES)
        for r in range(SUBLANES):
            n = base + r
            pltpu.make_async_copy(yp_ref.at[pl.ds(d0_ref[0, 0, n], 1), :], a_scr.at[pl.ds(n, 1), :],
                                  sem).start(priority=0)
            pltpu.make_async_copy(yp_ref.at[pl.ds(d1_ref[0, 0, n], 1), :], b_scr.at[pl.ds(n, 1), :],
                                  sem).start(priority=1)
        return c
    lax.fori_loop(0, tt // SUBLANES, issue, 0)

    def drain(n, c):
        pltpu.make_async_copy(yp_ref.at[pl.ds(0, 1), :], a_scr.at[pl.ds(0, 1), :], sem).wait()
        pltpu.make_async_copy(yp_ref.at[pl.ds(0, 1), :], b_scr.at[pl.ds(0, 1), :], sem).wait()
        return c
    lax.fori_loop(0, tt, drain, 0, unroll=8)
    w0 = _lane_tile(w0_ref[...])
    w1 = _lane_tile(w1_ref[...])
    o_ref[...] = x_ref[...] + mod_ref[5:6, :] * (w0 * a_scr[...] + w1 * b_scr[...])


def moe_combine(yp, d0, d1, x, mod, w0b, w1b, tt):
    t = x.shape[0]
    nt = t // tt
    ispec = pl.BlockSpec((1, 1, tt), lambda i: (i, 0, 0), memory_space=pltpu.SMEM)
    spec = pl.BlockSpec((tt, D), lambda i: (i, 0))
    lspec = pl.BlockSpec((tt, LANES), lambda i: (i, 0))
    return pl.pallas_call(
        functools.partial(_combine_body, tt),
        grid=(nt,),
        in_specs=[ispec, ispec, pl.BlockSpec(memory_space=pl.ANY), spec, _full((ADA_CHUNKS, D)), lspec, lspec],
        out_specs=spec,
        out_shape=jax.ShapeDtypeStruct((t, D), F32),
        scratch_shapes=[pltpu.VMEM((tt, D), F32), pltpu.VMEM((tt, D), F32), pltpu.SemaphoreType.DMA(())],
        compiler_params=_cparams(("arbitrary",)),
        name="moe_combine",
    )(d0.reshape(nt, 1, tt), d1.reshape(nt, 1, tt), yp, x, mod, w0b, w1b)


def _route_plan(e0, e1, tm):
    t = e0.shape[0]
    n = 2 * t
    ex = jnp.arange(N_EXPERTS, dtype=jnp.int32)
    oh0 = (e0[:, None] == ex[None, :]).astype(jnp.int32)
    oh1 = (e1[:, None] == ex[None, :]).astype(jnp.int32)
    both = oh0 + oh1
    csum = jnp.cumsum(both, axis=0)
    before = csum - both
    counts = csum[-1]
    padded = ((counts + tm - 1) // tm) * tm
    pad_end = jnp.cumsum(padded)
    pad_off = pad_end - padded
    total = pad_end[-1]
    d0 = jnp.sum(oh0 * (before + pad_off[None, :]), axis=1)
    d1 = jnp.sum(oh1 * (before + oh0 + pad_off[None, :]), axis=1)
    gap = padded - counts
    tail_off = jnp.cumsum(tm - gap) - (tm - gap)
    r = jnp.arange(tm, dtype=jnp.int32)[None, :]
    pad_pos = jnp.where(r < gap[:, None], (pad_off + counts)[:, None] + r,
                        total + tail_off[:, None] + (r - gap[:, None])).reshape(-1)
    ntiles = (n + N_EXPERTS * tm) // tm
    tstart = jnp.arange(ntiles, dtype=jnp.int32) * tm
    tile_v = (tstart < total).astype(jnp.int32)
    tile_e = jnp.sum((jnp.minimum(tstart, total - 1)[:, None] >= pad_end[None, :]).astype(jnp.int32), axis=1)
    return (d0.astype(jnp.int32), d1.astype(jnp.int32), pad_pos.astype(jnp.int32), tile_e.astype(jnp.int32), tile_v)


def moe_layer(x, xc, mod, modc, g, router, w_gu_b, w_down_b, li, need_ctx, tm=512):
    router_pad = jnp.pad(router, ((0, 0), (0, LANES - N_EXPERTS)))
    s = x.shape[0]
    info, w0b, w1b = moe_router(x, mod, g, router_pad, 1024)
    if need_ctx:
        sc = xc.shape[0]
        infoc, w0c, w1c = moe_router(xc, modc, g, router_pad, sc)
        e0 = jnp.concatenate([info[:, 0], infoc[:, 0]])
        e1 = jnp.concatenate([info[:, 1], infoc[:, 1]])
    else:
        e0, e1 = info[:, 0], info[:, 1]
    d0, d1, pad_pos, tile_e, tile_v = _route_plan(e0, e1, tm)
    xg = row_scatter(x, xc if need_ctx else None, mod, modc, g, d0, d1, pad_pos)
    yp = moe_ffn(xg, tile_e, tile_v, w_gu_b, w_down_b, li, tm)
    x_new = moe_combine(yp, d0[:s], d1[:s], x, mod, w0b, w1b, 512)
    xc_new = None
    if need_ctx:
        xc_new = moe_combine(yp, d0[s:], d1[s:], xc, modc, w0c, w1c, sc)
    return x_new, xc_new


def kernel(x, c, ctx, c_ctx, ada_w, ada_b, norm_g, rg_w_in, rg_conv_w, rg_conv_b, rg_wa, rg_ba, rg_wi, rg_bi,
           rg_lambda, rg_w_out, na_w_qkv, na_q_g, na_k_g, na_rpb, na_w_o, ft_w_out, ffn_w_gu, ffn_w_down,
           moe_router, moe_w_gu, moe_w_down):
    depth = ada_w.shape[0]
    assert x.shape[0] == 1 and x.shape[2] == D
    xs = x[0]
    xc = ctx[0]
    mods = ada_modulation(c, c_ctx, ada_w, ada_b)
    ffn_gu_b, ffn_dn_b = ffn_w_gu.astype(BF16), ffn_w_down.astype(BF16)
    moe_gu_b, moe_dn_b = moe_w_gu.astype(BF16), moe_w_down.astype(BF16)
    mix_idx = [0] * N_MIXERS
    dense_idx = 0
    moe_idx = 0
    for layer in range(depth):
        need_ctx = layer != depth - 1
        mod, modc = mods[layer, 0], mods[layer, 1]
        g0 = norm_g[layer, 0][None]
        g1 = norm_g[layer, 1][None]
        kind = layer % N_MIXERS
        j = mix_idx[kind]
        mix_idx[kind] += 1
        if kind == 0:
            xs, xcn = rglru_layer(xs, xc, mod, modc, g0, rg_w_in[j], rg_conv_w[j], rg_conv_b[j], rg_wa[j], rg_wi[j],
                                  rg_ba[j], rg_bi[j], rg_lambda[j], rg_w_out[j], need_ctx)
        elif kind == 1:
            xs, xcn = na_layer(xs, xc, mod, modc, g0, na_w_qkv[j], na_q_g[j], na_k_g[j], na_rpb[j], na_w_o[j],
                               need_ctx)
        else:
            xs, xcn = fourier_layer(xs, xc, mod, modc, g0, ft_w_out[j], need_ctx)
        if need_ctx:
            xc = xcn
        if layer % 2 == 0:
            if need_ctx:
                xc = ffn_dense(xc, modc, g1, ffn_gu_b, ffn_dn_b, dense_idx, xc.shape[0])
            xs = ffn_dense(xs, mod, g1, ffn_gu_b, ffn_dn_b, dense_idx, 512)
            dense_idx += 1
        else:
            xs, xcn = moe_layer(xs, xc, mod, modc, g1, moe_router[moe_idx], moe_gu_b, moe_dn_b, moe_idx, need_ctx)
            moe_idx += 1
            if need_ctx:
                xc = xcn
    return xs[None]
```

```python
import functools
import math

import numpy as np
import jax
import jax.numpy as jnp
from jax import lax
from jax.experimental import pallas as pl
from jax.experimental.pallas import tpu as pltpu

F32 = jnp.float32
BF16 = jnp.bfloat16

D = 1024
D_FF = 3584
N_EXPERTS = 8
GRID_W = 64
NA_HEADS = 16
NA_HEAD_DIM = 64
NA_ROWS = 8
NA_COLS = 16
FT_GROUP_W = 256
RG_BLOCK_W = 256
RMS_EPS = 1e-6
LRU_C = 8.0
N_MIXERS = 3
ADA_CHUNKS = 6

LANES = 128
SUBLANES = 8
VMEM_LIMIT = 56 * 1024 * 1024
MOE_VMEM_LIMIT = 60 * 1024 * 1024
NEG_BIG = -1e30
LOG2E = math.log2(math.e)


def _cparams(sem):
    return pltpu.CompilerParams(dimension_semantics=sem, vmem_limit_bytes=VMEM_LIMIT)


def _full(shape):
    nd = len(shape)
    return pl.BlockSpec(shape, lambda *_: (0,) * nd)


def _normmod(x, g, scale, shift):
    ms = jnp.mean(x * x, axis=-1, keepdims=True)
    y = x * lax.rsqrt(ms + RMS_EPS)
    return (y * g) * (1.0 + scale) + shift


def _lane_tile(v):
    return jnp.concatenate([v] * (D // LANES), axis=1)


def _sigmoid(v):
    return 1.0 / (1.0 + jnp.exp(-v))


def _gelu_tanh(v):
    c = math.sqrt(2.0 / math.pi)
    return v * (0.5 * (1.0 + jnp.tanh(c * (v + 0.044715 * (v * v * v)))))


def _ada_body(cin_ref, w_ref, b_ref, o_ref):
    v = cin_ref[...]
    s = v * _sigmoid(v)
    w = w_ref[0]
    r0 = jnp.sum(s[:, 0:1] * w, axis=0, keepdims=True)
    r1 = jnp.sum(s[:, 1:2] * w, axis=0, keepdims=True)
    o_ref[0] = jnp.concatenate([r0, r1], axis=0) + b_ref[0]


def ada_modulation(c, c_ctx, ada_w, ada_b):
    depth = ada_w.shape[0]
    n = ada_w.shape[2]
    nc = n // 4
    cin = jnp.stack([c[0], c_ctx], axis=1)
    out = pl.pallas_call(
        _ada_body,
        grid=(depth, n // nc),
        in_specs=[
            pl.BlockSpec((D, 2), lambda l, j: (0, 0)),
            pl.BlockSpec((1, D, nc), lambda l, j: (l, 0, j)),
            pl.BlockSpec((1, 1, nc), lambda l, j: (l, 0, j)),
        ],
        out_specs=pl.BlockSpec((1, 2, nc), lambda l, j: (l, 0, j)),
        out_shape=jax.ShapeDtypeStruct((depth, 2, n), F32),
        compiler_params=_cparams(("arbitrary", "arbitrary")),
        name="ada_mod",
    )(cin, ada_w, ada_b.reshape(depth, 1, n))
    return out.reshape(depth, 2, ADA_CHUNKS, D)


FFN_CHUNK = 512


def _swiglu_chunks(h, wg_ref, wu_ref, wd_ref):
    acc = None
    for c in range(D_FF // FFN_CHUNK):
        sl = slice(c * FFN_CHUNK, (c + 1) * FFN_CHUNK)
        gg = jnp.dot(h, wg_ref[:, sl], preferred_element_type=F32)
        uu = jnp.dot(h, wu_ref[:, sl], preferred_element_type=F32)
        a = ((gg * _sigmoid(gg)) * uu).astype(BF16)
        part = jnp.dot(a, wd_ref[sl, :], preferred_element_type=F32)
        acc = part if acc is None else acc + part
    return acc


def _ffn_body(x_ref, mod_ref, g_ref, wg_ref, wu_ref, wd_ref, o_ref):
    x = x_ref[...]
    h = _normmod(x, g_ref[...], mod_ref[4:5, :], mod_ref[3:4, :]).astype(BF16)
    o_ref[...] = x + mod_ref[5:6, :] * _swiglu_chunks(h, wg_ref, wu_ref, wd_ref)


def ffn_dense(x, mod, g, w_gu, w_down, li, tm):
    t = x.shape[0]
    once = pl.Buffered(1)
    return pl.pallas_call(
        _ffn_body,
        grid=(t // tm,),
        in_specs=[
            pl.BlockSpec((tm, D), lambda i: (i, 0)),
            _full((ADA_CHUNKS, D)),
            _full((1, D)),
            pl.BlockSpec((None, D, D_FF), lambda i: (li, 0, 0), pipeline_mode=once),
            pl.BlockSpec((None, D, D_FF), lambda i: (li, 0, 1), pipeline_mode=once),
            pl.BlockSpec((None, D_FF, D), lambda i: (li, 0, 0), pipeline_mode=once),
        ],
        out_specs=pl.BlockSpec((tm, D), lambda i: (i, 0)),
        out_shape=jax.ShapeDtypeStruct((t, D), F32),
        compiler_params=_cparams(("arbitrary",)),
        name="ffn_dense",
    )(x, mod, g, w_gu, w_gu, w_down)


def _proj_body(gate_row, a_ref, w_ref, x_ref, mod_ref, o_ref):
    y = jnp.dot(a_ref[...], w_ref[...], preferred_element_type=F32)
    o_ref[...] = x_ref[...] + mod_ref[gate_row:gate_row + 1, :] * y


def proj_residual(a, w, x, mod, gate_row, tm):
    t, k = a.shape
    return pl.pallas_call(
        functools.partial(_proj_body, gate_row),
        grid=(t // tm,),
        in_specs=[
            pl.BlockSpec((tm, k), lambda i: (i, 0)),
            _full((k, D)),
            pl.BlockSpec((tm, D), lambda i: (i, 0)),
            _full((ADA_CHUNKS, D)),
        ],
        out_specs=pl.BlockSpec((tm, D), lambda i: (i, 0)),
        out_shape=jax.ShapeDtypeStruct((t, D), F32),
        compiler_params=_cparams(("arbitrary",)),
        name="proj_residual",
    )(a, w, x, mod)


HALO = SUBLANES


def _rg_in_body(tm, xp_ref, x_ref, xn_ref, mod_ref, g_ref, w_ref, cw_ref, cb_ref, xc_ref, gg_ref):
    i = pl.program_id(0)
    last = pl.num_programs(0) - 1
    xa = jnp.concatenate([xp_ref[...], x_ref[...], xn_ref[...]], axis=0)
    h = _normmod(xa, g_ref[...], mod_ref[1:2, :], mod_ref[0:1, :]).astype(BF16)
    z = jnp.dot(h, w_ref[...], preferred_element_type=F32)
    row = lax.broadcasted_iota(jnp.int32, (tm + 2 * HALO, 1), 0)
    valid = jnp.logical_and(jnp.logical_or(row >= HALO, i > 0),
                            jnp.logical_or(row < tm + HALO, i < last))
    xz = jnp.where(valid, z[:, :D], 0.0)
    y = cb_ref[...] + cw_ref[2:3, :] * xz[HALO:HALO + tm]
    y = y + cw_ref[0:1, :] * xz[HALO - 2:HALO - 2 + tm]
    y = y + cw_ref[1:2, :] * xz[HALO - 1:HALO - 1 + tm]
    y = y + cw_ref[3:4, :] * xz[HALO + 1:HALO + 1 + tm]
    xc_ref[...] = y
    gg_ref[...] = _gelu_tanh(z[HALO:HALO + tm, D:]).astype(BF16)


def rg_in(x, mod, g, w_in, conv_w, conv_b, tm):
    t = x.shape[0]
    nb = tm // HALO
    nblk = t // HALO
    return pl.pallas_call(
        functools.partial(_rg_in_body, tm),
        grid=(t // tm,),
        in_specs=[
            pl.BlockSpec((HALO, D), lambda i: (jnp.maximum(i * nb - 1, 0), 0)),
            pl.BlockSpec((tm, D), lambda i: (i, 0)),
            pl.BlockSpec((HALO, D), lambda i: (jnp.minimum((i + 1) * nb, nblk - 1), 0)),
            _full((ADA_CHUNKS, D)),
            _full((1, D)),
            _full((D, 2 * D)),
            _full((4, D)),
            _full((1, D)),
        ],
        out_specs=[pl.BlockSpec((tm, D), lambda i: (i, 0)), pl.BlockSpec((tm, D), lambda i: (i, 0))],
        out_shape=[jax.ShapeDtypeStruct((t, D), F32), jax.ShapeDtypeStruct((t, D), BF16)],
        compiler_params=_cparams(("arbitrary",)),
        name="rg_in",
    )(x, x, x, mod, g, w_in, conv_w, conv_b)


def _rg_gates(xc, wa_ref, wi_ref, ba, bi, lam):
    xb = xc.astype(BF16)
    nblk = D // RG_BLOCK_W
    r = jnp.concatenate([jnp.dot(xb[:, n * RG_BLOCK_W:(n + 1) * RG_BLOCK_W], wa_ref[n],
                                 preferred_element_type=F32) for n in range(nblk)], axis=1)
    ig = jnp.concatenate([jnp.dot(xb[:, n * RG_BLOCK_W:(n + 1) * RG_BLOCK_W], wi_ref[n],
                                  preferred_element_type=F32) for n in range(nblk)], axis=1)
    r = 0.5 + 0.5 * jnp.tanh(0.5 * (r + ba))
    ig = 0.5 + 0.5 * jnp.tanh(0.5 * (ig + bi))
    nl = -lam
    softplus = jnp.maximum(nl, 0.0) + jnp.log1p(jnp.exp(-jnp.abs(nl)))
    log_a = (-LRU_C * r) * softplus
    a = jnp.exp(log_a)
    b = jnp.sqrt(1.0 - a * a) * (ig * xc)
    return a, b


def _rg_scan_body(reverse, epilogue, emit_h, tc, *refs):
    xc_ref, wa_ref, wi_ref, ba_ref, bi_ref, lam_ref, h0_ref = refs[:7]
    refs = refs[7:]
    if epilogue:
        hf_ref, gg_ref, wo_ref, x_ref, mod_ref = refs[:5]
        refs = refs[5:]
    if emit_h:
        h_ref = refs[0]
        refs = refs[1:]
    if epilogue:
        o_ref = refs[0]
        refs = refs[1:]
    a_scr, b_scr, h_scr, carry_scr = refs
    c = pl.program_id(0)

    @pl.when(c == 0)
    def _():
        carry_scr[...] = jnp.broadcast_to(h0_ref[...], (SUBLANES, D))

    a, b = _rg_gates(xc_ref[...], wa_ref, wi_ref, ba_ref[...], bi_ref[...], lam_ref[...])
    a_scr[...] = a
    b_scr[...] = b
    nblk = tc // SUBLANES
    row = lax.broadcasted_iota(jnp.int32, (SUBLANES, D), 0)
    first = (row == SUBLANES - 1) if reverse else (row == 0)

    def block(n, carry):
        blk = (nblk - 1 - n) if reverse else n
        off = pl.multiple_of(blk * SUBLANES, SUBLANES)
        av = a_scr[pl.ds(off, SUBLANES), :]
        bv = b_scr[pl.ds(off, SUBLANES), :]
        bv = jnp.where(first, av * carry + bv, bv)
        av = jnp.where(first, 0.0, av)
        for k in (1, 2, 4):
            shift = (SUBLANES - k) if reverse else k
            bv = av * pltpu.roll(bv, shift, 0) + bv
            if k != 4:
                av = av * pltpu.roll(av, shift, 0)
        h_scr[pl.ds(off, SUBLANES), :] = bv
        edge = bv[0:1, :] if reverse else bv[SUBLANES - 1:SUBLANES, :]
        return jnp.broadcast_to(edge, (SUBLANES, D))

    carry_scr[...] = lax.fori_loop(0, nblk, block, carry_scr[...], unroll=2)

    if emit_h:
        h_ref[...] = h_scr[...].astype(h_ref.dtype)
    if epilogue:
        y = ((hf_ref[...].astype(F32) + h_scr[...]) * gg_ref[...].astype(F32)).astype(BF16)
        o_ref[...] = x_ref[...] + mod_ref[2:3, :] * jnp.dot(y, wo_ref[...], preferred_element_type=F32)


def rg_scan(xconv, wa, wi, ba, bi, lam, h0, tc, reverse, epi=None, h_dtype=F32):
    t = xconv.shape[0]
    nchunks = t // tc
    idx = (lambda c: (nchunks - 1 - c, 0)) if reverse else (lambda c: (c, 0))
    nb = D // RG_BLOCK_W
    blk = pl.BlockSpec((tc, D), idx)
    in_specs = [
        blk,
        _full((nb, RG_BLOCK_W, RG_BLOCK_W)),
        _full((nb, RG_BLOCK_W, RG_BLOCK_W)),
        _full((1, D)), _full((1, D)), _full((1, D)), _full((1, D)),
    ]
    args = [xconv, wa, wi, ba, bi, lam, h0]
    out_specs = []
    out_shape = []
    if epi is not None:
        hf, gg, w_out, x, mod = epi
        in_specs += [blk, blk, _full((D, D)), blk, _full((ADA_CHUNKS, D))]
        args += [hf, gg, w_out, x, mod]
    if h_dtype is not None:
        out_specs.append(blk)
        out_shape.append(jax.ShapeDtypeStruct((t, D), h_dtype))
    if epi is not None:
        out_specs.append(blk)
        out_shape.append(jax.ShapeDtypeStruct((t, D), F32))
    return pl.pallas_call(
        functools.partial(_rg_scan_body, reverse, epi is not None, h_dtype is not None, tc),
        grid=(nchunks,),
        in_specs=in_specs,
        out_specs=out_specs,
        out_shape=out_shape,
        scratch_shapes=[pltpu.VMEM((tc, D), F32), pltpu.VMEM((tc, D), F32), pltpu.VMEM((tc, D), F32),
                        pltpu.VMEM((SUBLANES, D), F32)],
        compiler_params=_cparams(("arbitrary",)),
        name="rg_scan_bwd" if reverse else "rg_scan_fwd",
    )(*args)


def rglru_layer(x, xc, mod, modc, g, w_in, conv_w, conv_b, wa, wi, ba, bi, lam, w_out, need_ctx):
    w_in_b = w_in.astype(BF16)
    wa_b = wa.astype(BF16)
    wi_b = wi.astype(BF16)
    w_out_b = w_out.astype(BF16)
    cb = conv_b[None]
    tcx = xc.shape[0]
    xcl, ggl = rg_in(x, mod, g, w_in_b, conv_w, cb, 512)
    xcc, ggc = rg_in(xc, modc, g, w_in_b, conv_w, cb, tcx)
    zeros = jnp.zeros((1, D), F32)
    p = lambda d: (wa_b[d], wi_b[d], ba[d][None], bi[d][None], lam[d][None])
    (hcf,) = rg_scan(xcc, *p(0), zeros, tcx, False)
    (hlf,) = rg_scan(xcl, *p(0), hcf[tcx - 1:tcx], 512, False, h_dtype=BF16)
    if need_ctx:
        hcb, xc_new = rg_scan(xcc, *p(1), zeros, tcx, True, epi=(hcf, ggc, w_out_b, xc, modc))
    else:
        (hcb,) = rg_scan(xcc, *p(1), zeros, tcx, True)
        xc_new = None
    (x_new,) = rg_scan(xcl, *p(1), hcb[0:1], 512, True, epi=(hlf, ggl, w_out_b, x, mod), h_dtype=None)
    return x_new, xc_new


def _qkv_body(x_ref, mod_ref, g_ref, w_ref, gm_ref, qg_ref, kg_ref, q_ref, k_ref, v_ref):
    h = _normmod(x_ref[...], g_ref[...], mod_ref[1:2, :], mod_ref[0:1, :]).astype(BF16)
    z = jnp.dot(h, w_ref[...], preferred_element_type=F32)

    def headnorm(v, gain):
        ms = jnp.dot((v * v).astype(BF16), gm_ref[...], preferred_element_type=F32)
        return (v * lax.rsqrt(ms + RMS_EPS)) * gain

    q_ref[...] = headnorm(z[:, :D], qg_ref[...]).astype(BF16)
    k_ref[...] = headnorm(z[:, D:2 * D], kg_ref[...]).astype(BF16)
    v_ref[...] = z[:, 2 * D:].astype(BF16)


def qkv_proj(x, mod, g, w_qkv, gmean, qg, kg, tm):
    t = x.shape[0]
    spec = pl.BlockSpec((tm, D), lambda i: (i, 0))
    return pl.pallas_call(
        _qkv_body,
        grid=(t // tm,),
        in_specs=[spec, _full((ADA_CHUNKS, D)), _full((1, D)), _full((D, 3 * D)), _full((D, D)),
                  _full((1, D)), _full((1, D))],
        out_specs=[spec, spec, spec],
        out_shape=[jax.ShapeDtypeStruct((t, D), BF16)] * 3,
        compiler_params=_cparams(("arbitrary",)),
        name="qkv_proj",
    )(x, mod, g, w_qkv, gmean, qg, kg)


def _attend_pair(q2, keys, vals, biases):
    m_rows = q2.shape[0]
    lane = lax.broadcasted_iota(jnp.int32, q2.shape, 1)
    zero = jnp.zeros_like(q2)
    qs = jnp.concatenate([jnp.where(lane < NA_HEAD_DIM, q2, zero), jnp.where(lane >= NA_HEAD_DIM, q2, zero)], axis=0)
    ss = []
    for kseg, bseg in zip(keys, biases):
        s = lax.dot_general(qs, kseg, (((1,), (1,)), ((), ())), preferred_element_type=F32)
        if bseg is not None:
            s = s + jnp.concatenate([bseg[0], bseg[1]], axis=0)
        ss.append(s)
    m = ss[0].max(axis=-1, keepdims=True)
    for s in ss[1:]:
        m = jnp.maximum(m, s.max(axis=-1, keepdims=True))
    den = None
    acc = None
    for s, vseg in zip(ss, vals):
        p = jnp.exp2(s - m)
        d = jnp.sum(p, axis=-1, keepdims=True)
        o = jnp.dot(p.astype(BF16), vseg, preferred_element_type=F32)
        den = d if den is None else den + d
        acc = o if acc is None else acc + o
    out = acc / den
    return jnp.where(lane < NA_HEAD_DIM, out[:m_rows], out[m_rows:])


NA_QROWS = 2
NA_UNION = NA_ROWS + NA_QROWS - 1


def _na_body(var_ref, q_ref, kl_ref, vl_ref, kc_ref, vc_ref, bias_ref, o_ref):
    for pr in range(NA_HEADS // 2):
        sl = slice(pr * LANES, (pr + 1) * LANES)
        o_ref[:, sl] = _attend_pair(
            q_ref[:, sl], [kl_ref[:, sl], kc_ref[:, sl]], [vl_ref[:, sl], vc_ref[:, sl]],
            [(bias_ref[0, 2 * pr], bias_ref[0, 2 * pr + 1]), None]).astype(BF16)


def _na_geometry(rows):
    steps = rows // NA_QROWS
    g = np.arange(steps)
    base = np.clip(NA_QROWS * g - NA_ROWS // 2, 0, rows - NA_UNION)
    r = NA_QROWS * g[:, None] + np.arange(NA_QROWS)[None, :]
    rs = np.clip(r - NA_ROWS // 2, 0, rows - NA_ROWS)
    key = np.concatenate([(base - NA_QROWS * g)[:, None], rs - r], axis=1)
    uniq, first, var = np.unique(key, axis=0, return_index=True, return_inverse=True)
    return base, var.reshape(-1).astype(np.int32), g[first]


def na_attention(q, k, v, kc, vc, bias_tab, var):
    t = q.shape[0]
    rows = t // GRID_W
    nctx = kc.shape[0]
    steps = rows // NA_QROWS

    def kbase(g):
        return jnp.clip(NA_QROWS * g - NA_ROWS // 2, 0, rows - NA_UNION)

    qrows = NA_QROWS * GRID_W
    nloc = NA_UNION * GRID_W
    kspec = pl.BlockSpec((pl.Element(nloc), pl.Element(D)), lambda g, var: (kbase(g) * GRID_W, 0))
    grid_spec = pltpu.PrefetchScalarGridSpec(
        num_scalar_prefetch=1,
        grid=(steps,),
        in_specs=[pl.BlockSpec((qrows, D), lambda g, var: (g, 0)), kspec, kspec] + [
            pl.BlockSpec((nctx, D), lambda g, var: (0, 0)), pl.BlockSpec((nctx, D), lambda g, var: (0, 0)),
            pl.BlockSpec((1, NA_HEADS, qrows, nloc), lambda g, var: (var[g], 0, 0, 0)),
        ],
        out_specs=pl.BlockSpec((qrows, D), lambda g, var: (g, 0)),
    )
    return pl.pallas_call(
        _na_body,
        grid_spec=grid_spec,
        out_shape=jax.ShapeDtypeStruct((t, D), BF16),
        compiler_params=_cparams(("arbitrary",)),
        name="na_attention",
    )(var, q, k, v, kc, vc, bias_tab)


def _ctx_attn_body(q_ref, k_ref, v_ref, o_ref):
    for pr in range(NA_HEADS // 2):
        sl = slice(pr * LANES, (pr + 1) * LANES)
        o_ref[:, sl] = _attend_pair(q_ref[:, sl], [k_ref[:, sl]], [v_ref[:, sl]], [None]).astype(BF16)


def ctx_attention(q, k, v):
    t = q.shape[0]
    return pl.pallas_call(
        _ctx_attn_body,
        grid=(1,),
        in_specs=[_full((t, D))] * 3,
        out_specs=_full((t, D)),
        out_shape=jax.ShapeDtypeStruct((t, D), BF16),
        compiler_params=_cparams(("arbitrary",)),
        name="ctx_attention",
    )(q, k, v)


def _na_bias_table(rpb, rows):
    base, var, reps = _na_geometry(rows)
    cols = np.arange(GRID_W)
    cstart = np.clip(cols - NA_COLS // 2, 0, GRID_W - NA_COLS)
    kcol = np.arange(GRID_W)
    inwin = (kcol[None, :] >= cstart[:, None]) & (kcol[None, :] < cstart[:, None] + NA_COLS)
    r = NA_QROWS * reps[:, None] + np.arange(NA_QROWS)[None, :]
    rs = np.clip(r - NA_ROWS // 2, 0, rows - NA_ROWS)
    krow = base[reps][:, None] + np.arange(NA_UNION)[None, :]
    rvalid = (krow[:, None, :] >= rs[:, :, None]) & (krow[:, None, :] < rs[:, :, None] + NA_ROWS)
    ridx = np.clip(krow[:, None, :] - r[:, :, None] + (NA_ROWS - 1), 0, 2 * NA_ROWS - 2)
    nd = 2 * NA_COLS - 1
    w = jnp.pad(rpb.astype(F32), ((0, 0), (0, 0), (GRID_W - NA_COLS, 2 * GRID_W - (GRID_W - NA_COLS) - nd)))
    flat = jnp.tile(w, (1, 1, GRID_W))[:, :, :GRID_W * (2 * GRID_W - 1)]
    blk = flat.reshape(NA_HEADS, 2 * NA_ROWS - 1, GRID_W, 2 * GRID_W - 1)[..., GRID_W - 1:]
    blk = jnp.where(jnp.asarray(inwin)[None, None], blk, NEG_BIG)
    neg = jnp.full((NA_HEADS, GRID_W, GRID_W), NEG_BIG, F32)
    variants = []
    for v in range(len(reps)):
        strips = [jnp.concatenate([blk[:, ridx[v, a, j]] if rvalid[v, a, j] else neg for j in range(NA_UNION)], axis=2)
                  for a in range(NA_QROWS)]
        variants.append(jnp.concatenate(strips, axis=1))
    return jnp.stack(variants, axis=0), jnp.asarray(var)


def na_layer(x, xc, mod, modc, g, w_qkv, q_g, k_g, rpb, w_o, need_ctx):
    w_qkv_b = w_qkv.astype(BF16)
    w_o_b = w_o.astype(BF16)
    gm = np.kron(np.eye(NA_HEADS), np.full((NA_HEAD_DIM, NA_HEAD_DIM), 1.0 / NA_HEAD_DIM))
    gmean = jnp.asarray(gm, dtype=BF16)
    qg = jnp.tile(q_g, NA_HEADS)[None] * (NA_HEAD_DIM ** -0.5 * LOG2E)
    kg = jnp.tile(k_g, NA_HEADS)[None]
    q, k, v = qkv_proj(x, mod, g, w_qkv_b, gmean, qg, kg, 512)
    qc, kc, vc = qkv_proj(xc, modc, g, w_qkv_b, gmean, qg, kg, xc.shape[0])
    bias_tab, var = _na_bias_table(rpb * LOG2E, x.shape[0] // GRID_W)
    o = na_attention(q, k, v, kc, vc, bias_tab, var)
    x_new = proj_residual(o, w_o_b, x, mod, 2, 1024)
    xc_new = None
    if need_ctx:
        oc = ctx_attention(qc, kc, vc)
        xc_new = proj_residual(oc, w_o_b, xc, modc, 2, xc.shape[0])
    return x_new, xc_new


def _dft_mats(n):
    ang = 2.0 * np.pi * np.outer(np.arange(n), np.arange(n)) / n
    return np.cos(ang), np.sin(ang)


def _channel_dft(h, wc):
    us = [jnp.dot(h[:, gi * FT_GROUP_W:(gi + 1) * FT_GROUP_W], wc, preferred_element_type=F32).astype(BF16)
          for gi in range(D // FT_GROUP_W)]
    return jnp.concatenate([u[:, :FT_GROUP_W] for u in us] + [u[:, FT_GROUP_W:] for u in us], axis=1)


def _ft_a_body(n, nj, x_ref, mod_ref, g_ref, perm_ref, wc_ref, ma_ref, tc_ref, ts_ref, yr_ref, yi_ref):
    h3 = _normmod(x_ref[...], g_ref[...], mod_ref[1:2, :], mod_ref[0:1, :])
    h = jnp.dot(perm_ref[...], h3.reshape(n * nj, D).astype(BF16), preferred_element_type=F32).astype(BF16)
    u = _channel_dft(h, wc_ref[...])
    for j in range(nj):
        uj = u[j * n:(j + 1) * n]
        y = jnp.dot(ma_ref[...], jnp.concatenate([uj[:, :D], uj[:, D:]], axis=0), preferred_element_type=F32)
        yr, yi = y[:n], y[n:]
        tc = _lane_tile(tc_ref[0, :, j * LANES:(j + 1) * LANES])
        ts = _lane_tile(ts_ref[0, :, j * LANES:(j + 1) * LANES])
        yr_ref[:, j, :] = yr * tc + yi * ts
        yi_ref[:, j, :] = yi * tc - yr * ts


def _ft_c_body(n, nj, yr_ref, yi_ref, mc_ref, wf_ref, x_ref, mod_ref, o_ref):
    fs = []
    for j in range(nj):
        ys = jnp.concatenate([yr_ref[j].astype(BF16), yi_ref[j].astype(BF16)], axis=0)
        fs.append(jnp.dot(mc_ref[...], ys, preferred_element_type=F32).astype(BF16))
    z = jnp.dot(jnp.concatenate(fs, axis=0), wf_ref[...], preferred_element_type=F32)
    gate = mod_ref[2:3, :]
    for j in range(nj):
        o_ref[:, j, :] = x_ref[:, j, :] + gate * z[j * n:(j + 1) * n]


def _ft_ctx_body(x_ref, mod_ref, g_ref, wc_ref, ml_ref, wf_ref, o_ref):
    x = x_ref[...]
    h = _normmod(x, g_ref[...], mod_ref[1:2, :], mod_ref[0:1, :]).astype(BF16)
    u = _channel_dft(h, wc_ref[...])
    us = jnp.concatenate([u[:, :D], u[:, D:]], axis=0)
    f = jnp.dot(ml_ref[...], us, preferred_element_type=F32).astype(BF16)
    o_ref[...] = x + mod_ref[2:3, :] * jnp.dot(f, wf_ref[...], preferred_element_type=F32)


def fourier_layer(x, xc, mod, modc, g, w_f, need_ctx):
    t = x.shape[0]
    n = math.isqrt(t)
    assert n * n == t and n % 16 == 0
    w_f_b = w_f.astype(BF16)
    cw, sw = _dft_mats(FT_GROUP_W)
    wc = jnp.asarray(np.concatenate([cw, -sw], axis=1) / math.sqrt(FT_GROUP_W), dtype=F32).astype(BF16)
    wcspec = _full((FT_GROUP_W, 2 * FT_GROUP_W))
    cn, sn = _dft_mats(n)
    ma = jnp.asarray(np.block([[cn, sn], [-sn, cn]]) / math.sqrt(n), dtype=F32).astype(BF16)
    mc = jnp.asarray(np.concatenate([cn, sn], axis=1) / math.sqrt(n), dtype=F32).astype(BF16)
    nj = 8
    ang = 2.0 * np.pi * np.outer(np.arange(n), np.arange(n)) / t
    def expand(tab):
        a = jnp.asarray(tab, dtype=F32).reshape(n // nj, nj, n).transpose(0, 2, 1)
        return jnp.repeat(a, LANES, axis=2)
    twc, tws = expand(np.cos(ang)), expand(np.sin(ang))
    xblk = pl.BlockSpec((n, nj, D), lambda b: (0, b, 0))
    yblk = pl.BlockSpec((nj, n, D), lambda b: (b, 0, 0))
    tblk = pl.BlockSpec((1, n, nj * LANES), lambda b: (b, 0, 0))
    x3 = x.reshape(n, n, D)
    src = (np.arange(n)[None, :] * nj + np.arange(nj)[:, None]).reshape(-1)
    perm = jnp.asarray(np.eye(n * nj)[src], dtype=BF16)
    yr, yi = pl.pallas_call(
        functools.partial(_ft_a_body, n, nj),
        grid=(n // nj,),
        in_specs=[xblk, _full((ADA_CHUNKS, D)), _full((1, D)), _full((n * nj, n * nj)), wcspec,
                  _full((2 * n, 2 * n)), tblk, tblk],
        out_specs=[xblk, xblk],
        out_shape=[jax.ShapeDtypeStruct((n, n, D), F32)] * 2,
        compiler_params=_cparams(("arbitrary",)),
        name="ft_stage_a",
    )(x3, mod, g, perm, wc, ma, twc, tws)
    x_new = pl.pallas_call(
        functools.partial(_ft_c_body, n, nj),
        grid=(n // nj,),
        in_specs=[yblk, yblk, _full((n, 2 * n)), _full((D, D)), xblk, _full((ADA_CHUNKS, D))],
        out_specs=xblk,
        out_shape=jax.ShapeDtypeStruct((n, n, D), F32),
        compiler_params=_cparams(("arbitrary",)),
        name="ft_stage_c",
    )(yr, yi, mc, w_f_b, x3, mod).reshape(t, D)
    xc_new = None
    if need_ctx:
        lc = xc.shape[0]
        cl, sl = _dft_mats(lc)
        ml = jnp.asarray(np.concatenate([cl, sl], axis=1) / math.sqrt(lc), dtype=F32).astype(BF16)
        xc_new = pl.pallas_call(
            _ft_ctx_body,
            grid=(1,),
            in_specs=[_full((lc, D)), _full((ADA_CHUNKS, D)), _full((1, D)), wcspec,
                      _full((lc, 2 * lc)), _full((D, D))],
            out_specs=_full((lc, D)),
            out_shape=jax.ShapeDtypeStruct((lc, D), F32),
            compiler_params=_cparams(("arbitrary",)),
            name="ft_ctx",
        )(xc, modc, g, wc, ml, w_f_b)
    return x_new, xc_new


def _router_body(x_ref, mod_ref, g_ref, r_ref, info_ref, w0_ref, w1_ref):
    h = _normmod(x_ref[...], g_ref[...], mod_ref[4:5, :], mod_ref[3:4, :])
    hh = h.astype(BF16)
    hl = (h - hh.astype(F32)).astype(BF16)
    r = r_ref[...]
    rh = r.astype(BF16)
    rl = (r - rh.astype(F32)).astype(BF16)
    logits = (jnp.dot(hh, rh, preferred_element_type=F32) + jnp.dot(hh, rl, preferred_element_type=F32)
              + jnp.dot(hl, rh, preferred_element_type=F32))
    lane = lax.broadcasted_iota(jnp.int32, logits.shape, 1)
    logits = jnp.where(lane < N_EXPERTS, logits, NEG_BIG)
    v0 = jnp.max(logits, axis=-1, keepdims=True)
    i0 = jnp.min(jnp.where(logits == v0, lane, LANES), axis=-1, keepdims=True)
    rest = jnp.where(lane == i0, NEG_BIG, logits)
    v1 = jnp.max(rest, axis=-1, keepdims=True)
    i1 = jnp.min(jnp.where(rest == v1, lane, LANES), axis=-1, keepdims=True)
    e = jnp.exp(v1 - v0)
    w0 = 1.0 / (1.0 + e)
    w1 = e / (1.0 + e)
    info_ref[...] = jnp.where(lane == 0, i0, jnp.where(lane == 1, i1, 0))
    w0_ref[...] = jnp.broadcast_to(w0, logits.shape)
    w1_ref[...] = jnp.broadcast_to(w1, logits.shape)


def moe_router(x, mod, g, router_pad, tm):
    t = x.shape[0]
    spec = pl.BlockSpec((tm, D), lambda i: (i, 0))
    lspec = pl.BlockSpec((tm, LANES), lambda i: (i, 0))
    return pl.pallas_call(
        _router_body,
        grid=(t // tm,),
        in_specs=[spec, _full((ADA_CHUNKS, D)), _full((1, D)), _full((D, LANES))],
        out_specs=[lspec, lspec, lspec],
        out_shape=[jax.ShapeDtypeStruct((t, LANES), jnp.int32),
                   jax.ShapeDtypeStruct((t, LANES), F32), jax.ShapeDtypeStruct((t, LANES), F32)],
        compiler_params=_cparams(("arbitrary",)),
        name="moe_router",
    )(x, mod, g, router_pad)


SCATTER_TOKENS = 256


def _row_scatter_body(nlat, nctx, didx_ref, g_ref, x_ref, mod_ref, *rest):
    if nctx:
        xc_ref, modc_ref, dst_ref, h_scr, zero_scr, sem = rest
    else:
        dst_ref, h_scr, zero_scr, sem = rest
    i = pl.program_id(0)
    ts = SCATTER_TOKENS

    def run(src_ref, src_is_zero_rows):
        def issue(grp, c):
            base = pl.multiple_of(grp * SUBLANES, SUBLANES)
            for r in range(SUBLANES):
                src = src_ref.at[pl.ds(r if src_is_zero_rows else base + r, 1), :]
                for half in range(2):
                    d = didx_ref[0, 0, base + r + half * ts]
                    pltpu.make_async_copy(src, dst_ref.at[pl.ds(d, 1), :], sem).start(priority=half)
            return c
        lax.fori_loop(0, ts // SUBLANES, issue, 0)

        def drain(n, c):
            pltpu.make_async_copy(src_ref.at[pl.ds(0, 1), :], dst_ref.at[pl.ds(0, 1), :], sem).wait()
            return c
        lax.fori_loop(0, 2 * ts, drain, 0, unroll=8)

    @pl.when(i == 0)
    def _():
        zero_scr[...] = jnp.zeros_like(zero_scr)

    def stage_and_run(src_ref, m_ref):
        h_scr[...] = _normmod(src_ref[...], g_ref[...], m_ref[4:5, :], m_ref[3:4, :])
        run(h_scr, False)

    @pl.when(i < nlat)
    def _():
        stage_and_run(x_ref, mod_ref)

    if nctx:
        @pl.when(jnp.logical_and(i >= nlat, i < nlat + nctx))
        def _():
            stage_and_run(xc_ref, modc_ref)

    @pl.when(i >= nlat + nctx)
    def _():
        run(zero_scr, True)


def row_scatter(x, xc, mod, modc, g, d0, d1, pad_pos):
    ts = SCATTER_TOKENS
    nlat = x.shape[0] // ts
    nctx = 0 if xc is None else 1
    assert xc is None or xc.shape[0] == ts
    ntok = nlat + nctx
    npad = pad_pos.shape[0] // (2 * ts)
    didx = jnp.concatenate([jnp.concatenate([d0.reshape(ntok, 1, ts), d1.reshape(ntok, 1, ts)], axis=2),
                            pad_pos.reshape(npad, 1, 2 * ts)], axis=0)
    in_specs = [pl.BlockSpec((1, 1, 2 * ts), lambda i: (i, 0, 0), memory_space=pltpu.SMEM),
                _full((1, D)),
                pl.BlockSpec((ts, D), lambda i: (jnp.minimum(i, nlat - 1), 0)),
                _full((ADA_CHUNKS, D))]
    args = [didx, g, x, mod]
    if nctx:
        in_specs += [_full((ts, D)), _full((ADA_CHUNKS, D))]
        args += [xc, modc]
    return pl.pallas_call(
        functools.partial(_row_scatter_body, nlat, nctx),
        grid=(ntok + npad,),
        in_specs=in_specs,
        out_specs=pl.BlockSpec(memory_space=pl.ANY),
        out_shape=jax.ShapeDtypeStruct((2 * ntok * ts + pad_pos.shape[0], D), F32),
        scratch_shapes=[pltpu.VMEM((ts, D), F32), pltpu.VMEM((SUBLANES, D), F32), pltpu.SemaphoreType.DMA(())],
        compiler_params=_cparams(("arbitrary",)),
        name="moe_row_scatter",
    )(*args)


def _moe_ffn_body(te_ref, tv_ref, xg_ref, wg_ref, wu_ref, wd_ref, o_ref):
    i = pl.program_id(0)

    @pl.when(tv_ref[i] > 0)
    def _():
        o_ref[...] = _swiglu_chunks(xg_ref[...].astype(BF16), wg_ref, wu_ref, wd_ref)

    @pl.when(tv_ref[i] == 0)
    def _():
        o_ref[...] = jnp.zeros_like(o_ref)


def moe_ffn(xg, tile_e, tile_v, w_gu, w_down, li, tm):
    p = xg.shape[0]
    grid_spec = pltpu.PrefetchScalarGridSpec(
        num_scalar_prefetch=2,
        grid=(p // tm,),
        in_specs=[
            pl.BlockSpec((tm, D), lambda i, te, tv: (i, 0)),
            pl.BlockSpec((None, None, D, D_FF), lambda i, te, tv: (li, te[i], 0, 0)),
            pl.BlockSpec((None, None, D, D_FF), lambda i, te, tv: (li, te[i], 0, 1)),
            pl.BlockSpec((None, None, D_FF, D), lambda i, te, tv: (li, te[i], 0, 0)),
        ],
        out_specs=pl.BlockSpec((tm, D), lambda i, te, tv: (i, 0)),
    )
    return pl.pallas_call(
        _moe_ffn_body,
        grid_spec=grid_spec,
        out_shape=jax.ShapeDtypeStruct((p, D), F32),
        compiler_params=pltpu.CompilerParams(dimension_semantics=("arbitrary",), vmem_limit_bytes=MOE_VMEM_LIMIT),
        name="moe_ffn",
    )(tile_e, tile_v, xg, w_gu, w_gu, w_down)


def _combine_body(tt, d0_ref, d1_ref, yp_ref, x_ref, mod_ref, w0_ref, w1_ref, o_ref, a_scr, b_scr, sem):
    def issue(grp, c):
        base = pl.multiple_of(grp * SUBLANES, SUBLANES)
        for r in range(SUBLANES):
            n = base + r
            pltpu.make_async_copy(yp_ref.at[pl.ds(d0_ref[0, 0, n], 1), :], a_scr.at[pl.ds(n, 1), :],
                                  sem).start(priority=0)
            pltpu.make_async_copy(yp_ref.at[pl.ds(d1_ref[0, 0, n], 1), :], b_scr.at[pl.ds(n, 1), :],
                                  sem).start(priority=1)
        return c
    lax.fori_loop(0, tt // SUBLANES, issue, 0)

    def drain(n, c):
        pltpu.make_async_copy(yp_ref.at[pl.ds(0, 1), :], a_scr.at[pl.ds(0, 1), :], sem).wait()
        pltpu.make_async_copy(yp_ref.at[pl.ds(0, 1), :], b_scr.at[pl.ds(0, 1), :], sem).wait()
        return c
    lax.fori_loop(0, tt, drain, 0, unroll=8)
    w0 = _lane_tile(w0_ref[...])
    w1 = _lane_tile(w1_ref[...])
    o_ref[...] = x_ref[...] + mod_ref[5:6, :] * (w0 * a_scr[...] + w1 * b_scr[...])


def moe_combine(yp, d0, d1, x, mod, w0b, w1b, tt):
    t = x.shape[0]
    nt = t // tt
    ispec = pl.BlockSpec((1, 1, tt), lambda i: (i, 0, 0), memory_space=pltpu.SMEM)
    spec = pl.BlockSpec((tt, D), lambda i: (i, 0))
    lspec = pl.BlockSpec((tt, LANES), lambda i: (i, 0))
    return pl.pallas_call(
        functools.partial(_combine_body, tt),
        grid=(nt,),
        in_specs=[ispec, ispec, pl.BlockSpec(memory_space=pl.ANY), spec, _full((ADA_CHUNKS, D)), lspec, lspec],
        out_specs=spec,
        out_shape=jax.ShapeDtypeStruct((t, D), F32),
        scratch_shapes=[pltpu.VMEM((tt, D), F32), pltpu.VMEM((tt, D), F32), pltpu.SemaphoreType.DMA(())],
        compiler_params=_cparams(("arbitrary",)),
        name="moe_combine",
    )(d0.reshape(nt, 1, tt), d1.reshape(nt, 1, tt), yp, x, mod, w0b, w1b)


def _route_plan(e0, e1, tm):
    t = e0.shape[0]
    n = 2 * t
    ex = jnp.arange(N_EXPERTS, dtype=jnp.int32)
    oh0 = (e0[:, None] == ex[None, :]).astype(jnp.int32)
    oh1 = (e1[:, None] == ex[None, :]).astype(jnp.int32)
    both = oh0 + oh1
    csum = jnp.cumsum(both, axis=0)
    before = csum - both
    counts = csum[-1]
    padded = ((counts + tm - 1) // tm) * tm
    pad_end = jnp.cumsum(padded)
    pad_off = pad_end - padded
    total = pad_end[-1]
    d0 = jnp.sum(oh0 * (before + pad_off[None, :]), axis=1)
    d1 = jnp.sum(oh1 * (before + oh0 + pad_off[None, :]), axis=1)
    gap = padded - counts
    tail_off = jnp.cumsum(tm - gap) - (tm - gap)
    r = jnp.arange(tm, dtype=jnp.int32)[None, :]
    pad_pos = jnp.where(r < gap[:, None], (pad_off + counts)[:, None] + r,
                        total + tail_off[:, None] + (r - gap[:, None])).reshape(-1)
    ntiles = (n + N_EXPERTS * tm) // tm
    tstart = jnp.arange(ntiles, dtype=jnp.int32) * tm
    tile_v = (tstart < total).astype(jnp.int32)
    tile_e = jnp.sum((jnp.minimum(tstart, total - 1)[:, None] >= pad_end[None, :]).astype(jnp.int32), axis=1)
    return (d0.astype(jnp.int32), d1.astype(jnp.int32), pad_pos.astype(jnp.int32), tile_e.astype(jnp.int32), tile_v)


def moe_layer(x, xc, mod, modc, g, router, w_gu_b, w_down_b, li, need_ctx, tm=512):
    router_pad = jnp.pad(router, ((0, 0), (0, LANES - N_EXPERTS)))
    s = x.shape[0]
    info, w0b, w1b = moe_router(x, mod, g, router_pad, 1024)
    if need_ctx:
        sc = xc.shape[0]
        infoc, w0c, w1c = moe_router(xc, modc, g, router_pad, sc)
        e0 = jnp.concatenate([info[:, 0], infoc[:, 0]])
        e1 = jnp.concatenate([info[:, 1], infoc[:, 1]])
    else:
        e0, e1 = info[:, 0], info[:, 1]
    d0, d1, pad_pos, tile_e, tile_v = _route_plan(e0, e1, tm)
    xg = row_scatter(x, xc if need_ctx else None, mod, modc, g, d0, d1, pad_pos)
    yp = moe_ffn(xg, tile_e, tile_v, w_gu_b, w_down_b, li, tm)
    x_new = moe_combine(yp, d0[:s], d1[:s], x, mod, w0b, w1b, 512)
    xc_new = None
    if need_ctx:
        xc_new = moe_combine(yp, d0[s:], d1[s:], xc, modc, w0c, w1c, sc)
    return x_new, xc_new


def kernel(x, c, ctx, c_ctx, ada_w, ada_b, norm_g, rg_w_in, rg_conv_w, rg_conv_b, rg_wa, rg_ba, rg_wi, rg_bi,
           rg_lambda, rg_w_out, na_w_qkv, na_q_g, na_k_g, na_rpb, na_w_o, ft_w_out, ffn_w_gu, ffn_w_down,
           moe_router, moe_w_gu, moe_w_down):
    depth = ada_w.shape[0]
    assert x.shape[0] == 1 and x.shape[2] == D
    xs = x[0]
    xc = ctx[0]
    mods = ada_modulation(c, c_ctx, ada_w, ada_b)
    ffn_gu_b, ffn_dn_b = ffn_w_gu.astype(BF16), ffn_w_down.astype(BF16)
    moe_gu_b, moe_dn_b = moe_w_gu.astype(BF16), moe_w_down.astype(BF16)
    mix_idx = [0] * N_MIXERS
    dense_idx = 0
    moe_idx = 0
    for layer in range(depth):
        need_ctx = layer != depth - 1
        mod, modc = mods[layer, 0], mods[layer, 1]
        g0 = norm_g[layer, 0][None]
        g1 = norm_g[layer, 1][None]
        kind = layer % N_MIXERS
        j = mix_idx[kind]
        mix_idx[kind] += 1
        if kind == 0:
            xs, xcn = rglru_layer(xs, xc, mod, modc, g0, rg_w_in[j], rg_conv_w[j], rg_conv_b[j], rg_wa[j], rg_wi[j],
                                  rg_ba[j], rg_bi[j], rg_lambda[j], rg_w_out[j], need_ctx)
        elif kind == 1:
            xs, xcn = na_layer(xs, xc, mod, modc, g0, na_w_qkv[j], na_q_g[j], na_k_g[j], na_rpb[j], na_w_o[j],
                               need_ctx)
        else:
            xs, xcn = fourier_layer(xs, xc, mod, modc, g0, ft_w_out[j], need_ctx)
        if need_ctx:
            xc = xcn
        if layer % 2 == 0:
            if need_ctx:
                xc = ffn_dense(xc, modc, g1, ffn_gu_b, ffn_dn_b, dense_idx, xc.shape[0])
            xs = ffn_dense(xs, mod, g1, ffn_gu_b, ffn_dn_b, dense_idx, 512)
            dense_idx += 1
        else:
            xs, xcn = moe_layer(xs, xc, mod, modc, g1, moe_router[moe_idx], moe_gu_b, moe_dn_b, moe_idx, need_ctx)
            moe_idx += 1
            if need_ctx:
                xc = xcn
    return xs[None]
```

```python
import functools
import math

import numpy as np
import jax
import jax.numpy as jnp
from jax import lax
from jax.experimental import pallas as pl
from jax.experimental.pallas import tpu as pltpu

F32 = jnp.float32
BF16 = jnp.bfloat16

D = 1024
D_FF = 3584
N_EXPERTS = 8
GRID_W = 64
NA_HEADS = 16
NA_HEAD_DIM = 64
NA_ROWS = 8
NA_COLS = 16
FT_GROUP_W = 256
RG_BLOCK_W = 256
RMS_EPS = 1e-6
LRU_C = 8.0
N_MIXERS = 3
ADA_CHUNKS = 6

LANES = 128
SUBLANES = 8
VMEM_LIMIT = 56 * 1024 * 1024
MOE_VMEM_LIMIT = 60 * 1024 * 1024
NEG_BIG = -1e30
LOG2E = math.log2(math.e)


def _cparams(sem):
    return pltpu.CompilerParams(dimension_semantics=sem, vmem_limit_bytes=VMEM_LIMIT)


def _full(shape):
    nd = len(shape)
    return pl.BlockSpec(shape, lambda *_: (0,) * nd)


def _normmod(x, g, scale, shift):
    ms = jnp.mean(x * x, axis=-1, keepdims=True)
    y = x * lax.rsqrt(ms + RMS_EPS)
    return (y * g) * (1.0 + scale) + shift


def _lane_tile(v):
    return jnp.concatenate([v] * (D // LANES), axis=1)


def _sigmoid(v):
    return 1.0 / (1.0 + jnp.exp(-v))


def _gelu_tanh(v):
    c = math.sqrt(2.0 / math.pi)
    return v * (0.5 * (1.0 + jnp.tanh(c * (v + 0.044715 * (v * v * v)))))


def _ada_body(cin_ref, w_ref, b_ref, o_ref):
    v = cin_ref[...]
    s = v * _sigmoid(v)
    w = w_ref[0]
    r0 = jnp.sum(s[:, 0:1] * w, axis=0, keepdims=True)
    r1 = jnp.sum(s[:, 1:2] * w, axis=0, keepdims=True)
    o_ref[0] = jnp.concatenate([r0, r1], axis=0) + b_ref[0]


def ada_modulation(c, c_ctx, ada_w, ada_b):
    depth = ada_w.shape[0]
    n = ada_w.shape[2]
    nc = n // 4
    cin = jnp.stack([c[0], c_ctx], axis=1)
    out = pl.pallas_call(
        _ada_body,
        grid=(depth, n // nc),
        in_specs=[
            pl.BlockSpec((D, 2), lambda l, j: (0, 0)),
            pl.BlockSpec((1, D, nc), lambda l, j: (l, 0, j)),
            pl.BlockSpec((1, 1, nc), lambda l, j: (l, 0, j)),
        ],
        out_specs=pl.BlockSpec((1, 2, nc), lambda l, j: (l, 0, j)),
        out_shape=jax.ShapeDtypeStruct((depth, 2, n), F32),
        compiler_params=_cparams(("arbitrary", "arbitrary")),
        name="ada_mod",
    )(cin, ada_w, ada_b.reshape(depth, 1, n))
    return out.reshape(depth, 2, ADA_CHUNKS, D)


FFN_CHUNK = 512


def _swiglu_chunks(h, wg_ref, wu_ref, wd_ref):
    acc = None
    for c in range(D_FF // FFN_CHUNK):
        sl = slice(c * FFN_CHUNK, (c + 1) * FFN_CHUNK)
        gg = jnp.dot(h, wg_ref[:, sl], preferred_element_type=F32)
        uu = jnp.dot(h, wu_ref[:, sl], preferred_element_type=F32)
        a = ((gg * _sigmoid(gg)) * uu).astype(BF16)
        part = jnp.dot(a, wd_ref[sl, :], preferred_element_type=F32)
        acc = part if acc is None else acc + part
    return acc


def _ffn_body(x_ref, mod_ref, g_ref, wg_ref, wu_ref, wd_ref, o_ref):
    x = x_ref[...]
    h = _normmod(x, g_ref[...], mod_ref[4:5, :], mod_ref[3:4, :]).astype(BF16)
    o_ref[...] = x + mod_ref[5:6, :] * _swiglu_chunks(h, wg_ref, wu_ref, wd_ref)


def ffn_dense(x, mod, g, w_gu, w_down, li, tm):
    t = x.shape[0]
    once = pl.Buffered(1)
    return pl.pallas_call(
        _ffn_body,
        grid=(t // tm,),
        in_specs=[
            pl.BlockSpec((tm, D), lambda i: (i, 0)),
            _full((ADA_CHUNKS, D)),
            _full((1, D)),
            pl.BlockSpec((None, D, D_FF), lambda i: (li, 0, 0), pipeline_mode=once),
            pl.BlockSpec((None, D, D_FF), lambda i: (li, 0, 1), pipeline_mode=once),
            pl.BlockSpec((None, D_FF, D), lambda i: (li, 0, 0), pipeline_mode=once),
        ],
        out_specs=pl.BlockSpec((tm, D), lambda i: (i, 0)),
        out_shape=jax.ShapeDtypeStruct((t, D), F32),
        compiler_params=_cparams(("arbitrary",)),
        name="ffn_dense",
    )(x, mod, g, w_gu, w_gu, w_down)


def _proj_body(gate_row, a_ref, w_ref, x_ref, mod_ref, o_ref):
    y = jnp.dot(a_ref[...], w_ref[...], preferred_element_type=F32)
    o_ref[...] = x_ref[...] + mod_ref[gate_row:gate_row + 1, :] * y


def proj_residual(a, w, x, mod, gate_row, tm):
    t, k = a.shape
    return pl.pallas_call(
        functools.partial(_proj_body, gate_row),
        grid=(t // tm,),
        in_specs=[
            pl.BlockSpec((tm, k), lambda i: (i, 0)),
            _full((k, D)),
            pl.BlockSpec((tm, D), lambda i: (i, 0)),
            _full((ADA_CHUNKS, D)),
        ],
        out_specs=pl.BlockSpec((tm, D), lambda i: (i, 0)),
        out_shape=jax.ShapeDtypeStruct((t, D), F32),
        compiler_params=_cparams(("arbitrary",)),
        name="proj_residual",
    )(a, w, x, mod)


HALO = SUBLANES
RG_IN_PIECE = 128


def _rg_in_body(tm, xp_ref, x_ref, xn_ref, mod_ref, g_ref, w_ref, cw_ref, cb_ref, xc_ref, gg_ref):
    i = pl.program_id(0)
    last = pl.num_programs(0) - 1
    xa = jnp.concatenate([xp_ref[...], x_ref[...], xn_ref[...]], axis=0)
    npiece = tm // RG_IN_PIECE
    bounds = [0] + [2 * HALO + RG_IN_PIECE * (k + 1) for k in range(npiece - 1)] + [tm + 2 * HALO]
    zs = []
    for k in range(npiece):
        hk = _normmod(xa[bounds[k]:bounds[k + 1]], g_ref[...], mod_ref[1:2, :], mod_ref[0:1, :]).astype(BF16)
        zs.append(jnp.dot(hk, w_ref[...], preferred_element_type=F32))
    z = jnp.concatenate(zs, axis=0)
    row = lax.broadcasted_iota(jnp.int32, (tm + 2 * HALO, 1), 0)
    valid = jnp.logical_and(jnp.logical_or(row >= HALO, i > 0),
                            jnp.logical_or(row < tm + HALO, i < last))
    xz = jnp.where(valid, z[:, :D], 0.0)
    y = cb_ref[...] + cw_ref[2:3, :] * xz[HALO:HALO + tm]
    y = y + cw_ref[0:1, :] * xz[HALO - 2:HALO - 2 + tm]
    y = y + cw_ref[1:2, :] * xz[HALO - 1:HALO - 1 + tm]
    y = y + cw_ref[3:4, :] * xz[HALO + 1:HALO + 1 + tm]
    xc_ref[...] = y
    gg_ref[...] = _gelu_tanh(z[HALO:HALO + tm, D:]).astype(BF16)


def rg_in(x, mod, g, w_in, conv_w, conv_b, tm):
    t = x.shape[0]
    nb = tm // HALO
    nblk = t // HALO
    return pl.pallas_call(
        functools.partial(_rg_in_body, tm),
        grid=(t // tm,),
        in_specs=[
            pl.BlockSpec((HALO, D), lambda i: (jnp.maximum(i * nb - 1, 0), 0)),
            pl.BlockSpec((tm, D), lambda i: (i, 0)),
            pl.BlockSpec((HALO, D), lambda i: (jnp.minimum((i + 1) * nb, nblk - 1), 0)),
            _full((ADA_CHUNKS, D)),
            _full((1, D)),
            _full((D, 2 * D)),
            _full((4, D)),
            _full((1, D)),
        ],
        out_specs=[pl.BlockSpec((tm, D), lambda i: (i, 0)), pl.BlockSpec((tm, D), lambda i: (i, 0))],
        out_shape=[jax.ShapeDtypeStruct((t, D), F32), jax.ShapeDtypeStruct((t, D), BF16)],
        compiler_params=_cparams(("arbitrary",)),
        name="rg_in",
    )(x, x, x, mod, g, w_in, conv_w, conv_b)


def _rg_gates(xc, wa_ref, wi_ref, ba, bi, lam):
    xb = xc.astype(BF16)
    nblk = D // RG_BLOCK_W
    r = jnp.concatenate([jnp.dot(xb[:, n * RG_BLOCK_W:(n + 1) * RG_BLOCK_W], wa_ref[n],
                                 preferred_element_type=F32) for n in range(nblk)], axis=1)
    ig = jnp.concatenate([jnp.dot(xb[:, n * RG_BLOCK_W:(n + 1) * RG_BLOCK_W], wi_ref[n],
                                  preferred_element_type=F32) for n in range(nblk)], axis=1)
    t_r = jnp.tanh(r + 0.5 * ba)
    t_i = jnp.tanh(ig + 0.5 * bi)
    nl = -lam
    softplus = jnp.maximum(nl, 0.0) + jnp.log1p(jnp.exp(-jnp.abs(nl)))
    half_c = (-0.5 * LRU_C) * softplus
    log_a = half_c + half_c * t_r
    a = jnp.exp(log_a)
    xh = 0.5 * xc
    b = jnp.sqrt(1.0 - a * a) * (xh + xh * t_i)
    return a, b


def _rg_scan_body(reverse, epilogue, emit_h, tc, *refs):
    xc_ref, wa_ref, wi_ref, ba_ref, bi_ref, lam_ref, h0_ref = refs[:7]
    refs = refs[7:]
    if epilogue:
        hf_ref, gg_ref, wo_ref, x_ref, mod_ref = refs[:5]
        refs = refs[5:]
    if emit_h:
        h_ref = refs[0]
        refs = refs[1:]
    if epilogue:
        o_ref = refs[0]
        refs = refs[1:]
    a_scr, b_scr, h_scr, carry_scr = refs
    c = pl.program_id(0)

    @pl.when(c == 0)
    def _():
        carry_scr[...] = jnp.broadcast_to(h0_ref[...], (SUBLANES, D))

    a, b = _rg_gates(xc_ref[...], wa_ref, wi_ref, ba_ref[...], bi_ref[...], lam_ref[...])
    a_scr[...] = a
    b_scr[...] = b
    nblk = tc // SUBLANES
    row = lax.broadcasted_iota(jnp.int32, (SUBLANES, D), 0)
    first = (row == SUBLANES - 1) if reverse else (row == 0)

    def block(n, carry):
        blk = (nblk - 1 - n) if reverse else n
        off = pl.multiple_of(blk * SUBLANES, SUBLANES)
        av = a_scr[pl.ds(off, SUBLANES), :]
        bv = b_scr[pl.ds(off, SUBLANES), :]
        bv = jnp.where(first, av * carry + bv, bv)
        av = jnp.where(first, 0.0, av)
        for k in (1, 2, 4):
            shift = (SUBLANES - k) if reverse else k
            bv = av * pltpu.roll(bv, shift, 0) + bv
            if k != 4:
                av = av * pltpu.roll(av, shift, 0)
        h_scr[pl.ds(off, SUBLANES), :] = bv
        edge = bv[0:1, :] if reverse else bv[SUBLANES - 1:SUBLANES, :]
        return jnp.broadcast_to(edge, (SUBLANES, D))

    carry_scr[...] = lax.fori_loop(0, nblk, block, carry_scr[...], unroll=2)

    if emit_h:
        h_ref[...] = h_scr[...].astype(h_ref.dtype)
    if epilogue:
        y = ((hf_ref[...].astype(F32) + h_scr[...]) * gg_ref[...].astype(F32)).astype(BF16)
        o_ref[...] = x_ref[...] + mod_ref[2:3, :] * jnp.dot(y, wo_ref[...], preferred_element_type=F32)


def rg_scan(xconv, wa, wi, ba, bi, lam, h0, tc, reverse, epi=None, h_dtype=F32):
    t = xconv.shape[0]
    nchunks = t // tc
    idx = (lambda c: (nchunks - 1 - c, 0)) if reverse else (lambda c: (c, 0))
    nb = D // RG_BLOCK_W
    blk = pl.BlockSpec((tc, D), idx)
    in_specs = [
        blk,
        _full((nb, RG_BLOCK_W, RG_BLOCK_W)),
        _full((nb, RG_BLOCK_W, RG_BLOCK_W)),
        _full((1, D)), _full((1, D)), _full((1, D)), _full((1, D)),
    ]
    args = [xconv, wa, wi, ba, bi, lam, h0]
    out_specs = []
    out_shape = []
    if epi is not None:
        hf, gg, w_out, x, mod = epi
        in_specs += [blk, blk, _full((D, D)), blk, _full((ADA_CHUNKS, D))]
        args += [hf, gg, w_out, x, mod]
    if h_dtype is not None:
        out_specs.append(blk)
        out_shape.append(jax.ShapeDtypeStruct((t, D), h_dtype))
    if epi is not None:
        out_specs.append(blk)
        out_shape.append(jax.ShapeDtypeStruct((t, D), F32))
    return pl.pallas_call(
        functools.partial(_rg_scan_body, reverse, epi is not None, h_dtype is not None, tc),
        grid=(nchunks,),
        in_specs=in_specs,
        out_specs=out_specs,
        out_shape=out_shape,
        scratch_shapes=[pltpu.VMEM((tc, D), F32), pltpu.VMEM((tc, D), F32), pltpu.VMEM((tc, D), F32),
                        pltpu.VMEM((SUBLANES, D), F32)],
        compiler_params=_cparams(("arbitrary",)),
        name="rg_scan_bwd" if reverse else "rg_scan_fwd",
    )(*args)


def rglru_layer(x, xc, mod, modc, g, w_in, conv_w, conv_b, wa, wi, ba, bi, lam, w_out, need_ctx):
    w_in_b = w_in.astype(BF16)
    wa_b = (0.5 * wa).astype(BF16)
    wi_b = (0.5 * wi).astype(BF16)
    w_out_b = w_out.astype(BF16)
    cb = conv_b[None]
    tcx = xc.shape[0]
    xcl, ggl = rg_in(x, mod, g, w_in_b, conv_w, cb, 512)
    xcc, ggc = rg_in(xc, modc, g, w_in_b, conv_w, cb, tcx)
    zeros = jnp.zeros((1, D), F32)
    p = lambda d: (wa_b[d], wi_b[d], ba[d][None], bi[d][None], lam[d][None])
    (hcf,) = rg_scan(xcc, *p(0), zeros, tcx, False)
    (hlf,) = rg_scan(xcl, *p(0), hcf[tcx - 1:tcx], 512, False, h_dtype=BF16)
    if need_ctx:
        hcb, xc_new = rg_scan(xcc, *p(1), zeros, tcx, True, epi=(hcf, ggc, w_out_b, xc, modc))
    else:
        (hcb,) = rg_scan(xcc, *p(1), zeros, tcx, True)
        xc_new = None
    (x_new,) = rg_scan(xcl, *p(1), hcb[0:1], 512, True, epi=(hlf, ggl, w_out_b, x, mod), h_dtype=None)
    return x_new, xc_new


def _qkv_body(x_ref, mod_ref, g_ref, w_ref, gm_ref, qg_ref, kg_ref, q_ref, k_ref, v_ref):
    h = _normmod(x_ref[...], g_ref[...], mod_ref[1:2, :], mod_ref[0:1, :]).astype(BF16)
    z = jnp.dot(h, w_ref[...], preferred_element_type=F32)

    def headnorm(v, gain):
        ms = jnp.dot((v * v).astype(BF16), gm_ref[...], preferred_element_type=F32)
        return (v * lax.rsqrt(ms + RMS_EPS)) * gain

    q_ref[...] = headnorm(z[:, :D], qg_ref[...]).astype(BF16)
    k_ref[...] = headnorm(z[:, D:2 * D], kg_ref[...]).astype(BF16)
    v_ref[...] = z[:, 2 * D:].astype(BF16)


def qkv_proj(x, mod, g, w_qkv, gmean, qg, kg, tm):
    t = x.shape[0]
    spec = pl.BlockSpec((tm, D), lambda i: (i, 0))
    return pl.pallas_call(
        _qkv_body,
        grid=(t // tm,),
        in_specs=[spec, _full((ADA_CHUNKS, D)), _full((1, D)), _full((D, 3 * D)), _full((D, D)),
                  _full((1, D)), _full((1, D))],
        out_specs=[spec, spec, spec],
        out_shape=[jax.ShapeDtypeStruct((t, D), BF16)] * 3,
        compiler_params=_cparams(("arbitrary",)),
        name="qkv_proj",
    )(x, mod, g, w_qkv, gmean, qg, kg)


def _attend_pair(q2, keys, vals, biases):
    m_rows = q2.shape[0]
    lane = lax.broadcasted_iota(jnp.int32, q2.shape, 1)
    zero = jnp.zeros_like(q2)
    qs = jnp.concatenate([jnp.where(lane < NA_HEAD_DIM, q2, zero), jnp.where(lane >= NA_HEAD_DIM, q2, zero)], axis=0)
    ss = []
    for kseg, bseg in zip(keys, biases):
        s = lax.dot_general(qs, kseg, (((1,), (1,)), ((), ())), preferred_element_type=F32)
        if bseg is not None:
            s = s + jnp.concatenate([bseg[0], bseg[1]], axis=0)
        ss.append(s)
    m = ss[0].max(axis=-1, keepdims=True)
    for s in ss[1:]:
        m = jnp.maximum(m, s.max(axis=-1, keepdims=True))
    den = None
    acc = None
    for s, vseg in zip(ss, vals):
        p = jnp.exp2(s - m)
        d = jnp.sum(p, axis=-1, keepdims=True)
        o = jnp.dot(p.astype(BF16), vseg, preferred_element_type=F32)
        den = d if den is None else den + d
        acc = o if acc is None else acc + o
    out = acc / den
    return jnp.where(lane < NA_HEAD_DIM, out[:m_rows], out[m_rows:])


NA_QROWS = 2
NA_UNION = NA_ROWS + NA_QROWS - 1


def _na_body(var_ref, q_ref, kl_ref, vl_ref, kc_ref, vc_ref, bias_ref, o_ref):
    for pr in range(NA_HEADS // 2):
        sl = slice(pr * LANES, (pr + 1) * LANES)
        o_ref[:, sl] = _attend_pair(
            q_ref[:, sl], [kl_ref[:, sl], kc_ref[:, sl]], [vl_ref[:, sl], vc_ref[:, sl]],
            [(bias_ref[0, 2 * pr], bias_ref[0, 2 * pr + 1]), None]).astype(BF16)


def _na_geometry(rows):
    steps = rows // NA_QROWS
    g = np.arange(steps)
    base = np.clip(NA_QROWS * g - NA_ROWS // 2, 0, rows - NA_UNION)
    r = NA_QROWS * g[:, None] + np.arange(NA_QROWS)[None, :]
    rs = np.clip(r - NA_ROWS // 2, 0, rows - NA_ROWS)
    key = np.concatenate([(base - NA_QROWS * g)[:, None], rs - r], axis=1)
    uniq, first, var = np.unique(key, axis=0, return_index=True, return_inverse=True)
    return base, var.reshape(-1).astype(np.int32), g[first]


def na_attention(q, k, v, kc, vc, bias_tab, var):
    t = q.shape[0]
    rows = t // GRID_W
    nctx = kc.shape[0]
    steps = rows // NA_QROWS

    def kbase(g):
        return jnp.clip(NA_QROWS * g - NA_ROWS // 2, 0, rows - NA_UNION)

    qrows = NA_QROWS * GRID_W
    nloc = NA_UNION * GRID_W
    kspec = pl.BlockSpec((pl.Element(nloc), pl.Element(D)), lambda g, var: (kbase(g) * GRID_W, 0))
    grid_spec = pltpu.PrefetchScalarGridSpec(
        num_scalar_prefetch=1,
        grid=(steps,),
        in_specs=[pl.BlockSpec((qrows, D), lambda g, var: (g, 0)), kspec, kspec] + [
            pl.BlockSpec((nctx, D), lambda g, var: (0, 0)), pl.BlockSpec((nctx, D), lambda g, var: (0, 0)),
            pl.BlockSpec((1, NA_HEADS, qrows, nloc), lambda g, var: (var[g], 0, 0, 0)),
        ],
        out_specs=pl.BlockSpec((qrows, D), lambda g, var: (g, 0)),
    )
    return pl.pallas_call(
        _na_body,
        grid_spec=grid_spec,
        out_shape=jax.ShapeDtypeStruct((t, D), BF16),
        compiler_params=_cparams(("arbitrary",)),
        name="na_attention",
    )(var, q, k, v, kc, vc, bias_tab)


def _ctx_attn_body(q_ref, k_ref, v_ref, o_ref):
    for pr in range(NA_HEADS // 2):
        sl = slice(pr * LANES, (pr + 1) * LANES)
        o_ref[:, sl] = _attend_pair(q_ref[:, sl], [k_ref[:, sl]], [v_ref[:, sl]], [None]).astype(BF16)


def ctx_attention(q, k, v):
    t = q.shape[0]
    return pl.pallas_call(
        _ctx_attn_body,
        grid=(1,),
        in_specs=[_full((t, D))] * 3,
        out_specs=_full((t, D)),
        out_shape=jax.ShapeDtypeStruct((t, D), BF16),
        compiler_params=_cparams(("arbitrary",)),
        name="ctx_attention",
    )(q, k, v)


def _na_bias_table(rpb, rows):
    base, var, reps = _na_geometry(rows)
    cols = np.arange(GRID_W)
    cstart = np.clip(cols - NA_COLS // 2, 0, GRID_W - NA_COLS)
    kcol = np.arange(GRID_W)
    inwin = (kcol[None, :] >= cstart[:, None]) & (kcol[None, :] < cstart[:, None] + NA_COLS)
    r = NA_QROWS * reps[:, None] + np.arange(NA_QROWS)[None, :]
    rs = np.clip(r - NA_ROWS // 2, 0, rows - NA_ROWS)
    krow = base[reps][:, None] + np.arange(NA_UNION)[None, :]
    rvalid = (krow[:, None, :] >= rs[:, :, None]) & (krow[:, None, :] < rs[:, :, None] + NA_ROWS)
    ridx = np.clip(krow[:, None, :] - r[:, :, None] + (NA_ROWS - 1), 0, 2 * NA_ROWS - 2)
    nd = 2 * NA_COLS - 1
    w = jnp.pad(rpb.astype(F32), ((0, 0), (0, 0), (GRID_W - NA_COLS, 2 * GRID_W - (GRID_W - NA_COLS) - nd)))
    flat = jnp.tile(w, (1, 1, GRID_W))[:, :, :GRID_W * (2 * GRID_W - 1)]
    blk = flat.reshape(NA_HEADS, 2 * NA_ROWS - 1, GRID_W, 2 * GRID_W - 1)[..., GRID_W - 1:]
    blk = jnp.where(jnp.asarray(inwin)[None, None], blk, NEG_BIG)
    neg = jnp.full((NA_HEADS, GRID_W, GRID_W), NEG_BIG, F32)
    variants = []
    for v in range(len(reps)):
        strips = [jnp.concatenate([blk[:, ridx[v, a, j]] if rvalid[v, a, j] else neg for j in range(NA_UNION)], axis=2)
                  for a in range(NA_QROWS)]
        variants.append(jnp.concatenate(strips, axis=1))
    return jnp.stack(variants, axis=0), jnp.asarray(var)


def na_layer(x, xc, mod, modc, g, w_qkv, q_g, k_g, rpb, w_o, need_ctx):
    w_qkv_b = w_qkv.astype(BF16)
    w_o_b = w_o.astype(BF16)
    gm = np.kron(np.eye(NA_HEADS), np.full((NA_HEAD_DIM, NA_HEAD_DIM), 1.0 / NA_HEAD_DIM))
    gmean = jnp.asarray(gm, dtype=BF16)
    qg = jnp.tile(q_g, NA_HEADS)[None] * (NA_HEAD_DIM ** -0.5 * LOG2E)
    kg = jnp.tile(k_g, NA_HEADS)[None]
    q, k, v = qkv_proj(x, mod, g, w_qkv_b, gmean, qg, kg, 512)
    qc, kc, vc = qkv_proj(xc, modc, g, w_qkv_b, gmean, qg, kg, xc.shape[0])
    bias_tab, var = _na_bias_table(rpb * LOG2E, x.shape[0] // GRID_W)
    o = na_attention(q, k, v, kc, vc, bias_tab, var)
    x_new = proj_residual(o, w_o_b, x, mod, 2, 1024)
    xc_new = None
    if need_ctx:
        oc = ctx_attention(qc, kc, vc)
        xc_new = proj_residual(oc, w_o_b, xc, modc, 2, xc.shape[0])
    return x_new, xc_new


def _dft_mats(n):
    ang = 2.0 * np.pi * np.outer(np.arange(n), np.arange(n)) / n
    return np.cos(ang), np.sin(ang)


def _channel_dft(h, wc):
    us = [jnp.dot(h[:, gi * FT_GROUP_W:(gi + 1) * FT_GROUP_W], wc, preferred_element_type=F32).astype(BF16)
          for gi in range(D // FT_GROUP_W)]
    return jnp.concatenate([u[:, :FT_GROUP_W] for u in us] + [u[:, FT_GROUP_W:] for u in us], axis=1)


def _ft_a_body(n, nj, x_ref, mod_ref, g_ref, perm_ref, wc_ref, ma_ref, tc_ref, ts_ref, yr_ref, yi_ref):
    h3 = _normmod(x_ref[...], g_ref[...], mod_ref[1:2, :], mod_ref[0:1, :])
    h = jnp.dot(perm_ref[...], h3.reshape(n * nj, D).astype(BF16), preferred_element_type=F32).astype(BF16)
    u = _channel_dft(h, wc_ref[...])
    for j in range(nj):
        uj = u[j * n:(j + 1) * n]
        y = jnp.dot(ma_ref[...], jnp.concatenate([uj[:, :D], uj[:, D:]], axis=0), preferred_element_type=F32)
        yr, yi = y[:n], y[n:]
        tc = _lane_tile(tc_ref[0, :, j * LANES:(j + 1) * LANES])
        ts = _lane_tile(ts_ref[0, :, j * LANES:(j + 1) * LANES])
        yr_ref[:, j, :] = yr * tc + yi * ts
        yi_ref[:, j, :] = yi * tc - yr * ts


def _ft_c_body(n, nj, yr_ref, yi_ref, mc_ref, wf_ref, x_ref, mod_ref, o_ref):
    fs = []
    for j in range(nj):
        ys = jnp.concatenate([yr_ref[j].astype(BF16), yi_ref[j].astype(BF16)], axis=0)
        fs.append(jnp.dot(mc_ref[...], ys, preferred_element_type=F32).astype(BF16))
    z = jnp.dot(jnp.concatenate(fs, axis=0), wf_ref[...], preferred_element_type=F32)
    gate = mod_ref[2:3, :]
    for j in range(nj):
        o_ref[:, j, :] = x_ref[:, j, :] + gate * z[j * n:(j + 1) * n]


def _ft_ctx_body(x_ref, mod_ref, g_ref, wc_ref, ml_ref, wf_ref, o_ref):
    x = x_ref[...]
    h = _normmod(x, g_ref[...], mod_ref[1:2, :], mod_ref[0:1, :]).astype(BF16)
    u = _channel_dft(h, wc_ref[...])
    us = jnp.concatenate([u[:, :D], u[:, D:]], axis=0)
    f = jnp.dot(ml_ref[...], us, preferred_element_type=F32).astype(BF16)
    o_ref[...] = x + mod_ref[2:3, :] * jnp.dot(f, wf_ref[...], preferred_element_type=F32)


def fourier_layer(x, xc, mod, modc, g, w_f, need_ctx):
    t = x.shape[0]
    n = math.isqrt(t)
    assert n * n == t and n % 16 == 0
    w_f_b = w_f.astype(BF16)
    cw, sw = _dft_mats(FT_GROUP_W)
    wc = jnp.asarray(np.concatenate([cw, -sw], axis=1) / math.sqrt(FT_GROUP_W), dtype=F32).astype(BF16)
    wcspec = _full((FT_GROUP_W, 2 * FT_GROUP_W))
    cn, sn = _dft_mats(n)
    ma = jnp.asarray(np.block([[cn, sn], [-sn, cn]]) / math.sqrt(n), dtype=F32).astype(BF16)
    mc = jnp.asarray(np.concatenate([cn, sn], axis=1) / math.sqrt(n), dtype=F32).astype(BF16)
    nj = 8
    ang = 2.0 * np.pi * np.outer(np.arange(n), np.arange(n)) / t
    def expand(tab):
        a = jnp.asarray(tab, dtype=F32).reshape(n // nj, nj, n).transpose(0, 2, 1)
        return jnp.repeat(a, LANES, axis=2)
    twc, tws = expand(np.cos(ang)), expand(np.sin(ang))
    xblk = pl.BlockSpec((n, nj, D), lambda b: (0, b, 0))
    yblk = pl.BlockSpec((nj, n, D), lambda b: (b, 0, 0))
    tblk = pl.BlockSpec((1, n, nj * LANES), lambda b: (b, 0, 0))
    x3 = x.reshape(n, n, D)
    src = (np.arange(n)[None, :] * nj + np.arange(nj)[:, None]).reshape(-1)
    perm = jnp.asarray(np.eye(n * nj)[src], dtype=BF16)
    yr, yi = pl.pallas_call(
        functools.partial(_ft_a_body, n, nj),
        grid=(n // nj,),
        in_specs=[xblk, _full((ADA_CHUNKS, D)), _full((1, D)), _full((n * nj, n * nj)), wcspec,
                  _full((2 * n, 2 * n)), tblk, tblk],
        out_specs=[xblk, xblk],
        out_shape=[jax.ShapeDtypeStruct((n, n, D), F32)] * 2,
        compiler_params=_cparams(("arbitrary",)),
        name="ft_stage_a",
    )(x3, mod, g, perm, wc, ma, twc, tws)
    x_new = pl.pallas_call(
        functools.partial(_ft_c_body, n, nj),
        grid=(n // nj,),
        in_specs=[yblk, yblk, _full((n, 2 * n)), _full((D, D)), xblk, _full((ADA_CHUNKS, D))],
        out_specs=xblk,
        out_shape=jax.ShapeDtypeStruct((n, n, D), F32),
        compiler_params=_cparams(("arbitrary",)),
        name="ft_stage_c",
    )(yr, yi, mc, w_f_b, x3, mod).reshape(t, D)
    xc_new = None
    if need_ctx:
        lc = xc.shape[0]
        cl, sl = _dft_mats(lc)
        ml = jnp.asarray(np.concatenate([cl, sl], axis=1) / math.sqrt(lc), dtype=F32).astype(BF16)
        xc_new = pl.pallas_call(
            _ft_ctx_body,
            grid=(1,),
            in_specs=[_full((lc, D)), _full((ADA_CHUNKS, D)), _full((1, D)), wcspec,
                      _full((lc, 2 * lc)), _full((D, D))],
            out_specs=_full((lc, D)),
            out_shape=jax.ShapeDtypeStruct((lc, D), F32),
            compiler_params=_cparams(("arbitrary",)),
            name="ft_ctx",
        )(xc, modc, g, wc, ml, w_f_b)
    return x_new, xc_new


def _router_body(x_ref, mod_ref, g_ref, r_ref, info_ref, w0_ref, w1_ref):
    h = _normmod(x_ref[...], g_ref[...], mod_ref[4:5, :], mod_ref[3:4, :])
    hh = h.astype(BF16)
    hl = (h - hh.astype(F32)).astype(BF16)
    r = r_ref[...]
    rh = r.astype(BF16)
    rl = (r - rh.astype(F32)).astype(BF16)
    logits = (jnp.dot(hh, rh, preferred_element_type=F32) + jnp.dot(hh, rl, preferred_element_type=F32)
              + jnp.dot(hl, rh, preferred_element_type=F32))
    lane = lax.broadcasted_iota(jnp.int32, logits.shape, 1)
    logits = jnp.where(lane < N_EXPERTS, logits, NEG_BIG)
    v0 = jnp.max(logits, axis=-1, keepdims=True)
    i0 = jnp.min(jnp.where(logits == v0, lane, LANES), axis=-1, keepdims=True)
    rest = jnp.where(lane == i0, NEG_BIG, logits)
    v1 = jnp.max(rest, axis=-1, keepdims=True)
    i1 = jnp.min(jnp.where(rest == v1, lane, LANES), axis=-1, keepdims=True)
    e = jnp.exp(v1 - v0)
    w0 = 1.0 / (1.0 + e)
    w1 = e / (1.0 + e)
    info_ref[...] = jnp.where(lane == 0, i0, jnp.where(lane == 1, i1, 0))
    w0_ref[...] = jnp.broadcast_to(w0, logits.shape)
    w1_ref[...] = jnp.broadcast_to(w1, logits.shape)


def moe_router(x, mod, g, router_pad, tm):
    t = x.shape[0]
    spec = pl.BlockSpec((tm, D), lambda i: (i, 0))
    lspec = pl.BlockSpec((tm, LANES), lambda i: (i, 0))
    return pl.pallas_call(
        _router_body,
        grid=(t // tm,),
        in_specs=[spec, _full((ADA_CHUNKS, D)), _full((1, D)), _full((D, LANES))],
        out_specs=[lspec, lspec, lspec],
        out_shape=[jax.ShapeDtypeStruct((t, LANES), jnp.int32),
                   jax.ShapeDtypeStruct((t, LANES), F32), jax.ShapeDtypeStruct((t, LANES), F32)],
        compiler_params=_cparams(("arbitrary",)),
        name="moe_router",
    )(x, mod, g, router_pad)


SCATTER_TOKENS = 256


def _row_scatter_body(nlat, nctx, didx_ref, g_ref, x_ref, mod_ref, *rest):
    if nctx:
        xc_ref, modc_ref, dst_ref, h_scr, zero_scr, sem = rest
    else:
        dst_ref, h_scr, zero_scr, sem = rest
    i = pl.program_id(0)
    ts = SCATTER_TOKENS

    def run(src_ref, src_is_zero_rows):
        def issue(grp, c):
            base = pl.multiple_of(grp * SUBLANES, SUBLANES)
            for r in range(SUBLANES):
                src = src_ref.at[pl.ds(r if src_is_zero_rows else base + r, 1), :]
                for half in range(2):
                    d = didx_ref[0, 0, base + r + half * ts]
                    pltpu.make_async_copy(src, dst_ref.at[pl.ds(d, 1), :], sem).start(priority=half)
            return c
        lax.fori_loop(0, ts // SUBLANES, issue, 0)

        def drain(n, c):
            pltpu.make_async_copy(src_ref.at[pl.ds(0, 1), :], dst_ref.at[pl.ds(0, 1), :], sem).wait()
            return c
        lax.fori_loop(0, 2 * ts, drain, 0, unroll=8)

    @pl.when(i == 0)
    def _():
        zero_scr[...] = jnp.zeros_like(zero_scr)

    def stage_and_run(src_ref, m_ref):
        h_scr[...] = _normmod(src_ref[...], g_ref[...], m_ref[4:5, :], m_ref[3:4, :])
        run(h_scr, False)

    @pl.when(i < nlat)
    def _():
        stage_and_run(x_ref, mod_ref)

    if nctx:
        @pl.when(jnp.logical_and(i >= nlat, i < nlat + nctx))
        def _():
            stage_and_run(xc_ref, modc_ref)

    @pl.when(i >= nlat + nctx)
    def _():
        run(zero_scr, True)


def row_scatter(x, xc, mod, modc, g, d0, d1, pad_pos):
    ts = SCATTER_TOKENS
    nlat = x.shape[0] // ts
    nctx = 0 if xc is None else 1
    assert xc is None or xc.shape[0] == ts
    ntok = nlat + nctx
    npad = pad_pos.shape[0] // (2 * ts)
    didx = jnp.concatenate([jnp.concatenate([d0.reshape(ntok, 1, ts), d1.reshape(ntok, 1, ts)], axis=2),
                            pad_pos.reshape(npad, 1, 2 * ts)], axis=0)
    in_specs = [pl.BlockSpec((1, 1, 2 * ts), lambda i: (i, 0, 0), memory_space=pltpu.SMEM),
                _full((1, D)),
                pl.BlockSpec((ts, D), lambda i: (jnp.minimum(i, nlat - 1), 0)),
                _full((ADA_CHUNKS, D))]
    args = [didx, g, x, mod]
    if nctx:
        in_specs += [_full((ts, D)), _full((ADA_CHUNKS, D))]
        args += [xc, modc]
    return pl.pallas_call(
        functools.partial(_row_scatter_body, nlat, nctx),
        grid=(ntok + npad,),
        in_specs=in_specs,
        out_specs=pl.BlockSpec(memory_space=pl.ANY),
        out_shape=jax.ShapeDtypeStruct((2 * ntok * ts + pad_pos.shape[0], D), F32),
        scratch_shapes=[pltpu.VMEM((ts, D), F32), pltpu.VMEM((SUBLANES, D), F32), pltpu.SemaphoreType.DMA(())],
        compiler_params=_cparams(("arbitrary",)),
        name="moe_row_scatter",
    )(*args)


def _moe_ffn_body(te_ref, tv_ref, xg_ref, wg_ref, wu_ref, wd_ref, o_ref):
    i = pl.program_id(0)

    @pl.when(tv_ref[i] > 0)
    def _():
        o_ref[...] = _swiglu_chunks(xg_ref[...].astype(BF16), wg_ref, wu_ref, wd_ref)

    @pl.when(tv_ref[i] == 0)
    def _():
        o_ref[...] = jnp.zeros_like(o_ref)


def moe_ffn(xg, tile_e, tile_v, w_gu, w_down, li, tm):
    p = xg.shape[0]
    grid_spec = pltpu.PrefetchScalarGridSpec(
        num_scalar_prefetch=2,
        grid=(p // tm,),
        in_specs=[
            pl.BlockSpec((tm, D), lambda i, te, tv: (i, 0)),
            pl.BlockSpec((None, None, D, D_FF), lambda i, te, tv: (li, te[i], 0, 0)),
            pl.BlockSpec((None, None, D, D_FF), lambda i, te, tv: (li, te[i], 0, 1)),
            pl.BlockSpec((None, None, D_FF, D), lambda i, te, tv: (li, te[i], 0, 0)),
        ],
        out_specs=pl.BlockSpec((tm, D), lambda i, te, tv: (i, 0)),
    )
    return pl.pallas_call(
        _moe_ffn_body,
        grid_spec=grid_spec,
        out_shape=jax.ShapeDtypeStruct((p, D), F32),
        compiler_params=pltpu.CompilerParams(dimension_semantics=("arbitrary",), vmem_limit_bytes=MOE_VMEM_LIMIT),
        name="moe_ffn",
    )(tile_e, tile_v, xg, w_gu, w_gu, w_down)


def _combine_body(tt, d0_ref, d1_ref, yp_ref, x_ref, mod_ref, w0_ref, w1_ref, o_ref, a_scr, b_scr, sem):
    def issue(grp, c):
        base = pl.multiple_of(grp * SUBLANES, SUBLANES)
        for r in range(SUBLANES):
            n = base + r
            pltpu.make_async_copy(yp_ref.at[pl.ds(d0_ref[0, 0, n], 1), :], a_scr.at[pl.ds(n, 1), :],
                                  sem).start(priority=0)
            pltpu.make_async_copy(yp_ref.at[pl.ds(d1_ref[0, 0, n], 1), :], b_scr.at[pl.ds(n, 1), :],
                                  sem).start(priority=1)
        return c
    lax.fori_loop(0, tt // SUBLANES, issue, 0)

    def drain(n, c):
        pltpu.make_async_copy(yp_ref.at[pl.ds(0, 1), :], a_scr.at[pl.ds(0, 1), :], sem).wait()
        pltpu.make_async_copy(yp_ref.at[pl.ds(0, 1), :], b_scr.at[pl.ds(0, 1), :], sem).wait()
        return c
    lax.fori_loop(0, tt, drain, 0, unroll=8)
    w0 = _lane_tile(w0_ref[...])
    w1 = _lane_tile(w1_ref[...])
    o_ref[...] = x_ref[...] + mod_ref[5:6, :] * (w0 * a_scr[...] + w1 * b_scr[...])


def moe_combine(yp, d0, d1, x, mod, w0b, w1b, tt):
    t = x.shape[0]
    nt = t // tt
    ispec = pl.BlockSpec((1, 1, tt), lambda i: (i, 0, 0), memory_space=pltpu.SMEM)
    spec = pl.BlockSpec((tt, D), lambda i: (i, 0))
    lspec = pl.BlockSpec((tt, LANES), lambda i: (i, 0))
    return pl.pallas_call(
        functools.partial(_combine_body, tt),
        grid=(nt,),
        in_specs=[ispec, ispec, pl.BlockSpec(memory_space=pl.ANY), spec, _full((ADA_CHUNKS, D)), lspec, lspec],
        out_specs=spec,
        out_shape=jax.ShapeDtypeStruct((t, D), F32),
        scratch_shapes=[pltpu.VMEM((tt, D), F32), pltpu.VMEM((tt, D), F32), pltpu.SemaphoreType.DMA(())],
        compiler_params=_cparams(("arbitrary",)),
        name="moe_combine",
    )(d0.reshape(nt, 1, tt), d1.reshape(nt, 1, tt), yp, x, mod, w0b, w1b)


def _route_plan(e0, e1, tm):
    t = e0.shape[0]
    n = 2 * t
    ex = jnp.arange(N_EXPERTS, dtype=jnp.int32)
    oh0 = (e0[:, None] == ex[None, :]).astype(jnp.int32)
    oh1 = (e1[:, None] == ex[None, :]).astype(jnp.int32)
    both = oh0 + oh1
    csum = jnp.cumsum(both, axis=0)
    before = csum - both
    counts = csum[-1]
    padded = ((counts + tm - 1) // tm) * tm
    pad_end = jnp.cumsum(padded)
    pad_off = pad_end - padded
    total = pad_end[-1]
    d0 = jnp.sum(oh0 * (before + pad_off[None, :]), axis=1)
    d1 = jnp.sum(oh1 * (before + oh0 + pad_off[None, :]), axis=1)
    gap = padded - counts
    tail_off = jnp.cumsum(tm - gap) - (tm - gap)
    r = jnp.arange(tm, dtype=jnp.int32)[None, :]
    pad_pos = jnp.where(r < gap[:, None], (pad_off + counts)[:, None] + r,
                        total + tail_off[:, None] + (r - gap[:, None])).reshape(-1)
    ntiles = (n + N_EXPERTS * tm) // tm
    tstart = jnp.arange(ntiles, dtype=jnp.int32) * tm
    tile_v = (tstart < total).astype(jnp.int32)
    tile_e = jnp.sum((jnp.minimum(tstart, total - 1)[:, None] >= pad_end[None, :]).astype(jnp.int32), axis=1)
    return (d0.astype(jnp.int32), d1.astype(jnp.int32), pad_pos.astype(jnp.int32), tile_e.astype(jnp.int32), tile_v)


def moe_layer(x, xc, mod, modc, g, router, w_gu_b, w_down_b, li, need_ctx, tm=512):
    router_pad = jnp.pad(router, ((0, 0), (0, LANES - N_EXPERTS)))
    s = x.shape[0]
    info, w0b, w1b = moe_router(x, mod, g, router_pad, 1024)
    if need_ctx:
        sc = xc.shape[0]
        infoc, w0c, w1c = moe_router(xc, modc, g, router_pad, sc)
        e0 = jnp.concatenate([info[:, 0], infoc[:, 0]])
        e1 = jnp.concatenate([info[:, 1], infoc[:, 1]])
    else:
        e0, e1 = info[:, 0], info[:, 1]
    d0, d1, pad_pos, tile_e, tile_v = _route_plan(e0, e1, tm)
    xg = row_scatter(x, xc if need_ctx else None, mod, modc, g, d0, d1, pad_pos)
    yp = moe_ffn(xg, tile_e, tile_v, w_gu_b, w_down_b, li, tm)
    x_new = moe_combine(yp, d0[:s], d1[:s], x, mod, w0b, w1b, 512)
    xc_new = None
    if need_ctx:
        xc_new = moe_combine(yp, d0[s:], d1[s:], xc, modc, w0c, w1c, sc)
    return x_new, xc_new


def kernel(x, c, ctx, c_ctx, ada_w, ada_b, norm_g, rg_w_in, rg_conv_w, rg_conv_b, rg_wa, rg_ba, rg_wi, rg_bi,
           rg_lambda, rg_w_out, na_w_qkv, na_q_g, na_k_g, na_rpb, na_w_o, ft_w_out, ffn_w_gu, ffn_w_down,
           moe_router, moe_w_gu, moe_w_down):
    depth = ada_w.shape[0]
    assert x.shape[0] == 1 and x.shape[2] == D
    xs = x[0]
    xc = ctx[0]
    mods = ada_modulation(c, c_ctx, ada_w, ada_b)
    ffn_gu_b, ffn_dn_b = ffn_w_gu.astype(BF16), ffn_w_down.astype(BF16)
    moe_gu_b, moe_dn_b = moe_w_gu.astype(BF16), moe_w_down.astype(BF16)
    mix_idx = [0] * N_MIXERS
    dense_idx = 0
    moe_idx = 0
    for layer in range(depth):
        need_ctx = layer != depth - 1
        mod, modc = mods[layer, 0], mods[layer, 1]
        g0 = norm_g[layer, 0][None]
        g1 = norm_g[layer, 1][None]
        kind = layer % N_MIXERS
        j = mix_idx[kind]
        mix_idx[kind] += 1
        if kind == 0:
            xs, xcn = rglru_layer(xs, xc, mod, modc, g0, rg_w_in[j], rg_conv_w[j], rg_conv_b[j], rg_wa[j], rg_wi[j],
                                  rg_ba[j], rg_bi[j], rg_lambda[j], rg_w_out[j], need_ctx)
        elif kind == 1:
            xs, xcn = na_layer(xs, xc, mod, modc, g0, na_w_qkv[j], na_q_g[j], na_k_g[j], na_rpb[j], na_w_o[j],
                               need_ctx)
        else:
            xs, xcn = fourier_layer(xs, xc, mod, modc, g0, ft_w_out[j], need_ctx)
        if need_ctx:
            xc = xcn
        if layer % 2 == 0:
            if need_ctx:
                xc = ffn_dense(xc, modc, g1, ffn_gu_b, ffn_dn_b, dense_idx, xc.shape[0])
            xs = ffn_dense(xs, mod, g1, ffn_gu_b, ffn_dn_b, dense_idx, 512)
            dense_idx += 1
        else:
            xs, xcn = moe_layer(xs, xc, mod, modc, g1, moe_router[moe_idx], moe_gu_b, moe_dn_b, moe_idx, need_ctx)
            moe_idx += 1
            if need_ctx:
                xc = xcn
    return xs[None]
```

```python
import functools
import math

import numpy as np
import jax
import jax.numpy as jnp
from jax import lax
from jax.experimental import pallas as pl
from jax.experimental.pallas import tpu as pltpu

F32 = jnp.float32
BF16 = jnp.bfloat16

D = 1024
D_FF = 3584
N_EXPERTS = 8
GRID_W = 64
NA_HEADS = 16
NA_HEAD_DIM = 64
NA_ROWS = 8
NA_COLS = 16
FT_GROUP_W = 256
RG_BLOCK_W = 256
RMS_EPS = 1e-6
LRU_C = 8.0
N_MIXERS = 3
ADA_CHUNKS = 6

LANES = 128
SUBLANES = 8
VMEM_LIMIT = 56 * 1024 * 1024
MOE_VMEM_LIMIT = 60 * 1024 * 1024
NEG_BIG = -1e30
LOG2E = math.log2(math.e)


def _cparams(sem):
    return pltpu.CompilerParams(dimension_semantics=sem, vmem_limit_bytes=VMEM_LIMIT)


def _full(shape):
    nd = len(shape)
    return pl.BlockSpec(shape, lambda *_: (0,) * nd)


def _normmod(x, g, scale, shift):
    ms = jnp.mean(x * x, axis=-1, keepdims=True)
    y = x * lax.rsqrt(ms + RMS_EPS)
    return (y * g) * (1.0 + scale) + shift


def _lane_tile(v):
    return jnp.concatenate([v] * (D // LANES), axis=1)


def _sigmoid(v):
    return 1.0 / (1.0 + jnp.exp(-v))


def _gelu_tanh(v):
    c = math.sqrt(2.0 / math.pi)
    return v * (0.5 * (1.0 + jnp.tanh(c * (v + 0.044715 * (v * v * v)))))


def _ada_body(cin_ref, w_ref, b_ref, o_ref):
    v = cin_ref[...]
    s = v * _sigmoid(v)
    w = w_ref[0]
    r0 = jnp.sum(s[:, 0:1] * w, axis=0, keepdims=True)
    r1 = jnp.sum(s[:, 1:2] * w, axis=0, keepdims=True)
    o_ref[0] = jnp.concatenate([r0, r1], axis=0) + b_ref[0]


def ada_modulation(c, c_ctx, ada_w, ada_b):
    depth = ada_w.shape[0]
    n = ada_w.shape[2]
    nc = n // 4
    cin = jnp.stack([c[0], c_ctx], axis=1)
    out = pl.pallas_call(
        _ada_body,
        grid=(depth, n // nc),
        in_specs=[
            pl.BlockSpec((D, 2), lambda l, j: (0, 0)),
            pl.BlockSpec((1, D, nc), lambda l, j: (l, 0, j)),
            pl.BlockSpec((1, 1, nc), lambda l, j: (l, 0, j)),
        ],
        out_specs=pl.BlockSpec((1, 2, nc), lambda l, j: (l, 0, j)),
        out_shape=jax.ShapeDtypeStruct((depth, 2, n), F32),
        compiler_params=_cparams(("arbitrary", "arbitrary")),
        name="ada_mod",
    )(cin, ada_w, ada_b.reshape(depth, 1, n))
    return out.reshape(depth, 2, ADA_CHUNKS, D)


FFN_CHUNK = 512


def _swiglu_chunks(h, wg_ref, wu_ref, wd_ref):
    acc = None
    for c in range(D_FF // FFN_CHUNK):
        sl = slice(c * FFN_CHUNK, (c + 1) * FFN_CHUNK)
        gg = jnp.dot(h, wg_ref[:, sl], preferred_element_type=F32)
        uu = jnp.dot(h, wu_ref[:, sl], preferred_element_type=F32)
        a = ((gg * _sigmoid(gg)) * uu).astype(BF16)
        part = jnp.dot(a, wd_ref[sl, :], preferred_element_type=F32)
        acc = part if acc is None else acc + part
    return acc


def _ffn_body(x_ref, mod_ref, g_ref, wg_ref, wu_ref, wd_ref, o_ref):
    x = x_ref[...]
    h = _normmod(x, g_ref[...], mod_ref[4:5, :], mod_ref[3:4, :]).astype(BF16)
    o_ref[...] = x + mod_ref[5:6, :] * _swiglu_chunks(h, wg_ref, wu_ref, wd_ref)


def ffn_dense(x, mod, g, w_gu, w_down, li, tm):
    t = x.shape[0]
    once = pl.Buffered(1)
    return pl.pallas_call(
        _ffn_body,
        grid=(t // tm,),
        in_specs=[
            pl.BlockSpec((tm, D), lambda i: (i, 0)),
            _full((ADA_CHUNKS, D)),
            _full((1, D)),
            pl.BlockSpec((None, D, D_FF), lambda i: (li, 0, 0), pipeline_mode=once),
            pl.BlockSpec((None, D, D_FF), lambda i: (li, 0, 1), pipeline_mode=once),
            pl.BlockSpec((None, D_FF, D), lambda i: (li, 0, 0), pipeline_mode=once),
        ],
        out_specs=pl.BlockSpec((tm, D), lambda i: (i, 0)),
        out_shape=jax.ShapeDtypeStruct((t, D), F32),
        compiler_params=_cparams(("arbitrary",)),
        name="ffn_dense",
    )(x, mod, g, w_gu, w_gu, w_down)


def _proj_body(gate_row, a_ref, w_ref, x_ref, mod_ref, o_ref):
    y = jnp.dot(a_ref[...], w_ref[...], preferred_element_type=F32)
    o_ref[...] = x_ref[...] + mod_ref[gate_row:gate_row + 1, :] * y


def proj_residual(a, w, x, mod, gate_row, tm):
    t, k = a.shape
    return pl.pallas_call(
        functools.partial(_proj_body, gate_row),
        grid=(t // tm,),
        in_specs=[
            pl.BlockSpec((tm, k), lambda i: (i, 0)),
            _full((k, D)),
            pl.BlockSpec((tm, D), lambda i: (i, 0)),
            _full((ADA_CHUNKS, D)),
        ],
        out_specs=pl.BlockSpec((tm, D), lambda i: (i, 0)),
        out_shape=jax.ShapeDtypeStruct((t, D), F32),
        compiler_params=_cparams(("arbitrary",)),
        name="proj_residual",
    )(a, w, x, mod)


HALO = SUBLANES
RG_IN_PIECE = 128


def _rg_in_body(tm, xp_ref, x_ref, xn_ref, mod_ref, g_ref, w_ref, cw_ref, cb_ref, xc_ref, gg_ref):
    i = pl.program_id(0)
    last = pl.num_programs(0) - 1
    xa = jnp.concatenate([xp_ref[...], x_ref[...], xn_ref[...]], axis=0)
    npiece = tm // RG_IN_PIECE
    bounds = [0] + [2 * HALO + RG_IN_PIECE * (k + 1) for k in range(npiece - 1)] + [tm + 2 * HALO]
    zs = []
    for k in range(npiece):
        hk = _normmod(xa[bounds[k]:bounds[k + 1]], g_ref[...], mod_ref[1:2, :], mod_ref[0:1, :]).astype(BF16)
        zs.append(jnp.dot(hk, w_ref[...], preferred_element_type=F32))
    z = jnp.concatenate(zs, axis=0)
    row = lax.broadcasted_iota(jnp.int32, (tm + 2 * HALO, 1), 0)
    valid = jnp.logical_and(jnp.logical_or(row >= HALO, i > 0),
                            jnp.logical_or(row < tm + HALO, i < last))
    xz = jnp.where(valid, z[:, :D], 0.0)
    y = cb_ref[...] + cw_ref[2:3, :] * xz[HALO:HALO + tm]
    y = y + cw_ref[0:1, :] * xz[HALO - 2:HALO - 2 + tm]
    y = y + cw_ref[1:2, :] * xz[HALO - 1:HALO - 1 + tm]
    y = y + cw_ref[3:4, :] * xz[HALO + 1:HALO + 1 + tm]
    xc_ref[...] = y
    gg_ref[...] = _gelu_tanh(z[HALO:HALO + tm, D:]).astype(BF16)


def rg_in(x, mod, g, w_in, conv_w, conv_b, tm):
    t = x.shape[0]
    nb = tm // HALO
    nblk = t // HALO
    return pl.pallas_call(
        functools.partial(_rg_in_body, tm),
        grid=(t // tm,),
        in_specs=[
            pl.BlockSpec((HALO, D), lambda i: (jnp.maximum(i * nb - 1, 0), 0)),
            pl.BlockSpec((tm, D), lambda i: (i, 0)),
            pl.BlockSpec((HALO, D), lambda i: (jnp.minimum((i + 1) * nb, nblk - 1), 0)),
            _full((ADA_CHUNKS, D)),
            _full((1, D)),
            _full((D, 2 * D)),
            _full((4, D)),
            _full((1, D)),
        ],
        out_specs=[pl.BlockSpec((tm, D), lambda i: (i, 0)), pl.BlockSpec((tm, D), lambda i: (i, 0))],
        out_shape=[jax.ShapeDtypeStruct((t, D), F32), jax.ShapeDtypeStruct((t, D), BF16)],
        compiler_params=_cparams(("arbitrary",)),
        name="rg_in",
    )(x, x, x, mod, g, w_in, conv_w, conv_b)


def _rg_gates(xc, wa_ref, wi_ref, ba, bi, lam):
    xb = xc.astype(BF16)
    nblk = D // RG_BLOCK_W
    r = jnp.concatenate([jnp.dot(xb[:, n * RG_BLOCK_W:(n + 1) * RG_BLOCK_W], wa_ref[n],
                                 preferred_element_type=F32) for n in range(nblk)], axis=1)
    ig = jnp.concatenate([jnp.dot(xb[:, n * RG_BLOCK_W:(n + 1) * RG_BLOCK_W], wi_ref[n],
                                  preferred_element_type=F32) for n in range(nblk)], axis=1)
    t_r = jnp.tanh(r + 0.5 * ba)
    t_i = jnp.tanh(ig + 0.5 * bi)
    nl = -lam
    softplus = jnp.maximum(nl, 0.0) + jnp.log1p(jnp.exp(-jnp.abs(nl)))
    half_c = (-0.5 * LRU_C) * softplus
    log_a = half_c + half_c * t_r
    a = jnp.exp(log_a)
    xh = 0.5 * xc
    b = jnp.sqrt(1.0 - a * a) * (xh + xh * t_i)
    return a, b


def _rg_scan_body(reverse, epilogue, emit_h, tc, *refs):
    xc_ref, wa_ref, wi_ref, ba_ref, bi_ref, lam_ref, h0_ref = refs[:7]
    refs = refs[7:]
    if epilogue:
        hf_ref, gg_ref, wo_ref, x_ref, mod_ref = refs[:5]
        refs = refs[5:]
    if emit_h:
        h_ref = refs[0]
        refs = refs[1:]
    if epilogue:
        o_ref = refs[0]
        refs = refs[1:]
    a_scr, b_scr, h_scr, carry_scr = refs
    c = pl.program_id(0)

    @pl.when(c == 0)
    def _():
        carry_scr[...] = jnp.broadcast_to(h0_ref[...], (SUBLANES, D))

    a, b = _rg_gates(xc_ref[...], wa_ref, wi_ref, ba_ref[...], bi_ref[...], lam_ref[...])
    a_scr[...] = a
    b_scr[...] = b
    nblk = tc // SUBLANES
    row = lax.broadcasted_iota(jnp.int32, (SUBLANES, D), 0)
    first = (row == SUBLANES - 1) if reverse else (row == 0)

    def block(n, carry):
        blk = (nblk - 1 - n) if reverse else n
        off = pl.multiple_of(blk * SUBLANES, SUBLANES)
        av = a_scr[pl.ds(off, SUBLANES), :]
        bv = b_scr[pl.ds(off, SUBLANES), :]
        bv = jnp.where(first, av * carry + bv, bv)
        av = jnp.where(first, 0.0, av)
        for k in (1, 2, 4):
            shift = (SUBLANES - k) if reverse else k
            bv = av * pltpu.roll(bv, shift, 0) + bv
            if k != 4:
                av = av * pltpu.roll(av, shift, 0)
        h_scr[pl.ds(off, SUBLANES), :] = bv
        edge = bv[0:1, :] if reverse else bv[SUBLANES - 1:SUBLANES, :]
        return jnp.broadcast_to(edge, (SUBLANES, D))

    carry_scr[...] = lax.fori_loop(0, nblk, block, carry_scr[...], unroll=2)

    if emit_h:
        h_ref[...] = h_scr[...].astype(h_ref.dtype)
    if epilogue:
        y = ((hf_ref[...].astype(F32) + h_scr[...]) * gg_ref[...].astype(F32)).astype(BF16)
        o_ref[...] = x_ref[...] + mod_ref[2:3, :] * jnp.dot(y, wo_ref[...], preferred_element_type=F32)


def rg_scan(xconv, wa, wi, ba, bi, lam, h0, tc, reverse, epi=None, h_dtype=F32):
    t = xconv.shape[0]
    nchunks = t // tc
    idx = (lambda c: (nchunks - 1 - c, 0)) if reverse else (lambda c: (c, 0))
    nb = D // RG_BLOCK_W
    blk = pl.BlockSpec((tc, D), idx)
    in_specs = [
        blk,
        _full((nb, RG_BLOCK_W, RG_BLOCK_W)),
        _full((nb, RG_BLOCK_W, RG_BLOCK_W)),
        _full((1, D)), _full((1, D)), _full((1, D)), _full((1, D)),
    ]
    args = [xconv, wa, wi, ba, bi, lam, h0]
    out_specs = []
    out_shape = []
    if epi is not None:
        hf, gg, w_out, x, mod = epi
        in_specs += [blk, blk, _full((D, D)), blk, _full((ADA_CHUNKS, D))]
        args += [hf, gg, w_out, x, mod]
    if h_dtype is not None:
        out_specs.append(blk)
        out_shape.append(jax.ShapeDtypeStruct((t, D), h_dtype))
    if epi is not None:
        out_specs.append(blk)
        out_shape.append(jax.ShapeDtypeStruct((t, D), F32))
    return pl.pallas_call(
        functools.partial(_rg_scan_body, reverse, epi is not None, h_dtype is not None, tc),
        grid=(nchunks,),
        in_specs=in_specs,
        out_specs=out_specs,
        out_shape=out_shape,
        scratch_shapes=[pltpu.VMEM((tc, D), F32), pltpu.VMEM((tc, D), F32), pltpu.VMEM((tc, D), F32),
                        pltpu.VMEM((SUBLANES, D), F32)],
        compiler_params=_cparams(("arbitrary",)),
        name="rg_scan_bwd" if reverse else "rg_scan_fwd",
    )(*args)


def rglru_layer(x, xc, mod, modc, g, w_in, conv_w, conv_b, wa, wi, ba, bi, lam, w_out, need_ctx):
    w_in_b = w_in.astype(BF16)
    wa_b = (0.5 * wa).astype(BF16)
    wi_b = (0.5 * wi).astype(BF16)
    w_out_b = w_out.astype(BF16)
    cb = conv_b[None]
    tcx = xc.shape[0]
    xcl, ggl = rg_in(x, mod, g, w_in_b, conv_w, cb, 512)
    xcc, ggc = rg_in(xc, modc, g, w_in_b, conv_w, cb, tcx)
    zeros = jnp.zeros((1, D), F32)
    p = lambda d: (wa_b[d], wi_b[d], ba[d][None], bi[d][None], lam[d][None])
    (hcf,) = rg_scan(xcc, *p(0), zeros, tcx, False)
    (hlf,) = rg_scan(xcl, *p(0), hcf[tcx - 1:tcx], 512, False, h_dtype=BF16)
    if need_ctx:
        hcb, xc_new = rg_scan(xcc, *p(1), zeros, tcx, True, epi=(hcf, ggc, w_out_b, xc, modc))
    else:
        (hcb,) = rg_scan(xcc, *p(1), zeros, tcx, True)
        xc_new = None
    (x_new,) = rg_scan(xcl, *p(1), hcb[0:1], 512, True, epi=(hlf, ggl, w_out_b, x, mod), h_dtype=None)
    return x_new, xc_new


def _qkv_body(x_ref, mod_ref, g_ref, w_ref, gm_ref, qg_ref, kg_ref, q_ref, k_ref, v_ref):
    h = _normmod(x_ref[...], g_ref[...], mod_ref[1:2, :], mod_ref[0:1, :]).astype(BF16)
    z = jnp.dot(h, w_ref[...], preferred_element_type=F32)

    def headnorm(v, gain):
        ms = jnp.dot((v * v).astype(BF16), gm_ref[...], preferred_element_type=F32)
        return (v * lax.rsqrt(ms + RMS_EPS)) * gain

    q_ref[...] = headnorm(z[:, :D], qg_ref[...]).astype(BF16)
    k_ref[...] = headnorm(z[:, D:2 * D], kg_ref[...]).astype(BF16)
    v_ref[...] = z[:, 2 * D:].astype(BF16)


def qkv_proj(x, mod, g, w_qkv, gmean, qg, kg, tm):
    t = x.shape[0]
    spec = pl.BlockSpec((tm, D), lambda i: (i, 0))
    return pl.pallas_call(
        _qkv_body,
        grid=(t // tm,),
        in_specs=[spec, _full((ADA_CHUNKS, D)), _full((1, D)), _full((D, 3 * D)), _full((D, D)),
                  _full((1, D)), _full((1, D))],
        out_specs=[spec, spec, spec],
        out_shape=[jax.ShapeDtypeStruct((t, D), BF16)] * 3,
        compiler_params=_cparams(("arbitrary",)),
        name="qkv_proj",
    )(x, mod, g, w_qkv, gmean, qg, kg)


def _attend_pair(q2, keys, vals, biases):
    m_rows = q2.shape[0]
    lane = lax.broadcasted_iota(jnp.int32, q2.shape, 1)
    zero = jnp.zeros_like(q2)
    qs = jnp.concatenate([jnp.where(lane < NA_HEAD_DIM, q2, zero), jnp.where(lane >= NA_HEAD_DIM, q2, zero)], axis=0)
    ss = []
    for kseg, bseg in zip(keys, biases):
        s = lax.dot_general(qs, kseg, (((1,), (1,)), ((), ())), preferred_element_type=F32)
        if bseg is not None:
            s = s + jnp.concatenate([bseg[0], bseg[1]], axis=0)
        ss.append(s)
    m = ss[0].max(axis=-1, keepdims=True)
    for s in ss[1:]:
        m = jnp.maximum(m, s.max(axis=-1, keepdims=True))
    den = None
    acc = None
    for s, vseg in zip(ss, vals):
        p = jnp.exp2(s - m)
        d = jnp.sum(p, axis=-1, keepdims=True)
        o = jnp.dot(p.astype(BF16), vseg, preferred_element_type=F32)
        den = d if den is None else den + d
        acc = o if acc is None else acc + o
    out = acc / den
    return jnp.where(lane < NA_HEAD_DIM, out[:m_rows], out[m_rows:])


NA_QROWS = 2
NA_UNION = NA_ROWS + NA_QROWS - 1


def _na_body(nside, var_ref, q_ref, kl_ref, vl_ref, kc_ref, vc_ref, bias_ref, *refs):
    side_in, o_ref, side_out = refs[:nside], refs[nside], refs[nside + 1:]
    for src, dst in zip(side_in, side_out):
        dst[...] = src[...].astype(BF16)
    for pr in range(NA_HEADS // 2):
        sl = slice(pr * LANES, (pr + 1) * LANES)
        o_ref[:, sl] = _attend_pair(
            q_ref[:, sl], [kl_ref[:, sl], kc_ref[:, sl]], [vl_ref[:, sl], vc_ref[:, sl]],
            [(bias_ref[0, 2 * pr], bias_ref[0, 2 * pr + 1]), None]).astype(BF16)


def _na_geometry(rows):
    steps = rows // NA_QROWS
    g = np.arange(steps)
    base = np.clip(NA_QROWS * g - NA_ROWS // 2, 0, rows - NA_UNION)
    r = NA_QROWS * g[:, None] + np.arange(NA_QROWS)[None, :]
    rs = np.clip(r - NA_ROWS // 2, 0, rows - NA_ROWS)
    key = np.concatenate([(base - NA_QROWS * g)[:, None], rs - r], axis=1)
    uniq, first, var = np.unique(key, axis=0, return_index=True, return_inverse=True)
    return base, var.reshape(-1).astype(np.int32), g[first]


def na_attention(q, k, v, kc, vc, bias_tab, var, side=()):
    t = q.shape[0]
    rows = t // GRID_W
    nctx = kc.shape[0]
    steps = rows // NA_QROWS
    side_specs = []
    for a in side:
        assert a.shape[0] % (steps * 16) == 0
        side_specs.append(pl.BlockSpec((a.shape[0] // steps, a.shape[1]), lambda g, var: (g, 0)))

    def kbase(g):
        return jnp.clip(NA_QROWS * g - NA_ROWS // 2, 0, rows - NA_UNION)

    qrows = NA_QROWS * GRID_W
    nloc = NA_UNION * GRID_W
    kspec = pl.BlockSpec((pl.Element(nloc), pl.Element(D)), lambda g, var: (kbase(g) * GRID_W, 0))
    grid_spec = pltpu.PrefetchScalarGridSpec(
        num_scalar_prefetch=1,
        grid=(steps,),
        in_specs=[pl.BlockSpec((qrows, D), lambda g, var: (g, 0)), kspec, kspec] + [
            pl.BlockSpec((nctx, D), lambda g, var: (0, 0)), pl.BlockSpec((nctx, D), lambda g, var: (0, 0)),
            pl.BlockSpec((1, NA_HEADS, qrows, nloc), lambda g, var: (var[g], 0, 0, 0)),
        ] + side_specs,
        out_specs=[pl.BlockSpec((qrows, D), lambda g, var: (g, 0))] + side_specs,
    )
    return pl.pallas_call(
        functools.partial(_na_body, len(side)),
        grid_spec=grid_spec,
        out_shape=[jax.ShapeDtypeStruct((t, D), BF16)] + [jax.ShapeDtypeStruct(a.shape, BF16) for a in side],
        compiler_params=_cparams(("arbitrary",)),
        name="na_attention",
    )(var, q, k, v, kc, vc, bias_tab, *side)


def _ctx_attn_body(q_ref, k_ref, v_ref, o_ref):
    for pr in range(NA_HEADS // 2):
        sl = slice(pr * LANES, (pr + 1) * LANES)
        o_ref[:, sl] = _attend_pair(q_ref[:, sl], [k_ref[:, sl]], [v_ref[:, sl]], [None]).astype(BF16)


def ctx_attention(q, k, v):
    t = q.shape[0]
    return pl.pallas_call(
        _ctx_attn_body,
        grid=(1,),
        in_specs=[_full((t, D))] * 3,
        out_specs=_full((t, D)),
        out_shape=jax.ShapeDtypeStruct((t, D), BF16),
        compiler_params=_cparams(("arbitrary",)),
        name="ctx_attention",
    )(q, k, v)


def _na_bias_table(rpb, rows):
    base, var, reps = _na_geometry(rows)
    cols = np.arange(GRID_W)
    cstart = np.clip(cols - NA_COLS // 2, 0, GRID_W - NA_COLS)
    kcol = np.arange(GRID_W)
    inwin = (kcol[None, :] >= cstart[:, None]) & (kcol[None, :] < cstart[:, None] + NA_COLS)
    r = NA_QROWS * reps[:, None] + np.arange(NA_QROWS)[None, :]
    rs = np.clip(r - NA_ROWS // 2, 0, rows - NA_ROWS)
    krow = base[reps][:, None] + np.arange(NA_UNION)[None, :]
    rvalid = (krow[:, None, :] >= rs[:, :, None]) & (krow[:, None, :] < rs[:, :, None] + NA_ROWS)
    ridx = np.clip(krow[:, None, :] - r[:, :, None] + (NA_ROWS - 1), 0, 2 * NA_ROWS - 2)
    nd = 2 * NA_COLS - 1
    w = jnp.pad(rpb.astype(F32), ((0, 0), (0, 0), (GRID_W - NA_COLS, 2 * GRID_W - (GRID_W - NA_COLS) - nd)))
    flat = jnp.tile(w, (1, 1, GRID_W))[:, :, :GRID_W * (2 * GRID_W - 1)]
    blk = flat.reshape(NA_HEADS, 2 * NA_ROWS - 1, GRID_W, 2 * GRID_W - 1)[..., GRID_W - 1:]
    blk = jnp.where(jnp.asarray(inwin)[None, None], blk, NEG_BIG)
    neg = jnp.full((NA_HEADS, GRID_W, GRID_W), NEG_BIG, F32)
    variants = []
    for v in range(len(reps)):
        strips = [jnp.concatenate([blk[:, ridx[v, a, j]] if rvalid[v, a, j] else neg for j in range(NA_UNION)], axis=2)
                  for a in range(NA_QROWS)]
        variants.append(jnp.concatenate(strips, axis=1))
    return jnp.stack(variants, axis=0), jnp.asarray(var)


def na_layer(x, xc, mod, modc, g, w_qkv, q_g, k_g, rpb, w_o, need_ctx, side=()):
    w_qkv_b = w_qkv.astype(BF16)
    w_o_b = w_o.astype(BF16)
    gm = np.kron(np.eye(NA_HEADS), np.full((NA_HEAD_DIM, NA_HEAD_DIM), 1.0 / NA_HEAD_DIM))
    gmean = jnp.asarray(gm, dtype=BF16)
    qg = jnp.tile(q_g, NA_HEADS)[None] * (NA_HEAD_DIM ** -0.5 * LOG2E)
    kg = jnp.tile(k_g, NA_HEADS)[None]
    q, k, v = qkv_proj(x, mod, g, w_qkv_b, gmean, qg, kg, 512)
    qc, kc, vc = qkv_proj(xc, modc, g, w_qkv_b, gmean, qg, kg, xc.shape[0])
    bias_tab, var = _na_bias_table(rpb * LOG2E, x.shape[0] // GRID_W)
    o, *side_b = na_attention(q, k, v, kc, vc, bias_tab, var, side)
    x_new = proj_residual(o, w_o_b, x, mod, 2, 1024)
    xc_new = None
    if need_ctx:
        oc = ctx_attention(qc, kc, vc)
        xc_new = proj_residual(oc, w_o_b, xc, modc, 2, xc.shape[0])
    return x_new, xc_new, side_b


def _dft_mats(n):
    ang = 2.0 * np.pi * np.outer(np.arange(n), np.arange(n)) / n
    return np.cos(ang), np.sin(ang)


def _channel_dft(h, wc):
    us = [jnp.dot(h[:, gi * FT_GROUP_W:(gi + 1) * FT_GROUP_W], wc, preferred_element_type=F32).astype(BF16)
          for gi in range(D // FT_GROUP_W)]
    return jnp.concatenate([u[:, :FT_GROUP_W] for u in us] + [u[:, FT_GROUP_W:] for u in us], axis=1)


def _ft_a_body(n, nj, x_ref, mod_ref, g_ref, perm_ref, wc_ref, ma_ref, tc_ref, ts_ref, yr_ref, yi_ref):
    h3 = _normmod(x_ref[...], g_ref[...], mod_ref[1:2, :], mod_ref[0:1, :])
    h = jnp.dot(perm_ref[...], h3.reshape(n * nj, D).astype(BF16), preferred_element_type=F32).astype(BF16)
    u = _channel_dft(h, wc_ref[...])
    for j in range(nj):
        uj = u[j * n:(j + 1) * n]
        y = jnp.dot(ma_ref[...], jnp.concatenate([uj[:, :D], uj[:, D:]], axis=0), preferred_element_type=F32)
        yr, yi = y[:n], y[n:]
        tc = _lane_tile(tc_ref[0, :, j * LANES:(j + 1) * LANES])
        ts = _lane_tile(ts_ref[0, :, j * LANES:(j + 1) * LANES])
        yr_ref[:, j, :] = yr * tc + yi * ts
        yi_ref[:, j, :] = yi * tc - yr * ts


def _ft_c_body(n, nj, yr_ref, yi_ref, mc_ref, wf_ref, x_ref, mod_ref, o_ref):
    fs = []
    for j in range(nj):
        ys = jnp.concatenate([yr_ref[j].astype(BF16), yi_ref[j].astype(BF16)], axis=0)
        fs.append(jnp.dot(mc_ref[...], ys, preferred_element_type=F32).astype(BF16))
    z = jnp.dot(jnp.concatenate(fs, axis=0), wf_ref[...], preferred_element_type=F32)
    gate = mod_ref[2:3, :]
    for j in range(nj):
        o_ref[:, j, :] = x_ref[:, j, :] + gate * z[j * n:(j + 1) * n]


def _ft_ctx_body(x_ref, mod_ref, g_ref, wc_ref, ml_ref, wf_ref, o_ref):
    x = x_ref[...]
    h = _normmod(x, g_ref[...], mod_ref[1:2, :], mod_ref[0:1, :]).astype(BF16)
    u = _channel_dft(h, wc_ref[...])
    us = jnp.concatenate([u[:, :D], u[:, D:]], axis=0)
    f = jnp.dot(ml_ref[...], us, preferred_element_type=F32).astype(BF16)
    o_ref[...] = x + mod_ref[2:3, :] * jnp.dot(f, wf_ref[...], preferred_element_type=F32)


def fourier_layer(x, xc, mod, modc, g, w_f, need_ctx):
    t = x.shape[0]
    n = math.isqrt(t)
    assert n * n == t and n % 16 == 0
    w_f_b = w_f.astype(BF16)
    cw, sw = _dft_mats(FT_GROUP_W)
    wc = jnp.asarray(np.concatenate([cw, -sw], axis=1) / math.sqrt(FT_GROUP_W), dtype=F32).astype(BF16)
    wcspec = _full((FT_GROUP_W, 2 * FT_GROUP_W))
    cn, sn = _dft_mats(n)
    ma = jnp.asarray(np.block([[cn, sn], [-sn, cn]]) / math.sqrt(n), dtype=F32).astype(BF16)
    mc = jnp.asarray(np.concatenate([cn, sn], axis=1) / math.sqrt(n), dtype=F32).astype(BF16)
    nj = 8
    ang = 2.0 * np.pi * np.outer(np.arange(n), np.arange(n)) / t
    def expand(tab):
        a = jnp.asarray(tab, dtype=F32).reshape(n // nj, nj, n).transpose(0, 2, 1)
        return jnp.repeat(a, LANES, axis=2)
    twc, tws = expand(np.cos(ang)), expand(np.sin(ang))
    xblk = pl.BlockSpec((n, nj, D), lambda b: (0, b, 0))
    yblk = pl.BlockSpec((nj, n, D), lambda b: (b, 0, 0))
    tblk = pl.BlockSpec((1, n, nj * LANES), lambda b: (b, 0, 0))
    x3 = x.reshape(n, n, D)
    src = (np.arange(n)[None, :] * nj + np.arange(nj)[:, None]).reshape(-1)
    perm = jnp.asarray(np.eye(n * nj)[src], dtype=BF16)
    yr, yi = pl.pallas_call(
        functools.partial(_ft_a_body, n, nj),
        grid=(n // nj,),
        in_specs=[xblk, _full((ADA_CHUNKS, D)), _full((1, D)), _full((n * nj, n * nj)), wcspec,
                  _full((2 * n, 2 * n)), tblk, tblk],
        out_specs=[xblk, xblk],
        out_shape=[jax.ShapeDtypeStruct((n, n, D), F32)] * 2,
        compiler_params=_cparams(("arbitrary",)),
        name="ft_stage_a",
    )(x3, mod, g, perm, wc, ma, twc, tws)
    x_new = pl.pallas_call(
        functools.partial(_ft_c_body, n, nj),
        grid=(n // nj,),
        in_specs=[yblk, yblk, _full((n, 2 * n)), _full((D, D)), xblk, _full((ADA_CHUNKS, D))],
        out_specs=xblk,
        out_shape=jax.ShapeDtypeStruct((n, n, D), F32),
        compiler_params=_cparams(("arbitrary",)),
        name="ft_stage_c",
    )(yr, yi, mc, w_f_b, x3, mod).reshape(t, D)
    xc_new = None
    if need_ctx:
        lc = xc.shape[0]
        cl, sl = _dft_mats(lc)
        ml = jnp.asarray(np.concatenate([cl, sl], axis=1) / math.sqrt(lc), dtype=F32).astype(BF16)
        xc_new = pl.pallas_call(
            _ft_ctx_body,
            grid=(1,),
            in_specs=[_full((lc, D)), _full((ADA_CHUNKS, D)), _full((1, D)), wcspec,
                      _full((lc, 2 * lc)), _full((D, D))],
            out_specs=_full((lc, D)),
            out_shape=jax.ShapeDtypeStruct((lc, D), F32),
            compiler_params=_cparams(("arbitrary",)),
            name="ft_ctx",
        )(xc, modc, g, wc, ml, w_f_b)
    return x_new, xc_new


def _router_body(x_ref, mod_ref, g_ref, r_ref, info_ref, w0_ref, w1_ref):
    h = _normmod(x_ref[...], g_ref[...], mod_ref[4:5, :], mod_ref[3:4, :])
    hh = h.astype(BF16)
    hl = (h - hh.astype(F32)).astype(BF16)
    r = r_ref[...]
    rh = r.astype(BF16)
    rl = (r - rh.astype(F32)).astype(BF16)
    logits = (jnp.dot(hh, rh, preferred_element_type=F32) + jnp.dot(hh, rl, preferred_element_type=F32)
              + jnp.dot(hl, rh, preferred_element_type=F32))
    lane = lax.broadcasted_iota(jnp.int32, logits.shape, 1)
    logits = jnp.where(lane < N_EXPERTS, logits, NEG_BIG)
    v0 = jnp.max(logits, axis=-1, keepdims=True)
    i0 = jnp.min(jnp.where(logits == v0, lane, LANES), axis=-1, keepdims=True)
    rest = jnp.where(lane == i0, NEG_BIG, logits)
    v1 = jnp.max(rest, axis=-1, keepdims=True)
    i1 = jnp.min(jnp.where(rest == v1, lane, LANES), axis=-1, keepdims=True)
    e = jnp.exp(v1 - v0)
    w0 = 1.0 / (1.0 + e)
    w1 = e / (1.0 + e)
    info_ref[...] = jnp.where(lane == 0, i0, jnp.where(lane == 1, i1, 0))
    w0_ref[...] = jnp.broadcast_to(w0, logits.shape)
    w1_ref[...] = jnp.broadcast_to(w1, logits.shape)


def moe_router(x, mod, g, router_pad, tm):
    t = x.shape[0]
    spec = pl.BlockSpec((tm, D), lambda i: (i, 0))
    lspec = pl.BlockSpec((tm, LANES), lambda i: (i, 0))
    return pl.pallas_call(
        _router_body,
        grid=(t // tm,),
        in_specs=[spec, _full((ADA_CHUNKS, D)), _full((1, D)), _full((D, LANES))],
        out_specs=[lspec, lspec, lspec],
        out_shape=[jax.ShapeDtypeStruct((t, LANES), jnp.int32),
                   jax.ShapeDtypeStruct((t, LANES), F32), jax.ShapeDtypeStruct((t, LANES), F32)],
        compiler_params=_cparams(("arbitrary",)),
        name="moe_router",
    )(x, mod, g, router_pad)


SCATTER_TOKENS = 256


def _row_scatter_body(nlat, nctx, didx_ref, g_ref, x_ref, mod_ref, *rest):
    if nctx:
        xc_ref, modc_ref, dst_ref, h_scr, zero_scr, sem = rest
    else:
        dst_ref, h_scr, zero_scr, sem = rest
    i = pl.program_id(0)
    ts = SCATTER_TOKENS

    def run(src_ref, src_is_zero_rows):
        def issue(grp, c):
            base = pl.multiple_of(grp * SUBLANES, SUBLANES)
            for r in range(SUBLANES):
                src = src_ref.at[pl.ds(r if src_is_zero_rows else base + r, 1), :]
                for half in range(2):
                    d = didx_ref[0, 0, base + r + half * ts]
                    pltpu.make_async_copy(src, dst_ref.at[pl.ds(d, 1), :], sem).start(priority=half)
            return c
        lax.fori_loop(0, ts // SUBLANES, issue, 0)

        def drain(n, c):
            pltpu.make_async_copy(src_ref.at[pl.ds(0, 1), :], dst_ref.at[pl.ds(0, 1), :], sem).wait()
            return c
        lax.fori_loop(0, 2 * ts, drain, 0, unroll=8)

    @pl.when(i == 0)
    def _():
        zero_scr[...] = jnp.zeros_like(zero_scr)

    def stage_and_run(src_ref, m_ref):
        h_scr[...] = _normmod(src_ref[...], g_ref[...], m_ref[4:5, :], m_ref[3:4, :])
        run(h_scr, False)

    @pl.when(i < nlat)
    def _():
        stage_and_run(x_ref, mod_ref)

    if nctx:
        @pl.when(jnp.logical_and(i >= nlat, i < nlat + nctx))
        def _():
            stage_and_run(xc_ref, modc_ref)

    @pl.when(i >= nlat + nctx)
    def _():
        run(zero_scr, True)


def row_scatter(x, xc, mod, modc, g, d0, d1, pad_pos):
    ts = SCATTER_TOKENS
    nlat = x.shape[0] // ts
    nctx = 0 if xc is None else 1
    assert xc is None or xc.shape[0] == ts
    ntok = nlat + nctx
    npad = pad_pos.shape[0] // (2 * ts)
    didx = jnp.concatenate([jnp.concatenate([d0.reshape(ntok, 1, ts), d1.reshape(ntok, 1, ts)], axis=2),
                            pad_pos.reshape(npad, 1, 2 * ts)], axis=0)
    in_specs = [pl.BlockSpec((1, 1, 2 * ts), lambda i: (i, 0, 0), memory_space=pltpu.SMEM),
                _full((1, D)),
                pl.BlockSpec((ts, D), lambda i: (jnp.minimum(i, nlat - 1), 0)),
                _full((ADA_CHUNKS, D))]
    args = [didx, g, x, mod]
    if nctx:
        in_specs += [_full((ts, D)), _full((ADA_CHUNKS, D))]
        args += [xc, modc]
    return pl.pallas_call(
        functools.partial(_row_scatter_body, nlat, nctx),
        grid=(ntok + npad,),
        in_specs=in_specs,
        out_specs=pl.BlockSpec(memory_space=pl.ANY),
        out_shape=jax.ShapeDtypeStruct((2 * ntok * ts + pad_pos.shape[0], D), F32),
        scratch_shapes=[pltpu.VMEM((ts, D), F32), pltpu.VMEM((SUBLANES, D), F32), pltpu.SemaphoreType.DMA(())],
        compiler_params=_cparams(("arbitrary",)),
        name="moe_row_scatter",
    )(*args)


def _moe_ffn_body(te_ref, tv_ref, xg_ref, wg_ref, wu_ref, wd_ref, o_ref):
    i = pl.program_id(0)

    @pl.when(tv_ref[i] > 0)
    def _():
        o_ref[...] = _swiglu_chunks(xg_ref[...].astype(BF16), wg_ref, wu_ref, wd_ref)

    @pl.when(tv_ref[i] == 0)
    def _():
        o_ref[...] = jnp.zeros_like(o_ref)


def moe_ffn(xg, tile_e, tile_v, w_gu, w_down, li, tm):
    p = xg.shape[0]
    grid_spec = pltpu.PrefetchScalarGridSpec(
        num_scalar_prefetch=2,
        grid=(p // tm,),
        in_specs=[
            pl.BlockSpec((tm, D), lambda i, te, tv: (i, 0)),
            pl.BlockSpec((None, None, D, D_FF), lambda i, te, tv: (li, te[i], 0, 0)),
            pl.BlockSpec((None, None, D, D_FF), lambda i, te, tv: (li, te[i], 0, 1)),
            pl.BlockSpec((None, None, D_FF, D), lambda i, te, tv: (li, te[i], 0, 0)),
        ],
        out_specs=pl.BlockSpec((tm, D), lambda i, te, tv: (i, 0)),
    )
    return pl.pallas_call(
        _moe_ffn_body,
        grid_spec=grid_spec,
        out_shape=jax.ShapeDtypeStruct((p, D), F32),
        compiler_params=pltpu.CompilerParams(dimension_semantics=("arbitrary",), vmem_limit_bytes=MOE_VMEM_LIMIT),
        name="moe_ffn",
    )(tile_e, tile_v, xg, w_gu, w_gu, w_down)


def _combine_body(tt, d0_ref, d1_ref, yp_ref, x_ref, mod_ref, w0_ref, w1_ref, o_ref, a_scr, b_scr, sem):
    def issue(grp, c):
        base = pl.multiple_of(grp * SUBLANES, SUBLANES)
        for r in range(SUBLANES):
            n = base + r
            pltpu.make_async_copy(yp_ref.at[pl.ds(d0_ref[0, 0, n], 1), :], a_scr.at[pl.ds(n, 1), :],
                                  sem).start(priority=0)
            pltpu.make_async_copy(yp_ref.at[pl.ds(d1_ref[0, 0, n], 1), :], b_scr.at[pl.ds(n, 1), :],
                                  sem).start(priority=1)
        return c
    lax.fori_loop(0, tt // SUBLANES, issue, 0)

    def drain(n, c):
        pltpu.make_async_copy(yp_ref.at[pl.ds(0, 1), :], a_scr.at[pl.ds(0, 1), :], sem).wait()
        pltpu.make_async_copy(yp_ref.at[pl.ds(0, 1), :], b_scr.at[pl.ds(0, 1), :], sem).wait()
        return c
    lax.fori_loop(0, tt, drain, 0, unroll=8)
    w0 = _lane_tile(w0_ref[...])
    w1 = _lane_tile(w1_ref[...])
    o_ref[...] = x_ref[...] + mod_ref[5:6, :] * (w0 * a_scr[...] + w1 * b_scr[...])


def moe_combine(yp, d0, d1, x, mod, w0b, w1b, tt):
    t = x.shape[0]
    nt = t // tt
    ispec = pl.BlockSpec((1, 1, tt), lambda i: (i, 0, 0), memory_space=pltpu.SMEM)
    spec = pl.BlockSpec((tt, D), lambda i: (i, 0))
    lspec = pl.BlockSpec((tt, LANES), lambda i: (i, 0))
    return pl.pallas_call(
        functools.partial(_combine_body, tt),
        grid=(nt,),
        in_specs=[ispec, ispec, pl.BlockSpec(memory_space=pl.ANY), spec, _full((ADA_CHUNKS, D)), lspec, lspec],
        out_specs=spec,
        out_shape=jax.ShapeDtypeStruct((t, D), F32),
        scratch_shapes=[pltpu.VMEM((tt, D), F32), pltpu.VMEM((tt, D), F32), pltpu.SemaphoreType.DMA(())],
        compiler_params=_cparams(("arbitrary",)),
        name="moe_combine",
    )(d0.reshape(nt, 1, tt), d1.reshape(nt, 1, tt), yp, x, mod, w0b, w1b)


def _route_plan(e0, e1, tm):
    t = e0.shape[0]
    n = 2 * t
    ex = jnp.arange(N_EXPERTS, dtype=jnp.int32)
    oh0 = (e0[:, None] == ex[None, :]).astype(jnp.int32)
    oh1 = (e1[:, None] == ex[None, :]).astype(jnp.int32)
    both = oh0 + oh1
    csum = jnp.cumsum(both, axis=0)
    before = csum - both
    counts = csum[-1]
    padded = ((counts + tm - 1) // tm) * tm
    pad_end = jnp.cumsum(padded)
    pad_off = pad_end - padded
    total = pad_end[-1]
    d0 = jnp.sum(oh0 * (before + pad_off[None, :]), axis=1)
    d1 = jnp.sum(oh1 * (before + oh0 + pad_off[None, :]), axis=1)
    gap = padded - counts
    tail_off = jnp.cumsum(tm - gap) - (tm - gap)
    r = jnp.arange(tm, dtype=jnp.int32)[None, :]
    pad_pos = jnp.where(r < gap[:, None], (pad_off + counts)[:, None] + r,
                        total + tail_off[:, None] + (r - gap[:, None])).reshape(-1)
    ntiles = (n + N_EXPERTS * tm) // tm
    tstart = jnp.arange(ntiles, dtype=jnp.int32) * tm
    tile_v = (tstart < total).astype(jnp.int32)
    tile_e = jnp.sum((jnp.minimum(tstart, total - 1)[:, None] >= pad_end[None, :]).astype(jnp.int32), axis=1)
    return (d0.astype(jnp.int32), d1.astype(jnp.int32), pad_pos.astype(jnp.int32), tile_e.astype(jnp.int32), tile_v)


def moe_layer(x, xc, mod, modc, g, router, w_gu_b, w_down_b, li, need_ctx, tm=512):
    router_pad = jnp.pad(router, ((0, 0), (0, LANES - N_EXPERTS)))
    s = x.shape[0]
    info, w0b, w1b = moe_router(x, mod, g, router_pad, 1024)
    if need_ctx:
        sc = xc.shape[0]
        infoc, w0c, w1c = moe_router(xc, modc, g, router_pad, sc)
        e0 = jnp.concatenate([info[:, 0], infoc[:, 0]])
        e1 = jnp.concatenate([info[:, 1], infoc[:, 1]])
    else:
        e0, e1 = info[:, 0], info[:, 1]
    d0, d1, pad_pos, tile_e, tile_v = _route_plan(e0, e1, tm)
    xg = row_scatter(x, xc if need_ctx else None, mod, modc, g, d0, d1, pad_pos)
    yp = moe_ffn(xg, tile_e, tile_v, w_gu_b, w_down_b, li, tm)
    x_new = moe_combine(yp, d0[:s], d1[:s], x, mod, w0b, w1b, 512)
    xc_new = None
    if need_ctx:
        xc_new = moe_combine(yp, d0[s:], d1[s:], xc, modc, w0c, w1c, sc)
    return x_new, xc_new


def kernel(x, c, ctx, c_ctx, ada_w, ada_b, norm_g, rg_w_in, rg_conv_w, rg_conv_b, rg_wa, rg_ba, rg_wi, rg_bi,
           rg_lambda, rg_w_out, na_w_qkv, na_q_g, na_k_g, na_rpb, na_w_o, ft_w_out, ffn_w_gu, ffn_w_down,
           moe_router, moe_w_gu, moe_w_down):
    depth = ada_w.shape[0]
    assert x.shape[0] == 1 and x.shape[2] == D
    xs = x[0]
    xc = ctx[0]
    mods = ada_modulation(c, c_ctx, ada_w, ada_b)
    ffn_gu_b, ffn_dn_b = ffn_w_gu.astype(BF16), ffn_w_down.astype(BF16)
    moe_gu_b = moe_dn_b = None
    mix_idx = [0] * N_MIXERS
    dense_idx = 0
    moe_idx = 0
    for layer in range(depth):
        need_ctx = layer != depth - 1
        mod, modc = mods[layer, 0], mods[layer, 1]
        g0 = norm_g[layer, 0][None]
        g1 = norm_g[layer, 1][None]
        kind = layer % N_MIXERS
        j = mix_idx[kind]
        mix_idx[kind] += 1
        if kind == 0:
            xs, xcn = rglru_layer(xs, xc, mod, modc, g0, rg_w_in[j], rg_conv_w[j], rg_conv_b[j], rg_wa[j], rg_wi[j],
                                  rg_ba[j], rg_bi[j], rg_lambda[j], rg_w_out[j], need_ctx)
        elif kind == 1:
            side = ()
            if moe_gu_b is None:
                side = (moe_w_gu.reshape(-1, moe_w_gu.shape[-1]), moe_w_down.reshape(-1, moe_w_down.shape[-1]))
            xs, xcn, side_b = na_layer(xs, xc, mod, modc, g0, na_w_qkv[j], na_q_g[j], na_k_g[j], na_rpb[j],
                                       na_w_o[j], need_ctx, side)
            if side_b:
                moe_gu_b, moe_dn_b = side_b[0].reshape(moe_w_gu.shape), side_b[1].reshape(moe_w_down.shape)
        else:
            xs, xcn = fourier_layer(xs, xc, mod, modc, g0, ft_w_out[j], need_ctx)
        if need_ctx:
            xc = xcn
        if layer % 2 == 0:
            if need_ctx:
                xc = ffn_dense(xc, modc, g1, ffn_gu_b, ffn_dn_b, dense_idx, xc.shape[0])
            xs = ffn_dense(xs, mod, g1, ffn_gu_b, ffn_dn_b, dense_idx, 512)
            dense_idx += 1
        else:
            if moe_gu_b is None:
                moe_gu_b, moe_dn_b = moe_w_gu.astype(BF16), moe_w_down.astype(BF16)
            xs, xcn = moe_layer(xs, xc, mod, modc, g1, moe_router[moe_idx], moe_gu_b, moe_dn_b, moe_idx, need_ctx)
            moe_idx += 1
            if need_ctx:
                xc = xcn
    return xs[None]
```

```python
import functools
import math

import numpy as np
import jax
import jax.numpy as jnp
from jax import lax
from jax.experimental import pallas as pl
from jax.experimental.pallas import tpu as pltpu

F32 = jnp.float32
BF16 = jnp.bfloat16

D = 1024
D_FF = 3584
N_EXPERTS = 8
GRID_W = 64
NA_HEADS = 16
NA_HEAD_DIM = 64
NA_ROWS = 8
NA_COLS = 16
FT_GROUP_W = 256
RG_BLOCK_W = 256
RMS_EPS = 1e-6
LRU_C = 8.0
N_MIXERS = 3
ADA_CHUNKS = 6

LANES = 128
SUBLANES = 8
VMEM_LIMIT = 56 * 1024 * 1024
MOE_VMEM_LIMIT = 60 * 1024 * 1024
NEG_BIG = -1e30
LOG2E = math.log2(math.e)


def _cparams(sem):
    return pltpu.CompilerParams(dimension_semantics=sem, vmem_limit_bytes=VMEM_LIMIT)


def _full(shape):
    nd = len(shape)
    return pl.BlockSpec(shape, lambda *_: (0,) * nd)


def _normmod(x, g, scale, shift):
    ms = jnp.mean(x * x, axis=-1, keepdims=True)
    y = x * lax.rsqrt(ms + RMS_EPS)
    return (y * g) * (1.0 + scale) + shift


def _lane_tile(v):
    return jnp.concatenate([v] * (D // LANES), axis=1)


def _sigmoid(v):
    return 1.0 / (1.0 + jnp.exp(-v))


def _gelu_tanh(v):
    c = math.sqrt(2.0 / math.pi)
    return v * (0.5 * (1.0 + jnp.tanh(c * (v + 0.044715 * (v * v * v)))))


def _ada_body(cin_ref, w_ref, b_ref, o_ref):
    v = cin_ref[...]
    s = v * _sigmoid(v)
    w = w_ref[0]
    r0 = jnp.sum(s[:, 0:1] * w, axis=0, keepdims=True)
    r1 = jnp.sum(s[:, 1:2] * w, axis=0, keepdims=True)
    o_ref[0] = jnp.concatenate([r0, r1], axis=0) + b_ref[0]


def ada_modulation(c, c_ctx, ada_w, ada_b):
    depth = ada_w.shape[0]
    n = ada_w.shape[2]
    nc = n // 4
    cin = jnp.stack([c[0], c_ctx], axis=1)
    out = pl.pallas_call(
        _ada_body,
        grid=(depth, n // nc),
        in_specs=[
            pl.BlockSpec((D, 2), lambda l, j: (0, 0)),
            pl.BlockSpec((1, D, nc), lambda l, j: (l, 0, j)),
            pl.BlockSpec((1, 1, nc), lambda l, j: (l, 0, j)),
        ],
        out_specs=pl.BlockSpec((1, 2, nc), lambda l, j: (l, 0, j)),
        out_shape=jax.ShapeDtypeStruct((depth, 2, n), F32),
        compiler_params=_cparams(("arbitrary", "arbitrary")),
        name="ada_mod",
    )(cin, ada_w, ada_b.reshape(depth, 1, n))
    return out.reshape(depth, 2, ADA_CHUNKS, D)


FFN_CHUNK = 512


def _swiglu_chunks(h, wg_ref, wu_ref, wd_ref):
    acc = None
    for c in range(D_FF // FFN_CHUNK):
        sl = slice(c * FFN_CHUNK, (c + 1) * FFN_CHUNK)
        gg = jnp.dot(h, wg_ref[:, sl], preferred_element_type=F32)
        uu = jnp.dot(h, wu_ref[:, sl], preferred_element_type=F32)
        a = ((gg * _sigmoid(gg)) * uu).astype(BF16)
        part = jnp.dot(a, wd_ref[sl, :], preferred_element_type=F32)
        acc = part if acc is None else acc + part
    return acc


def _ffn_body(x_ref, mod_ref, g_ref, wg_ref, wu_ref, wd_ref, o_ref):
    x = x_ref[...]
    h = _normmod(x, g_ref[...], mod_ref[4:5, :], mod_ref[3:4, :]).astype(BF16)
    o_ref[...] = x + mod_ref[5:6, :] * _swiglu_chunks(h, wg_ref, wu_ref, wd_ref)


def ffn_dense(x, mod, g, w_gu, w_down, li, tm):
    t = x.shape[0]
    once = pl.Buffered(1)
    return pl.pallas_call(
        _ffn_body,
        grid=(t // tm,),
        in_specs=[
            pl.BlockSpec((tm, D), lambda i: (i, 0)),
            _full((ADA_CHUNKS, D)),
            _full((1, D)),
            pl.BlockSpec((None, D, D_FF), lambda i: (li, 0, 0), pipeline_mode=once),
            pl.BlockSpec((None, D, D_FF), lambda i: (li, 0, 1), pipeline_mode=once),
            pl.BlockSpec((None, D_FF, D), lambda i: (li, 0, 0), pipeline_mode=once),
        ],
        out_specs=pl.BlockSpec((tm, D), lambda i: (i, 0)),
        out_shape=jax.ShapeDtypeStruct((t, D), F32),
        compiler_params=_cparams(("arbitrary",)),
        name="ffn_dense",
    )(x, mod, g, w_gu, w_gu, w_down)


def _proj_body(gate_row, a_ref, w_ref, x_ref, mod_ref, o_ref):
    y = jnp.dot(a_ref[...], w_ref[...], preferred_element_type=F32)
    o_ref[...] = x_ref[...] + mod_ref[gate_row:gate_row + 1, :] * y


def proj_residual(a, w, x, mod, gate_row, tm):
    t, k = a.shape
    return pl.pallas_call(
        functools.partial(_proj_body, gate_row),
        grid=(t // tm,),
        in_specs=[
            pl.BlockSpec((tm, k), lambda i: (i, 0)),
            _full((k, D)),
            pl.BlockSpec((tm, D), lambda i: (i, 0)),
            _full((ADA_CHUNKS, D)),
        ],
        out_specs=pl.BlockSpec((tm, D), lambda i: (i, 0)),
        out_shape=jax.ShapeDtypeStruct((t, D), F32),
        compiler_params=_cparams(("arbitrary",)),
        name="proj_residual",
    )(a, w, x, mod)


HALO = SUBLANES
RG_IN_PIECE = 128


def _rg_in_body(tm, xp_ref, x_ref, xn_ref, mod_ref, g_ref, w_ref, cw_ref, cb_ref, xc_ref, gg_ref):
    i = pl.program_id(0)
    last = pl.num_programs(0) - 1
    xa = jnp.concatenate([xp_ref[...], x_ref[...], xn_ref[...]], axis=0)
    npiece = tm // RG_IN_PIECE
    bounds = [0] + [2 * HALO + RG_IN_PIECE * (k + 1) for k in range(npiece - 1)] + [tm + 2 * HALO]
    zs = []
    for k in range(npiece):
        hk = _normmod(xa[bounds[k]:bounds[k + 1]], g_ref[...], mod_ref[1:2, :], mod_ref[0:1, :]).astype(BF16)
        zs.append(jnp.dot(hk, w_ref[...], preferred_element_type=F32))
    z = jnp.concatenate(zs, axis=0)
    row = lax.broadcasted_iota(jnp.int32, (tm + 2 * HALO, 1), 0)
    valid = jnp.logical_and(jnp.logical_or(row >= HALO, i > 0),
                            jnp.logical_or(row < tm + HALO, i < last))
    xz = jnp.where(valid, z[:, :D], 0.0)
    y = cb_ref[...] + cw_ref[2:3, :] * xz[HALO:HALO + tm]
    y = y + cw_ref[0:1, :] * xz[HALO - 2:HALO - 2 + tm]
    y = y + cw_ref[1:2, :] * xz[HALO - 1:HALO - 1 + tm]
    y = y + cw_ref[3:4, :] * xz[HALO + 1:HALO + 1 + tm]
    xc_ref[...] = y
    gg_ref[...] = _gelu_tanh(z[HALO:HALO + tm, D:]).astype(BF16)


def rg_in(x, mod, g, w_in, conv_w, conv_b, tm):
    t = x.shape[0]
    nb = tm // HALO
    nblk = t // HALO
    return pl.pallas_call(
        functools.partial(_rg_in_body, tm),
        grid=(t // tm,),
        in_specs=[
            pl.BlockSpec((HALO, D), lambda i: (jnp.maximum(i * nb - 1, 0), 0)),
            pl.BlockSpec((tm, D), lambda i: (i, 0)),
            pl.BlockSpec((HALO, D), lambda i: (jnp.minimum((i + 1) * nb, nblk - 1), 0)),
            _full((ADA_CHUNKS, D)),
            _full((1, D)),
            _full((D, 2 * D)),
            _full((4, D)),
            _full((1, D)),
        ],
        out_specs=[pl.BlockSpec((tm, D), lambda i: (i, 0)), pl.BlockSpec((tm, D), lambda i: (i, 0))],
        out_shape=[jax.ShapeDtypeStruct((t, D), F32), jax.ShapeDtypeStruct((t, D), BF16)],
        compiler_params=_cparams(("arbitrary",)),
        name="rg_in",
    )(x, x, x, mod, g, w_in, conv_w, conv_b)


def _rg_gates(xc, wa_ref, wi_ref, ba, bi, lam):
    xb = xc.astype(BF16)
    nblk = D // RG_BLOCK_W
    r = jnp.concatenate([jnp.dot(xb[:, n * RG_BLOCK_W:(n + 1) * RG_BLOCK_W], wa_ref[n],
                                 preferred_element_type=F32) for n in range(nblk)], axis=1)
    ig = jnp.concatenate([jnp.dot(xb[:, n * RG_BLOCK_W:(n + 1) * RG_BLOCK_W], wi_ref[n],
                                  preferred_element_type=F32) for n in range(nblk)], axis=1)
    t_r = jnp.tanh(r + 0.5 * ba)
    t_i = jnp.tanh(ig + 0.5 * bi)
    nl = -lam
    softplus = jnp.maximum(nl, 0.0) + jnp.log1p(jnp.exp(-jnp.abs(nl)))
    half_c = (-0.5 * LRU_C) * softplus
    log_a = half_c + half_c * t_r
    a = jnp.exp(log_a)
    xh = 0.5 * xc
    b = jnp.sqrt(1.0 - a * a) * (xh + xh * t_i)
    return a, b


def _rg_scan_body(reverse, epilogue, emit_h, tc, *refs):
    xc_ref, wa_ref, wi_ref, ba_ref, bi_ref, lam_ref, h0_ref = refs[:7]
    refs = refs[7:]
    if epilogue:
        hf_ref, gg_ref, wo_ref, x_ref, mod_ref = refs[:5]
        refs = refs[5:]
    if emit_h:
        h_ref = refs[0]
        refs = refs[1:]
    if epilogue:
        o_ref = refs[0]
        refs = refs[1:]
    a_scr, b_scr, h_scr, carry_scr = refs
    c = pl.program_id(0)

    @pl.when(c == 0)
    def _():
        carry_scr[...] = jnp.broadcast_to(h0_ref[...], (SUBLANES, D))

    a, b = _rg_gates(xc_ref[...], wa_ref, wi_ref, ba_ref[...], bi_ref[...], lam_ref[...])
    a_scr[...] = a
    b_scr[...] = b
    nblk = tc // SUBLANES
    row = lax.broadcasted_iota(jnp.int32, (SUBLANES, D), 0)
    first = (row == SUBLANES - 1) if reverse else (row == 0)

    def block(n, carry):
        blk = (nblk - 1 - n) if reverse else n
        off = pl.multiple_of(blk * SUBLANES, SUBLANES)
        av = a_scr[pl.ds(off, SUBLANES), :]
        bv = b_scr[pl.ds(off, SUBLANES), :]
        bv = jnp.where(first, av * carry + bv, bv)
        av = jnp.where(first, 0.0, av)
        for k in (1, 2, 4):
            shift = (SUBLANES - k) if reverse else k
            bv = av * pltpu.roll(bv, shift, 0) + bv
            if k != 4:
                av = av * pltpu.roll(av, shift, 0)
        h_scr[pl.ds(off, SUBLANES), :] = bv
        edge = bv[0:1, :] if reverse else bv[SUBLANES - 1:SUBLANES, :]
        return jnp.broadcast_to(edge, (SUBLANES, D))

    carry_scr[...] = lax.fori_loop(0, nblk, block, carry_scr[...], unroll=2)

    if emit_h:
        h_ref[...] = h_scr[...].astype(h_ref.dtype)
    if epilogue:
        y = ((hf_ref[...].astype(F32) + h_scr[...]) * gg_ref[...].astype(F32)).astype(BF16)
        o_ref[...] = x_ref[...] + mod_ref[2:3, :] * jnp.dot(y, wo_ref[...], preferred_element_type=F32)


def rg_scan(xconv, wa, wi, ba, bi, lam, h0, tc, reverse, epi=None, h_dtype=F32):
    t = xconv.shape[0]
    nchunks = t // tc
    idx = (lambda c: (nchunks - 1 - c, 0)) if reverse else (lambda c: (c, 0))
    nb = D // RG_BLOCK_W
    blk = pl.BlockSpec((tc, D), idx)
    in_specs = [
        blk,
        _full((nb, RG_BLOCK_W, RG_BLOCK_W)),
        _full((nb, RG_BLOCK_W, RG_BLOCK_W)),
        _full((1, D)), _full((1, D)), _full((1, D)), _full((1, D)),
    ]
    args = [xconv, wa, wi, ba, bi, lam, h0]
    out_specs = []
    out_shape = []
    if epi is not None:
        hf, gg, w_out, x, mod = epi
        in_specs += [blk, blk, _full((D, D)), blk, _full((ADA_CHUNKS, D))]
        args += [hf, gg, w_out, x, mod]
    if h_dtype is not None:
        out_specs.append(blk)
        out_shape.append(jax.ShapeDtypeStruct((t, D), h_dtype))
    if epi is not None:
        out_specs.append(blk)
        out_shape.append(jax.ShapeDtypeStruct((t, D), F32))
    return pl.pallas_call(
        functools.partial(_rg_scan_body, reverse, epi is not None, h_dtype is not None, tc),
        grid=(nchunks,),
        in_specs=in_specs,
        out_specs=out_specs,
        out_shape=out_shape,
        scratch_shapes=[pltpu.VMEM((tc, D), F32), pltpu.VMEM((tc, D), F32), pltpu.VMEM((tc, D), F32),
                        pltpu.VMEM((SUBLANES, D), F32)],
        compiler_params=_cparams(("arbitrary",)),
        name="rg_scan_bwd" if reverse else "rg_scan_fwd",
    )(*args)


def rglru_layer(x, xc, mod, modc, g, w_in, conv_w, conv_b, wa, wi, ba, bi, lam, w_out, need_ctx):
    w_in_b = w_in.astype(BF16)
    wa_b = (0.5 * wa).astype(BF16)
    wi_b = (0.5 * wi).astype(BF16)
    w_out_b = w_out.astype(BF16)
    cb = conv_b[None]
    tcx = xc.shape[0]
    xcl, ggl = rg_in(x, mod, g, w_in_b, conv_w, cb, 512)
    xcc, ggc = rg_in(xc, modc, g, w_in_b, conv_w, cb, tcx)
    zeros = jnp.zeros((1, D), F32)
    p = lambda d: (wa_b[d], wi_b[d], ba[d][None], bi[d][None], lam[d][None])
    (hcf,) = rg_scan(xcc, *p(0), zeros, tcx, False)
    (hlf,) = rg_scan(xcl, *p(0), hcf[tcx - 1:tcx], 512, False, h_dtype=BF16)
    if need_ctx:
        hcb, xc_new = rg_scan(xcc, *p(1), zeros, tcx, True, epi=(hcf, ggc, w_out_b, xc, modc))
    else:
        (hcb,) = rg_scan(xcc, *p(1), zeros, tcx, True)
        xc_new = None
    (x_new,) = rg_scan(xcl, *p(1), hcb[0:1], 512, True, epi=(hlf, ggl, w_out_b, x, mod), h_dtype=None)
    return x_new, xc_new


def _qkv_body(x_ref, mod_ref, g_ref, w_ref, gm_ref, qg_ref, kg_ref, q_ref, k_ref, v_ref):
    h = _normmod(x_ref[...], g_ref[...], mod_ref[1:2, :], mod_ref[0:1, :]).astype(BF16)
    z = jnp.dot(h, w_ref[...], preferred_element_type=F32)

    def headnorm(v, gain):
        ms = jnp.dot((v * v).astype(BF16), gm_ref[...], preferred_element_type=F32)
        return (v * lax.rsqrt(ms + RMS_EPS)) * gain

    q_ref[...] = headnorm(z[:, :D], qg_ref[...]).astype(BF16)
    k_ref[...] = headnorm(z[:, D:2 * D], kg_ref[...]).astype(BF16)
    v_ref[...] = z[:, 2 * D:].astype(BF16)


def qkv_proj(x, mod, g, w_qkv, gmean, qg, kg, tm):
    t = x.shape[0]
    spec = pl.BlockSpec((tm, D), lambda i: (i, 0))
    return pl.pallas_call(
        _qkv_body,
        grid=(t // tm,),
        in_specs=[spec, _full((ADA_CHUNKS, D)), _full((1, D)), _full((D, 3 * D)), _full((D, D)),
                  _full((1, D)), _full((1, D))],
        out_specs=[spec, spec, spec],
        out_shape=[jax.ShapeDtypeStruct((t, D), BF16)] * 3,
        compiler_params=_cparams(("arbitrary",)),
        name="qkv_proj",
    )(x, mod, g, w_qkv, gmean, qg, kg)


def _attend_pair(q2, keys, vals, biases):
    m_rows = q2.shape[0]
    lane = lax.broadcasted_iota(jnp.int32, q2.shape, 1)
    zero = jnp.zeros_like(q2)
    qs = jnp.concatenate([jnp.where(lane < NA_HEAD_DIM, q2, zero), jnp.where(lane >= NA_HEAD_DIM, q2, zero)], axis=0)
    ss = []
    for kseg, bseg in zip(keys, biases):
        s = lax.dot_general(qs, kseg, (((1,), (1,)), ((), ())), preferred_element_type=F32)
        if bseg is not None:
            s = s + jnp.concatenate([bseg[0], bseg[1]], axis=0)
        ss.append(s)
    m = ss[0].max(axis=-1, keepdims=True)
    for s in ss[1:]:
        m = jnp.maximum(m, s.max(axis=-1, keepdims=True))
    den = None
    acc = None
    for s, vseg in zip(ss, vals):
        p = jnp.exp2(s - m)
        d = jnp.sum(p, axis=-1, keepdims=True)
        o = jnp.dot(p.astype(BF16), vseg, preferred_element_type=F32)
        den = d if den is None else den + d
        acc = o if acc is None else acc + o
    out = acc / den
    return jnp.where(lane < NA_HEAD_DIM, out[:m_rows], out[m_rows:])


NA_QROWS = 2
NA_UNION = NA_ROWS + NA_QROWS - 1


def _na_body(nside, var_ref, q_ref, kl_ref, vl_ref, kc_ref, vc_ref, bias_ref, *refs):
    side_in, o_ref, side_out = refs[:nside], refs[nside], refs[nside + 1:]
    for src, dst in zip(side_in, side_out):
        dst[...] = src[...].astype(BF16)
    for pr in range(NA_HEADS // 2):
        sl = slice(pr * LANES, (pr + 1) * LANES)
        o_ref[:, sl] = _attend_pair(
            q_ref[:, sl], [kl_ref[:, sl], kc_ref[:, sl]], [vl_ref[:, sl], vc_ref[:, sl]],
            [(bias_ref[0, 2 * pr], bias_ref[0, 2 * pr + 1]), None]).astype(BF16)


def _na_geometry(rows):
    steps = rows // NA_QROWS
    g = np.arange(steps)
    base = np.clip(NA_QROWS * g - NA_ROWS // 2, 0, rows - NA_UNION)
    r = NA_QROWS * g[:, None] + np.arange(NA_QROWS)[None, :]
    rs = np.clip(r - NA_ROWS // 2, 0, rows - NA_ROWS)
    key = np.concatenate([(base - NA_QROWS * g)[:, None], rs - r], axis=1)
    uniq, first, var = np.unique(key, axis=0, return_index=True, return_inverse=True)
    return base, var.reshape(-1).astype(np.int32), g[first]


def na_attention(q, k, v, kc, vc, bias_tab, var, side=()):
    t = q.shape[0]
    rows = t // GRID_W
    nctx = kc.shape[0]
    steps = rows // NA_QROWS
    side_specs = []
    for a in side:
        assert a.shape[0] % (steps * 16) == 0
        side_specs.append(pl.BlockSpec((a.shape[0] // steps, a.shape[1]), lambda g, var: (g, 0)))

    def kbase(g):
        return jnp.clip(NA_QROWS * g - NA_ROWS // 2, 0, rows - NA_UNION)

    qrows = NA_QROWS * GRID_W
    nloc = NA_UNION * GRID_W
    kspec = pl.BlockSpec((pl.Element(nloc), pl.Element(D)), lambda g, var: (kbase(g) * GRID_W, 0))
    grid_spec = pltpu.PrefetchScalarGridSpec(
        num_scalar_prefetch=1,
        grid=(steps,),
        in_specs=[pl.BlockSpec((qrows, D), lambda g, var: (g, 0)), kspec, kspec] + [
            pl.BlockSpec((nctx, D), lambda g, var: (0, 0)), pl.BlockSpec((nctx, D), lambda g, var: (0, 0)),
            pl.BlockSpec((1, NA_HEADS, qrows, nloc), lambda g, var: (var[g], 0, 0, 0)),
        ] + side_specs,
        out_specs=[pl.BlockSpec((qrows, D), lambda g, var: (g, 0))] + side_specs,
    )
    return pl.pallas_call(
        functools.partial(_na_body, len(side)),
        grid_spec=grid_spec,
        out_shape=[jax.ShapeDtypeStruct((t, D), BF16)] + [jax.ShapeDtypeStruct(a.shape, BF16) for a in side],
        compiler_params=_cparams(("arbitrary",)),
        name="na_attention",
    )(var, q, k, v, kc, vc, bias_tab, *side)


def _ctx_attn_body(q_ref, k_ref, v_ref, o_ref):
    for pr in range(NA_HEADS // 2):
        sl = slice(pr * LANES, (pr + 1) * LANES)
        o_ref[:, sl] = _attend_pair(q_ref[:, sl], [k_ref[:, sl]], [v_ref[:, sl]], [None]).astype(BF16)


def ctx_attention(q, k, v):
    t = q.shape[0]
    return pl.pallas_call(
        _ctx_attn_body,
        grid=(1,),
        in_specs=[_full((t, D))] * 3,
        out_specs=_full((t, D)),
        out_shape=jax.ShapeDtypeStruct((t, D), BF16),
        compiler_params=_cparams(("arbitrary",)),
        name="ctx_attention",
    )(q, k, v)


def _na_bias_table(rpb, rows):
    base, var, reps = _na_geometry(rows)
    cols = np.arange(GRID_W)
    cstart = np.clip(cols - NA_COLS // 2, 0, GRID_W - NA_COLS)
    kcol = np.arange(GRID_W)
    inwin = (kcol[None, :] >= cstart[:, None]) & (kcol[None, :] < cstart[:, None] + NA_COLS)
    r = NA_QROWS * reps[:, None] + np.arange(NA_QROWS)[None, :]
    rs = np.clip(r - NA_ROWS // 2, 0, rows - NA_ROWS)
    krow = base[reps][:, None] + np.arange(NA_UNION)[None, :]
    rvalid = (krow[:, None, :] >= rs[:, :, None]) & (krow[:, None, :] < rs[:, :, None] + NA_ROWS)
    ridx = np.clip(krow[:, None, :] - r[:, :, None] + (NA_ROWS - 1), 0, 2 * NA_ROWS - 2)
    nd = 2 * NA_COLS - 1
    w = jnp.pad(rpb.astype(F32), ((0, 0), (0, 0), (GRID_W - NA_COLS, 2 * GRID_W - (GRID_W - NA_COLS) - nd)))
    flat = jnp.tile(w, (1, 1, GRID_W))[:, :, :GRID_W * (2 * GRID_W - 1)]
    blk = flat.reshape(NA_HEADS, 2 * NA_ROWS - 1, GRID_W, 2 * GRID_W - 1)[..., GRID_W - 1:]
    blk = jnp.where(jnp.asarray(inwin)[None, None], blk, NEG_BIG)
    neg = jnp.full((NA_HEADS, GRID_W, GRID_W), NEG_BIG, F32)
    variants = []
    for v in range(len(reps)):
        strips = [jnp.concatenate([blk[:, ridx[v, a, j]] if rvalid[v, a, j] else neg for j in range(NA_UNION)], axis=2)
                  for a in range(NA_QROWS)]
        variants.append(jnp.concatenate(strips, axis=1))
    return jnp.stack(variants, axis=0), jnp.asarray(var)


def na_layer(x, xc, mod, modc, g, w_qkv, q_g, k_g, rpb, w_o, need_ctx, side=()):
    w_qkv_b = w_qkv.astype(BF16)
    w_o_b = w_o.astype(BF16)
    gm = np.kron(np.eye(NA_HEADS), np.full((NA_HEAD_DIM, NA_HEAD_DIM), 1.0 / NA_HEAD_DIM))
    gmean = jnp.asarray(gm, dtype=BF16)
    qg = jnp.tile(q_g, NA_HEADS)[None] * (NA_HEAD_DIM ** -0.5 * LOG2E)
    kg = jnp.tile(k_g, NA_HEADS)[None]
    q, k, v = qkv_proj(x, mod, g, w_qkv_b, gmean, qg, kg, 512)
    qc, kc, vc = qkv_proj(xc, modc, g, w_qkv_b, gmean, qg, kg, xc.shape[0])
    bias_tab, var = _na_bias_table(rpb * LOG2E, x.shape[0] // GRID_W)
    o, *side_b = na_attention(q, k, v, kc, vc, bias_tab, var, side)
    x_new = proj_residual(o, w_o_b, x, mod, 2, 1024)
    xc_new = None
    if need_ctx:
        oc = ctx_attention(qc, kc, vc)
        xc_new = proj_residual(oc, w_o_b, xc, modc, 2, xc.shape[0])
    return x_new, xc_new, side_b


def _dft_mats(n):
    ang = 2.0 * np.pi * np.outer(np.arange(n), np.arange(n)) / n
    return np.cos(ang), np.sin(ang)


def _channel_dft(h, wc):
    us = [jnp.dot(h[:, gi * FT_GROUP_W:(gi + 1) * FT_GROUP_W], wc, preferred_element_type=F32).astype(BF16)
          for gi in range(D // FT_GROUP_W)]
    return jnp.concatenate([u[:, :FT_GROUP_W] for u in us] + [u[:, FT_GROUP_W:] for u in us], axis=1)


def _ft_a_body(n, nj, x_ref, mod_ref, g_ref, perm_ref, wc_ref, ma_ref, tc_ref, ts_ref, yr_ref, yi_ref):
    h3 = _normmod(x_ref[...], g_ref[...], mod_ref[1:2, :], mod_ref[0:1, :])
    h = jnp.dot(perm_ref[...], h3.reshape(n * nj, D).astype(BF16), preferred_element_type=F32).astype(BF16)
    u = _channel_dft(h, wc_ref[...])
    for j in range(nj):
        uj = u[j * n:(j + 1) * n]
        y = jnp.dot(ma_ref[...], jnp.concatenate([uj[:, :D], uj[:, D:]], axis=0), preferred_element_type=F32)
        yr, yi = y[:n], y[n:]
        tc = _lane_tile(tc_ref[0, :, j * LANES:(j + 1) * LANES])
        ts = _lane_tile(ts_ref[0, :, j * LANES:(j + 1) * LANES])
        yr_ref[:, j, :] = yr * tc + yi * ts
        yi_ref[:, j, :] = yi * tc - yr * ts


def _ft_c_body(n, nj, yr_ref, yi_ref, mc_ref, wf_ref, x_ref, mod_ref, o_ref):
    fs = []
    for j in range(nj):
        ys = jnp.concatenate([yr_ref[j].astype(BF16), yi_ref[j].astype(BF16)], axis=0)
        fs.append(jnp.dot(mc_ref[...], ys, preferred_element_type=F32).astype(BF16))
    z = jnp.dot(jnp.concatenate(fs, axis=0), wf_ref[...], preferred_element_type=F32)
    gate = mod_ref[2:3, :]
    for j in range(nj):
        o_ref[:, j, :] = x_ref[:, j, :] + gate * z[j * n:(j + 1) * n]


def _ft_ctx_body(x_ref, mod_ref, g_ref, wc_ref, ml_ref, wf_ref, o_ref):
    x = x_ref[...]
    h = _normmod(x, g_ref[...], mod_ref[1:2, :], mod_ref[0:1, :]).astype(BF16)
    u = _channel_dft(h, wc_ref[...])
    us = jnp.concatenate([u[:, :D], u[:, D:]], axis=0)
    f = jnp.dot(ml_ref[...], us, preferred_element_type=F32).astype(BF16)
    o_ref[...] = x + mod_ref[2:3, :] * jnp.dot(f, wf_ref[...], preferred_element_type=F32)


def fourier_layer(x, xc, mod, modc, g, w_f, need_ctx):
    t = x.shape[0]
    n = math.isqrt(t)
    assert n * n == t and n % 16 == 0
    w_f_b = w_f.astype(BF16)
    cw, sw = _dft_mats(FT_GROUP_W)
    wc = jnp.asarray(np.concatenate([cw, -sw], axis=1) / math.sqrt(FT_GROUP_W), dtype=F32).astype(BF16)
    wcspec = _full((FT_GROUP_W, 2 * FT_GROUP_W))
    cn, sn = _dft_mats(n)
    ma = jnp.asarray(np.block([[cn, sn], [-sn, cn]]) / math.sqrt(n), dtype=F32).astype(BF16)
    mc = jnp.asarray(np.concatenate([cn, sn], axis=1) / math.sqrt(n), dtype=F32).astype(BF16)
    nj = 8
    ang = 2.0 * np.pi * np.outer(np.arange(n), np.arange(n)) / t
    def expand(tab):
        a = jnp.asarray(tab, dtype=F32).reshape(n // nj, nj, n).transpose(0, 2, 1)
        return jnp.repeat(a, LANES, axis=2)
    twc, tws = expand(np.cos(ang)), expand(np.sin(ang))
    xblk = pl.BlockSpec((n, nj, D), lambda b: (0, b, 0))
    yblk = pl.BlockSpec((nj, n, D), lambda b: (b, 0, 0))
    tblk = pl.BlockSpec((1, n, nj * LANES), lambda b: (b, 0, 0))
    x3 = x.reshape(n, n, D)
    src = (np.arange(n)[None, :] * nj + np.arange(nj)[:, None]).reshape(-1)
    perm = jnp.asarray(np.eye(n * nj)[src], dtype=BF16)
    yr, yi = pl.pallas_call(
        functools.partial(_ft_a_body, n, nj),
        grid=(n // nj,),
        in_specs=[xblk, _full((ADA_CHUNKS, D)), _full((1, D)), _full((n * nj, n * nj)), wcspec,
                  _full((2 * n, 2 * n)), tblk, tblk],
        out_specs=[xblk, xblk],
        out_shape=[jax.ShapeDtypeStruct((n, n, D), F32)] * 2,
        compiler_params=_cparams(("arbitrary",)),
        name="ft_stage_a",
    )(x3, mod, g, perm, wc, ma, twc, tws)
    x_new = pl.pallas_call(
        functools.partial(_ft_c_body, n, nj),
        grid=(n // nj,),
        in_specs=[yblk, yblk, _full((n, 2 * n)), _full((D, D)), xblk, _full((ADA_CHUNKS, D))],
        out_specs=xblk,
        out_shape=jax.ShapeDtypeStruct((n, n, D), F32),
        compiler_params=_cparams(("arbitrary",)),
        name="ft_stage_c",
    )(yr, yi, mc, w_f_b, x3, mod).reshape(t, D)
    xc_new = None
    if need_ctx:
        lc = xc.shape[0]
        cl, sl = _dft_mats(lc)
        ml = jnp.asarray(np.concatenate([cl, sl], axis=1) / math.sqrt(lc), dtype=F32).astype(BF16)
        xc_new = pl.pallas_call(
            _ft_ctx_body,
            grid=(1,),
            in_specs=[_full((lc, D)), _full((ADA_CHUNKS, D)), _full((1, D)), wcspec,
                      _full((lc, 2 * lc)), _full((D, D))],
            out_specs=_full((lc, D)),
            out_shape=jax.ShapeDtypeStruct((lc, D), F32),
            compiler_params=_cparams(("arbitrary",)),
            name="ft_ctx",
        )(xc, modc, g, wc, ml, w_f_b)
    return x_new, xc_new


def _router_body(x_ref, mod_ref, g_ref, r_ref, info_ref, w0_ref, w1_ref):
    h = _normmod(x_ref[...], g_ref[...], mod_ref[4:5, :], mod_ref[3:4, :])
    hh = h.astype(BF16)
    hl = (h - hh.astype(F32)).astype(BF16)
    r = r_ref[...]
    rh = r.astype(BF16)
    rl = (r - rh.astype(F32)).astype(BF16)
    logits = (jnp.dot(hh, rh, preferred_element_type=F32) + jnp.dot(hh, rl, preferred_element_type=F32)
              + jnp.dot(hl, rh, preferred_element_type=F32))
    lane = lax.broadcasted_iota(jnp.int32, logits.shape, 1)
    logits = jnp.where(lane < N_EXPERTS, logits, NEG_BIG)
    v0 = jnp.max(logits, axis=-1, keepdims=True)
    i0 = jnp.min(jnp.where(logits == v0, lane, LANES), axis=-1, keepdims=True)
    rest = jnp.where(lane == i0, NEG_BIG, logits)
    v1 = jnp.max(rest, axis=-1, keepdims=True)
    i1 = jnp.min(jnp.where(rest == v1, lane, LANES), axis=-1, keepdims=True)
    e = jnp.exp(v1 - v0)
    w0 = 1.0 / (1.0 + e)
    w1 = e / (1.0 + e)
    info_ref[...] = jnp.where(lane == 0, i0, jnp.where(lane == 1, i1, 0))
    w0_ref[...] = jnp.broadcast_to(w0, logits.shape)
    w1_ref[...] = jnp.broadcast_to(w1, logits.shape)


def moe_router(x, mod, g, router_pad, tm):
    t = x.shape[0]
    spec = pl.BlockSpec((tm, D), lambda i: (i, 0))
    lspec = pl.BlockSpec((tm, LANES), lambda i: (i, 0))
    return pl.pallas_call(
        _router_body,
        grid=(t // tm,),
        in_specs=[spec, _full((ADA_CHUNKS, D)), _full((1, D)), _full((D, LANES))],
        out_specs=[lspec, lspec, lspec],
        out_shape=[jax.ShapeDtypeStruct((t, LANES), jnp.int32),
                   jax.ShapeDtypeStruct((t, LANES), F32), jax.ShapeDtypeStruct((t, LANES), F32)],
        compiler_params=_cparams(("arbitrary",)),
        name="moe_router",
    )(x, mod, g, router_pad)


SCATTER_TOKENS = 256


def _row_scatter_body(nlat, nctx, didx_ref, g_ref, x_ref, mod_ref, *rest):
    if nctx:
        xc_ref, modc_ref, dst_ref, h_scr, zero_scr, sems = rest
    else:
        dst_ref, h_scr, zero_scr, sems = rest
    i = pl.program_id(0)
    nsteps = pl.num_programs(0)
    ts = SCATTER_TOKENS
    slot = i % 2

    def start_all(src_ref, src_is_zero_rows):
        def issue(grp, c):
            base = pl.multiple_of(grp * SUBLANES, SUBLANES)
            for r in range(SUBLANES):
                src = src_ref.at[pl.ds(r if src_is_zero_rows else base + r, 1), :]
                for half in range(2):
                    d = didx_ref[0, 0, base + r + half * ts]
                    pltpu.make_async_copy(src, dst_ref.at[pl.ds(d, 1), :], sems.at[slot]).start(priority=half)
            return c
        lax.fori_loop(0, ts // SUBLANES, issue, 0)

    def wait_all(which):
        def drain(n, c):
            pltpu.make_async_copy(zero_scr.at[pl.ds(0, 1), :], dst_ref.at[pl.ds(0, 1), :], sems.at[which]).wait()
            return c
        lax.fori_loop(0, 2 * ts, drain, 0, unroll=8)

    @pl.when(i == 0)
    def _():
        zero_scr[...] = jnp.zeros_like(zero_scr)

    def stage_and_start(src_ref, m_ref):
        h_scr[slot] = _normmod(src_ref[...], g_ref[...], m_ref[4:5, :], m_ref[3:4, :])
        start_all(h_scr.at[slot], False)

    @pl.when(i < nlat)
    def _():
        stage_and_start(x_ref, mod_ref)

    if nctx:
        @pl.when(jnp.logical_and(i >= nlat, i < nlat + nctx))
        def _():
            stage_and_start(xc_ref, modc_ref)

    @pl.when(i >= nlat + nctx)
    def _():
        start_all(zero_scr, True)

    @pl.when(i > 0)
    def _():
        wait_all(1 - slot)

    @pl.when(i == nsteps - 1)
    def _():
        wait_all(slot)


def row_scatter(x, xc, mod, modc, g, d0, d1, pad_pos):
    ts = SCATTER_TOKENS
    nlat = x.shape[0] // ts
    nctx = 0 if xc is None else 1
    assert xc is None or xc.shape[0] == ts
    ntok = nlat + nctx
    npad = pad_pos.shape[0] // (2 * ts)
    didx = jnp.concatenate([jnp.concatenate([d0.reshape(ntok, 1, ts), d1.reshape(ntok, 1, ts)], axis=2),
                            pad_pos.reshape(npad, 1, 2 * ts)], axis=0)
    in_specs = [pl.BlockSpec((1, 1, 2 * ts), lambda i: (i, 0, 0), memory_space=pltpu.SMEM),
                _full((1, D)),
                pl.BlockSpec((ts, D), lambda i: (jnp.minimum(i, nlat - 1), 0)),
                _full((ADA_CHUNKS, D))]
    args = [didx, g, x, mod]
    if nctx:
        in_specs += [_full((ts, D)), _full((ADA_CHUNKS, D))]
        args += [xc, modc]
    return pl.pallas_call(
        functools.partial(_row_scatter_body, nlat, nctx),
        grid=(ntok + npad,),
        in_specs=in_specs,
        out_specs=pl.BlockSpec(memory_space=pl.ANY),
        out_shape=jax.ShapeDtypeStruct((2 * ntok * ts + pad_pos.shape[0], D), F32),
        scratch_shapes=[pltpu.VMEM((2, ts, D), F32), pltpu.VMEM((SUBLANES, D), F32), pltpu.SemaphoreType.DMA((2,))],
        compiler_params=_cparams(("arbitrary",)),
        name="moe_row_scatter",
    )(*args)


def _moe_ffn_body(te_ref, tv_ref, xg_ref, wg_ref, wu_ref, wd_ref, o_ref):
    i = pl.program_id(0)

    @pl.when(tv_ref[i] > 0)
    def _():
        o_ref[...] = _swiglu_chunks(xg_ref[...].astype(BF16), wg_ref, wu_ref, wd_ref)

    @pl.when(tv_ref[i] == 0)
    def _():
        o_ref[...] = jnp.zeros_like(o_ref)


def moe_ffn(xg, tile_e, tile_v, w_gu, w_down, li, tm):
    p = xg.shape[0]
    grid_spec = pltpu.PrefetchScalarGridSpec(
        num_scalar_prefetch=2,
        grid=(p // tm,),
        in_specs=[
            pl.BlockSpec((tm, D), lambda i, te, tv: (i, 0)),
            pl.BlockSpec((None, None, D, D_FF), lambda i, te, tv: (li, te[i], 0, 0)),
            pl.BlockSpec((None, None, D, D_FF), lambda i, te, tv: (li, te[i], 0, 1)),
            pl.BlockSpec((None, None, D_FF, D), lambda i, te, tv: (li, te[i], 0, 0)),
        ],
        out_specs=pl.BlockSpec((tm, D), lambda i, te, tv: (i, 0)),
    )
    return pl.pallas_call(
        _moe_ffn_body,
        grid_spec=grid_spec,
        out_shape=jax.ShapeDtypeStruct((p, D), F32),
        compiler_params=pltpu.CompilerParams(dimension_semantics=("arbitrary",), vmem_limit_bytes=MOE_VMEM_LIMIT),
        name="moe_ffn",
    )(tile_e, tile_v, xg, w_gu, w_gu, w_down)


def _combine_body(tt, d0_ref, d1_ref, d0n_ref, d1n_ref, yp_ref, x_ref, mod_ref, w0_ref, w1_ref, o_ref,
                  a_scr, b_scr, sems):
    i = pl.program_id(0)
    nsteps = pl.num_programs(0)
    slot = i % 2

    def start_all(i0_ref, i1_ref, which):
        def issue(grp, c):
            base = pl.multiple_of(grp * SUBLANES, SUBLANES)
            for r in range(SUBLANES):
                n = base + r
                pltpu.make_async_copy(yp_ref.at[pl.ds(i0_ref[0, 0, n], 1), :], a_scr.at[which, pl.ds(n, 1), :],
                                      sems.at[which]).start(priority=0)
                pltpu.make_async_copy(yp_ref.at[pl.ds(i1_ref[0, 0, n], 1), :], b_scr.at[which, pl.ds(n, 1), :],
                                      sems.at[which]).start(priority=1)
            return c
        lax.fori_loop(0, tt // SUBLANES, issue, 0)

    @pl.when(i == 0)
    def _():
        start_all(d0_ref, d1_ref, 0)

    @pl.when(i + 1 < nsteps)
    def _():
        start_all(d0n_ref, d1n_ref, 1 - slot)

    def drain(n, c):
        pltpu.make_async_copy(yp_ref.at[pl.ds(0, 1), :], a_scr.at[slot, pl.ds(0, 1), :], sems.at[slot]).wait()
        pltpu.make_async_copy(yp_ref.at[pl.ds(0, 1), :], b_scr.at[slot, pl.ds(0, 1), :], sems.at[slot]).wait()
        return c
    lax.fori_loop(0, tt, drain, 0, unroll=8)
    w0 = _lane_tile(w0_ref[...])
    w1 = _lane_tile(w1_ref[...])
    o_ref[...] = x_ref[...] + mod_ref[5:6, :] * (w0 * a_scr[slot] + w1 * b_scr[slot])


def moe_combine(yp, d0, d1, x, mod, w0b, w1b, tt):
    t = x.shape[0]
    nt = t // tt
    ispec = pl.BlockSpec((1, 1, tt), lambda i: (i, 0, 0), memory_space=pltpu.SMEM)
    nspec = pl.BlockSpec((1, 1, tt), lambda i: (jnp.minimum(i + 1, nt - 1), 0, 0), memory_space=pltpu.SMEM)
    spec = pl.BlockSpec((tt, D), lambda i: (i, 0))
    lspec = pl.BlockSpec((tt, LANES), lambda i: (i, 0))
    d0r, d1r = d0.reshape(nt, 1, tt), d1.reshape(nt, 1, tt)
    return pl.pallas_call(
        functools.partial(_combine_body, tt),
        grid=(nt,),
        in_specs=[ispec, ispec, nspec, nspec, pl.BlockSpec(memory_space=pl.ANY), spec, _full((ADA_CHUNKS, D)),
                  lspec, lspec],
        out_specs=spec,
        out_shape=jax.ShapeDtypeStruct((t, D), F32),
        scratch_shapes=[pltpu.VMEM((2, tt, D), F32), pltpu.VMEM((2, tt, D), F32), pltpu.SemaphoreType.DMA((2,))],
        compiler_params=_cparams(("arbitrary",)),
        name="moe_combine",
    )(d0r, d1r, d0r, d1r, yp, x, mod, w0b, w1b)


def _route_plan(e0, e1, tm):
    t = e0.shape[0]
    n = 2 * t
    ex = jnp.arange(N_EXPERTS, dtype=jnp.int32)
    oh0 = (e0[:, None] == ex[None, :]).astype(jnp.int32)
    oh1 = (e1[:, None] == ex[None, :]).astype(jnp.int32)
    both = oh0 + oh1
    csum = jnp.cumsum(both, axis=0)
    before = csum - both
    counts = csum[-1]
    padded = ((counts + tm - 1) // tm) * tm
    pad_end = jnp.cumsum(padded)
    pad_off = pad_end - padded
    total = pad_end[-1]
    d0 = jnp.sum(oh0 * (before + pad_off[None, :]), axis=1)
    d1 = jnp.sum(oh1 * (before + oh0 + pad_off[None, :]), axis=1)
    gap = padded - counts
    tail_off = jnp.cumsum(tm - gap) - (tm - gap)
    r = jnp.arange(tm, dtype=jnp.int32)[None, :]
    pad_pos = jnp.where(r < gap[:, None], (pad_off + counts)[:, None] + r,
                        total + tail_off[:, None] + (r - gap[:, None])).reshape(-1)
    ntiles = (n + N_EXPERTS * tm) // tm
    tstart = jnp.arange(ntiles, dtype=jnp.int32) * tm
    tile_v = (tstart < total).astype(jnp.int32)
    tile_e = jnp.sum((jnp.minimum(tstart, total - 1)[:, None] >= pad_end[None, :]).astype(jnp.int32), axis=1)
    return (d0.astype(jnp.int32), d1.astype(jnp.int32), pad_pos.astype(jnp.int32), tile_e.astype(jnp.int32), tile_v)


def moe_layer(x, xc, mod, modc, g, router, w_gu_b, w_down_b, li, need_ctx, tm=512):
    router_pad = jnp.pad(router, ((0, 0), (0, LANES - N_EXPERTS)))
    s = x.shape[0]
    info, w0b, w1b = moe_router(x, mod, g, router_pad, 1024)
    if need_ctx:
        sc = xc.shape[0]
        infoc, w0c, w1c = moe_router(xc, modc, g, router_pad, sc)
        e0 = jnp.concatenate([info[:, 0], infoc[:, 0]])
        e1 = jnp.concatenate([info[:, 1], infoc[:, 1]])
    else:
        e0, e1 = info[:, 0], info[:, 1]
    d0, d1, pad_pos, tile_e, tile_v = _route_plan(e0, e1, tm)
    xg = row_scatter(x, xc if need_ctx else None, mod, modc, g, d0, d1, pad_pos)
    yp = moe_ffn(xg, tile_e, tile_v, w_gu_b, w_down_b, li, tm)
    x_new = moe_combine(yp, d0[:s], d1[:s], x, mod, w0b, w1b, 512)
    xc_new = None
    if need_ctx:
        xc_new = moe_combine(yp, d0[s:], d1[s:], xc, modc, w0c, w1c, sc)
    return x_new, xc_new


def kernel(x, c, ctx, c_ctx, ada_w, ada_b, norm_g, rg_w_in, rg_conv_w, rg_conv_b, rg_wa, rg_ba, rg_wi, rg_bi,
           rg_lambda, rg_w_out, na_w_qkv, na_q_g, na_k_g, na_rpb, na_w_o, ft_w_out, ffn_w_gu, ffn_w_down,
           moe_router, moe_w_gu, moe_w_down):
    depth = ada_w.shape[0]
    assert x.shape[0] == 1 and x.shape[2] == D
    xs = x[0]
    xc = ctx[0]
    mods = ada_modulation(c, c_ctx, ada_w, ada_b)
    ffn_gu_b, ffn_dn_b = ffn_w_gu.astype(BF16), ffn_w_down.astype(BF16)
    moe_gu_b = moe_dn_b = None
    mix_idx = [0] * N_MIXERS
    dense_idx = 0
    moe_idx = 0
    for layer in range(depth):
        need_ctx = layer != depth - 1
        mod, modc = mods[layer, 0], mods[layer, 1]
        g0 = norm_g[layer, 0][None]
        g1 = norm_g[layer, 1][None]
        kind = layer % N_MIXERS
        j = mix_idx[kind]
        mix_idx[kind] += 1
        if kind == 0:
            xs, xcn = rglru_layer(xs, xc, mod, modc, g0, rg_w_in[j], rg_conv_w[j], rg_conv_b[j], rg_wa[j], rg_wi[j],
                                  rg_ba[j], rg_bi[j], rg_lambda[j], rg_w_out[j], need_ctx)
        elif kind == 1:
            side = ()
            if moe_gu_b is None:
                side = (moe_w_gu.reshape(-1, moe_w_gu.shape[-1]), moe_w_down.reshape(-1, moe_w_down.shape[-1]))
            xs, xcn, side_b = na_layer(xs, xc, mod, modc, g0, na_w_qkv[j], na_q_g[j], na_k_g[j], na_rpb[j],
                                       na_w_o[j], need_ctx, side)
            if side_b:
                moe_gu_b, moe_dn_b = side_b[0].reshape(moe_w_gu.shape), side_b[1].reshape(moe_w_down.shape)
        else:
            xs, xcn = fourier_layer(xs, xc, mod, modc, g0, ft_w_out[j], need_ctx)
        if need_ctx:
            xc = xcn
        if layer % 2 == 0:
            if need_ctx:
                xc = ffn_dense(xc, modc, g1, ffn_gu_b, ffn_dn_b, dense_idx, xc.shape[0])
            xs = ffn_dense(xs, mod, g1, ffn_gu_b, ffn_dn_b, dense_idx, 512)
            dense_idx += 1
        else:
            if moe_gu_b is None:
                moe_gu_b, moe_dn_b = moe_w_gu.astype(BF16), moe_w_down.astype(BF16)
            xs, xcn = moe_layer(xs, xc, mod, modc, g1, moe_router[moe_idx], moe_gu_b, moe_dn_b, moe_idx, need_ctx)
            moe_idx += 1
            if need_ctx:
                xc = xcn
    return xs[None]
```

```python
import functools
import math

import numpy as np
import jax
import jax.numpy as jnp
from jax import lax
from jax.experimental import pallas as pl
from jax.experimental.pallas import tpu as pltpu

F32 = jnp.float32
BF16 = jnp.bfloat16

D = 1024
D_FF = 3584
N_EXPERTS = 8
GRID_W = 64
NA_HEADS = 16
NA_HEAD_DIM = 64
NA_ROWS = 8
NA_COLS = 16
FT_GROUP_W = 256
RG_BLOCK_W = 256
RMS_EPS = 1e-6
LRU_C = 8.0
N_MIXERS = 3
ADA_CHUNKS = 6

LANES = 128
SUBLANES = 8
VMEM_LIMIT = 56 * 1024 * 1024
MOE_VMEM_LIMIT = 60 * 1024 * 1024

ROW_TILE = 512
WIDE_TILE = 1024
NEG_BIG = -1e30
LOG2E = math.log2(math.e)


def _cparams(sem):
    return pltpu.CompilerParams(dimension_semantics=sem, vmem_limit_bytes=VMEM_LIMIT)


def _full(shape):
    nd = len(shape)
    return pl.BlockSpec(shape, lambda *_: (0,) * nd)


def _normmod(x, g, scale, shift):
    ms = jnp.mean(x * x, axis=-1, keepdims=True)
    y = x * lax.rsqrt(ms + RMS_EPS)
    return (y * g) * (1.0 + scale) + shift


def _lane_tile(v):
    return jnp.concatenate([v] * (D // LANES), axis=1)


def _sigmoid(v):
    return 1.0 / (1.0 + jnp.exp(-v))


def _gelu_tanh(v):
    c = math.sqrt(2.0 / math.pi)
    return v * (0.5 * (1.0 + jnp.tanh(c * (v + 0.044715 * (v * v * v)))))


def _ada_body(cin_ref, w_ref, b_ref, o_ref):
    v = cin_ref[...]
    s = v * _sigmoid(v)
    w = w_ref[0]
    r0 = jnp.sum(s[:, 0:1] * w, axis=0, keepdims=True)
    r1 = jnp.sum(s[:, 1:2] * w, axis=0, keepdims=True)
    o_ref[0] = jnp.concatenate([r0, r1], axis=0) + b_ref[0]


def ada_modulation(c, c_ctx, ada_w, ada_b):
    depth = ada_w.shape[0]
    n = ada_w.shape[2]
    nc = n // 4
    cin = jnp.stack([c[0], c_ctx], axis=1)
    out = pl.pallas_call(
        _ada_body,
        grid=(depth, n // nc),
        in_specs=[
            pl.BlockSpec((D, 2), lambda l, j: (0, 0)),
            pl.BlockSpec((1, D, nc), lambda l, j: (l, 0, j)),
            pl.BlockSpec((1, 1, nc), lambda l, j: (l, 0, j)),
        ],
        out_specs=pl.BlockSpec((1, 2, nc), lambda l, j: (l, 0, j)),
        out_shape=jax.ShapeDtypeStruct((depth, 2, n), F32),
        compiler_params=_cparams(("arbitrary", "arbitrary")),
        name="ada_mod",
    )(cin, ada_w, ada_b.reshape(depth, 1, n))
    return out.reshape(depth, 2, ADA_CHUNKS, D)


FFN_CHUNK = 512


def _swiglu_chunks(h, wg_ref, wu_ref, wd_ref):
    acc = None
    for c in range(D_FF // FFN_CHUNK):
        sl = slice(c * FFN_CHUNK, (c + 1) * FFN_CHUNK)
        gg = jnp.dot(h, wg_ref[:, sl], preferred_element_type=F32)
        uu = jnp.dot(h, wu_ref[:, sl], preferred_element_type=F32)
        a = ((gg * _sigmoid(gg)) * uu).astype(BF16)
        part = jnp.dot(a, wd_ref[sl, :], preferred_element_type=F32)
        acc = part if acc is None else acc + part
    return acc


def _ffn_body(x_ref, mod_ref, g_ref, wg_ref, wu_ref, wd_ref, o_ref):
    x = x_ref[...]
    h = _normmod(x, g_ref[...], mod_ref[4:5, :], mod_ref[3:4, :]).astype(BF16)
    o_ref[...] = x + mod_ref[5:6, :] * _swiglu_chunks(h, wg_ref, wu_ref, wd_ref)


def ffn_dense(x, mod, g, w_gu, w_down, li, tm):
    t = x.shape[0]
    once = pl.Buffered(1)
    return pl.pallas_call(
        _ffn_body,
        grid=(t // tm,),
        in_specs=[
            pl.BlockSpec((tm, D), lambda i: (i, 0)),
            _full((ADA_CHUNKS, D)),
            _full((1, D)),
            pl.BlockSpec((None, D, D_FF), lambda i: (li, 0, 0), pipeline_mode=once),
            pl.BlockSpec((None, D, D_FF), lambda i: (li, 0, 1), pipeline_mode=once),
            pl.BlockSpec((None, D_FF, D), lambda i: (li, 0, 0), pipeline_mode=once),
        ],
        out_specs=pl.BlockSpec((tm, D), lambda i: (i, 0)),
        out_shape=jax.ShapeDtypeStruct((t, D), F32),
        compiler_params=_cparams(("arbitrary",)),
        name="ffn_dense",
    )(x, mod, g, w_gu, w_gu, w_down)


def _proj_body(gate_row, a_ref, w_ref, x_ref, mod_ref, o_ref):
    y = jnp.dot(a_ref[...], w_ref[...], preferred_element_type=F32)
    o_ref[...] = x_ref[...] + mod_ref[gate_row:gate_row + 1, :] * y


def proj_residual(a, w, x, mod, gate_row, tm):
    t, k = a.shape
    return pl.pallas_call(
        functools.partial(_proj_body, gate_row),
        grid=(t // tm,),
        in_specs=[
            pl.BlockSpec((tm, k), lambda i: (i, 0)),
            _full((k, D)),
            pl.BlockSpec((tm, D), lambda i: (i, 0)),
            _full((ADA_CHUNKS, D)),
        ],
        out_specs=pl.BlockSpec((tm, D), lambda i: (i, 0)),
        out_shape=jax.ShapeDtypeStruct((t, D), F32),
        compiler_params=_cparams(("arbitrary",)),
        name="proj_residual",
    )(a, w, x, mod)


HALO = SUBLANES
RG_IN_PIECE = 128


def _rg_in_body(tm, xp_ref, x_ref, xn_ref, mod_ref, g_ref, w_ref, cw_ref, cb_ref, xc_ref, gg_ref):
    i = pl.program_id(0)
    last = pl.num_programs(0) - 1
    xa = jnp.concatenate([xp_ref[...], x_ref[...], xn_ref[...]], axis=0)
    npiece = tm // RG_IN_PIECE
    bounds = [0] + [2 * HALO + RG_IN_PIECE * (k + 1) for k in range(npiece - 1)] + [tm + 2 * HALO]
    zs = []
    for k in range(npiece):
        hk = _normmod(xa[bounds[k]:bounds[k + 1]], g_ref[...], mod_ref[1:2, :], mod_ref[0:1, :]).astype(BF16)
        zs.append(jnp.dot(hk, w_ref[...], preferred_element_type=F32))
    z = jnp.concatenate(zs, axis=0)
    row = lax.broadcasted_iota(jnp.int32, (tm + 2 * HALO, 1), 0)
    valid = jnp.logical_and(jnp.logical_or(row >= HALO, i > 0),
                            jnp.logical_or(row < tm + HALO, i < last))
    xz = jnp.where(valid, z[:, :D], 0.0)
    y = cb_ref[...] + cw_ref[2:3, :] * xz[HALO:HALO + tm]
    y = y + cw_ref[0:1, :] * xz[HALO - 2:HALO - 2 + tm]
    y = y + cw_ref[1:2, :] * xz[HALO - 1:HALO - 1 + tm]
    y = y + cw_ref[3:4, :] * xz[HALO + 1:HALO + 1 + tm]
    xc_ref[...] = y
    gg_ref[...] = _gelu_tanh(z[HALO:HALO + tm, D:]).astype(BF16)


def rg_in(x, mod, g, w_in, conv_w, conv_b, tm):
    t = x.shape[0]
    nb = tm // HALO
    nblk = t // HALO
    return pl.pallas_call(
        functools.partial(_rg_in_body, tm),
        grid=(t // tm,),
        in_specs=[
            pl.BlockSpec((HALO, D), lambda i: (jnp.maximum(i * nb - 1, 0), 0)),
            pl.BlockSpec((tm, D), lambda i: (i, 0)),
            pl.BlockSpec((HALO, D), lambda i: (jnp.minimum((i + 1) * nb, nblk - 1), 0)),
            _full((ADA_CHUNKS, D)),
            _full((1, D)),
            _full((D, 2 * D)),
            _full((4, D)),
            _full((1, D)),
        ],
        out_specs=[pl.BlockSpec((tm, D), lambda i: (i, 0)), pl.BlockSpec((tm, D), lambda i: (i, 0))],
        out_shape=[jax.ShapeDtypeStruct((t, D), F32), jax.ShapeDtypeStruct((t, D), BF16)],
        compiler_params=_cparams(("arbitrary",)),
        name="rg_in",
    )(x, x, x, mod, g, w_in, conv_w, conv_b)


def _rg_gates(xc, wa_ref, wi_ref, ba, bi, lam):
    xb = xc.astype(BF16)
    nblk = D // RG_BLOCK_W
    r = jnp.concatenate([jnp.dot(xb[:, n * RG_BLOCK_W:(n + 1) * RG_BLOCK_W], wa_ref[n],
                                 preferred_element_type=F32) for n in range(nblk)], axis=1)
    ig = jnp.concatenate([jnp.dot(xb[:, n * RG_BLOCK_W:(n + 1) * RG_BLOCK_W], wi_ref[n],
                                  preferred_element_type=F32) for n in range(nblk)], axis=1)
    t_r = jnp.tanh(r + 0.5 * ba)
    t_i = jnp.tanh(ig + 0.5 * bi)
    nl = -lam
    softplus = jnp.maximum(nl, 0.0) + jnp.log1p(jnp.exp(-jnp.abs(nl)))
    half_c = (-0.5 * LRU_C) * softplus
    log_a = half_c + half_c * t_r
    a = jnp.exp(log_a)
    xh = 0.5 * xc
    b = jnp.sqrt(1.0 - a * a) * (xh + xh * t_i)
    return a, b


def _rg_scan_body(reverse, epilogue, emit_h, tc, *refs):
    xc_ref, wa_ref, wi_ref, ba_ref, bi_ref, lam_ref, h0_ref = refs[:7]
    refs = refs[7:]
    if epilogue:
        hf_ref, gg_ref, wo_ref, x_ref, mod_ref = refs[:5]
        refs = refs[5:]
    if emit_h:
        h_ref = refs[0]
        refs = refs[1:]
    if epilogue:
        o_ref = refs[0]
        refs = refs[1:]
    a_scr, b_scr, h_scr, carry_scr = refs
    c = pl.program_id(0)

    @pl.when(c == 0)
    def _():
        carry_scr[...] = jnp.broadcast_to(h0_ref[...], (SUBLANES, D))

    a, b = _rg_gates(xc_ref[...], wa_ref, wi_ref, ba_ref[...], bi_ref[...], lam_ref[...])
    a_scr[...] = a
    b_scr[...] = b
    nblk = tc // SUBLANES
    row = lax.broadcasted_iota(jnp.int32, (SUBLANES, D), 0)
    first = (row == SUBLANES - 1) if reverse else (row == 0)

    def block(n, carry):
        blk = (nblk - 1 - n) if reverse else n
        off = pl.multiple_of(blk * SUBLANES, SUBLANES)
        av = a_scr[pl.ds(off, SUBLANES), :]
        bv = b_scr[pl.ds(off, SUBLANES), :]
        bv = jnp.where(first, av * carry + bv, bv)
        av = jnp.where(first, 0.0, av)
        for k in (1, 2, 4):
            shift = (SUBLANES - k) if reverse else k
            bv = av * pltpu.roll(bv, shift, 0) + bv
            if k != 4:
                av = av * pltpu.roll(av, shift, 0)
        h_scr[pl.ds(off, SUBLANES), :] = bv
        edge = bv[0:1, :] if reverse else bv[SUBLANES - 1:SUBLANES, :]
        return jnp.broadcast_to(edge, (SUBLANES, D))

    carry_scr[...] = lax.fori_loop(0, nblk, block, carry_scr[...], unroll=2)

    if emit_h:
        h_ref[...] = h_scr[...].astype(h_ref.dtype)
    if epilogue:
        y = ((hf_ref[...].astype(F32) + h_scr[...]) * gg_ref[...].astype(F32)).astype(BF16)
        o_ref[...] = x_ref[...] + mod_ref[2:3, :] * jnp.dot(y, wo_ref[...], preferred_element_type=F32)


def rg_scan(xconv, wa, wi, ba, bi, lam, h0, tc, reverse, epi=None, h_dtype=F32):
    t = xconv.shape[0]
    nchunks = t // tc
    idx = (lambda c: (nchunks - 1 - c, 0)) if reverse else (lambda c: (c, 0))
    nb = D // RG_BLOCK_W
    blk = pl.BlockSpec((tc, D), idx)
    in_specs = [
        blk,
        _full((nb, RG_BLOCK_W, RG_BLOCK_W)),
        _full((nb, RG_BLOCK_W, RG_BLOCK_W)),
        _full((1, D)), _full((1, D)), _full((1, D)), _full((1, D)),
    ]
    args = [xconv, wa, wi, ba, bi, lam, h0]
    out_specs = []
    out_shape = []
    if epi is not None:
        hf, gg, w_out, x, mod = epi
        in_specs += [blk, blk, _full((D, D)), blk, _full((ADA_CHUNKS, D))]
        args += [hf, gg, w_out, x, mod]
    if h_dtype is not None:
        out_specs.append(blk)
        out_shape.append(jax.ShapeDtypeStruct((t, D), h_dtype))
    if epi is not None:
        out_specs.append(blk)
        out_shape.append(jax.ShapeDtypeStruct((t, D), F32))
    return pl.pallas_call(
        functools.partial(_rg_scan_body, reverse, epi is not None, h_dtype is not None, tc),
        grid=(nchunks,),
        in_specs=in_specs,
        out_specs=out_specs,
        out_shape=out_shape,
        scratch_shapes=[pltpu.VMEM((tc, D), F32), pltpu.VMEM((tc, D), F32), pltpu.VMEM((tc, D), F32),
                        pltpu.VMEM((SUBLANES, D), F32)],
        compiler_params=_cparams(("arbitrary",)),
        name="rg_scan_bwd" if reverse else "rg_scan_fwd",
    )(*args)


def rglru_layer(x, xc, mod, modc, g, w_in, conv_w, conv_b, wa, wi, ba, bi, lam, w_out, need_ctx):
    w_in_b = w_in.astype(BF16)
    wa_b = (0.5 * wa).astype(BF16)
    wi_b = (0.5 * wi).astype(BF16)
    w_out_b = w_out.astype(BF16)
    cb = conv_b[None]
    tcx = xc.shape[0]
    xcl, ggl = rg_in(x, mod, g, w_in_b, conv_w, cb, ROW_TILE)
    xcc, ggc = rg_in(xc, modc, g, w_in_b, conv_w, cb, tcx)
    zeros = jnp.zeros((1, D), F32)
    p = lambda d: (wa_b[d], wi_b[d], ba[d][None], bi[d][None], lam[d][None])
    (hcf,) = rg_scan(xcc, *p(0), zeros, tcx, False)
    (hlf,) = rg_scan(xcl, *p(0), hcf[tcx - 1:tcx], ROW_TILE, False, h_dtype=BF16)
    if need_ctx:
        hcb, xc_new = rg_scan(xcc, *p(1), zeros, tcx, True, epi=(hcf, ggc, w_out_b, xc, modc))
    else:
        (hcb,) = rg_scan(xcc, *p(1), zeros, tcx, True)
        xc_new = None
    (x_new,) = rg_scan(xcl, *p(1), hcb[0:1], ROW_TILE, True, epi=(hlf, ggl, w_out_b, x, mod), h_dtype=None)
    return x_new, xc_new


def _qkv_body(x_ref, mod_ref, g_ref, w_ref, gm_ref, qg_ref, kg_ref, q_ref, k_ref, v_ref):
    h = _normmod(x_ref[...], g_ref[...], mod_ref[1:2, :], mod_ref[0:1, :]).astype(BF16)
    z = jnp.dot(h, w_ref[...], preferred_element_type=F32)

    def headnorm(v, gain):
        ms = jnp.dot((v * v).astype(BF16), gm_ref[...], preferred_element_type=F32)
        return (v * lax.rsqrt(ms + RMS_EPS)) * gain

    q_ref[...] = headnorm(z[:, :D], qg_ref[...]).astype(BF16)
    k_ref[...] = headnorm(z[:, D:2 * D], kg_ref[...]).astype(BF16)
    v_ref[...] = z[:, 2 * D:].astype(BF16)


def qkv_proj(x, mod, g, w_qkv, gmean, qg, kg, tm):
    t = x.shape[0]
    spec = pl.BlockSpec((tm, D), lambda i: (i, 0))
    return pl.pallas_call(
        _qkv_body,
        grid=(t // tm,),
        in_specs=[spec, _full((ADA_CHUNKS, D)), _full((1, D)), _full((D, 3 * D)), _full((D, D)),
                  _full((1, D)), _full((1, D))],
        out_specs=[spec, spec, spec],
        out_shape=[jax.ShapeDtypeStruct((t, D), BF16)] * 3,
        compiler_params=_cparams(("arbitrary",)),
        name="qkv_proj",
    )(x, mod, g, w_qkv, gmean, qg, kg)


def _attend_pair(q2, keys, vals, biases):
    m_rows = q2.shape[0]
    lane = lax.broadcasted_iota(jnp.int32, q2.shape, 1)
    zero = jnp.zeros_like(q2)
    qs = jnp.concatenate([jnp.where(lane < NA_HEAD_DIM, q2, zero), jnp.where(lane >= NA_HEAD_DIM, q2, zero)], axis=0)
    ss = []
    for kseg, bseg in zip(keys, biases):
        s = lax.dot_general(qs, kseg, (((1,), (1,)), ((), ())), preferred_element_type=F32)
        if bseg is not None:
            s = s + jnp.concatenate([bseg[0], bseg[1]], axis=0)
        ss.append(s)
    m = ss[0].max(axis=-1, keepdims=True)
    for s in ss[1:]:
        m = jnp.maximum(m, s.max(axis=-1, keepdims=True))
    den = None
    acc = None
    for s, vseg in zip(ss, vals):
        p = jnp.exp2(s - m)
        d = jnp.sum(p, axis=-1, keepdims=True)
        o = jnp.dot(p.astype(BF16), vseg, preferred_element_type=F32)
        den = d if den is None else den + d
        acc = o if acc is None else acc + o
    out = acc / den
    return jnp.where(lane < NA_HEAD_DIM, out[:m_rows], out[m_rows:])


NA_QROWS = 2
NA_UNION = NA_ROWS + NA_QROWS - 1


def _na_body(nside, var_ref, q_ref, kl_ref, vl_ref, kc_ref, vc_ref, bias_ref, *refs):
    side_in, o_ref, side_out = refs[:nside], refs[nside], refs[nside + 1:]
    for src, dst in zip(side_in, side_out):
        dst[...] = src[...].astype(BF16)
    for pr in range(NA_HEADS // 2):
        sl = slice(pr * LANES, (pr + 1) * LANES)
        o_ref[:, sl] = _attend_pair(
            q_ref[:, sl], [kl_ref[:, sl], kc_ref[:, sl]], [vl_ref[:, sl], vc_ref[:, sl]],
            [(bias_ref[0, 2 * pr], bias_ref[0, 2 * pr + 1]), None]).astype(BF16)


def _na_geometry(rows):
    steps = rows // NA_QROWS
    g = np.arange(steps)
    base = np.clip(NA_QROWS * g - NA_ROWS // 2, 0, rows - NA_UNION)
    r = NA_QROWS * g[:, None] + np.arange(NA_QROWS)[None, :]
    rs = np.clip(r - NA_ROWS // 2, 0, rows - NA_ROWS)
    key = np.concatenate([(base - NA_QROWS * g)[:, None], rs - r], axis=1)
    uniq, first, var = np.unique(key, axis=0, return_index=True, return_inverse=True)
    return base, var.reshape(-1).astype(np.int32), g[first]


def na_attention(q, k, v, kc, vc, bias_tab, var, side=()):
    t = q.shape[0]
    rows = t // GRID_W
    nctx = kc.shape[0]
    steps = rows // NA_QROWS
    side_specs = []
    for a in side:
        assert a.shape[0] % (steps * 16) == 0
        side_specs.append(pl.BlockSpec((a.shape[0] // steps, a.shape[1]), lambda g, var: (g, 0)))

    def kbase(g):
        return jnp.clip(NA_QROWS * g - NA_ROWS // 2, 0, rows - NA_UNION)

    qrows = NA_QROWS * GRID_W
    nloc = NA_UNION * GRID_W
    kspec = pl.BlockSpec((pl.Element(nloc), pl.Element(D)), lambda g, var: (kbase(g) * GRID_W, 0))
    grid_spec = pltpu.PrefetchScalarGridSpec(
        num_scalar_prefetch=1,
        grid=(steps,),
        in_specs=[pl.BlockSpec((qrows, D), lambda g, var: (g, 0)), kspec, kspec] + [
            pl.BlockSpec((nctx, D), lambda g, var: (0, 0)), pl.BlockSpec((nctx, D), lambda g, var: (0, 0)),
            pl.BlockSpec((1, NA_HEADS, qrows, nloc), lambda g, var: (var[g], 0, 0, 0)),
        ] + side_specs,
        out_specs=[pl.BlockSpec((qrows, D), lambda g, var: (g, 0))] + side_specs,
    )
    return pl.pallas_call(
        functools.partial(_na_body, len(side)),
        grid_spec=grid_spec,
        out_shape=[jax.ShapeDtypeStruct((t, D), BF16)] + [jax.ShapeDtypeStruct(a.shape, BF16) for a in side],
        compiler_params=_cparams(("arbitrary",)),
        name="na_attention",
    )(var, q, k, v, kc, vc, bias_tab, *side)


def _ctx_attn_body(q_ref, k_ref, v_ref, o_ref):
    for pr in range(NA_HEADS // 2):
        sl = slice(pr * LANES, (pr + 1) * LANES)
        o_ref[:, sl] = _attend_pair(q_ref[:, sl], [k_ref[:, sl]], [v_ref[:, sl]], [None]).astype(BF16)


def ctx_attention(q, k, v):
    t = q.shape[0]
    return pl.pallas_call(
        _ctx_attn_body,
        grid=(1,),
        in_specs=[_full((t, D))] * 3,
        out_specs=_full((t, D)),
        out_shape=jax.ShapeDtypeStruct((t, D), BF16),
        compiler_params=_cparams(("arbitrary",)),
        name="ctx_attention",
    )(q, k, v)


def _na_bias_table(rpb, rows):
    base, var, reps = _na_geometry(rows)
    cols = np.arange(GRID_W)
    cstart = np.clip(cols - NA_COLS // 2, 0, GRID_W - NA_COLS)
    kcol = np.arange(GRID_W)
    inwin = (kcol[None, :] >= cstart[:, None]) & (kcol[None, :] < cstart[:, None] + NA_COLS)
    r = NA_QROWS * reps[:, None] + np.arange(NA_QROWS)[None, :]
    rs = np.clip(r - NA_ROWS // 2, 0, rows - NA_ROWS)
    krow = base[reps][:, None] + np.arange(NA_UNION)[None, :]
    rvalid = (krow[:, None, :] >= rs[:, :, None]) & (krow[:, None, :] < rs[:, :, None] + NA_ROWS)
    ridx = np.clip(krow[:, None, :] - r[:, :, None] + (NA_ROWS - 1), 0, 2 * NA_ROWS - 2)
    nd = 2 * NA_COLS - 1
    w = jnp.pad(rpb.astype(F32), ((0, 0), (0, 0), (GRID_W - NA_COLS, 2 * GRID_W - (GRID_W - NA_COLS) - nd)))
    flat = jnp.tile(w, (1, 1, GRID_W))[:, :, :GRID_W * (2 * GRID_W - 1)]
    blk = flat.reshape(NA_HEADS, 2 * NA_ROWS - 1, GRID_W, 2 * GRID_W - 1)[..., GRID_W - 1:]
    blk = jnp.where(jnp.asarray(inwin)[None, None], blk, NEG_BIG)
    neg = jnp.full((NA_HEADS, GRID_W, GRID_W), NEG_BIG, F32)
    variants = []
    for v in range(len(reps)):
        strips = [jnp.concatenate([blk[:, ridx[v, a, j]] if rvalid[v, a, j] else neg for j in range(NA_UNION)], axis=2)
                  for a in range(NA_QROWS)]
        variants.append(jnp.concatenate(strips, axis=1))
    return jnp.stack(variants, axis=0), jnp.asarray(var)


def na_layer(x, xc, mod, modc, g, w_qkv, q_g, k_g, rpb, w_o, need_ctx, side=()):
    w_qkv_b = w_qkv.astype(BF16)
    w_o_b = w_o.astype(BF16)
    gm = np.kron(np.eye(NA_HEADS), np.full((NA_HEAD_DIM, NA_HEAD_DIM), 1.0 / NA_HEAD_DIM))
    gmean = jnp.asarray(gm, dtype=BF16)
    qg = jnp.tile(q_g, NA_HEADS)[None] * (NA_HEAD_DIM ** -0.5 * LOG2E)
    kg = jnp.tile(k_g, NA_HEADS)[None]
    q, k, v = qkv_proj(x, mod, g, w_qkv_b, gmean, qg, kg, ROW_TILE)
    qc, kc, vc = qkv_proj(xc, modc, g, w_qkv_b, gmean, qg, kg, xc.shape[0])
    bias_tab, var = _na_bias_table(rpb * LOG2E, x.shape[0] // GRID_W)
    o, *side_b = na_attention(q, k, v, kc, vc, bias_tab, var, side)
    x_new = proj_residual(o, w_o_b, x, mod, 2, WIDE_TILE)
    xc_new = None
    if need_ctx:
        oc = ctx_attention(qc, kc, vc)
        xc_new = proj_residual(oc, w_o_b, xc, modc, 2, xc.shape[0])
    return x_new, xc_new, side_b


def _dft_mats(n):
    ang = 2.0 * np.pi * np.outer(np.arange(n), np.arange(n)) / n
    return np.cos(ang), np.sin(ang)


def _channel_dft(h, wc):
    us = [jnp.dot(h[:, gi * FT_GROUP_W:(gi + 1) * FT_GROUP_W], wc, preferred_element_type=F32).astype(BF16)
          for gi in range(D // FT_GROUP_W)]
    return jnp.concatenate([u[:, :FT_GROUP_W] for u in us] + [u[:, FT_GROUP_W:] for u in us], axis=1)


def _ft_a_body(n, nj, x_ref, mod_ref, g_ref, perm_ref, wc_ref, ma_ref, tc_ref, ts_ref, yr_ref, yi_ref):
    h3 = _normmod(x_ref[...], g_ref[...], mod_ref[1:2, :], mod_ref[0:1, :])
    h = jnp.dot(perm_ref[...], h3.reshape(n * nj, D).astype(BF16), preferred_element_type=F32).astype(BF16)
    u = _channel_dft(h, wc_ref[...])
    for j in range(nj):
        uj = u[j * n:(j + 1) * n]
        y = jnp.dot(ma_ref[...], jnp.concatenate([uj[:, :D], uj[:, D:]], axis=0), preferred_element_type=F32)
        yr, yi = y[:n], y[n:]
        tc = _lane_tile(tc_ref[0, :, j * LANES:(j + 1) * LANES])
        ts = _lane_tile(ts_ref[0, :, j * LANES:(j + 1) * LANES])
        yr_ref[:, j, :] = yr * tc + yi * ts
        yi_ref[:, j, :] = yi * tc - yr * ts


def _ft_c_body(n, nj, yr_ref, yi_ref, mc_ref, wf_ref, x_ref, mod_ref, o_ref):
    fs = []
    for j in range(nj):
        ys = jnp.concatenate([yr_ref[j].astype(BF16), yi_ref[j].astype(BF16)], axis=0)
        fs.append(jnp.dot(mc_ref[...], ys, preferred_element_type=F32).astype(BF16))
    z = jnp.dot(jnp.concatenate(fs, axis=0), wf_ref[...], preferred_element_type=F32)
    gate = mod_ref[2:3, :]
    for j in range(nj):
        o_ref[:, j, :] = x_ref[:, j, :] + gate * z[j * n:(j + 1) * n]


def _ft_ctx_body(x_ref, mod_ref, g_ref, wc_ref, ml_ref, wf_ref, o_ref):
    x = x_ref[...]
    h = _normmod(x, g_ref[...], mod_ref[1:2, :], mod_ref[0:1, :]).astype(BF16)
    u = _channel_dft(h, wc_ref[...])
    us = jnp.concatenate([u[:, :D], u[:, D:]], axis=0)
    f = jnp.dot(ml_ref[...], us, preferred_element_type=F32).astype(BF16)
    o_ref[...] = x + mod_ref[2:3, :] * jnp.dot(f, wf_ref[...], preferred_element_type=F32)


def fourier_layer(x, xc, mod, modc, g, w_f, need_ctx):
    t = x.shape[0]
    n = math.isqrt(t)
    assert n * n == t and n % 16 == 0
    w_f_b = w_f.astype(BF16)
    cw, sw = _dft_mats(FT_GROUP_W)
    wc = jnp.asarray(np.concatenate([cw, -sw], axis=1) / math.sqrt(FT_GROUP_W), dtype=F32).astype(BF16)
    wcspec = _full((FT_GROUP_W, 2 * FT_GROUP_W))
    cn, sn = _dft_mats(n)
    ma = jnp.asarray(np.block([[cn, sn], [-sn, cn]]) / math.sqrt(n), dtype=F32).astype(BF16)
    mc = jnp.asarray(np.concatenate([cn, sn], axis=1) / math.sqrt(n), dtype=F32).astype(BF16)
    nj = 8
    ang = 2.0 * np.pi * np.outer(np.arange(n), np.arange(n)) / t
    def expand(tab):
        a = jnp.asarray(tab, dtype=F32).reshape(n // nj, nj, n).transpose(0, 2, 1)
        return jnp.repeat(a, LANES, axis=2)
    twc, tws = expand(np.cos(ang)), expand(np.sin(ang))
    xblk = pl.BlockSpec((n, nj, D), lambda b: (0, b, 0))
    yblk = pl.BlockSpec((nj, n, D), lambda b: (b, 0, 0))
    tblk = pl.BlockSpec((1, n, nj * LANES), lambda b: (b, 0, 0))
    x3 = x.reshape(n, n, D)
    src = (np.arange(n)[None, :] * nj + np.arange(nj)[:, None]).reshape(-1)
    perm = jnp.asarray(np.eye(n * nj)[src], dtype=BF16)
    yr, yi = pl.pallas_call(
        functools.partial(_ft_a_body, n, nj),
        grid=(n // nj,),
        in_specs=[xblk, _full((ADA_CHUNKS, D)), _full((1, D)), _full((n * nj, n * nj)), wcspec,
                  _full((2 * n, 2 * n)), tblk, tblk],
        out_specs=[xblk, xblk],
        out_shape=[jax.ShapeDtypeStruct((n, n, D), F32)] * 2,
        compiler_params=_cparams(("arbitrary",)),
        name="ft_stage_a",
    )(x3, mod, g, perm, wc, ma, twc, tws)
    x_new = pl.pallas_call(
        functools.partial(_ft_c_body, n, nj),
        grid=(n // nj,),
        in_specs=[yblk, yblk, _full((n, 2 * n)), _full((D, D)), xblk, _full((ADA_CHUNKS, D))],
        out_specs=xblk,
        out_shape=jax.ShapeDtypeStruct((n, n, D), F32),
        compiler_params=_cparams(("arbitrary",)),
        name="ft_stage_c",
    )(yr, yi, mc, w_f_b, x3, mod).reshape(t, D)
    xc_new = None
    if need_ctx:
        lc = xc.shape[0]
        cl, sl = _dft_mats(lc)
        ml = jnp.asarray(np.concatenate([cl, sl], axis=1) / math.sqrt(lc), dtype=F32).astype(BF16)
        xc_new = pl.pallas_call(
            _ft_ctx_body,
            grid=(1,),
            in_specs=[_full((lc, D)), _full((ADA_CHUNKS, D)), _full((1, D)), wcspec,
                      _full((lc, 2 * lc)), _full((D, D))],
            out_specs=_full((lc, D)),
            out_shape=jax.ShapeDtypeStruct((lc, D), F32),
            compiler_params=_cparams(("arbitrary",)),
            name="ft_ctx",
        )(xc, modc, g, wc, ml, w_f_b)
    return x_new, xc_new


def _router_body(x_ref, mod_ref, g_ref, r_ref, info_ref, w0_ref, w1_ref):
    h = _normmod(x_ref[...], g_ref[...], mod_ref[4:5, :], mod_ref[3:4, :])
    hh = h.astype(BF16)
    hl = (h - hh.astype(F32)).astype(BF16)
    r = r_ref[...]
    rh = r.astype(BF16)
    rl = (r - rh.astype(F32)).astype(BF16)
    logits = (jnp.dot(hh, rh, preferred_element_type=F32) + jnp.dot(hh, rl, preferred_element_type=F32)
              + jnp.dot(hl, rh, preferred_element_type=F32))
    lane = lax.broadcasted_iota(jnp.int32, logits.shape, 1)
    logits = jnp.where(lane < N_EXPERTS, logits, NEG_BIG)
    v0 = jnp.max(logits, axis=-1, keepdims=True)
    i0 = jnp.min(jnp.where(logits == v0, lane, LANES), axis=-1, keepdims=True)
    rest = jnp.where(lane == i0, NEG_BIG, logits)
    v1 = jnp.max(rest, axis=-1, keepdims=True)
    i1 = jnp.min(jnp.where(rest == v1, lane, LANES), axis=-1, keepdims=True)
    e = jnp.exp(v1 - v0)
    w0 = 1.0 / (1.0 + e)
    w1 = e / (1.0 + e)
    info_ref[...] = jnp.where(lane == 0, i0, jnp.where(lane == 1, i1, 0))
    w0_ref[...] = jnp.broadcast_to(w0, logits.shape)
    w1_ref[...] = jnp.broadcast_to(w1, logits.shape)


def moe_router(x, mod, g, router_pad, tm):
    t = x.shape[0]
    spec = pl.BlockSpec((tm, D), lambda i: (i, 0))
    lspec = pl.BlockSpec((tm, LANES), lambda i: (i, 0))
    return pl.pallas_call(
        _router_body,
        grid=(t // tm,),
        in_specs=[spec, _full((ADA_CHUNKS, D)), _full((1, D)), _full((D, LANES))],
        out_specs=[lspec, lspec, lspec],
        out_shape=[jax.ShapeDtypeStruct((t, LANES), jnp.int32),
                   jax.ShapeDtypeStruct((t, LANES), F32), jax.ShapeDtypeStruct((t, LANES), F32)],
        compiler_params=_cparams(("arbitrary",)),
        name="moe_router",
    )(x, mod, g, router_pad)


SCATTER_TOKENS = 256


def _row_scatter_body(nlat, nctx, didx_ref, g_ref, x_ref, mod_ref, *rest):
    if nctx:
        xc_ref, modc_ref, dst_ref, h_scr, zero_scr, sems = rest
    else:
        dst_ref, h_scr, zero_scr, sems = rest
    i = pl.program_id(0)
    nsteps = pl.num_programs(0)
    ts = SCATTER_TOKENS
    slot = i % 2

    def start_all(src_ref, src_is_zero_rows):
        def issue(grp, c):
            base = pl.multiple_of(grp * SUBLANES, SUBLANES)
            for r in range(SUBLANES):
                src = src_ref.at[pl.ds(r if src_is_zero_rows else base + r, 1), :]
                for half in range(2):
                    d = didx_ref[0, 0, base + r + half * ts]
                    pltpu.make_async_copy(src, dst_ref.at[pl.ds(d, 1), :], sems.at[slot]).start(priority=half)
            return c
        lax.fori_loop(0, ts // SUBLANES, issue, 0)

    def wait_all(which):
        def drain(n, c):
            pltpu.make_async_copy(zero_scr.at[pl.ds(0, 1), :], dst_ref.at[pl.ds(0, 1), :], sems.at[which]).wait()
            return c
        lax.fori_loop(0, 2 * ts, drain, 0, unroll=8)

    @pl.when(i == 0)
    def _():
        zero_scr[...] = jnp.zeros_like(zero_scr)

    def stage_and_start(src_ref, m_ref):
        h_scr[slot] = _normmod(src_ref[...], g_ref[...], m_ref[4:5, :], m_ref[3:4, :])
        start_all(h_scr.at[slot], False)

    @pl.when(i < nlat)
    def _():
        stage_and_start(x_ref, mod_ref)

    if nctx:
        @pl.when(jnp.logical_and(i >= nlat, i < nlat + nctx))
        def _():
            stage_and_start(xc_ref, modc_ref)

    @pl.when(i >= nlat + nctx)
    def _():
        start_all(zero_scr, True)

    @pl.when(i > 0)
    def _():
        wait_all(1 - slot)

    @pl.when(i == nsteps - 1)
    def _():
        wait_all(slot)


def row_scatter(x, xc, mod, modc, g, d0, d1, pad_pos):
    ts = SCATTER_TOKENS
    nlat = x.shape[0] // ts
    nctx = 0 if xc is None else 1
    assert xc is None or xc.shape[0] == ts
    ntok = nlat + nctx
    npad = pad_pos.shape[0] // (2 * ts)
    didx = jnp.concatenate([jnp.concatenate([d0.reshape(ntok, 1, ts), d1.reshape(ntok, 1, ts)], axis=2),
                            pad_pos.reshape(npad, 1, 2 * ts)], axis=0)
    in_specs = [pl.BlockSpec((1, 1, 2 * ts), lambda i: (i, 0, 0), memory_space=pltpu.SMEM),
                _full((1, D)),
                pl.BlockSpec((ts, D), lambda i: (jnp.minimum(i, nlat - 1), 0)),
                _full((ADA_CHUNKS, D))]
    args = [didx, g, x, mod]
    if nctx:
        in_specs += [_full((ts, D)), _full((ADA_CHUNKS, D))]
        args += [xc, modc]
    return pl.pallas_call(
        functools.partial(_row_scatter_body, nlat, nctx),
        grid=(ntok + npad,),
        in_specs=in_specs,
        out_specs=pl.BlockSpec(memory_space=pl.ANY),
        out_shape=jax.ShapeDtypeStruct((2 * ntok * ts + pad_pos.shape[0], D), F32),
        scratch_shapes=[pltpu.VMEM((2, ts, D), F32), pltpu.VMEM((SUBLANES, D), F32), pltpu.SemaphoreType.DMA((2,))],
        compiler_params=_cparams(("arbitrary",)),
        name="moe_row_scatter",
    )(*args)


def _moe_ffn_body(te_ref, tv_ref, xg_ref, wg_ref, wu_ref, wd_ref, o_ref):
    i = pl.program_id(0)

    @pl.when(tv_ref[i] > 0)
    def _():
        o_ref[...] = _swiglu_chunks(xg_ref[...].astype(BF16), wg_ref, wu_ref, wd_ref)

    @pl.when(tv_ref[i] == 0)
    def _():
        o_ref[...] = jnp.zeros_like(o_ref)


def moe_ffn(xg, tile_e, tile_v, w_gu, w_down, li, tm):
    p = xg.shape[0]
    grid_spec = pltpu.PrefetchScalarGridSpec(
        num_scalar_prefetch=2,
        grid=(p // tm,),
        in_specs=[
            pl.BlockSpec((tm, D), lambda i, te, tv: (i, 0)),
            pl.BlockSpec((None, None, D, D_FF), lambda i, te, tv: (li, te[i], 0, 0)),
            pl.BlockSpec((None, None, D, D_FF), lambda i, te, tv: (li, te[i], 0, 1)),
            pl.BlockSpec((None, None, D_FF, D), lambda i, te, tv: (li, te[i], 0, 0)),
        ],
        out_specs=pl.BlockSpec((tm, D), lambda i, te, tv: (i, 0)),
    )
    return pl.pallas_call(
        _moe_ffn_body,
        grid_spec=grid_spec,
        out_shape=jax.ShapeDtypeStruct((p, D), F32),
        compiler_params=pltpu.CompilerParams(dimension_semantics=("arbitrary",), vmem_limit_bytes=MOE_VMEM_LIMIT),
        name="moe_ffn",
    )(tile_e, tile_v, xg, w_gu, w_gu, w_down)


def _combine_body(tt, d0_ref, d1_ref, d0n_ref, d1n_ref, yp_ref, x_ref, mod_ref, w0_ref, w1_ref, o_ref,
                  a_scr, b_scr, sems):
    i = pl.program_id(0)
    nsteps = pl.num_programs(0)
    slot = i % 2

    def start_all(i0_ref, i1_ref, which):
        def issue(grp, c):
            base = pl.multiple_of(grp * SUBLANES, SUBLANES)
            for r in range(SUBLANES):
                n = base + r
                pltpu.make_async_copy(yp_ref.at[pl.ds(i0_ref[0, 0, n], 1), :], a_scr.at[which, pl.ds(n, 1), :],
                                      sems.at[which]).start(priority=0)
                pltpu.make_async_copy(yp_ref.at[pl.ds(i1_ref[0, 0, n], 1), :], b_scr.at[which, pl.ds(n, 1), :],
                                      sems.at[which]).start(priority=1)
            return c
        lax.fori_loop(0, tt // SUBLANES, issue, 0)

    @pl.when(i == 0)
    def _():
        start_all(d0_ref, d1_ref, 0)

    @pl.when(i + 1 < nsteps)
    def _():
        start_all(d0n_ref, d1n_ref, 1 - slot)

    def drain(n, c):
        pltpu.make_async_copy(yp_ref.at[pl.ds(0, 1), :], a_scr.at[slot, pl.ds(0, 1), :], sems.at[slot]).wait()
        pltpu.make_async_copy(yp_ref.at[pl.ds(0, 1), :], b_scr.at[slot, pl.ds(0, 1), :], sems.at[slot]).wait()
        return c
    lax.fori_loop(0, tt, drain, 0, unroll=8)
    w0 = _lane_tile(w0_ref[...])
    w1 = _lane_tile(w1_ref[...])
    o_ref[...] = x_ref[...] + mod_ref[5:6, :] * (w0 * a_scr[slot] + w1 * b_scr[slot])


def moe_combine(yp, d0, d1, x, mod, w0b, w1b, tt):
    t = x.shape[0]
    nt = t // tt
    ispec = pl.BlockSpec((1, 1, tt), lambda i: (i, 0, 0), memory_space=pltpu.SMEM)
    nspec = pl.BlockSpec((1, 1, tt), lambda i: (jnp.minimum(i + 1, nt - 1), 0, 0), memory_space=pltpu.SMEM)
    spec = pl.BlockSpec((tt, D), lambda i: (i, 0))
    lspec = pl.BlockSpec((tt, LANES), lambda i: (i, 0))
    d0r, d1r = d0.reshape(nt, 1, tt), d1.reshape(nt, 1, tt)
    return pl.pallas_call(
        functools.partial(_combine_body, tt),
        grid=(nt,),
        in_specs=[ispec, ispec, nspec, nspec, pl.BlockSpec(memory_space=pl.ANY), spec, _full((ADA_CHUNKS, D)),
                  lspec, lspec],
        out_specs=spec,
        out_shape=jax.ShapeDtypeStruct((t, D), F32),
        scratch_shapes=[pltpu.VMEM((2, tt, D), F32), pltpu.VMEM((2, tt, D), F32), pltpu.SemaphoreType.DMA((2,))],
        compiler_params=_cparams(("arbitrary",)),
        name="moe_combine",
    )(d0r, d1r, d0r, d1r, yp, x, mod, w0b, w1b)


def _route_plan(e0, e1, tm):
    t = e0.shape[0]
    n = 2 * t
    ex = jnp.arange(N_EXPERTS, dtype=jnp.int32)
    oh0 = (e0[:, None] == ex[None, :]).astype(jnp.int32)
    oh1 = (e1[:, None] == ex[None, :]).astype(jnp.int32)
    both = oh0 + oh1
    csum = jnp.cumsum(both, axis=0)
    before = csum - both
    counts = csum[-1]
    padded = ((counts + tm - 1) // tm) * tm
    pad_end = jnp.cumsum(padded)
    pad_off = pad_end - padded
    total = pad_end[-1]
    d0 = jnp.sum(oh0 * (before + pad_off[None, :]), axis=1)
    d1 = jnp.sum(oh1 * (before + oh0 + pad_off[None, :]), axis=1)
    gap = padded - counts
    tail_off = jnp.cumsum(tm - gap) - (tm - gap)
    r = jnp.arange(tm, dtype=jnp.int32)[None, :]
    pad_pos = jnp.where(r < gap[:, None], (pad_off + counts)[:, None] + r,
                        total + tail_off[:, None] + (r - gap[:, None])).reshape(-1)
    ntiles = (n + N_EXPERTS * tm) // tm
    tstart = jnp.arange(ntiles, dtype=jnp.int32) * tm
    tile_v = (tstart < total).astype(jnp.int32)
    tile_e = jnp.sum((jnp.minimum(tstart, total - 1)[:, None] >= pad_end[None, :]).astype(jnp.int32), axis=1)
    return (d0.astype(jnp.int32), d1.astype(jnp.int32), pad_pos.astype(jnp.int32), tile_e.astype(jnp.int32), tile_v)


def moe_layer(x, xc, mod, modc, g, router, w_gu_b, w_down_b, li, need_ctx, tm=ROW_TILE):
    router_pad = jnp.pad(router, ((0, 0), (0, LANES - N_EXPERTS)))
    s = x.shape[0]
    info, w0b, w1b = moe_router(x, mod, g, router_pad, WIDE_TILE)
    if need_ctx:
        sc = xc.shape[0]
        infoc, w0c, w1c = moe_router(xc, modc, g, router_pad, sc)
        e0 = jnp.concatenate([info[:, 0], infoc[:, 0]])
        e1 = jnp.concatenate([info[:, 1], infoc[:, 1]])
    else:
        e0, e1 = info[:, 0], info[:, 1]
    d0, d1, pad_pos, tile_e, tile_v = _route_plan(e0, e1, tm)
    xg = row_scatter(x, xc if need_ctx else None, mod, modc, g, d0, d1, pad_pos)
    yp = moe_ffn(xg, tile_e, tile_v, w_gu_b, w_down_b, li, tm)
    x_new = moe_combine(yp, d0[:s], d1[:s], x, mod, w0b, w1b, ROW_TILE)
    xc_new = None
    if need_ctx:
        xc_new = moe_combine(yp, d0[s:], d1[s:], xc, modc, w0c, w1c, sc)
    return x_new, xc_new


def kernel(x, c, ctx, c_ctx, ada_w, ada_b, norm_g, rg_w_in, rg_conv_w, rg_conv_b, rg_wa, rg_ba, rg_wi, rg_bi,
           rg_lambda, rg_w_out, na_w_qkv, na_q_g, na_k_g, na_rpb, na_w_o, ft_w_out, ffn_w_gu, ffn_w_down,
           moe_router, moe_w_gu, moe_w_down):
    depth = ada_w.shape[0]
    assert x.shape[0] == 1 and x.shape[2] == D
    xs = x[0]
    xc = ctx[0]
    mods = ada_modulation(c, c_ctx, ada_w, ada_b)
    ffn_gu_b, ffn_dn_b = ffn_w_gu.astype(BF16), ffn_w_down.astype(BF16)
    moe_gu_b = moe_dn_b = None
    mix_idx = [0] * N_MIXERS
    dense_idx = 0
    moe_idx = 0
    for layer in range(depth):
        need_ctx = layer != depth - 1
        mod, modc = mods[layer, 0], mods[layer, 1]
        g0 = norm_g[layer, 0][None]
        g1 = norm_g[layer, 1][None]
        kind = layer % N_MIXERS
        j = mix_idx[kind]
        mix_idx[kind] += 1
        if kind == 0:
            xs, xcn = rglru_layer(xs, xc, mod, modc, g0, rg_w_in[j], rg_conv_w[j], rg_conv_b[j], rg_wa[j], rg_wi[j],
                                  rg_ba[j], rg_bi[j], rg_lambda[j], rg_w_out[j], need_ctx)
        elif kind == 1:
            side = ()
            if moe_gu_b is None:
                side = (moe_w_gu.reshape(-1, moe_w_gu.shape[-1]), moe_w_down.reshape(-1, moe_w_down.shape[-1]))
            xs, xcn, side_b = na_layer(xs, xc, mod, modc, g0, na_w_qkv[j], na_q_g[j], na_k_g[j], na_rpb[j],
                                       na_w_o[j], need_ctx, side)
            if side_b:
                moe_gu_b, moe_dn_b = side_b[0].reshape(moe_w_gu.shape), side_b[1].reshape(moe_w_down.shape)
        else:
            xs, xcn = fourier_layer(xs, xc, mod, modc, g0, ft_w_out[j], need_ctx)
        if need_ctx:
            xc = xcn
        if layer % 2 == 0:
            if need_ctx:
                xc = ffn_dense(xc, modc, g1, ffn_gu_b, ffn_dn_b, dense_idx, xc.shape[0])
            xs = ffn_dense(xs, mod, g1, ffn_gu_b, ffn_dn_b, dense_idx, ROW_TILE)
            dense_idx += 1
        else:
            if moe_gu_b is None:
                moe_gu_b, moe_dn_b = moe_w_gu.astype(BF16), moe_w_down.astype(BF16)
            xs, xcn = moe_layer(xs, xc, mod, modc, g1, moe_router[moe_idx], moe_gu_b, moe_dn_b, moe_idx, need_ctx)
            moe_idx += 1
            if need_ctx:
                xc = xcn
    return xs[None]
```

```python
import functools
import math

import numpy as np
import jax
import jax.numpy as jnp
from jax import lax
from jax.experimental import pallas as pl
from jax.experimental.pallas import tpu as pltpu

F32 = jnp.float32
BF16 = jnp.bfloat16

D = 1024
D_FF = 3584
N_EXPERTS = 8
GRID_W = 64
NA_HEADS = 16
NA_HEAD_DIM = 64
NA_ROWS = 8
NA_COLS = 16
FT_GROUP_W = 256
RG_BLOCK_W = 256
RMS_EPS = 1e-6
LRU_C = 8.0
N_MIXERS = 3
ADA_CHUNKS = 6

LANES = 128
SUBLANES = 8
VMEM_LIMIT = 56 * 1024 * 1024
MOE_VMEM_LIMIT = 60 * 1024 * 1024

ROW_TILE = 512
WIDE_TILE = 1024
NEG_BIG = -1e30
LOG2E = math.log2(math.e)


def _cparams(sem):
    return pltpu.CompilerParams(dimension_semantics=sem, vmem_limit_bytes=VMEM_LIMIT)


def _full(shape):
    nd = len(shape)
    return pl.BlockSpec(shape, lambda *_: (0,) * nd)


def _normmod(x, g, scale, shift):
    ms = jnp.mean(x * x, axis=-1, keepdims=True)
    y = x * lax.rsqrt(ms + RMS_EPS)
    return (y * g) * (1.0 + scale) + shift


def _lane_tile(v):
    return jnp.concatenate([v] * (D // LANES), axis=1)


def _sigmoid(v):
    return 1.0 / (1.0 + jnp.exp(-v))


def _gelu_tanh(v):
    c = math.sqrt(2.0 / math.pi)
    return v * (0.5 * (1.0 + jnp.tanh(c * (v + 0.044715 * (v * v * v)))))


def _ada_body(cin_ref, w_ref, b_ref, o_ref):
    v = cin_ref[...]
    s = v * _sigmoid(v)
    w = w_ref[0]
    r0 = jnp.sum(s[:, 0:1] * w, axis=0, keepdims=True)
    r1 = jnp.sum(s[:, 1:2] * w, axis=0, keepdims=True)
    o_ref[0] = jnp.concatenate([r0, r1], axis=0) + b_ref[0]


def ada_modulation(c, c_ctx, ada_w, ada_b):
    depth = ada_w.shape[0]
    n = ada_w.shape[2]
    nc = n // 4
    cin = jnp.stack([c[0], c_ctx], axis=1)
    out = pl.pallas_call(
        _ada_body,
        grid=(depth, n // nc),
        in_specs=[
            pl.BlockSpec((D, 2), lambda l, j: (0, 0)),
            pl.BlockSpec((1, D, nc), lambda l, j: (l, 0, j)),
            pl.BlockSpec((1, 1, nc), lambda l, j: (l, 0, j)),
        ],
        out_specs=pl.BlockSpec((1, 2, nc), lambda l, j: (l, 0, j)),
        out_shape=jax.ShapeDtypeStruct((depth, 2, n), F32),
        compiler_params=_cparams(("arbitrary", "arbitrary")),
        name="ada_mod",
    )(cin, ada_w, ada_b.reshape(depth, 1, n))
    return out.reshape(depth, 2, ADA_CHUNKS, D)


FFN_CHUNK = 512


def _swiglu_chunks(h, wg_ref, wu_ref, wd_ref):
    acc = None
    for c in range(D_FF // FFN_CHUNK):
        sl = slice(c * FFN_CHUNK, (c + 1) * FFN_CHUNK)
        gg = jnp.dot(h, wg_ref[:, sl], preferred_element_type=F32)
        uu = jnp.dot(h, wu_ref[:, sl], preferred_element_type=F32)
        a = ((gg * _sigmoid(gg)) * uu).astype(BF16)
        part = jnp.dot(a, wd_ref[sl, :], preferred_element_type=F32)
        acc = part if acc is None else acc + part
    return acc


def _ffn_body(x_ref, mod_ref, g_ref, wg_ref, wu_ref, wd_ref, o_ref):
    x = x_ref[...]
    h = _normmod(x, g_ref[...], mod_ref[4:5, :], mod_ref[3:4, :]).astype(BF16)
    o_ref[...] = x + mod_ref[5:6, :] * _swiglu_chunks(h, wg_ref, wu_ref, wd_ref)


def ffn_dense(x, mod, g, w_gu, w_down, li, tm):
    t = x.shape[0]
    once = pl.Buffered(1)
    return pl.pallas_call(
        _ffn_body,
        grid=(t // tm,),
        in_specs=[
            pl.BlockSpec((tm, D), lambda i: (i, 0)),
            _full((ADA_CHUNKS, D)),
            _full((1, D)),
            pl.BlockSpec((None, D, D_FF), lambda i: (li, 0, 0), pipeline_mode=once),
            pl.BlockSpec((None, D, D_FF), lambda i: (li, 0, 1), pipeline_mode=once),
            pl.BlockSpec((None, D_FF, D), lambda i: (li, 0, 0), pipeline_mode=once),
        ],
        out_specs=pl.BlockSpec((tm, D), lambda i: (i, 0)),
        out_shape=jax.ShapeDtypeStruct((t, D), F32),
        compiler_params=_cparams(("arbitrary",)),
        name="ffn_dense",
    )(x, mod, g, w_gu, w_gu, w_down)


def _proj_body(gate_row, a_ref, w_ref, x_ref, mod_ref, o_ref):
    y = jnp.dot(a_ref[...], w_ref[...], preferred_element_type=F32)
    o_ref[...] = x_ref[...] + mod_ref[gate_row:gate_row + 1, :] * y


def proj_residual(a, w, x, mod, gate_row, tm):
    t, k = a.shape
    return pl.pallas_call(
        functools.partial(_proj_body, gate_row),
        grid=(t // tm,),
        in_specs=[
            pl.BlockSpec((tm, k), lambda i: (i, 0)),
            _full((k, D)),
            pl.BlockSpec((tm, D), lambda i: (i, 0)),
            _full((ADA_CHUNKS, D)),
        ],
        out_specs=pl.BlockSpec((tm, D), lambda i: (i, 0)),
        out_shape=jax.ShapeDtypeStruct((t, D), F32),
        compiler_params=_cparams(("arbitrary",)),
        name="proj_residual",
    )(a, w, x, mod)


HALO = SUBLANES
RG_IN_PIECE = 128


def _rg_in_body(tm, xp_ref, x_ref, xn_ref, mod_ref, g_ref, w_ref, cw_ref, cb_ref, xc_ref, gg_ref):
    i = pl.program_id(0)
    last = pl.num_programs(0) - 1
    xa = jnp.concatenate([xp_ref[...], x_ref[...], xn_ref[...]], axis=0)
    npiece = tm // RG_IN_PIECE
    bounds = [0] + [2 * HALO + RG_IN_PIECE * (k + 1) for k in range(npiece - 1)] + [tm + 2 * HALO]
    zs = []
    for k in range(npiece):
        hk = _normmod(xa[bounds[k]:bounds[k + 1]], g_ref[...], mod_ref[1:2, :], mod_ref[0:1, :]).astype(BF16)
        zs.append(jnp.dot(hk, w_ref[...], preferred_element_type=F32))
    z = jnp.concatenate(zs, axis=0)
    row = lax.broadcasted_iota(jnp.int32, (tm + 2 * HALO, 1), 0)
    valid = jnp.logical_and(jnp.logical_or(row >= HALO, i > 0),
                            jnp.logical_or(row < tm + HALO, i < last))
    xz = jnp.where(valid, z[:, :D], 0.0)
    y = cb_ref[...] + cw_ref[2:3, :] * xz[HALO:HALO + tm]
    y = y + cw_ref[0:1, :] * xz[HALO - 2:HALO - 2 + tm]
    y = y + cw_ref[1:2, :] * xz[HALO - 1:HALO - 1 + tm]
    y = y + cw_ref[3:4, :] * xz[HALO + 1:HALO + 1 + tm]
    xc_ref[...] = y
    gg_ref[...] = _gelu_tanh(z[HALO:HALO + tm, D:]).astype(BF16)


def rg_in(x, mod, g, w_in, conv_w, conv_b, tm):
    t = x.shape[0]
    nb = tm // HALO
    nblk = t // HALO
    return pl.pallas_call(
        functools.partial(_rg_in_body, tm),
        grid=(t // tm,),
        in_specs=[
            pl.BlockSpec((HALO, D), lambda i: (jnp.maximum(i * nb - 1, 0), 0)),
            pl.BlockSpec((tm, D), lambda i: (i, 0)),
            pl.BlockSpec((HALO, D), lambda i: (jnp.minimum((i + 1) * nb, nblk - 1), 0)),
            _full((ADA_CHUNKS, D)),
            _full((1, D)),
            _full((D, 2 * D)),
            _full((4, D)),
            _full((1, D)),
        ],
        out_specs=[pl.BlockSpec((tm, D), lambda i: (i, 0)), pl.BlockSpec((tm, D), lambda i: (i, 0))],
        out_shape=[jax.ShapeDtypeStruct((t, D), F32), jax.ShapeDtypeStruct((t, D), BF16)],
        compiler_params=_cparams(("arbitrary",)),
        name="rg_in",
    )(x, x, x, mod, g, w_in, conv_w, conv_b)


def _rg_gates(xc, wa_ref, wi_ref, ba, bi, lam):
    xb = xc.astype(BF16)
    nblk = D // RG_BLOCK_W
    r = jnp.concatenate([jnp.dot(xb[:, n * RG_BLOCK_W:(n + 1) * RG_BLOCK_W], wa_ref[n],
                                 preferred_element_type=F32) for n in range(nblk)], axis=1)
    ig = jnp.concatenate([jnp.dot(xb[:, n * RG_BLOCK_W:(n + 1) * RG_BLOCK_W], wi_ref[n],
                                  preferred_element_type=F32) for n in range(nblk)], axis=1)
    t_r = jnp.tanh(r + 0.5 * ba)
    t_i = jnp.tanh(ig + 0.5 * bi)
    nl = -lam
    softplus = jnp.maximum(nl, 0.0) + jnp.log1p(jnp.exp(-jnp.abs(nl)))
    half_c = (-0.5 * LRU_C) * softplus
    log_a = half_c + half_c * t_r
    a = jnp.exp(log_a)
    xh = 0.5 * xc
    b = jnp.sqrt(1.0 - a * a) * (xh + xh * t_i)
    return a, b


def _rg_scan_body(reverse, epilogue, emit_h, tc, *refs):
    xc_ref, wa_ref, wi_ref, ba_ref, bi_ref, lam_ref, h0_ref = refs[:7]
    refs = refs[7:]
    if epilogue:
        hf_ref, gg_ref, wo_ref, x_ref, mod_ref = refs[:5]
        refs = refs[5:]
    if emit_h:
        h_ref = refs[0]
        refs = refs[1:]
    if epilogue:
        o_ref = refs[0]
        refs = refs[1:]
    a_scr, b_scr, h_scr, carry_scr = refs
    c = pl.program_id(0)

    @pl.when(c == 0)
    def _():
        carry_scr[...] = jnp.broadcast_to(h0_ref[...], (SUBLANES, D))

    a, b = _rg_gates(xc_ref[...], wa_ref, wi_ref, ba_ref[...], bi_ref[...], lam_ref[...])
    a_scr[...] = a
    b_scr[...] = b
    nblk = tc // SUBLANES
    row = lax.broadcasted_iota(jnp.int32, (SUBLANES, D), 0)
    first = (row == SUBLANES - 1) if reverse else (row == 0)

    def block(n, carry):
        blk = (nblk - 1 - n) if reverse else n
        off = pl.multiple_of(blk * SUBLANES, SUBLANES)
        av = a_scr[pl.ds(off, SUBLANES), :]
        bv = b_scr[pl.ds(off, SUBLANES), :]
        bv = jnp.where(first, av * carry + bv, bv)
        av = jnp.where(first, 0.0, av)
        for k in (1, 2, 4):
            shift = (SUBLANES - k) if reverse else k
            bv = av * pltpu.roll(bv, shift, 0) + bv
            if k != 4:
                av = av * pltpu.roll(av, shift, 0)
        h_scr[pl.ds(off, SUBLANES), :] = bv
        edge = bv[0:1, :] if reverse else bv[SUBLANES - 1:SUBLANES, :]
        return jnp.broadcast_to(edge, (SUBLANES, D))

    carry_scr[...] = lax.fori_loop(0, nblk, block, carry_scr[...], unroll=2)

    if emit_h:
        h_ref[...] = h_scr[...].astype(h_ref.dtype)
    if epilogue:
        y = ((hf_ref[...].astype(F32) + h_scr[...]) * gg_ref[...].astype(F32)).astype(BF16)
        o_ref[...] = x_ref[...] + mod_ref[2:3, :] * jnp.dot(y, wo_ref[...], preferred_element_type=F32)


def rg_scan(xconv, wa, wi, ba, bi, lam, h0, tc, reverse, epi=None, h_dtype=F32):
    t = xconv.shape[0]
    nchunks = t // tc
    idx = (lambda c: (nchunks - 1 - c, 0)) if reverse else (lambda c: (c, 0))
    nb = D // RG_BLOCK_W
    blk = pl.BlockSpec((tc, D), idx)
    in_specs = [
        blk,
        _full((nb, RG_BLOCK_W, RG_BLOCK_W)),
        _full((nb, RG_BLOCK_W, RG_BLOCK_W)),
        _full((1, D)), _full((1, D)), _full((1, D)), _full((1, D)),
    ]
    args = [xconv, wa, wi, ba, bi, lam, h0]
    out_specs = []
    out_shape = []
    if epi is not None:
        hf, gg, w_out, x, mod = epi
        in_specs += [blk, blk, _full((D, D)), blk, _full((ADA_CHUNKS, D))]
        args += [hf, gg, w_out, x, mod]
    if h_dtype is not None:
        out_specs.append(blk)
        out_shape.append(jax.ShapeDtypeStruct((t, D), h_dtype))
    if epi is not None:
        out_specs.append(blk)
        out_shape.append(jax.ShapeDtypeStruct((t, D), F32))
    return pl.pallas_call(
        functools.partial(_rg_scan_body, reverse, epi is not None, h_dtype is not None, tc),
        grid=(nchunks,),
        in_specs=in_specs,
        out_specs=out_specs,
        out_shape=out_shape,
        scratch_shapes=[pltpu.VMEM((tc, D), F32), pltpu.VMEM((tc, D), F32), pltpu.VMEM((tc, D), F32),
                        pltpu.VMEM((SUBLANES, D), F32)],
        compiler_params=_cparams(("arbitrary",)),
        name="rg_scan_bwd" if reverse else "rg_scan_fwd",
    )(*args)


def rglru_layer(x, xc, mod, modc, g, w_in, conv_w, conv_b, wa, wi, ba, bi, lam, w_out, need_ctx):
    w_in_b = w_in.astype(BF16)
    wa_b = (0.5 * wa).astype(BF16)
    wi_b = (0.5 * wi).astype(BF16)
    w_out_b = w_out.astype(BF16)
    cb = conv_b[None]
    tcx = xc.shape[0]
    xcl, ggl = rg_in(x, mod, g, w_in_b, conv_w, cb, WIDE_TILE)
    xcc, ggc = rg_in(xc, modc, g, w_in_b, conv_w, cb, tcx)
    zeros = jnp.zeros((1, D), F32)
    p = lambda d: (wa_b[d], wi_b[d], ba[d][None], bi[d][None], lam[d][None])
    (hcf,) = rg_scan(xcc, *p(0), zeros, tcx, False)
    (hlf,) = rg_scan(xcl, *p(0), hcf[tcx - 1:tcx], WIDE_TILE, False, h_dtype=BF16)
    if need_ctx:
        hcb, xc_new = rg_scan(xcc, *p(1), zeros, tcx, True, epi=(hcf, ggc, w_out_b, xc, modc))
    else:
        (hcb,) = rg_scan(xcc, *p(1), zeros, tcx, True)
        xc_new = None
    (x_new,) = rg_scan(xcl, *p(1), hcb[0:1], WIDE_TILE, True, epi=(hlf, ggl, w_out_b, x, mod), h_dtype=None)
    return x_new, xc_new


def _qkv_body(x_ref, mod_ref, g_ref, w_ref, gm_ref, qg_ref, kg_ref, q_ref, k_ref, v_ref):
    h = _normmod(x_ref[...], g_ref[...], mod_ref[1:2, :], mod_ref[0:1, :]).astype(BF16)
    z = jnp.dot(h, w_ref[...], preferred_element_type=F32)

    def headnorm(v, gain):
        ms = jnp.dot((v * v).astype(BF16), gm_ref[...], preferred_element_type=F32)
        return (v * lax.rsqrt(ms + RMS_EPS)) * gain

    q_ref[...] = headnorm(z[:, :D], qg_ref[...]).astype(BF16)
    k_ref[...] = headnorm(z[:, D:2 * D], kg_ref[...]).astype(BF16)
    v_ref[...] = z[:, 2 * D:].astype(BF16)


def qkv_proj(x, mod, g, w_qkv, gmean, qg, kg, tm):
    t = x.shape[0]
    spec = pl.BlockSpec((tm, D), lambda i: (i, 0))
    return pl.pallas_call(
        _qkv_body,
        grid=(t // tm,),
        in_specs=[spec, _full((ADA_CHUNKS, D)), _full((1, D)), _full((D, 3 * D)), _full((D, D)),
                  _full((1, D)), _full((1, D))],
        out_specs=[spec, spec, spec],
        out_shape=[jax.ShapeDtypeStruct((t, D), BF16)] * 3,
        compiler_params=_cparams(("arbitrary",)),
        name="qkv_proj",
    )(x, mod, g, w_qkv, gmean, qg, kg)


def _attend_pair(q2, keys, vals, biases):
    m_rows = q2.shape[0]
    lane = lax.broadcasted_iota(jnp.int32, q2.shape, 1)
    zero = jnp.zeros_like(q2)
    qs = jnp.concatenate([jnp.where(lane < NA_HEAD_DIM, q2, zero), jnp.where(lane >= NA_HEAD_DIM, q2, zero)], axis=0)
    ss = []
    for kseg, bseg in zip(keys, biases):
        s = lax.dot_general(qs, kseg, (((1,), (1,)), ((), ())), preferred_element_type=F32)
        if bseg is not None:
            s = s + jnp.concatenate([bseg[0], bseg[1]], axis=0)
        ss.append(s)
    m = ss[0].max(axis=-1, keepdims=True)
    for s in ss[1:]:
        m = jnp.maximum(m, s.max(axis=-1, keepdims=True))
    den = None
    acc = None
    for s, vseg in zip(ss, vals):
        p = jnp.exp2(s - m)
        d = jnp.sum(p, axis=-1, keepdims=True)
        o = jnp.dot(p.astype(BF16), vseg, preferred_element_type=F32)
        den = d if den is None else den + d
        acc = o if acc is None else acc + o
    out = acc / den
    return jnp.where(lane < NA_HEAD_DIM, out[:m_rows], out[m_rows:])


NA_QROWS = 2
NA_UNION = NA_ROWS + NA_QROWS - 1


def _na_body(nside, var_ref, q_ref, kl_ref, vl_ref, kc_ref, vc_ref, bias_ref, *refs):
    side_in, o_ref, side_out = refs[:nside], refs[nside], refs[nside + 1:]
    for src, dst in zip(side_in, side_out):
        dst[...] = src[...].astype(BF16)
    for pr in range(NA_HEADS // 2):
        sl = slice(pr * LANES, (pr + 1) * LANES)
        o_ref[:, sl] = _attend_pair(
            q_ref[:, sl], [kl_ref[:, sl], kc_ref[:, sl]], [vl_ref[:, sl], vc_ref[:, sl]],
            [(bias_ref[0, 2 * pr], bias_ref[0, 2 * pr + 1]), None]).astype(BF16)


def _na_geometry(rows):
    steps = rows // NA_QROWS
    g = np.arange(steps)
    base = np.clip(NA_QROWS * g - NA_ROWS // 2, 0, rows - NA_UNION)
    r = NA_QROWS * g[:, None] + np.arange(NA_QROWS)[None, :]
    rs = np.clip(r - NA_ROWS // 2, 0, rows - NA_ROWS)
    key = np.concatenate([(base - NA_QROWS * g)[:, None], rs - r], axis=1)
    uniq, first, var = np.unique(key, axis=0, return_index=True, return_inverse=True)
    return base, var.reshape(-1).astype(np.int32), g[first]


def na_attention(q, k, v, kc, vc, bias_tab, var, side=()):
    t = q.shape[0]
    rows = t // GRID_W
    nctx = kc.shape[0]
    steps = rows // NA_QROWS
    side_specs = []
    for a in side:
        assert a.shape[0] % (steps * 16) == 0
        side_specs.append(pl.BlockSpec((a.shape[0] // steps, a.shape[1]), lambda g, var: (g, 0)))

    def kbase(g):
        return jnp.clip(NA_QROWS * g - NA_ROWS // 2, 0, rows - NA_UNION)

    qrows = NA_QROWS * GRID_W
    nloc = NA_UNION * GRID_W
    kspec = pl.BlockSpec((pl.Element(nloc), pl.Element(D)), lambda g, var: (kbase(g) * GRID_W, 0))
    grid_spec = pltpu.PrefetchScalarGridSpec(
        num_scalar_prefetch=1,
        grid=(steps,),
        in_specs=[pl.BlockSpec((qrows, D), lambda g, var: (g, 0)), kspec, kspec] + [
            pl.BlockSpec((nctx, D), lambda g, var: (0, 0)), pl.BlockSpec((nctx, D), lambda g, var: (0, 0)),
            pl.BlockSpec((1, NA_HEADS, qrows, nloc), lambda g, var: (var[g], 0, 0, 0)),
        ] + side_specs,
        out_specs=[pl.BlockSpec((qrows, D), lambda g, var: (g, 0))] + side_specs,
    )
    return pl.pallas_call(
        functools.partial(_na_body, len(side)),
        grid_spec=grid_spec,
        out_shape=[jax.ShapeDtypeStruct((t, D), BF16)] + [jax.ShapeDtypeStruct(a.shape, BF16) for a in side],
        compiler_params=_cparams(("arbitrary",)),
        name="na_attention",
    )(var, q, k, v, kc, vc, bias_tab, *side)


def _ctx_attn_body(q_ref, k_ref, v_ref, o_ref):
    for pr in range(NA_HEADS // 2):
        sl = slice(pr * LANES, (pr + 1) * LANES)
        o_ref[:, sl] = _attend_pair(q_ref[:, sl], [k_ref[:, sl]], [v_ref[:, sl]], [None]).astype(BF16)


def ctx_attention(q, k, v):
    t = q.shape[0]
    return pl.pallas_call(
        _ctx_attn_body,
        grid=(1,),
        in_specs=[_full((t, D))] * 3,
        out_specs=_full((t, D)),
        out_shape=jax.ShapeDtypeStruct((t, D), BF16),
        compiler_params=_cparams(("arbitrary",)),
        name="ctx_attention",
    )(q, k, v)


def _na_bias_table(rpb, rows):
    base, var, reps = _na_geometry(rows)
    cols = np.arange(GRID_W)
    cstart = np.clip(cols - NA_COLS // 2, 0, GRID_W - NA_COLS)
    kcol = np.arange(GRID_W)
    inwin = (kcol[None, :] >= cstart[:, None]) & (kcol[None, :] < cstart[:, None] + NA_COLS)
    r = NA_QROWS * reps[:, None] + np.arange(NA_QROWS)[None, :]
    rs = np.clip(r - NA_ROWS // 2, 0, rows - NA_ROWS)
    krow = base[reps][:, None] + np.arange(NA_UNION)[None, :]
    rvalid = (krow[:, None, :] >= rs[:, :, None]) & (krow[:, None, :] < rs[:, :, None] + NA_ROWS)
    ridx = np.clip(krow[:, None, :] - r[:, :, None] + (NA_ROWS - 1), 0, 2 * NA_ROWS - 2)
    nd = 2 * NA_COLS - 1
    w = jnp.pad(rpb.astype(F32), ((0, 0), (0, 0), (GRID_W - NA_COLS, 2 * GRID_W - (GRID_W - NA_COLS) - nd)))
    flat = jnp.tile(w, (1, 1, GRID_W))[:, :, :GRID_W * (2 * GRID_W - 1)]
    blk = flat.reshape(NA_HEADS, 2 * NA_ROWS - 1, GRID_W, 2 * GRID_W - 1)[..., GRID_W - 1:]
    blk = jnp.where(jnp.asarray(inwin)[None, None], blk, NEG_BIG)
    neg = jnp.full((NA_HEADS, GRID_W, GRID_W), NEG_BIG, F32)
    variants = []
    for v in range(len(reps)):
        strips = [jnp.concatenate([blk[:, ridx[v, a, j]] if rvalid[v, a, j] else neg for j in range(NA_UNION)], axis=2)
                  for a in range(NA_QROWS)]
        variants.append(jnp.concatenate(strips, axis=1))
    return jnp.stack(variants, axis=0), jnp.asarray(var)


def na_layer(x, xc, mod, modc, g, w_qkv, q_g, k_g, rpb, w_o, need_ctx, side=()):
    w_qkv_b = w_qkv.astype(BF16)
    w_o_b = w_o.astype(BF16)
    gm = np.kron(np.eye(NA_HEADS), np.full((NA_HEAD_DIM, NA_HEAD_DIM), 1.0 / NA_HEAD_DIM))
    gmean = jnp.asarray(gm, dtype=BF16)
    qg = jnp.tile(q_g, NA_HEADS)[None] * (NA_HEAD_DIM ** -0.5 * LOG2E)
    kg = jnp.tile(k_g, NA_HEADS)[None]
    q, k, v = qkv_proj(x, mod, g, w_qkv_b, gmean, qg, kg, ROW_TILE)
    qc, kc, vc = qkv_proj(xc, modc, g, w_qkv_b, gmean, qg, kg, xc.shape[0])
    bias_tab, var = _na_bias_table(rpb * LOG2E, x.shape[0] // GRID_W)
    o, *side_b = na_attention(q, k, v, kc, vc, bias_tab, var, side)
    x_new = proj_residual(o, w_o_b, x, mod, 2, WIDE_TILE)
    xc_new = None
    if need_ctx:
        oc = ctx_attention(qc, kc, vc)
        xc_new = proj_residual(oc, w_o_b, xc, modc, 2, xc.shape[0])
    return x_new, xc_new, side_b


def _dft_mats(n):
    ang = 2.0 * np.pi * np.outer(np.arange(n), np.arange(n)) / n
    return np.cos(ang), np.sin(ang)


def _channel_dft(h, wc):
    us = [jnp.dot(h[:, gi * FT_GROUP_W:(gi + 1) * FT_GROUP_W], wc, preferred_element_type=F32).astype(BF16)
          for gi in range(D // FT_GROUP_W)]
    return jnp.concatenate([u[:, :FT_GROUP_W] for u in us] + [u[:, FT_GROUP_W:] for u in us], axis=1)


def _ft_a_body(n, nj, x_ref, mod_ref, g_ref, perm_ref, wc_ref, ma_ref, tc_ref, ts_ref, yr_ref, yi_ref):
    h3 = _normmod(x_ref[...], g_ref[...], mod_ref[1:2, :], mod_ref[0:1, :])
    h = jnp.dot(perm_ref[...], h3.reshape(n * nj, D).astype(BF16), preferred_element_type=F32).astype(BF16)
    u = _channel_dft(h, wc_ref[...])
    for j in range(nj):
        uj = u[j * n:(j + 1) * n]
        y = jnp.dot(ma_ref[...], jnp.concatenate([uj[:, :D], uj[:, D:]], axis=0), preferred_element_type=F32)
        yr, yi = y[:n], y[n:]
        tc = _lane_tile(tc_ref[0, :, j * LANES:(j + 1) * LANES])
        ts = _lane_tile(ts_ref[0, :, j * LANES:(j + 1) * LANES])
        yr_ref[:, j, :] = yr * tc + yi * ts
        yi_ref[:, j, :] = yi * tc - yr * ts


def _ft_c_body(n, nj, yr_ref, yi_ref, mc_ref, wf_ref, x_ref, mod_ref, o_ref):
    fs = []
    for j in range(nj):
        ys = jnp.concatenate([yr_ref[j].astype(BF16), yi_ref[j].astype(BF16)], axis=0)
        fs.append(jnp.dot(mc_ref[...], ys, preferred_element_type=F32).astype(BF16))
    z = jnp.dot(jnp.concatenate(fs, axis=0), wf_ref[...], preferred_element_type=F32)
    gate = mod_ref[2:3, :]
    for j in range(nj):
        o_ref[:, j, :] = x_ref[:, j, :] + gate * z[j * n:(j + 1) * n]


def _ft_ctx_body(x_ref, mod_ref, g_ref, wc_ref, ml_ref, wf_ref, o_ref):
    x = x_ref[...]
    h = _normmod(x, g_ref[...], mod_ref[1:2, :], mod_ref[0:1, :]).astype(BF16)
    u = _channel_dft(h, wc_ref[...])
    us = jnp.concatenate([u[:, :D], u[:, D:]], axis=0)
    f = jnp.dot(ml_ref[...], us, preferred_element_type=F32).astype(BF16)
    o_ref[...] = x + mod_ref[2:3, :] * jnp.dot(f, wf_ref[...], preferred_element_type=F32)


def fourier_layer(x, xc, mod, modc, g, w_f, need_ctx):
    t = x.shape[0]
    n = math.isqrt(t)
    assert n * n == t and n % 16 == 0
    w_f_b = w_f.astype(BF16)
    cw, sw = _dft_mats(FT_GROUP_W)
    wc = jnp.asarray(np.concatenate([cw, -sw], axis=1) / math.sqrt(FT_GROUP_W), dtype=F32).astype(BF16)
    wcspec = _full((FT_GROUP_W, 2 * FT_GROUP_W))
    cn, sn = _dft_mats(n)
    ma = jnp.asarray(np.block([[cn, sn], [-sn, cn]]) / math.sqrt(n), dtype=F32).astype(BF16)
    mc = jnp.asarray(np.concatenate([cn, sn], axis=1) / math.sqrt(n), dtype=F32).astype(BF16)
    nj = 8
    ang = 2.0 * np.pi * np.outer(np.arange(n), np.arange(n)) / t
    def expand(tab):
        a = jnp.asarray(tab, dtype=F32).reshape(n // nj, nj, n).transpose(0, 2, 1)
        return jnp.repeat(a, LANES, axis=2)
    twc, tws = expand(np.cos(ang)), expand(np.sin(ang))
    xblk = pl.BlockSpec((n, nj, D), lambda b: (0, b, 0))
    yblk = pl.BlockSpec((nj, n, D), lambda b: (b, 0, 0))
    tblk = pl.BlockSpec((1, n, nj * LANES), lambda b: (b, 0, 0))
    x3 = x.reshape(n, n, D)
    src = (np.arange(n)[None, :] * nj + np.arange(nj)[:, None]).reshape(-1)
    perm = jnp.asarray(np.eye(n * nj)[src], dtype=BF16)
    yr, yi = pl.pallas_call(
        functools.partial(_ft_a_body, n, nj),
        grid=(n // nj,),
        in_specs=[xblk, _full((ADA_CHUNKS, D)), _full((1, D)), _full((n * nj, n * nj)), wcspec,
                  _full((2 * n, 2 * n)), tblk, tblk],
        out_specs=[xblk, xblk],
        out_shape=[jax.ShapeDtypeStruct((n, n, D), F32)] * 2,
        compiler_params=_cparams(("arbitrary",)),
        name="ft_stage_a",
    )(x3, mod, g, perm, wc, ma, twc, tws)
    x_new = pl.pallas_call(
        functools.partial(_ft_c_body, n, nj),
        grid=(n // nj,),
        in_specs=[yblk, yblk, _full((n, 2 * n)), _full((D, D)), xblk, _full((ADA_CHUNKS, D))],
        out_specs=xblk,
        out_shape=jax.ShapeDtypeStruct((n, n, D), F32),
        compiler_params=_cparams(("arbitrary",)),
        name="ft_stage_c",
    )(yr, yi, mc, w_f_b, x3, mod).reshape(t, D)
    xc_new = None
    if need_ctx:
        lc = xc.shape[0]
        cl, sl = _dft_mats(lc)
        ml = jnp.asarray(np.concatenate([cl, sl], axis=1) / math.sqrt(lc), dtype=F32).astype(BF16)
        xc_new = pl.pallas_call(
            _ft_ctx_body,
            grid=(1,),
            in_specs=[_full((lc, D)), _full((ADA_CHUNKS, D)), _full((1, D)), wcspec,
                      _full((lc, 2 * lc)), _full((D, D))],
            out_specs=_full((lc, D)),
            out_shape=jax.ShapeDtypeStruct((lc, D), F32),
            compiler_params=_cparams(("arbitrary",)),
            name="ft_ctx",
        )(xc, modc, g, wc, ml, w_f_b)
    return x_new, xc_new


def _router_body(x_ref, mod_ref, g_ref, r_ref, info_ref, w0_ref, w1_ref):
    h = _normmod(x_ref[...], g_ref[...], mod_ref[4:5, :], mod_ref[3:4, :])
    hh = h.astype(BF16)
    hl = (h - hh.astype(F32)).astype(BF16)
    r = r_ref[...]
    rh = r.astype(BF16)
    rl = (r - rh.astype(F32)).astype(BF16)
    logits = (jnp.dot(hh, rh, preferred_element_type=F32) + jnp.dot(hh, rl, preferred_element_type=F32)
              + jnp.dot(hl, rh, preferred_element_type=F32))
    lane = lax.broadcasted_iota(jnp.int32, logits.shape, 1)
    logits = jnp.where(lane < N_EXPERTS, logits, NEG_BIG)
    v0 = jnp.max(logits, axis=-1, keepdims=True)
    i0 = jnp.min(jnp.where(logits == v0, lane, LANES), axis=-1, keepdims=True)
    rest = jnp.where(lane == i0, NEG_BIG, logits)
    v1 = jnp.max(rest, axis=-1, keepdims=True)
    i1 = jnp.min(jnp.where(rest == v1, lane, LANES), axis=-1, keepdims=True)
    e = jnp.exp(v1 - v0)
    w0 = 1.0 / (1.0 + e)
    w1 = e / (1.0 + e)
    info_ref[...] = jnp.where(lane == 0, i0, jnp.where(lane == 1, i1, 0))
    w0_ref[...] = jnp.broadcast_to(w0, logits.shape)
    w1_ref[...] = jnp.broadcast_to(w1, logits.shape)


def moe_router(x, mod, g, router_pad, tm):
    t = x.shape[0]
    spec = pl.BlockSpec((tm, D), lambda i: (i, 0))
    lspec = pl.BlockSpec((tm, LANES), lambda i: (i, 0))
    return pl.pallas_call(
        _router_body,
        grid=(t // tm,),
        in_specs=[spec, _full((ADA_CHUNKS, D)), _full((1, D)), _full((D, LANES))],
        out_specs=[lspec, lspec, lspec],
        out_shape=[jax.ShapeDtypeStruct((t, LANES), jnp.int32),
                   jax.ShapeDtypeStruct((t, LANES), F32), jax.ShapeDtypeStruct((t, LANES), F32)],
        compiler_params=_cparams(("arbitrary",)),
        name="moe_router",
    )(x, mod, g, router_pad)


SCATTER_TOKENS = 256


def _row_scatter_body(nlat, nctx, didx_ref, g_ref, x_ref, mod_ref, *rest):
    if nctx:
        xc_ref, modc_ref, dst_ref, h_scr, zero_scr, sems = rest
    else:
        dst_ref, h_scr, zero_scr, sems = rest
    i = pl.program_id(0)
    nsteps = pl.num_programs(0)
    ts = SCATTER_TOKENS
    slot = i % 2

    def start_all(src_ref, src_is_zero_rows):
        def issue(grp, c):
            base = pl.multiple_of(grp * SUBLANES, SUBLANES)
            for r in range(SUBLANES):
                src = src_ref.at[pl.ds(r if src_is_zero_rows else base + r, 1), :]
                for half in range(2):
                    d = didx_ref[0, 0, base + r + half * ts]
                    pltpu.make_async_copy(src, dst_ref.at[pl.ds(d, 1), :], sems.at[slot]).start(priority=half)
            return c
        lax.fori_loop(0, ts // SUBLANES, issue, 0)

    def wait_all(which):
        def drain(n, c):
            pltpu.make_async_copy(zero_scr.at[pl.ds(0, 1), :], dst_ref.at[pl.ds(0, 1), :], sems.at[which]).wait()
            return c
        lax.fori_loop(0, 2 * ts, drain, 0, unroll=8)

    @pl.when(i == 0)
    def _():
        zero_scr[...] = jnp.zeros_like(zero_scr)

    def stage_and_start(src_ref, m_ref):
        h_scr[slot] = _normmod(src_ref[...], g_ref[...], m_ref[4:5, :], m_ref[3:4, :])
        start_all(h_scr.at[slot], False)

    @pl.when(i < nlat)
    def _():
        stage_and_start(x_ref, mod_ref)

    if nctx:
        @pl.when(jnp.logical_and(i >= nlat, i < nlat + nctx))
        def _():
            stage_and_start(xc_ref, modc_ref)

    @pl.when(i >= nlat + nctx)
    def _():
        start_all(zero_scr, True)

    @pl.when(i > 0)
    def _():
        wait_all(1 - slot)

    @pl.when(i == nsteps - 1)
    def _():
        wait_all(slot)


def row_scatter(x, xc, mod, modc, g, d0, d1, pad_pos):
    ts = SCATTER_TOKENS
    nlat = x.shape[0] // ts
    nctx = 0 if xc is None else 1
    assert xc is None or xc.shape[0] == ts
    ntok = nlat + nctx
    npad = pad_pos.shape[0] // (2 * ts)
    didx = jnp.concatenate([jnp.concatenate([d0.reshape(ntok, 1, ts), d1.reshape(ntok, 1, ts)], axis=2),
                            pad_pos.reshape(npad, 1, 2 * ts)], axis=0)
    in_specs = [pl.BlockSpec((1, 1, 2 * ts), lambda i: (i, 0, 0), memory_space=pltpu.SMEM),
                _full((1, D)),
                pl.BlockSpec((ts, D), lambda i: (jnp.minimum(i, nlat - 1), 0)),
                _full((ADA_CHUNKS, D))]
    args = [didx, g, x, mod]
    if nctx:
        in_specs += [_full((ts, D)), _full((ADA_CHUNKS, D))]
        args += [xc, modc]
    return pl.pallas_call(
        functools.partial(_row_scatter_body, nlat, nctx),
        grid=(ntok + npad,),
        in_specs=in_specs,
        out_specs=pl.BlockSpec(memory_space=pl.ANY),
        out_shape=jax.ShapeDtypeStruct((2 * ntok * ts + pad_pos.shape[0], D), F32),
        scratch_shapes=[pltpu.VMEM((2, ts, D), F32), pltpu.VMEM((SUBLANES, D), F32), pltpu.SemaphoreType.DMA((2,))],
        compiler_params=_cparams(("arbitrary",)),
        name="moe_row_scatter",
    )(*args)


def _moe_ffn_body(te_ref, tv_ref, xg_ref, wg_ref, wu_ref, wd_ref, o_ref):
    i = pl.program_id(0)

    @pl.when(tv_ref[i] > 0)
    def _():
        o_ref[...] = _swiglu_chunks(xg_ref[...].astype(BF16), wg_ref, wu_ref, wd_ref)

    @pl.when(tv_ref[i] == 0)
    def _():
        o_ref[...] = jnp.zeros_like(o_ref)


def moe_ffn(xg, tile_e, tile_v, w_gu, w_down, li, tm):
    p = xg.shape[0]
    grid_spec = pltpu.PrefetchScalarGridSpec(
        num_scalar_prefetch=2,
        grid=(p // tm,),
        in_specs=[
            pl.BlockSpec((tm, D), lambda i, te, tv: (i, 0)),
            pl.BlockSpec((None, None, D, D_FF), lambda i, te, tv: (li, te[i], 0, 0)),
            pl.BlockSpec((None, None, D, D_FF), lambda i, te, tv: (li, te[i], 0, 1)),
            pl.BlockSpec((None, None, D_FF, D), lambda i, te, tv: (li, te[i], 0, 0)),
        ],
        out_specs=pl.BlockSpec((tm, D), lambda i, te, tv: (i, 0)),
    )
    return pl.pallas_call(
        _moe_ffn_body,
        grid_spec=grid_spec,
        out_shape=jax.ShapeDtypeStruct((p, D), F32),
        compiler_params=pltpu.CompilerParams(dimension_semantics=("arbitrary",), vmem_limit_bytes=MOE_VMEM_LIMIT),
        name="moe_ffn",
    )(tile_e, tile_v, xg, w_gu, w_gu, w_down)


def _combine_body(tt, d0_ref, d1_ref, d0n_ref, d1n_ref, yp_ref, x_ref, mod_ref, w0_ref, w1_ref, o_ref,
                  a_scr, b_scr, sems):
    i = pl.program_id(0)
    nsteps = pl.num_programs(0)
    slot = i % 2

    def start_all(i0_ref, i1_ref, which):
        def issue(grp, c):
            base = pl.multiple_of(grp * SUBLANES, SUBLANES)
            for r in range(SUBLANES):
                n = base + r
                pltpu.make_async_copy(yp_ref.at[pl.ds(i0_ref[0, 0, n], 1), :], a_scr.at[which, pl.ds(n, 1), :],
                                      sems.at[which]).start(priority=0)
                pltpu.make_async_copy(yp_ref.at[pl.ds(i1_ref[0, 0, n], 1), :], b_scr.at[which, pl.ds(n, 1), :],
                                      sems.at[which]).start(priority=1)
            return c
        lax.fori_loop(0, tt // SUBLANES, issue, 0)

    @pl.when(i == 0)
    def _():
        start_all(d0_ref, d1_ref, 0)

    @pl.when(i + 1 < nsteps)
    def _():
        start_all(d0n_ref, d1n_ref, 1 - slot)

    def drain(n, c):
        pltpu.make_async_copy(yp_ref.at[pl.ds(0, 1), :], a_scr.at[slot, pl.ds(0, 1), :], sems.at[slot]).wait()
        pltpu.make_async_copy(yp_ref.at[pl.ds(0, 1), :], b_scr.at[slot, pl.ds(0, 1), :], sems.at[slot]).wait()
        return c
    lax.fori_loop(0, tt, drain, 0, unroll=8)
    w0 = _lane_tile(w0_ref[...])
    w1 = _lane_tile(w1_ref[...])
    o_ref[...] = x_ref[...] + mod_ref[5:6, :] * (w0 * a_scr[slot] + w1 * b_scr[slot])


def moe_combine(yp, d0, d1, x, mod, w0b, w1b, tt):
    t = x.shape[0]
    nt = t // tt
    ispec = pl.BlockSpec((1, 1, tt), lambda i: (i, 0, 0), memory_space=pltpu.SMEM)
    nspec = pl.BlockSpec((1, 1, tt), lambda i: (jnp.minimum(i + 1, nt - 1), 0, 0), memory_space=pltpu.SMEM)
    spec = pl.BlockSpec((tt, D), lambda i: (i, 0))
    lspec = pl.BlockSpec((tt, LANES), lambda i: (i, 0))
    d0r, d1r = d0.reshape(nt, 1, tt), d1.reshape(nt, 1, tt)
    return pl.pallas_call(
        functools.partial(_combine_body, tt),
        grid=(nt,),
        in_specs=[ispec, ispec, nspec, nspec, pl.BlockSpec(memory_space=pl.ANY), spec, _full((ADA_CHUNKS, D)),
                  lspec, lspec],
        out_specs=spec,
        out_shape=jax.ShapeDtypeStruct((t, D), F32),
        scratch_shapes=[pltpu.VMEM((2, tt, D), F32), pltpu.VMEM((2, tt, D), F32), pltpu.SemaphoreType.DMA((2,))],
        compiler_params=_cparams(("arbitrary",)),
        name="moe_combine",
    )(d0r, d1r, d0r, d1r, yp, x, mod, w0b, w1b)


def _route_plan(e0, e1, tm):
    t = e0.shape[0]
    n = 2 * t
    ex = jnp.arange(N_EXPERTS, dtype=jnp.int32)
    oh0 = (e0[:, None] == ex[None, :]).astype(jnp.int32)
    oh1 = (e1[:, None] == ex[None, :]).astype(jnp.int32)
    both = oh0 + oh1
    csum = jnp.cumsum(both, axis=0)
    before = csum - both
    counts = csum[-1]
    padded = ((counts + tm - 1) // tm) * tm
    pad_end = jnp.cumsum(padded)
    pad_off = pad_end - padded
    total = pad_end[-1]
    d0 = jnp.sum(oh0 * (before + pad_off[None, :]), axis=1)
    d1 = jnp.sum(oh1 * (before + oh0 + pad_off[None, :]), axis=1)
    gap = padded - counts
    tail_off = jnp.cumsum(tm - gap) - (tm - gap)
    r = jnp.arange(tm, dtype=jnp.int32)[None, :]
    pad_pos = jnp.where(r < gap[:, None], (pad_off + counts)[:, None] + r,
                        total + tail_off[:, None] + (r - gap[:, None])).reshape(-1)
    ntiles = (n + N_EXPERTS * tm) // tm
    tstart = jnp.arange(ntiles, dtype=jnp.int32) * tm
    tile_v = (tstart < total).astype(jnp.int32)
    tile_e = jnp.sum((jnp.minimum(tstart, total - 1)[:, None] >= pad_end[None, :]).astype(jnp.int32), axis=1)
    return (d0.astype(jnp.int32), d1.astype(jnp.int32), pad_pos.astype(jnp.int32), tile_e.astype(jnp.int32), tile_v)


def moe_layer(x, xc, mod, modc, g, router, w_gu_b, w_down_b, li, need_ctx, tm=ROW_TILE):
    router_pad = jnp.pad(router, ((0, 0), (0, LANES - N_EXPERTS)))
    s = x.shape[0]
    info, w0b, w1b = moe_router(x, mod, g, router_pad, WIDE_TILE)
    if need_ctx:
        sc = xc.shape[0]
        infoc, w0c, w1c = moe_router(xc, modc, g, router_pad, sc)
        e0 = jnp.concatenate([info[:, 0], infoc[:, 0]])
        e1 = jnp.concatenate([info[:, 1], infoc[:, 1]])
    else:
        e0, e1 = info[:, 0], info[:, 1]
    d0, d1, pad_pos, tile_e, tile_v = _route_plan(e0, e1, tm)
    xg = row_scatter(x, xc if need_ctx else None, mod, modc, g, d0, d1, pad_pos)
    yp = moe_ffn(xg, tile_e, tile_v, w_gu_b, w_down_b, li, tm)
    x_new = moe_combine(yp, d0[:s], d1[:s], x, mod, w0b, w1b, WIDE_TILE)
    xc_new = None
    if need_ctx:
        xc_new = moe_combine(yp, d0[s:], d1[s:], xc, modc, w0c, w1c, sc)
    return x_new, xc_new


def kernel(x, c, ctx, c_ctx, ada_w, ada_b, norm_g, rg_w_in, rg_conv_w, rg_conv_b, rg_wa, rg_ba, rg_wi, rg_bi,
           rg_lambda, rg_w_out, na_w_qkv, na_q_g, na_k_g, na_rpb, na_w_o, ft_w_out, ffn_w_gu, ffn_w_down,
           moe_router, moe_w_gu, moe_w_down):
    depth = ada_w.shape[0]
    assert x.shape[0] == 1 and x.shape[2] == D
    xs = x[0]
    xc = ctx[0]
    mods = ada_modulation(c, c_ctx, ada_w, ada_b)
    ffn_gu_b, ffn_dn_b = ffn_w_gu.astype(BF16), ffn_w_down.astype(BF16)
    moe_gu_b = moe_dn_b = None
    mix_idx = [0] * N_MIXERS
    dense_idx = 0
    moe_idx = 0
    for layer in range(depth):
        need_ctx = layer != depth - 1
        mod, modc = mods[layer, 0], mods[layer, 1]
        g0 = norm_g[layer, 0][None]
        g1 = norm_g[layer, 1][None]
        kind = layer % N_MIXERS
        j = mix_idx[kind]
        mix_idx[kind] += 1
        if kind == 0:
            xs, xcn = rglru_layer(xs, xc, mod, modc, g0, rg_w_in[j], rg_conv_w[j], rg_conv_b[j], rg_wa[j], rg_wi[j],
                                  rg_ba[j], rg_bi[j], rg_lambda[j], rg_w_out[j], need_ctx)
        elif kind == 1:
            side = ()
            if moe_gu_b is None:
                side = (moe_w_gu.reshape(-1, moe_w_gu.shape[-1]), moe_w_down.reshape(-1, moe_w_down.shape[-1]))
            xs, xcn, side_b = na_layer(xs, xc, mod, modc, g0, na_w_qkv[j], na_q_g[j], na_k_g[j], na_rpb[j],
                                       na_w_o[j], need_ctx, side)
            if side_b:
                moe_gu_b, moe_dn_b = side_b[0].reshape(moe_w_gu.shape), side_b[1].reshape(moe_w_down.shape)
        else:
            xs, xcn = fourier_layer(xs, xc, mod, modc, g0, ft_w_out[j], need_ctx)
        if need_ctx:
            xc = xcn
        if layer % 2 == 0:
            if need_ctx:
                xc = ffn_dense(xc, modc, g1, ffn_gu_b, ffn_dn_b, dense_idx, xc.shape[0])
            xs = ffn_dense(xs, mod, g1, ffn_gu_b, ffn_dn_b, dense_idx, ROW_TILE)
            dense_idx += 1
        else:
            if moe_gu_b is None:
                moe_gu_b, moe_dn_b = moe_w_gu.astype(BF16), moe_w_down.astype(BF16)
            xs, xcn = moe_layer(xs, xc, mod, modc, g1, moe_router[moe_idx], moe_gu_b, moe_dn_b, moe_idx, need_ctx)
            moe_idx += 1
            if need_ctx:
                xc = xcn
    return xs[None]
```

```python
import functools
import math

import numpy as np
import jax
import jax.numpy as jnp
from jax import lax
from jax.experimental import pallas as pl
from jax.experimental.pallas import tpu as pltpu

F32 = jnp.float32
BF16 = jnp.bfloat16

D = 1024
D_FF = 3584
N_EXPERTS = 8
GRID_W = 64
NA_HEADS = 16
NA_HEAD_DIM = 64
NA_ROWS = 8
NA_COLS = 16
FT_GROUP_W = 256
RG_BLOCK_W = 256
RMS_EPS = 1e-6
LRU_C = 8.0
N_MIXERS = 3
ADA_CHUNKS = 6

LANES = 128
SUBLANES = 8
VMEM_LIMIT = 56 * 1024 * 1024
MOE_VMEM_LIMIT = 60 * 1024 * 1024

ROW_TILE = 512
WIDE_TILE = 1024
NEG_BIG = -1e30
LOG2E = math.log2(math.e)


def _cparams(sem):
    return pltpu.CompilerParams(dimension_semantics=sem, vmem_limit_bytes=VMEM_LIMIT)


def _full(shape):
    nd = len(shape)
    return pl.BlockSpec(shape, lambda *_: (0,) * nd)


def _normmod(x, g, scale, shift):
    ms = jnp.mean(x * x, axis=-1, keepdims=True)
    y = x * lax.rsqrt(ms + RMS_EPS)
    return (y * g) * (1.0 + scale) + shift


def _lane_tile(v):
    return jnp.concatenate([v] * (D // LANES), axis=1)


def _sigmoid(v):
    return 1.0 / (1.0 + jnp.exp(-v))


def _gelu_tanh(v):
    c = math.sqrt(2.0 / math.pi)
    return v * (0.5 * (1.0 + jnp.tanh(c * (v + 0.044715 * (v * v * v)))))


def _ada_body(cin_ref, w_ref, b_ref, o_ref):
    v = cin_ref[...]
    s = v * _sigmoid(v)
    w = w_ref[0]
    r0 = jnp.sum(s[:, 0:1] * w, axis=0, keepdims=True)
    r1 = jnp.sum(s[:, 1:2] * w, axis=0, keepdims=True)
    o_ref[0] = jnp.concatenate([r0, r1], axis=0) + b_ref[0]


def ada_modulation(c, c_ctx, ada_w, ada_b):
    depth = ada_w.shape[0]
    n = ada_w.shape[2]
    nc = n // 4
    cin = jnp.stack([c[0], c_ctx], axis=1)
    out = pl.pallas_call(
        _ada_body,
        grid=(depth, n // nc),
        in_specs=[
            pl.BlockSpec((D, 2), lambda l, j: (0, 0)),
            pl.BlockSpec((1, D, nc), lambda l, j: (l, 0, j)),
            pl.BlockSpec((1, 1, nc), lambda l, j: (l, 0, j)),
        ],
        out_specs=pl.BlockSpec((1, 2, nc), lambda l, j: (l, 0, j)),
        out_shape=jax.ShapeDtypeStruct((depth, 2, n), F32),
        compiler_params=_cparams(("arbitrary", "arbitrary")),
        name="ada_mod",
    )(cin, ada_w, ada_b.reshape(depth, 1, n))
    return out.reshape(depth, 2, ADA_CHUNKS, D)


FFN_CHUNK = 512


def _swiglu_chunks(h, wg_ref, wu_ref, wd_ref):
    acc = None
    for c in range(D_FF // FFN_CHUNK):
        sl = slice(c * FFN_CHUNK, (c + 1) * FFN_CHUNK)
        gg = jnp.dot(h, wg_ref[:, sl], preferred_element_type=F32)
        uu = jnp.dot(h, wu_ref[:, sl], preferred_element_type=F32)
        a = ((gg * _sigmoid(gg)) * uu).astype(BF16)
        part = jnp.dot(a, wd_ref[sl, :], preferred_element_type=F32)
        acc = part if acc is None else acc + part
    return acc


def _ffn_body(x_ref, mod_ref, g_ref, wg_ref, wu_ref, wd_ref, o_ref):
    x = x_ref[...]
    h = _normmod(x, g_ref[...], mod_ref[4:5, :], mod_ref[3:4, :]).astype(BF16)
    o_ref[...] = x + mod_ref[5:6, :] * _swiglu_chunks(h, wg_ref, wu_ref, wd_ref)


def ffn_dense(x, mod, g, w_gu, w_down, li, tm):
    t = x.shape[0]
    once = pl.Buffered(1)
    return pl.pallas_call(
        _ffn_body,
        grid=(t // tm,),
        in_specs=[
            pl.BlockSpec((tm, D), lambda i: (i, 0)),
            _full((ADA_CHUNKS, D)),
            _full((1, D)),
            pl.BlockSpec((None, D, D_FF), lambda i: (li, 0, 0), pipeline_mode=once),
            pl.BlockSpec((None, D, D_FF), lambda i: (li, 0, 1), pipeline_mode=once),
            pl.BlockSpec((None, D_FF, D), lambda i: (li, 0, 0), pipeline_mode=once),
        ],
        out_specs=pl.BlockSpec((tm, D), lambda i: (i, 0)),
        out_shape=jax.ShapeDtypeStruct((t, D), F32),
        compiler_params=_cparams(("arbitrary",)),
        name="ffn_dense",
    )(x, mod, g, w_gu, w_gu, w_down)


def _proj_body(gate_row, a_ref, w_ref, x_ref, mod_ref, o_ref):
    y = jnp.dot(a_ref[...], w_ref[...], preferred_element_type=F32)
    o_ref[...] = x_ref[...] + mod_ref[gate_row:gate_row + 1, :] * y


def proj_residual(a, w, x, mod, gate_row, tm):
    t, k = a.shape
    return pl.pallas_call(
        functools.partial(_proj_body, gate_row),
        grid=(t // tm,),
        in_specs=[
            pl.BlockSpec((tm, k), lambda i: (i, 0)),
            _full((k, D)),
            pl.BlockSpec((tm, D), lambda i: (i, 0)),
            _full((ADA_CHUNKS, D)),
        ],
        out_specs=pl.BlockSpec((tm, D), lambda i: (i, 0)),
        out_shape=jax.ShapeDtypeStruct((t, D), F32),
        compiler_params=_cparams(("arbitrary",)),
        name="proj_residual",
    )(a, w, x, mod)


HALO = SUBLANES
RG_IN_PIECE = 128


def _rg_in_body(tm, xp_ref, x_ref, xn_ref, mod_ref, g_ref, w_ref, cw_ref, cb_ref, xc_ref, gg_ref):
    i = pl.program_id(0)
    last = pl.num_programs(0) - 1
    xa = jnp.concatenate([xp_ref[...], x_ref[...], xn_ref[...]], axis=0)
    npiece = tm // RG_IN_PIECE
    bounds = [0] + [2 * HALO + RG_IN_PIECE * (k + 1) for k in range(npiece - 1)] + [tm + 2 * HALO]
    zs = []
    for k in range(npiece):
        hk = _normmod(xa[bounds[k]:bounds[k + 1]], g_ref[...], mod_ref[1:2, :], mod_ref[0:1, :]).astype(BF16)
        zs.append(jnp.dot(hk, w_ref[...], preferred_element_type=F32))
    z = jnp.concatenate(zs, axis=0)
    row = lax.broadcasted_iota(jnp.int32, (tm + 2 * HALO, 1), 0)
    valid = jnp.logical_and(jnp.logical_or(row >= HALO, i > 0),
                            jnp.logical_or(row < tm + HALO, i < last))
    xz = jnp.where(valid, z[:, :D], 0.0)
    y = cb_ref[...] + cw_ref[2:3, :] * xz[HALO:HALO + tm]
    y = y + cw_ref[0:1, :] * xz[HALO - 2:HALO - 2 + tm]
    y = y + cw_ref[1:2, :] * xz[HALO - 1:HALO - 1 + tm]
    y = y + cw_ref[3:4, :] * xz[HALO + 1:HALO + 1 + tm]
    xc_ref[...] = y
    gg_ref[...] = _gelu_tanh(z[HALO:HALO + tm, D:]).astype(BF16)


def rg_in(x, mod, g, w_in, conv_w, conv_b, tm):
    t = x.shape[0]
    nb = tm // HALO
    nblk = t // HALO
    return pl.pallas_call(
        functools.partial(_rg_in_body, tm),
        grid=(t // tm,),
        in_specs=[
            pl.BlockSpec((HALO, D), lambda i: (jnp.maximum(i * nb - 1, 0), 0)),
            pl.BlockSpec((tm, D), lambda i: (i, 0)),
            pl.BlockSpec((HALO, D), lambda i: (jnp.minimum((i + 1) * nb, nblk - 1), 0)),
            _full((ADA_CHUNKS, D)),
            _full((1, D)),
            _full((D, 2 * D)),
            _full((4, D)),
            _full((1, D)),
        ],
        out_specs=[pl.BlockSpec((tm, D), lambda i: (i, 0)), pl.BlockSpec((tm, D), lambda i: (i, 0))],
        out_shape=[jax.ShapeDtypeStruct((t, D), F32), jax.ShapeDtypeStruct((t, D), BF16)],
        compiler_params=_cparams(("arbitrary",)),
        name="rg_in",
    )(x, x, x, mod, g, w_in, conv_w, conv_b)


def _rg_gates(xc, wa_ref, wi_ref, ba, bi, lam):
    xb = xc.astype(BF16)
    nblk = D // RG_BLOCK_W
    r = jnp.concatenate([jnp.dot(xb[:, n * RG_BLOCK_W:(n + 1) * RG_BLOCK_W], wa_ref[n],
                                 preferred_element_type=F32) for n in range(nblk)], axis=1)
    ig = jnp.concatenate([jnp.dot(xb[:, n * RG_BLOCK_W:(n + 1) * RG_BLOCK_W], wi_ref[n],
                                  preferred_element_type=F32) for n in range(nblk)], axis=1)
    t_r = jnp.tanh(r + 0.5 * ba)
    t_i = jnp.tanh(ig + 0.5 * bi)
    nl = -lam
    softplus = jnp.maximum(nl, 0.0) + jnp.log1p(jnp.exp(-jnp.abs(nl)))
    half_c = (-0.5 * LRU_C) * softplus
    log_a = half_c + half_c * t_r
    a = jnp.exp(log_a)
    xh = 0.5 * xc
    b = jnp.sqrt(1.0 - a * a) * (xh + xh * t_i)
    return a, b


def _rg_scan_body(reverse, epilogue, emit_h, tc, *refs):
    xc_ref, wa_ref, wi_ref, ba_ref, bi_ref, lam_ref, h0_ref = refs[:7]
    refs = refs[7:]
    if epilogue:
        hf_ref, gg_ref, wo_ref, x_ref, mod_ref = refs[:5]
        refs = refs[5:]
    if emit_h:
        h_ref = refs[0]
        refs = refs[1:]
    if epilogue:
        o_ref = refs[0]
        refs = refs[1:]
    a_scr, b_scr, h_scr, carry_scr = refs
    c = pl.program_id(0)

    @pl.when(c == 0)
    def _():
        carry_scr[...] = jnp.broadcast_to(h0_ref[...], (SUBLANES, D))

    a, b = _rg_gates(xc_ref[...], wa_ref, wi_ref, ba_ref[...], bi_ref[...], lam_ref[...])
    a_scr[...] = a
    b_scr[...] = b
    nblk = tc // SUBLANES
    row = lax.broadcasted_iota(jnp.int32, (SUBLANES, D), 0)
    first = (row == SUBLANES - 1) if reverse else (row == 0)

    def block(n, carry):
        blk = (nblk - 1 - n) if reverse else n
        off = pl.multiple_of(blk * SUBLANES, SUBLANES)
        av = a_scr[pl.ds(off, SUBLANES), :]
        bv = b_scr[pl.ds(off, SUBLANES), :]
        bv = jnp.where(first, av * carry + bv, bv)
        av = jnp.where(first, 0.0, av)
        for k in (1, 2, 4):
            shift = (SUBLANES - k) if reverse else k
            bv = av * pltpu.roll(bv, shift, 0) + bv
            if k != 4:
                av = av * pltpu.roll(av, shift, 0)
        h_scr[pl.ds(off, SUBLANES), :] = bv
        edge = bv[0:1, :] if reverse else bv[SUBLANES - 1:SUBLANES, :]
        return jnp.broadcast_to(edge, (SUBLANES, D))

    carry_scr[...] = lax.fori_loop(0, nblk, block, carry_scr[...], unroll=2)

    if emit_h:
        h_ref[...] = h_scr[...].astype(h_ref.dtype)
    if epilogue:
        y = ((hf_ref[...].astype(F32) + h_scr[...]) * gg_ref[...].astype(F32)).astype(BF16)
        o_ref[...] = x_ref[...] + mod_ref[2:3, :] * jnp.dot(y, wo_ref[...], preferred_element_type=F32)


def rg_scan(xconv, wa, wi, ba, bi, lam, h0, tc, reverse, epi=None, h_dtype=F32):
    t = xconv.shape[0]
    nchunks = t // tc
    idx = (lambda c: (nchunks - 1 - c, 0)) if reverse else (lambda c: (c, 0))
    nb = D // RG_BLOCK_W
    blk = pl.BlockSpec((tc, D), idx)
    in_specs = [
        blk,
        _full((nb, RG_BLOCK_W, RG_BLOCK_W)),
        _full((nb, RG_BLOCK_W, RG_BLOCK_W)),
        _full((1, D)), _full((1, D)), _full((1, D)), _full((1, D)),
    ]
    args = [xconv, wa, wi, ba, bi, lam, h0]
    out_specs = []
    out_shape = []
    if epi is not None:
        hf, gg, w_out, x, mod = epi
        in_specs += [blk, blk, _full((D, D)), blk, _full((ADA_CHUNKS, D))]
        args += [hf, gg, w_out, x, mod]
    if h_dtype is not None:
        out_specs.append(blk)
        out_shape.append(jax.ShapeDtypeStruct((t, D), h_dtype))
    if epi is not None:
        out_specs.append(blk)
        out_shape.append(jax.ShapeDtypeStruct((t, D), F32))
    return pl.pallas_call(
        functools.partial(_rg_scan_body, reverse, epi is not None, h_dtype is not None, tc),
        grid=(nchunks,),
        in_specs=in_specs,
        out_specs=out_specs,
        out_shape=out_shape,
        scratch_shapes=[pltpu.VMEM((tc, D), F32), pltpu.VMEM((tc, D), F32), pltpu.VMEM((tc, D), F32),
                        pltpu.VMEM((SUBLANES, D), F32)],
        compiler_params=_cparams(("arbitrary",)),
        name="rg_scan_bwd" if reverse else "rg_scan_fwd",
    )(*args)


def rglru_layer(x, xc, mod, modc, g, w_in, conv_w, conv_b, wa, wi, ba, bi, lam, w_out, need_ctx):
    w_in_b = w_in.astype(BF16)
    wa_b = (0.5 * wa).astype(BF16)
    wi_b = (0.5 * wi).astype(BF16)
    w_out_b = w_out.astype(BF16)
    cb = conv_b[None]
    tcx = xc.shape[0]
    xcl, ggl = rg_in(x, mod, g, w_in_b, conv_w, cb, WIDE_TILE)
    xcc, ggc = rg_in(xc, modc, g, w_in_b, conv_w, cb, tcx)
    zeros = jnp.zeros((1, D), F32)
    p = lambda d: (wa_b[d], wi_b[d], ba[d][None], bi[d][None], lam[d][None])
    (hcf,) = rg_scan(xcc, *p(0), zeros, tcx, False)
    (hlf,) = rg_scan(xcl, *p(0), hcf[tcx - 1:tcx], WIDE_TILE, False, h_dtype=BF16)
    if need_ctx:
        hcb, xc_new = rg_scan(xcc, *p(1), zeros, tcx, True, epi=(hcf, ggc, w_out_b, xc, modc))
    else:
        (hcb,) = rg_scan(xcc, *p(1), zeros, tcx, True)
        xc_new = None
    (x_new,) = rg_scan(xcl, *p(1), hcb[0:1], WIDE_TILE, True, epi=(hlf, ggl, w_out_b, x, mod), h_dtype=None)
    return x_new, xc_new


def _qkv_body(x_ref, mod_ref, g_ref, w_ref, gm_ref, qg_ref, kg_ref, q_ref, k_ref, v_ref):
    h = _normmod(x_ref[...], g_ref[...], mod_ref[1:2, :], mod_ref[0:1, :]).astype(BF16)
    z = jnp.dot(h, w_ref[...], preferred_element_type=F32)

    def headnorm(v, gain):
        ms = jnp.dot((v * v).astype(BF16), gm_ref[...], preferred_element_type=F32)
        return (v * lax.rsqrt(ms + RMS_EPS)) * gain

    q_ref[...] = headnorm(z[:, :D], qg_ref[...]).astype(BF16)
    k_ref[...] = headnorm(z[:, D:2 * D], kg_ref[...]).astype(BF16)
    v_ref[...] = z[:, 2 * D:].astype(BF16)


def qkv_proj(x, mod, g, w_qkv, gmean, qg, kg, tm):
    t = x.shape[0]
    spec = pl.BlockSpec((tm, D), lambda i: (i, 0))
    return pl.pallas_call(
        _qkv_body,
        grid=(t // tm,),
        in_specs=[spec, _full((ADA_CHUNKS, D)), _full((1, D)), _full((D, 3 * D)), _full((D, D)),
                  _full((1, D)), _full((1, D))],
        out_specs=[spec, spec, spec],
        out_shape=[jax.ShapeDtypeStruct((t, D), BF16)] * 3,
        compiler_params=_cparams(("arbitrary",)),
        name="qkv_proj",
    )(x, mod, g, w_qkv, gmean, qg, kg)


def _attend_pair(q2, keys, vals, biases):
    m_rows = q2.shape[0]
    lane = lax.broadcasted_iota(jnp.int32, q2.shape, 1)
    zero = jnp.zeros_like(q2)
    qs = jnp.concatenate([jnp.where(lane < NA_HEAD_DIM, q2, zero), jnp.where(lane >= NA_HEAD_DIM, q2, zero)], axis=0)
    ss = []
    for kseg, bseg in zip(keys, biases):
        s = lax.dot_general(qs, kseg, (((1,), (1,)), ((), ())), preferred_element_type=F32)
        if bseg is not None:
            s = s + jnp.concatenate([bseg[0], bseg[1]], axis=0)
        ss.append(s)
    m = ss[0].max(axis=-1, keepdims=True)
    for s in ss[1:]:
        m = jnp.maximum(m, s.max(axis=-1, keepdims=True))
    den = None
    acc = None
    for s, vseg in zip(ss, vals):
        p = jnp.exp2(s - m)
        d = jnp.sum(p, axis=-1, keepdims=True)
        o = jnp.dot(p.astype(BF16), vseg, preferred_element_type=F32)
        den = d if den is None else den + d
        acc = o if acc is None else acc + o
    out = acc / den
    return jnp.where(lane < NA_HEAD_DIM, out[:m_rows], out[m_rows:])


NA_QROWS = 2
NA_UNION = NA_ROWS + NA_QROWS - 1


def _na_body(nside, var_ref, q_ref, kl_ref, vl_ref, kc_ref, vc_ref, bias_ref, *refs):
    side_in, o_ref, side_out = refs[:nside], refs[nside], refs[nside + 1:]
    for src, dst in zip(side_in, side_out):
        dst[...] = src[...].astype(BF16)
    for pr in range(NA_HEADS // 2):
        sl = slice(pr * LANES, (pr + 1) * LANES)
        o_ref[:, sl] = _attend_pair(
            q_ref[:, sl], [kl_ref[:, sl], kc_ref[:, sl]], [vl_ref[:, sl], vc_ref[:, sl]],
            [(bias_ref[0, 2 * pr], bias_ref[0, 2 * pr + 1]), None]).astype(BF16)


def _na_geometry(rows):
    steps = rows // NA_QROWS
    g = np.arange(steps)
    base = np.clip(NA_QROWS * g - NA_ROWS // 2, 0, rows - NA_UNION)
    r = NA_QROWS * g[:, None] + np.arange(NA_QROWS)[None, :]
    rs = np.clip(r - NA_ROWS // 2, 0, rows - NA_ROWS)
    key = np.concatenate([(base - NA_QROWS * g)[:, None], rs - r], axis=1)
    uniq, first, var = np.unique(key, axis=0, return_index=True, return_inverse=True)
    return base, var.reshape(-1).astype(np.int32), g[first]


def na_attention(q, k, v, kc, vc, bias_tab, var, side=()):
    t = q.shape[0]
    rows = t // GRID_W
    nctx = kc.shape[0]
    steps = rows // NA_QROWS
    side_specs = []
    for a in side:
        assert a.shape[0] % (steps * 16) == 0
        side_specs.append(pl.BlockSpec((a.shape[0] // steps, a.shape[1]), lambda g, var: (g, 0)))

    def kbase(g):
        return jnp.clip(NA_QROWS * g - NA_ROWS // 2, 0, rows - NA_UNION)

    qrows = NA_QROWS * GRID_W
    nloc = NA_UNION * GRID_W
    kspec = pl.BlockSpec((pl.Element(nloc), pl.Element(D)), lambda g, var: (kbase(g) * GRID_W, 0))
    grid_spec = pltpu.PrefetchScalarGridSpec(
        num_scalar_prefetch=1,
        grid=(steps,),
        in_specs=[pl.BlockSpec((qrows, D), lambda g, var: (g, 0)), kspec, kspec] + [
            pl.BlockSpec((nctx, D), lambda g, var: (0, 0)), pl.BlockSpec((nctx, D), lambda g, var: (0, 0)),
            pl.BlockSpec((1, NA_HEADS, qrows, nloc), lambda g, var: (var[g], 0, 0, 0)),
        ] + side_specs,
        out_specs=[pl.BlockSpec((qrows, D), lambda g, var: (g, 0))] + side_specs,
    )
    return pl.pallas_call(
        functools.partial(_na_body, len(side)),
        grid_spec=grid_spec,
        out_shape=[jax.ShapeDtypeStruct((t, D), BF16)] + [jax.ShapeDtypeStruct(a.shape, BF16) for a in side],
        compiler_params=_cparams(("arbitrary",)),
        name="na_attention",
    )(var, q, k, v, kc, vc, bias_tab, *side)


def _ctx_attn_body(q_ref, k_ref, v_ref, o_ref):
    for pr in range(NA_HEADS // 2):
        sl = slice(pr * LANES, (pr + 1) * LANES)
        o_ref[:, sl] = _attend_pair(q_ref[:, sl], [k_ref[:, sl]], [v_ref[:, sl]], [None]).astype(BF16)


def ctx_attention(q, k, v):
    t = q.shape[0]
    return pl.pallas_call(
        _ctx_attn_body,
        grid=(1,),
        in_specs=[_full((t, D))] * 3,
        out_specs=_full((t, D)),
        out_shape=jax.ShapeDtypeStruct((t, D), BF16),
        compiler_params=_cparams(("arbitrary",)),
        name="ctx_attention",
    )(q, k, v)


def _na_bias_table(rpb, rows):
    base, var, reps = _na_geometry(rows)
    cols = np.arange(GRID_W)
    cstart = np.clip(cols - NA_COLS // 2, 0, GRID_W - NA_COLS)
    kcol = np.arange(GRID_W)
    inwin = (kcol[None, :] >= cstart[:, None]) & (kcol[None, :] < cstart[:, None] + NA_COLS)
    r = NA_QROWS * reps[:, None] + np.arange(NA_QROWS)[None, :]
    rs = np.clip(r - NA_ROWS // 2, 0, rows - NA_ROWS)
    krow = base[reps][:, None] + np.arange(NA_UNION)[None, :]
    rvalid = (krow[:, None, :] >= rs[:, :, None]) & (krow[:, None, :] < rs[:, :, None] + NA_ROWS)
    ridx = np.clip(krow[:, None, :] - r[:, :, None] + (NA_ROWS - 1), 0, 2 * NA_ROWS - 2)
    nd = 2 * NA_COLS - 1
    w = jnp.pad(rpb.astype(F32), ((0, 0), (0, 0), (GRID_W - NA_COLS, 2 * GRID_W - (GRID_W - NA_COLS) - nd)))
    flat = jnp.tile(w, (1, 1, GRID_W))[:, :, :GRID_W * (2 * GRID_W - 1)]
    blk = flat.reshape(NA_HEADS, 2 * NA_ROWS - 1, GRID_W, 2 * GRID_W - 1)[..., GRID_W - 1:]
    blk = jnp.where(jnp.asarray(inwin)[None, None], blk, NEG_BIG)
    neg = jnp.full((NA_HEADS, GRID_W, GRID_W), NEG_BIG, F32)
    variants = []
    for v in range(len(reps)):
        strips = [jnp.concatenate([blk[:, ridx[v, a, j]] if rvalid[v, a, j] else neg for j in range(NA_UNION)], axis=2)
                  for a in range(NA_QROWS)]
        variants.append(jnp.concatenate(strips, axis=1))
    return jnp.stack(variants, axis=0), jnp.asarray(var)


def na_layer(x, xc, mod, modc, g, w_qkv, q_g, k_g, rpb, w_o, need_ctx, side=()):
    w_qkv_b = w_qkv.astype(BF16)
    w_o_b = w_o.astype(BF16)
    gm = np.kron(np.eye(NA_HEADS), np.full((NA_HEAD_DIM, NA_HEAD_DIM), 1.0 / NA_HEAD_DIM))
    gmean = jnp.asarray(gm, dtype=BF16)
    qg = jnp.tile(q_g, NA_HEADS)[None] * (NA_HEAD_DIM ** -0.5 * LOG2E)
    kg = jnp.tile(k_g, NA_HEADS)[None]
    q, k, v = qkv_proj(x, mod, g, w_qkv_b, gmean, qg, kg, WIDE_TILE)
    qc, kc, vc = qkv_proj(xc, modc, g, w_qkv_b, gmean, qg, kg, xc.shape[0])
    bias_tab, var = _na_bias_table(rpb * LOG2E, x.shape[0] // GRID_W)
    o, *side_b = na_attention(q, k, v, kc, vc, bias_tab, var, side)
    x_new = proj_residual(o, w_o_b, x, mod, 2, WIDE_TILE)
    xc_new = None
    if need_ctx:
        oc = ctx_attention(qc, kc, vc)
        xc_new = proj_residual(oc, w_o_b, xc, modc, 2, xc.shape[0])
    return x_new, xc_new, side_b


def _dft_mats(n):
    ang = 2.0 * np.pi * np.outer(np.arange(n), np.arange(n)) / n
    return np.cos(ang), np.sin(ang)


def _channel_dft(h, wc):
    us = [jnp.dot(h[:, gi * FT_GROUP_W:(gi + 1) * FT_GROUP_W], wc, preferred_element_type=F32).astype(BF16)
          for gi in range(D // FT_GROUP_W)]
    return jnp.concatenate([u[:, :FT_GROUP_W] for u in us] + [u[:, FT_GROUP_W:] for u in us], axis=1)


def _ft_a_body(n, nj, x_ref, mod_ref, g_ref, perm_ref, wc_ref, ma_ref, tc_ref, ts_ref, yr_ref, yi_ref):
    h3 = _normmod(x_ref[...], g_ref[...], mod_ref[1:2, :], mod_ref[0:1, :])
    h = jnp.dot(perm_ref[...], h3.reshape(n * nj, D).astype(BF16), preferred_element_type=F32).astype(BF16)
    u = _channel_dft(h, wc_ref[...])
    for j in range(nj):
        uj = u[j * n:(j + 1) * n]
        y = jnp.dot(ma_ref[...], jnp.concatenate([uj[:, :D], uj[:, D:]], axis=0), preferred_element_type=F32)
        yr, yi = y[:n], y[n:]
        tc = _lane_tile(tc_ref[0, :, j * LANES:(j + 1) * LANES])
        ts = _lane_tile(ts_ref[0, :, j * LANES:(j + 1) * LANES])
        yr_ref[:, j, :] = yr * tc + yi * ts
        yi_ref[:, j, :] = yi * tc - yr * ts


def _ft_c_body(n, nj, yr_ref, yi_ref, mc_ref, wf_ref, x_ref, mod_ref, o_ref):
    fs = []
    for j in range(nj):
        ys = jnp.concatenate([yr_ref[j].astype(BF16), yi_ref[j].astype(BF16)], axis=0)
        fs.append(jnp.dot(mc_ref[...], ys, preferred_element_type=F32).astype(BF16))
    z = jnp.dot(jnp.concatenate(fs, axis=0), wf_ref[...], preferred_element_type=F32)
    gate = mod_ref[2:3, :]
    for j in range(nj):
        o_ref[:, j, :] = x_ref[:, j, :] + gate * z[j * n:(j + 1) * n]


def _ft_ctx_body(x_ref, mod_ref, g_ref, wc_ref, ml_ref, wf_ref, o_ref):
    x = x_ref[...]
    h = _normmod(x, g_ref[...], mod_ref[1:2, :], mod_ref[0:1, :]).astype(BF16)
    u = _channel_dft(h, wc_ref[...])
    us = jnp.concatenate([u[:, :D], u[:, D:]], axis=0)
    f = jnp.dot(ml_ref[...], us, preferred_element_type=F32).astype(BF16)
    o_ref[...] = x + mod_ref[2:3, :] * jnp.dot(f, wf_ref[...], preferred_element_type=F32)


def fourier_layer(x, xc, mod, modc, g, w_f, need_ctx):
    t = x.shape[0]
    n = math.isqrt(t)
    assert n * n == t and n % 16 == 0
    w_f_b = w_f.astype(BF16)
    cw, sw = _dft_mats(FT_GROUP_W)
    wc = jnp.asarray(np.concatenate([cw, -sw], axis=1) / math.sqrt(FT_GROUP_W), dtype=F32).astype(BF16)
    wcspec = _full((FT_GROUP_W, 2 * FT_GROUP_W))
    cn, sn = _dft_mats(n)
    ma = jnp.asarray(np.block([[cn, sn], [-sn, cn]]) / math.sqrt(n), dtype=F32).astype(BF16)
    mc = jnp.asarray(np.concatenate([cn, sn], axis=1) / math.sqrt(n), dtype=F32).astype(BF16)
    nj = 8
    ang = 2.0 * np.pi * np.outer(np.arange(n), np.arange(n)) / t
    def expand(tab):
        a = jnp.asarray(tab, dtype=F32).reshape(n // nj, nj, n).transpose(0, 2, 1)
        return jnp.repeat(a, LANES, axis=2)
    twc, tws = expand(np.cos(ang)), expand(np.sin(ang))
    xblk = pl.BlockSpec((n, nj, D), lambda b: (0, b, 0))
    yblk = pl.BlockSpec((nj, n, D), lambda b: (b, 0, 0))
    tblk = pl.BlockSpec((1, n, nj * LANES), lambda b: (b, 0, 0))
    x3 = x.reshape(n, n, D)
    src = (np.arange(n)[None, :] * nj + np.arange(nj)[:, None]).reshape(-1)
    perm = jnp.asarray(np.eye(n * nj)[src], dtype=BF16)
    yr, yi = pl.pallas_call(
        functools.partial(_ft_a_body, n, nj),
        grid=(n // nj,),
        in_specs=[xblk, _full((ADA_CHUNKS, D)), _full((1, D)), _full((n * nj, n * nj)), wcspec,
                  _full((2 * n, 2 * n)), tblk, tblk],
        out_specs=[xblk, xblk],
        out_shape=[jax.ShapeDtypeStruct((n, n, D), F32)] * 2,
        compiler_params=_cparams(("arbitrary",)),
        name="ft_stage_a",
    )(x3, mod, g, perm, wc, ma, twc, tws)
    x_new = pl.pallas_call(
        functools.partial(_ft_c_body, n, nj),
        grid=(n // nj,),
        in_specs=[yblk, yblk, _full((n, 2 * n)), _full((D, D)), xblk, _full((ADA_CHUNKS, D))],
        out_specs=xblk,
        out_shape=jax.ShapeDtypeStruct((n, n, D), F32),
        compiler_params=_cparams(("arbitrary",)),
        name="ft_stage_c",
    )(yr, yi, mc, w_f_b, x3, mod).reshape(t, D)
    xc_new = None
    if need_ctx:
        lc = xc.shape[0]
        cl, sl = _dft_mats(lc)
        ml = jnp.asarray(np.concatenate([cl, sl], axis=1) / math.sqrt(lc), dtype=F32).astype(BF16)
        xc_new = pl.pallas_call(
            _ft_ctx_body,
            grid=(1,),
            in_specs=[_full((lc, D)), _full((ADA_CHUNKS, D)), _full((1, D)), wcspec,
                      _full((lc, 2 * lc)), _full((D, D))],
            out_specs=_full((lc, D)),
            out_shape=jax.ShapeDtypeStruct((lc, D), F32),
            compiler_params=_cparams(("arbitrary",)),
            name="ft_ctx",
        )(xc, modc, g, wc, ml, w_f_b)
    return x_new, xc_new


def _router_body(x_ref, mod_ref, g_ref, r_ref, info_ref, w0_ref, w1_ref):
    h = _normmod(x_ref[...], g_ref[...], mod_ref[4:5, :], mod_ref[3:4, :])
    hh = h.astype(BF16)
    hl = (h - hh.astype(F32)).astype(BF16)
    r = r_ref[...]
    rh = r.astype(BF16)
    rl = (r - rh.astype(F32)).astype(BF16)
    logits = (jnp.dot(hh, rh, preferred_element_type=F32) + jnp.dot(hh, rl, preferred_element_type=F32)
              + jnp.dot(hl, rh, preferred_element_type=F32))
    lane = lax.broadcasted_iota(jnp.int32, logits.shape, 1)
    logits = jnp.where(lane < N_EXPERTS, logits, NEG_BIG)
    v0 = jnp.max(logits, axis=-1, keepdims=True)
    i0 = jnp.min(jnp.where(logits == v0, lane, LANES), axis=-1, keepdims=True)
    rest = jnp.where(lane == i0, NEG_BIG, logits)
    v1 = jnp.max(rest, axis=-1, keepdims=True)
    i1 = jnp.min(jnp.where(rest == v1, lane, LANES), axis=-1, keepdims=True)
    e = jnp.exp(v1 - v0)
    w0 = 1.0 / (1.0 + e)
    w1 = e / (1.0 + e)
    info_ref[...] = jnp.where(lane == 0, i0, jnp.where(lane == 1, i1, 0))
    w0_ref[...] = jnp.broadcast_to(w0, logits.shape)
    w1_ref[...] = jnp.broadcast_to(w1, logits.shape)


def moe_router(x, mod, g, router_pad, tm):
    t = x.shape[0]
    spec = pl.BlockSpec((tm, D), lambda i: (i, 0))
    lspec = pl.BlockSpec((tm, LANES), lambda i: (i, 0))
    return pl.pallas_call(
        _router_body,
        grid=(t // tm,),
        in_specs=[spec, _full((ADA_CHUNKS, D)), _full((1, D)), _full((D, LANES))],
        out_specs=[lspec, lspec, lspec],
        out_shape=[jax.ShapeDtypeStruct((t, LANES), jnp.int32),
                   jax.ShapeDtypeStruct((t, LANES), F32), jax.ShapeDtypeStruct((t, LANES), F32)],
        compiler_params=_cparams(("arbitrary",)),
        name="moe_router",
    )(x, mod, g, router_pad)


SCATTER_TOKENS = 256


def _row_scatter_body(nlat, nctx, didx_ref, g_ref, x_ref, mod_ref, *rest):
    if nctx:
        xc_ref, modc_ref, dst_ref, h_scr, zero_scr, sems = rest
    else:
        dst_ref, h_scr, zero_scr, sems = rest
    i = pl.program_id(0)
    nsteps = pl.num_programs(0)
    ts = SCATTER_TOKENS
    slot = i % 2

    def start_all(src_ref, src_is_zero_rows):
        def issue(grp, c):
            base = pl.multiple_of(grp * SUBLANES, SUBLANES)
            for r in range(SUBLANES):
                src = src_ref.at[pl.ds(r if src_is_zero_rows else base + r, 1), :]
                for half in range(2):
                    d = didx_ref[0, 0, base + r + half * ts]
                    pltpu.make_async_copy(src, dst_ref.at[pl.ds(d, 1), :], sems.at[slot]).start(priority=half)
            return c
        lax.fori_loop(0, ts // SUBLANES, issue, 0)

    def wait_all(which):
        def drain(n, c):
            pltpu.make_async_copy(zero_scr.at[pl.ds(0, 1), :], dst_ref.at[pl.ds(0, 1), :], sems.at[which]).wait()
            return c
        lax.fori_loop(0, 2 * ts, drain, 0, unroll=8)

    @pl.when(i == 0)
    def _():
        zero_scr[...] = jnp.zeros_like(zero_scr)

    def stage_and_start(src_ref, m_ref):
        h_scr[slot] = _normmod(src_ref[...], g_ref[...], m_ref[4:5, :], m_ref[3:4, :])
        start_all(h_scr.at[slot], False)

    @pl.when(i < nlat)
    def _():
        stage_and_start(x_ref, mod_ref)

    if nctx:
        @pl.when(jnp.logical_and(i >= nlat, i < nlat + nctx))
        def _():
            stage_and_start(xc_ref, modc_ref)

    @pl.when(i >= nlat + nctx)
    def _():
        start_all(zero_scr, True)

    @pl.when(i > 0)
    def _():
        wait_all(1 - slot)

    @pl.when(i == nsteps - 1)
    def _():
        wait_all(slot)


def row_scatter(x, xc, mod, modc, g, d0, d1, pad_pos):
    ts = SCATTER_TOKENS
    nlat = x.shape[0] // ts
    nctx = 0 if xc is None else 1
    assert xc is None or xc.shape[0] == ts
    ntok = nlat + nctx
    npad = pad_pos.shape[0] // (2 * ts)
    didx = jnp.concatenate([jnp.concatenate([d0.reshape(ntok, 1, ts), d1.reshape(ntok, 1, ts)], axis=2),
                            pad_pos.reshape(npad, 1, 2 * ts)], axis=0)
    in_specs = [pl.BlockSpec((1, 1, 2 * ts), lambda i: (i, 0, 0), memory_space=pltpu.SMEM),
                _full((1, D)),
                pl.BlockSpec((ts, D), lambda i: (jnp.minimum(i, nlat - 1), 0)),
                _full((ADA_CHUNKS, D))]
    args = [didx, g, x, mod]
    if nctx:
        in_specs += [_full((ts, D)), _full((ADA_CHUNKS, D))]
        args += [xc, modc]
    return pl.pallas_call(
        functools.partial(_row_scatter_body, nlat, nctx),
        grid=(ntok + npad,),
        in_specs=in_specs,
        out_specs=pl.BlockSpec(memory_space=pl.ANY),
        out_shape=jax.ShapeDtypeStruct((2 * ntok * ts + pad_pos.shape[0], D), F32),
        scratch_shapes=[pltpu.VMEM((2, ts, D), F32), pltpu.VMEM((SUBLANES, D), F32), pltpu.SemaphoreType.DMA((2,))],
        compiler_params=_cparams(("arbitrary",)),
        name="moe_row_scatter",
    )(*args)


def _moe_ffn_body(te_ref, tv_ref, xg_ref, wg_ref, wu_ref, wd_ref, o_ref):
    i = pl.program_id(0)

    @pl.when(tv_ref[i] > 0)
    def _():
        o_ref[...] = _swiglu_chunks(xg_ref[...].astype(BF16), wg_ref, wu_ref, wd_ref)

    @pl.when(tv_ref[i] == 0)
    def _():
        o_ref[...] = jnp.zeros_like(o_ref)


def moe_ffn(xg, tile_e, tile_v, w_gu, w_down, li, tm):
    p = xg.shape[0]
    grid_spec = pltpu.PrefetchScalarGridSpec(
        num_scalar_prefetch=2,
        grid=(p // tm,),
        in_specs=[
            pl.BlockSpec((tm, D), lambda i, te, tv: (i, 0)),
            pl.BlockSpec((None, None, D, D_FF), lambda i, te, tv: (li, te[i], 0, 0)),
            pl.BlockSpec((None, None, D, D_FF), lambda i, te, tv: (li, te[i], 0, 1)),
            pl.BlockSpec((None, None, D_FF, D), lambda i, te, tv: (li, te[i], 0, 0)),
        ],
        out_specs=pl.BlockSpec((tm, D), lambda i, te, tv: (i, 0)),
    )
    return pl.pallas_call(
        _moe_ffn_body,
        grid_spec=grid_spec,
        out_shape=jax.ShapeDtypeStruct((p, D), F32),
        compiler_params=pltpu.CompilerParams(dimension_semantics=("arbitrary",), vmem_limit_bytes=MOE_VMEM_LIMIT),
        name="moe_ffn",
    )(tile_e, tile_v, xg, w_gu, w_gu, w_down)


def _combine_body(tt, d0_ref, d1_ref, d0n_ref, d1n_ref, yp_ref, x_ref, mod_ref, w0_ref, w1_ref, o_ref,
                  a_scr, b_scr, sems):
    i = pl.program_id(0)
    nsteps = pl.num_programs(0)
    slot = i % 2

    def start_all(i0_ref, i1_ref, which):
        def issue(grp, c):
            base = pl.multiple_of(grp * SUBLANES, SUBLANES)
            for r in range(SUBLANES):
                n = base + r
                pltpu.make_async_copy(yp_ref.at[pl.ds(i0_ref[0, 0, n], 1), :], a_scr.at[which, pl.ds(n, 1), :],
                                      sems.at[which]).start(priority=0)
                pltpu.make_async_copy(yp_ref.at[pl.ds(i1_ref[0, 0, n], 1), :], b_scr.at[which, pl.ds(n, 1), :],
                                      sems.at[which]).start(priority=1)
            return c
        lax.fori_loop(0, tt // SUBLANES, issue, 0)

    @pl.when(i == 0)
    def _():
        start_all(d0_ref, d1_ref, 0)

    @pl.when(i + 1 < nsteps)
    def _():
        start_all(d0n_ref, d1n_ref, 1 - slot)

    def drain(n, c):
        pltpu.make_async_copy(yp_ref.at[pl.ds(0, 1), :], a_scr.at[slot, pl.ds(0, 1), :], sems.at[slot]).wait()
        pltpu.make_async_copy(yp_ref.at[pl.ds(0, 1), :], b_scr.at[slot, pl.ds(0, 1), :], sems.at[slot]).wait()
        return c
    lax.fori_loop(0, tt, drain, 0, unroll=8)
    w0 = _lane_tile(w0_ref[...])
    w1 = _lane_tile(w1_ref[...])
    o_ref[...] = x_ref[...] + mod_ref[5:6, :] * (w0 * a_scr[slot] + w1 * b_scr[slot])


def moe_combine(yp, d0, d1, x, mod, w0b, w1b, tt):
    t = x.shape[0]
    nt = t // tt
    ispec = pl.BlockSpec((1, 1, tt), lambda i: (i, 0, 0), memory_space=pltpu.SMEM)
    nspec = pl.BlockSpec((1, 1, tt), lambda i: (jnp.minimum(i + 1, nt - 1), 0, 0), memory_space=pltpu.SMEM)
    spec = pl.BlockSpec((tt, D), lambda i: (i, 0))
    lspec = pl.BlockSpec((tt, LANES), lambda i: (i, 0))
    d0r, d1r = d0.reshape(nt, 1, tt), d1.reshape(nt, 1, tt)
    return pl.pallas_call(
        functools.partial(_combine_body, tt),
        grid=(nt,),
        in_specs=[ispec, ispec, nspec, nspec, pl.BlockSpec(memory_space=pl.ANY), spec, _full((ADA_CHUNKS, D)),
                  lspec, lspec],
        out_specs=spec,
        out_shape=jax.ShapeDtypeStruct((t, D), F32),
        scratch_shapes=[pltpu.VMEM((2, tt, D), F32), pltpu.VMEM((2, tt, D), F32), pltpu.SemaphoreType.DMA((2,))],
        compiler_params=_cparams(("arbitrary",)),
        name="moe_combine",
    )(d0r, d1r, d0r, d1r, yp, x, mod, w0b, w1b)


def _route_plan(e0, e1, tm):
    t = e0.shape[0]
    n = 2 * t
    ex = jnp.arange(N_EXPERTS, dtype=jnp.int32)
    oh0 = (e0[:, None] == ex[None, :]).astype(jnp.int32)
    oh1 = (e1[:, None] == ex[None, :]).astype(jnp.int32)
    both = oh0 + oh1
    csum = jnp.cumsum(both, axis=0)
    before = csum - both
    counts = csum[-1]
    padded = ((counts + tm - 1) // tm) * tm
    pad_end = jnp.cumsum(padded)
    pad_off = pad_end - padded
    total = pad_end[-1]
    d0 = jnp.sum(oh0 * (before + pad_off[None, :]), axis=1)
    d1 = jnp.sum(oh1 * (before + oh0 + pad_off[None, :]), axis=1)
    gap = padded - counts
    tail_off = jnp.cumsum(tm - gap) - (tm - gap)
    r = jnp.arange(tm, dtype=jnp.int32)[None, :]
    pad_pos = jnp.where(r < gap[:, None], (pad_off + counts)[:, None] + r,
                        total + tail_off[:, None] + (r - gap[:, None])).reshape(-1)
    ntiles = (n + N_EXPERTS * tm) // tm
    tstart = jnp.arange(ntiles, dtype=jnp.int32) * tm
    tile_v = (tstart < total).astype(jnp.int32)
    tile_e = jnp.sum((jnp.minimum(tstart, total - 1)[:, None] >= pad_end[None, :]).astype(jnp.int32), axis=1)
    return (d0.astype(jnp.int32), d1.astype(jnp.int32), pad_pos.astype(jnp.int32), tile_e.astype(jnp.int32), tile_v)


def moe_layer(x, xc, mod, modc, g, router, w_gu_b, w_down_b, li, need_ctx, tm=ROW_TILE):
    router_pad = jnp.pad(router, ((0, 0), (0, LANES - N_EXPERTS)))
    s = x.shape[0]
    info, w0b, w1b = moe_router(x, mod, g, router_pad, WIDE_TILE)
    if need_ctx:
        sc = xc.shape[0]
        infoc, w0c, w1c = moe_router(xc, modc, g, router_pad, sc)
        e0 = jnp.concatenate([info[:, 0], infoc[:, 0]])
        e1 = jnp.concatenate([info[:, 1], infoc[:, 1]])
    else:
        e0, e1 = info[:, 0], info[:, 1]
    d0, d1, pad_pos, tile_e, tile_v = _route_plan(e0, e1, tm)
    xg = row_scatter(x, xc if need_ctx else None, mod, modc, g, d0, d1, pad_pos)
    yp = moe_ffn(xg, tile_e, tile_v, w_gu_b, w_down_b, li, tm)
    x_new = moe_combine(yp, d0[:s], d1[:s], x, mod, w0b, w1b, ROW_TILE)
    xc_new = None
    if need_ctx:
        xc_new = moe_combine(yp, d0[s:], d1[s:], xc, modc, w0c, w1c, sc)
    return x_new, xc_new


def kernel(x, c, ctx, c_ctx, ada_w, ada_b, norm_g, rg_w_in, rg_conv_w, rg_conv_b, rg_wa, rg_ba, rg_wi, rg_bi,
           rg_lambda, rg_w_out, na_w_qkv, na_q_g, na_k_g, na_rpb, na_w_o, ft_w_out, ffn_w_gu, ffn_w_down,
           moe_router, moe_w_gu, moe_w_down):
    depth = ada_w.shape[0]
    assert x.shape[0] == 1 and x.shape[2] == D
    xs = x[0]
    xc = ctx[0]
    mods = ada_modulation(c, c_ctx, ada_w, ada_b)
    ffn_gu_b, ffn_dn_b = ffn_w_gu.astype(BF16), ffn_w_down.astype(BF16)
    moe_gu_b = moe_dn_b = None
    mix_idx = [0] * N_MIXERS
    dense_idx = 0
    moe_idx = 0
    for layer in range(depth):
        need_ctx = layer != depth - 1
        mod, modc = mods[layer, 0], mods[layer, 1]
        g0 = norm_g[layer, 0][None]
        g1 = norm_g[layer, 1][None]
        kind = layer % N_MIXERS
        j = mix_idx[kind]
        mix_idx[kind] += 1
        if kind == 0:
            xs, xcn = rglru_layer(xs, xc, mod, modc, g0, rg_w_in[j], rg_conv_w[j], rg_conv_b[j], rg_wa[j], rg_wi[j],
                                  rg_ba[j], rg_bi[j], rg_lambda[j], rg_w_out[j], need_ctx)
        elif kind == 1:
            side = ()
            if moe_gu_b is None:
                side = (moe_w_gu.reshape(-1, moe_w_gu.shape[-1]), moe_w_down.reshape(-1, moe_w_down.shape[-1]))
            xs, xcn, side_b = na_layer(xs, xc, mod, modc, g0, na_w_qkv[j], na_q_g[j], na_k_g[j], na_rpb[j],
                                       na_w_o[j], need_ctx, side)
            if side_b:
                moe_gu_b, moe_dn_b = side_b[0].reshape(moe_w_gu.shape), side_b[1].reshape(moe_w_down.shape)
        else:
            xs, xcn = fourier_layer(xs, xc, mod, modc, g0, ft_w_out[j], need_ctx)
        if need_ctx:
            xc = xcn
        if layer % 2 == 0:
            if need_ctx:
                xc = ffn_dense(xc, modc, g1, ffn_gu_b, ffn_dn_b, dense_idx, xc.shape[0])
            xs = ffn_dense(xs, mod, g1, ffn_gu_b, ffn_dn_b, dense_idx, ROW_TILE)
            dense_idx += 1
        else:
            if moe_gu_b is None:
                moe_gu_b, moe_dn_b = moe_w_gu.astype(BF16), moe_w_down.astype(BF16)
            xs, xcn = moe_layer(xs, xc, mod, modc, g1, moe_router[moe_idx], moe_gu_b, moe_dn_b, moe_idx, need_ctx)
            moe_idx += 1
            if need_ctx:
                xc = xcn
    return xs[None]
```

```python
import functools
import math

import numpy as np
import jax
import jax.numpy as jnp
from jax import lax
from jax.experimental import pallas as pl
from jax.experimental.pallas import tpu as pltpu

F32 = jnp.float32
BF16 = jnp.bfloat16

D = 1024
D_FF = 3584
N_EXPERTS = 8
GRID_W = 64
NA_HEADS = 16
NA_HEAD_DIM = 64
NA_ROWS = 8
NA_COLS = 16
FT_GROUP_W = 256
RG_BLOCK_W = 256
RMS_EPS = 1e-6
LRU_C = 8.0
N_MIXERS = 3
ADA_CHUNKS = 6

LANES = 128
SUBLANES = 8
BF16_ROWS = 16
VMEM_LIMIT = 56 * 1024 * 1024
MOE_VMEM_LIMIT = 60 * 1024 * 1024

ROW_TILE = 512
WIDE_TILE = 1024
LIGHT_TILE = 2048
NEG_BIG = -1e30
LOG2E = math.log2(math.e)


def _cparams(sem):
    return pltpu.CompilerParams(dimension_semantics=sem, vmem_limit_bytes=VMEM_LIMIT)


def _full(shape):
    nd = len(shape)
    return pl.BlockSpec(shape, lambda *_: (0,) * nd)


def _normmod(x, g, scale, shift):
    ms = jnp.mean(x * x, axis=-1, keepdims=True)
    y = x * lax.rsqrt(ms + RMS_EPS)
    return (y * g) * (1.0 + scale) + shift


def _lane_tile(v):
    return jnp.concatenate([v] * (D // LANES), axis=1)


def _sigmoid(v):
    return 1.0 / (1.0 + jnp.exp(-v))


def _gelu_tanh(v):
    c = math.sqrt(2.0 / math.pi)
    return v * (0.5 * (1.0 + jnp.tanh(c * (v + 0.044715 * (v * v * v)))))


def _ada_body(cin_ref, w_ref, b_ref, o_ref):
    v = cin_ref[...]
    s = v * _sigmoid(v)
    w = w_ref[0]
    r0 = jnp.sum(s[:, 0:1] * w, axis=0, keepdims=True)
    r1 = jnp.sum(s[:, 1:2] * w, axis=0, keepdims=True)
    o_ref[0] = jnp.concatenate([r0, r1], axis=0) + b_ref[0]


def ada_modulation(c, c_ctx, ada_w, ada_b):
    depth = ada_w.shape[0]
    n = ada_w.shape[2]
    nc = n // 4
    cin = jnp.stack([c[0], c_ctx], axis=1)
    out = pl.pallas_call(
        _ada_body,
        grid=(depth, n // nc),
        in_specs=[
            pl.BlockSpec((D, 2), lambda l, j: (0, 0)),
            pl.BlockSpec((1, D, nc), lambda l, j: (l, 0, j)),
            pl.BlockSpec((1, 1, nc), lambda l, j: (l, 0, j)),
        ],
        out_specs=pl.BlockSpec((1, 2, nc), lambda l, j: (l, 0, j)),
        out_shape=jax.ShapeDtypeStruct((depth, 2, n), F32),
        compiler_params=_cparams(("arbitrary", "arbitrary")),
        name="ada_mod",
    )(cin, ada_w, ada_b.reshape(depth, 1, n))
    return out.reshape(depth, 2, ADA_CHUNKS, D)


FFN_CHUNK = 512


def _swiglu_chunks(h, wg_ref, wu_ref, wd_ref):
    acc = None
    for c in range(D_FF // FFN_CHUNK):
        sl = slice(c * FFN_CHUNK, (c + 1) * FFN_CHUNK)
        gg = jnp.dot(h, wg_ref[:, sl], preferred_element_type=F32)
        uu = jnp.dot(h, wu_ref[:, sl], preferred_element_type=F32)
        a = ((gg * _sigmoid(gg)) * uu).astype(BF16)
        part = jnp.dot(a, wd_ref[sl, :], preferred_element_type=F32)
        acc = part if acc is None else acc + part
    return acc


def _ffn_body(x_ref, mod_ref, g_ref, wg_ref, wu_ref, wd_ref, o_ref):
    x = x_ref[...]
    h = _normmod(x, g_ref[...], mod_ref[4:5, :], mod_ref[3:4, :]).astype(BF16)
    o_ref[...] = x + mod_ref[5:6, :] * _swiglu_chunks(h, wg_ref, wu_ref, wd_ref)


def ffn_dense(x, mod, g, w_gu, w_down, li, tm):
    t = x.shape[0]
    once = pl.Buffered(1)
    return pl.pallas_call(
        _ffn_body,
        grid=(t // tm,),
        in_specs=[
            pl.BlockSpec((tm, D), lambda i: (i, 0)),
            _full((ADA_CHUNKS, D)),
            _full((1, D)),
            pl.BlockSpec((None, D, D_FF), lambda i: (li, 0, 0), pipeline_mode=once),
            pl.BlockSpec((None, D, D_FF), lambda i: (li, 0, 1), pipeline_mode=once),
            pl.BlockSpec((None, D_FF, D), lambda i: (li, 0, 0), pipeline_mode=once),
        ],
        out_specs=pl.BlockSpec((tm, D), lambda i: (i, 0)),
        out_shape=jax.ShapeDtypeStruct((t, D), F32),
        compiler_params=_cparams(("arbitrary",)),
        name="ffn_dense",
    )(x, mod, g, w_gu, w_gu, w_down)


def _proj_body(gate_row, a_ref, w_ref, x_ref, mod_ref, o_ref):
    y = jnp.dot(a_ref[...], w_ref[...], preferred_element_type=F32)
    o_ref[...] = x_ref[...] + mod_ref[gate_row:gate_row + 1, :] * y


def proj_residual(a, w, x, mod, gate_row, tm):
    t, k = a.shape
    tm = min(tm, t)
    return pl.pallas_call(
        functools.partial(_proj_body, gate_row),
        grid=(t // tm,),
        in_specs=[
            pl.BlockSpec((tm, k), lambda i: (i, 0)),
            _full((k, D)),
            pl.BlockSpec((tm, D), lambda i: (i, 0)),
            _full((ADA_CHUNKS, D)),
        ],
        out_specs=pl.BlockSpec((tm, D), lambda i: (i, 0)),
        out_shape=jax.ShapeDtypeStruct((t, D), F32),
        compiler_params=_cparams(("arbitrary",)),
        name="proj_residual",
    )(a, w, x, mod)


HALO = SUBLANES
RG_IN_PIECE = 128


def _rg_in_body(tm, xp_ref, x_ref, xn_ref, mod_ref, g_ref, w_ref, cw_ref, cb_ref, xc_ref, gg_ref):
    i = pl.program_id(0)
    last = pl.num_programs(0) - 1
    xa = jnp.concatenate([xp_ref[...], x_ref[...], xn_ref[...]], axis=0)
    npiece = tm // RG_IN_PIECE
    bounds = [0] + [2 * HALO + RG_IN_PIECE * (k + 1) for k in range(npiece - 1)] + [tm + 2 * HALO]
    zs = []
    for k in range(npiece):
        hk = _normmod(xa[bounds[k]:bounds[k + 1]], g_ref[...], mod_ref[1:2, :], mod_ref[0:1, :]).astype(BF16)
        zs.append(jnp.dot(hk, w_ref[...], preferred_element_type=F32))
    z = jnp.concatenate(zs, axis=0)
    row = lax.broadcasted_iota(jnp.int32, (tm + 2 * HALO, 1), 0)
    valid = jnp.logical_and(jnp.logical_or(row >= HALO, i > 0),
                            jnp.logical_or(row < tm + HALO, i < last))
    xz = jnp.where(valid, z[:, :D], 0.0)
    y = cb_ref[...] + cw_ref[2:3, :] * xz[HALO:HALO + tm]
    y = y + cw_ref[0:1, :] * xz[HALO - 2:HALO - 2 + tm]
    y = y + cw_ref[1:2, :] * xz[HALO - 1:HALO - 1 + tm]
    y = y + cw_ref[3:4, :] * xz[HALO + 1:HALO + 1 + tm]
    xc_ref[...] = y
    gg_ref[...] = _gelu_tanh(z[HALO:HALO + tm, D:]).astype(BF16)


def rg_in(x, mod, g, w_in, conv_w, conv_b, tm):
    t = x.shape[0]
    nb = tm // HALO
    nblk = t // HALO
    return pl.pallas_call(
        functools.partial(_rg_in_body, tm),
        grid=(t // tm,),
        in_specs=[
            pl.BlockSpec((HALO, D), lambda i: (jnp.maximum(i * nb - 1, 0), 0)),
            pl.BlockSpec((tm, D), lambda i: (i, 0)),
            pl.BlockSpec((HALO, D), lambda i: (jnp.minimum((i + 1) * nb, nblk - 1), 0)),
            _full((ADA_CHUNKS, D)),
            _full((1, D)),
            _full((D, 2 * D)),
            _full((4, D)),
            _full((1, D)),
        ],
        out_specs=[pl.BlockSpec((tm, D), lambda i: (i, 0)), pl.BlockSpec((tm, D), lambda i: (i, 0))],
        out_shape=[jax.ShapeDtypeStruct((t, D), F32), jax.ShapeDtypeStruct((t, D), BF16)],
        compiler_params=_cparams(("arbitrary",)),
        name="rg_in",
    )(x, x, x, mod, g, w_in, conv_w, conv_b)


def _rg_gates(xc, wa_ref, wi_ref, ba, bi, lam):
    xb = xc.astype(BF16)
    nblk = D // RG_BLOCK_W
    r = jnp.concatenate([jnp.dot(xb[:, n * RG_BLOCK_W:(n + 1) * RG_BLOCK_W], wa_ref[n],
                                 preferred_element_type=F32) for n in range(nblk)], axis=1)
    ig = jnp.concatenate([jnp.dot(xb[:, n * RG_BLOCK_W:(n + 1) * RG_BLOCK_W], wi_ref[n],
                                  preferred_element_type=F32) for n in range(nblk)], axis=1)
    t_r = jnp.tanh(r + 0.5 * ba)
    t_i = jnp.tanh(ig + 0.5 * bi)
    nl = -lam
    softplus = jnp.maximum(nl, 0.0) + jnp.log1p(jnp.exp(-jnp.abs(nl)))
    half_c = (-0.5 * LRU_C) * softplus
    log_a = half_c + half_c * t_r
    a = jnp.exp(log_a)
    xh = 0.5 * xc
    b = jnp.sqrt(1.0 - a * a) * (xh + xh * t_i)
    return a, b


def _rg_scan_body(reverse, epilogue, emit_h, tc, *refs):
    xc_ref, wa_ref, wi_ref, ba_ref, bi_ref, lam_ref, h0_ref = refs[:7]
    refs = refs[7:]
    if epilogue:
        hf_ref, gg_ref, wo_ref, x_ref, mod_ref = refs[:5]
        refs = refs[5:]
    if emit_h:
        h_ref = refs[0]
        refs = refs[1:]
    if epilogue:
        o_ref = refs[0]
        refs = refs[1:]
    a_scr, b_scr, h_scr, carry_scr = refs
    c = pl.program_id(0)

    @pl.when(c == 0)
    def _():
        carry_scr[...] = jnp.broadcast_to(h0_ref[...], (SUBLANES, D))

    a, b = _rg_gates(xc_ref[...], wa_ref, wi_ref, ba_ref[...], bi_ref[...], lam_ref[...])
    a_scr[...] = a
    b_scr[...] = b
    nblk = tc // SUBLANES
    row = lax.broadcasted_iota(jnp.int32, (SUBLANES, D), 0)
    first = (row == SUBLANES - 1) if reverse else (row == 0)

    def block(n, carry):
        blk = (nblk - 1 - n) if reverse else n
        off = pl.multiple_of(blk * SUBLANES, SUBLANES)
        av = a_scr[pl.ds(off, SUBLANES), :]
        bv = b_scr[pl.ds(off, SUBLANES), :]
        bv = jnp.where(first, av * carry + bv, bv)
        av = jnp.where(first, 0.0, av)
        for k in (1, 2, 4):
            shift = (SUBLANES - k) if reverse else k
            bv = av * pltpu.roll(bv, shift, 0) + bv
            if k != 4:
                av = av * pltpu.roll(av, shift, 0)
        h_scr[pl.ds(off, SUBLANES), :] = bv
        edge = bv[0:1, :] if reverse else bv[SUBLANES - 1:SUBLANES, :]
        return jnp.broadcast_to(edge, (SUBLANES, D))

    carry_scr[...] = lax.fori_loop(0, nblk, block, carry_scr[...], unroll=2)

    if emit_h:
        h_ref[...] = h_scr[...].astype(h_ref.dtype)
    if epilogue:
        y = ((hf_ref[...].astype(F32) + h_scr[...]) * gg_ref[...].astype(F32)).astype(BF16)
        o_ref[...] = x_ref[...] + mod_ref[2:3, :] * jnp.dot(y, wo_ref[...], preferred_element_type=F32)


def rg_scan(xconv, wa, wi, ba, bi, lam, h0, tc, reverse, epi=None, h_dtype=F32):
    t = xconv.shape[0]
    nchunks = t // tc
    idx = (lambda c: (nchunks - 1 - c, 0)) if reverse else (lambda c: (c, 0))
    nb = D // RG_BLOCK_W
    blk = pl.BlockSpec((tc, D), idx)
    in_specs = [
        blk,
        _full((nb, RG_BLOCK_W, RG_BLOCK_W)),
        _full((nb, RG_BLOCK_W, RG_BLOCK_W)),
        _full((1, D)), _full((1, D)), _full((1, D)), _full((1, D)),
    ]
    args = [xconv, wa, wi, ba, bi, lam, h0]
    out_specs = []
    out_shape = []
    if epi is not None:
        hf, gg, w_out, x, mod = epi
        in_specs += [blk, blk, _full((D, D)), blk, _full((ADA_CHUNKS, D))]
        args += [hf, gg, w_out, x, mod]
    if h_dtype is not None:
        out_specs.append(blk)
        out_shape.append(jax.ShapeDtypeStruct((t, D), h_dtype))
    if epi is not None:
        out_specs.append(blk)
        out_shape.append(jax.ShapeDtypeStruct((t, D), F32))
    return pl.pallas_call(
        functools.partial(_rg_scan_body, reverse, epi is not None, h_dtype is not None, tc),
        grid=(nchunks,),
        in_specs=in_specs,
        out_specs=out_specs,
        out_shape=out_shape,
        scratch_shapes=[pltpu.VMEM((tc, D), F32), pltpu.VMEM((tc, D), F32), pltpu.VMEM((tc, D), F32),
                        pltpu.VMEM((SUBLANES, D), F32)],
        compiler_params=_cparams(("arbitrary",)),
        name="rg_scan_bwd" if reverse else "rg_scan_fwd",
    )(*args)


def rglru_layer(x, xc, mod, modc, g, w_in, conv_w, conv_b, wa, wi, ba, bi, lam, w_out, need_ctx):
    w_in_b = w_in.astype(BF16)
    wa_b = (0.5 * wa).astype(BF16)
    wi_b = (0.5 * wi).astype(BF16)
    w_out_b = w_out.astype(BF16)
    cb = conv_b[None]
    tcx = xc.shape[0]
    xcl, ggl = rg_in(x, mod, g, w_in_b, conv_w, cb, WIDE_TILE)
    xcc, ggc = rg_in(xc, modc, g, w_in_b, conv_w, cb, tcx)
    zeros = jnp.zeros((1, D), F32)
    p = lambda d: (wa_b[d], wi_b[d], ba[d][None], bi[d][None], lam[d][None])
    (hcf,) = rg_scan(xcc, *p(0), zeros, tcx, False)
    (hlf,) = rg_scan(xcl, *p(0), hcf[tcx - 1:tcx], WIDE_TILE, False, h_dtype=BF16)
    if need_ctx:
        hcb, xc_new = rg_scan(xcc, *p(1), zeros, tcx, True, epi=(hcf, ggc, w_out_b, xc, modc))
    else:
        (hcb,) = rg_scan(xcc, *p(1), zeros, tcx, True)
        xc_new = None
    (x_new,) = rg_scan(xcl, *p(1), hcb[0:1], WIDE_TILE, True, epi=(hlf, ggl, w_out_b, x, mod), h_dtype=None)
    return x_new, xc_new


def _qkv_body(x_ref, mod_ref, g_ref, w_ref, gm_ref, qg_ref, kg_ref, q_ref, k_ref, v_ref):
    h = _normmod(x_ref[...], g_ref[...], mod_ref[1:2, :], mod_ref[0:1, :]).astype(BF16)
    z = jnp.dot(h, w_ref[...], preferred_element_type=F32)

    def headnorm(v, gain):
        ms = jnp.dot((v * v).astype(BF16), gm_ref[...], preferred_element_type=F32)
        return (v * lax.rsqrt(ms + RMS_EPS)) * gain

    q_ref[...] = headnorm(z[:, :D], qg_ref[...]).astype(BF16)
    k_ref[...] = headnorm(z[:, D:2 * D], kg_ref[...]).astype(BF16)
    v_ref[...] = z[:, 2 * D:].astype(BF16)


def qkv_proj(x, mod, g, w_qkv, gmean, qg, kg, tm):
    t = x.shape[0]
    spec = pl.BlockSpec((tm, D), lambda i: (i, 0))
    return pl.pallas_call(
        _qkv_body,
        grid=(t // tm,),
        in_specs=[spec, _full((ADA_CHUNKS, D)), _full((1, D)), _full((D, 3 * D)), _full((D, D)),
                  _full((1, D)), _full((1, D))],
        out_specs=[spec, spec, spec],
        out_shape=[jax.ShapeDtypeStruct((t, D), BF16)] * 3,
        compiler_params=_cparams(("arbitrary",)),
        name="qkv_proj",
    )(x, mod, g, w_qkv, gmean, qg, kg)


def _attend_pair(q2, keys, vals, biases):
    m_rows = q2.shape[0]
    lane = lax.broadcasted_iota(jnp.int32, q2.shape, 1)
    zero = jnp.zeros_like(q2)
    qs = jnp.concatenate([jnp.where(lane < NA_HEAD_DIM, q2, zero), jnp.where(lane >= NA_HEAD_DIM, q2, zero)], axis=0)
    ss = []
    for kseg, bseg in zip(keys, biases):
        s = lax.dot_general(qs, kseg, (((1,), (1,)), ((), ())), preferred_element_type=F32)
        if bseg is not None:
            s = s + jnp.concatenate([bseg[0], bseg[1]], axis=0)
        ss.append(s)
    m = ss[0].max(axis=-1, keepdims=True)
    for s in ss[1:]:
        m = jnp.maximum(m, s.max(axis=-1, keepdims=True))
    den = None
    acc = None
    for s, vseg in zip(ss, vals):
        p = jnp.exp2(s - m)
        d = jnp.sum(p, axis=-1, keepdims=True)
        o = jnp.dot(p.astype(BF16), vseg, preferred_element_type=F32)
        den = d if den is None else den + d
        acc = o if acc is None else acc + o
    out = acc / den
    return jnp.where(lane < NA_HEAD_DIM, out[:m_rows], out[m_rows:])


NA_QROWS = 2
NA_UNION = NA_ROWS + NA_QROWS - 1


def _na_body(nside, var_ref, q_ref, kl_ref, vl_ref, kc_ref, vc_ref, bias_ref, *refs):
    side_in, o_ref, side_out = refs[:nside], refs[nside], refs[nside + 1:]
    npair = NA_HEADS // 2
    for pr in range(npair):
        sl = slice(pr * LANES, (pr + 1) * LANES)
        o_ref[:, sl] = _attend_pair(
            q_ref[:, sl], [kl_ref[:, sl], kc_ref[:, sl]], [vl_ref[:, sl], vc_ref[:, sl]],
            [(bias_ref[0, 2 * pr], bias_ref[0, 2 * pr + 1]), None]).astype(BF16)
        for src, dst in zip(side_in, side_out):
            slab = -(-src.shape[0] // (npair * BF16_ROWS)) * BF16_ROWS
            lo = min(pr * slab, src.shape[0])
            hi = min(lo + slab, src.shape[0])
            if hi > lo:
                dst[lo:hi, :] = src[lo:hi, :].astype(BF16)


def _na_geometry(rows):
    steps = rows // NA_QROWS
    g = np.arange(steps)
    base = np.clip(NA_QROWS * g - NA_ROWS // 2, 0, rows - NA_UNION)
    r = NA_QROWS * g[:, None] + np.arange(NA_QROWS)[None, :]
    rs = np.clip(r - NA_ROWS // 2, 0, rows - NA_ROWS)
    key = np.concatenate([(base - NA_QROWS * g)[:, None], rs - r], axis=1)
    uniq, first, var = np.unique(key, axis=0, return_index=True, return_inverse=True)
    return base, var.reshape(-1).astype(np.int32), g[first]


def na_attention(q, k, v, kc, vc, bias_tab, var, side=()):
    t = q.shape[0]
    rows = t // GRID_W
    nctx = kc.shape[0]
    steps = rows // NA_QROWS
    side_specs = []
    for a in side:
        assert a.shape[0] % (steps * 16) == 0
        side_specs.append(pl.BlockSpec((a.shape[0] // steps, a.shape[1]), lambda g, var: (g, 0)))

    def kbase(g):
        return jnp.clip(NA_QROWS * g - NA_ROWS // 2, 0, rows - NA_UNION)

    qrows = NA_QROWS * GRID_W
    nloc = NA_UNION * GRID_W
    kspec = pl.BlockSpec((pl.Element(nloc), pl.Element(D)), lambda g, var: (kbase(g) * GRID_W, 0))
    grid_spec = pltpu.PrefetchScalarGridSpec(
        num_scalar_prefetch=1,
        grid=(steps,),
        in_specs=[pl.BlockSpec((qrows, D), lambda g, var: (g, 0)), kspec, kspec] + [
            pl.BlockSpec((nctx, D), lambda g, var: (0, 0)), pl.BlockSpec((nctx, D), lambda g, var: (0, 0)),
            pl.BlockSpec((1, NA_HEADS, qrows, nloc), lambda g, var: (var[g], 0, 0, 0)),
        ] + side_specs,
        out_specs=[pl.BlockSpec((qrows, D), lambda g, var: (g, 0))] + side_specs,
    )
    return pl.pallas_call(
        functools.partial(_na_body, len(side)),
        grid_spec=grid_spec,
        out_shape=[jax.ShapeDtypeStruct((t, D), BF16)] + [jax.ShapeDtypeStruct(a.shape, BF16) for a in side],
        compiler_params=_cparams(("arbitrary",)),
        name="na_attention",
    )(var, q, k, v, kc, vc, bias_tab, *side)


def _ctx_attn_body(q_ref, k_ref, v_ref, o_ref):
    for pr in range(NA_HEADS // 2):
        sl = slice(pr * LANES, (pr + 1) * LANES)
        o_ref[:, sl] = _attend_pair(q_ref[:, sl], [k_ref[:, sl]], [v_ref[:, sl]], [None]).astype(BF16)


def ctx_attention(q, k, v):
    t = q.shape[0]
    return pl.pallas_call(
        _ctx_attn_body,
        grid=(1,),
        in_specs=[_full((t, D))] * 3,
        out_specs=_full((t, D)),
        out_shape=jax.ShapeDtypeStruct((t, D), BF16),
        compiler_params=_cparams(("arbitrary",)),
        name="ctx_attention",
    )(q, k, v)


def _na_bias_table(rpb, rows):
    base, var, reps = _na_geometry(rows)
    cols = np.arange(GRID_W)
    cstart = np.clip(cols - NA_COLS // 2, 0, GRID_W - NA_COLS)
    kcol = np.arange(GRID_W)
    inwin = (kcol[None, :] >= cstart[:, None]) & (kcol[None, :] < cstart[:, None] + NA_COLS)
    r = NA_QROWS * reps[:, None] + np.arange(NA_QROWS)[None, :]
    rs = np.clip(r - NA_ROWS // 2, 0, rows - NA_ROWS)
    krow = base[reps][:, None] + np.arange(NA_UNION)[None, :]
    rvalid = (krow[:, None, :] >= rs[:, :, None]) & (krow[:, None, :] < rs[:, :, None] + NA_ROWS)
    ridx = np.clip(krow[:, None, :] - r[:, :, None] + (NA_ROWS - 1), 0, 2 * NA_ROWS - 2)
    nd = 2 * NA_COLS - 1
    w = jnp.pad(rpb.astype(F32), ((0, 0), (0, 0), (GRID_W - NA_COLS, 2 * GRID_W - (GRID_W - NA_COLS) - nd)))
    flat = jnp.tile(w, (1, 1, GRID_W))[:, :, :GRID_W * (2 * GRID_W - 1)]
    blk = flat.reshape(NA_HEADS, 2 * NA_ROWS - 1, GRID_W, 2 * GRID_W - 1)[..., GRID_W - 1:]
    blk = jnp.where(jnp.asarray(inwin)[None, None], blk, NEG_BIG)
    neg = jnp.full((NA_HEADS, GRID_W, GRID_W), NEG_BIG, F32)
    variants = []
    for v in range(len(reps)):
        strips = [jnp.concatenate([blk[:, ridx[v, a, j]] if rvalid[v, a, j] else neg for j in range(NA_UNION)], axis=2)
                  for a in range(NA_QROWS)]
        variants.append(jnp.concatenate(strips, axis=1))
    return jnp.stack(variants, axis=0), jnp.asarray(var)


def na_layer(x, xc, mod, modc, g, w_qkv, q_g, k_g, rpb, w_o, need_ctx, side=()):
    w_qkv_b = w_qkv.astype(BF16)
    w_o_b = w_o.astype(BF16)
    gm = np.kron(np.eye(NA_HEADS), np.full((NA_HEAD_DIM, NA_HEAD_DIM), 1.0 / NA_HEAD_DIM))
    gmean = jnp.asarray(gm, dtype=BF16)
    qg = jnp.tile(q_g, NA_HEADS)[None] * (NA_HEAD_DIM ** -0.5 * LOG2E)
    kg = jnp.tile(k_g, NA_HEADS)[None]
    q, k, v = qkv_proj(x, mod, g, w_qkv_b, gmean, qg, kg, WIDE_TILE)
    qc, kc, vc = qkv_proj(xc, modc, g, w_qkv_b, gmean, qg, kg, xc.shape[0])
    bias_tab, var = _na_bias_table(rpb * LOG2E, x.shape[0] // GRID_W)
    o, *side_b = na_attention(q, k, v, kc, vc, bias_tab, var, side)
    x_new = proj_residual(o, w_o_b, x, mod, 2, LIGHT_TILE)
    xc_new = None
    if need_ctx:
        oc = ctx_attention(qc, kc, vc)
        xc_new = proj_residual(oc, w_o_b, xc, modc, 2, xc.shape[0])
    return x_new, xc_new, side_b


def _dft_mats(n):
    ang = 2.0 * np.pi * np.outer(np.arange(n), np.arange(n)) / n
    return np.cos(ang), np.sin(ang)


def _channel_dft(h, wc):
    us = [jnp.dot(h[:, gi * FT_GROUP_W:(gi + 1) * FT_GROUP_W], wc, preferred_element_type=F32).astype(BF16)
          for gi in range(D // FT_GROUP_W)]
    return jnp.concatenate([u[:, :FT_GROUP_W] for u in us] + [u[:, FT_GROUP_W:] for u in us], axis=1)


def _ft_a_body(n, nj, x_ref, mod_ref, g_ref, perm_ref, wc_ref, ma_ref, tc_ref, ts_ref, yr_ref, yi_ref):
    h3 = _normmod(x_ref[...], g_ref[...], mod_ref[1:2, :], mod_ref[0:1, :])
    h = jnp.dot(perm_ref[...], h3.reshape(n * nj, D).astype(BF16), preferred_element_type=F32).astype(BF16)
    u = _channel_dft(h, wc_ref[...])
    for j in range(nj):
        uj = u[j * n:(j + 1) * n]
        y = jnp.dot(ma_ref[...], jnp.concatenate([uj[:, :D], uj[:, D:]], axis=0), preferred_element_type=F32)
        yr, yi = y[:n], y[n:]
        tc = _lane_tile(tc_ref[0, :, j * LANES:(j + 1) * LANES])
        ts = _lane_tile(ts_ref[0, :, j * LANES:(j + 1) * LANES])
        yr_ref[:, j, :] = yr * tc + yi * ts
        yi_ref[:, j, :] = yi * tc - yr * ts


def _ft_c_body(n, nj, yr_ref, yi_ref, mc_ref, wf_ref, x_ref, mod_ref, o_ref):
    fs = []
    for j in range(nj):
        ys = jnp.concatenate([yr_ref[j].astype(BF16), yi_ref[j].astype(BF16)], axis=0)
        fs.append(jnp.dot(mc_ref[...], ys, preferred_element_type=F32).astype(BF16))
    z = jnp.dot(jnp.concatenate(fs, axis=0), wf_ref[...], preferred_element_type=F32)
    gate = mod_ref[2:3, :]
    for j in range(nj):
        o_ref[:, j, :] = x_ref[:, j, :] + gate * z[j * n:(j + 1) * n]


def _ft_ctx_body(x_ref, mod_ref, g_ref, wc_ref, ml_ref, wf_ref, o_ref):
    x = x_ref[...]
    h = _normmod(x, g_ref[...], mod_ref[1:2, :], mod_ref[0:1, :]).astype(BF16)
    u = _channel_dft(h, wc_ref[...])
    us = jnp.concatenate([u[:, :D], u[:, D:]], axis=0)
    f = jnp.dot(ml_ref[...], us, preferred_element_type=F32).astype(BF16)
    o_ref[...] = x + mod_ref[2:3, :] * jnp.dot(f, wf_ref[...], preferred_element_type=F32)


def fourier_layer(x, xc, mod, modc, g, w_f, need_ctx):
    t = x.shape[0]
    n = math.isqrt(t)
    assert n * n == t and n % 16 == 0
    w_f_b = w_f.astype(BF16)
    cw, sw = _dft_mats(FT_GROUP_W)
    wc = jnp.asarray(np.concatenate([cw, -sw], axis=1) / math.sqrt(FT_GROUP_W), dtype=F32).astype(BF16)
    wcspec = _full((FT_GROUP_W, 2 * FT_GROUP_W))
    cn, sn = _dft_mats(n)
    ma = jnp.asarray(np.block([[cn, sn], [-sn, cn]]) / math.sqrt(n), dtype=F32).astype(BF16)
    mc = jnp.asarray(np.concatenate([cn, sn], axis=1) / math.sqrt(n), dtype=F32).astype(BF16)
    nj = 8
    ang = 2.0 * np.pi * np.outer(np.arange(n), np.arange(n)) / t
    def expand(tab):
        a = jnp.asarray(tab, dtype=F32).reshape(n // nj, nj, n).transpose(0, 2, 1)
        return jnp.repeat(a, LANES, axis=2)
    twc, tws = expand(np.cos(ang)), expand(np.sin(ang))
    xblk = pl.BlockSpec((n, nj, D), lambda b: (0, b, 0))
    yblk = pl.BlockSpec((nj, n, D), lambda b: (b, 0, 0))
    tblk = pl.BlockSpec((1, n, nj * LANES), lambda b: (b, 0, 0))
    x3 = x.reshape(n, n, D)
    src = (np.arange(n)[None, :] * nj + np.arange(nj)[:, None]).reshape(-1)
    perm = jnp.asarray(np.eye(n * nj)[src], dtype=BF16)
    yr, yi = pl.pallas_call(
        functools.partial(_ft_a_body, n, nj),
        grid=(n // nj,),
        in_specs=[xblk, _full((ADA_CHUNKS, D)), _full((1, D)), _full((n * nj, n * nj)), wcspec,
                  _full((2 * n, 2 * n)), tblk, tblk],
        out_specs=[xblk, xblk],
        out_shape=[jax.ShapeDtypeStruct((n, n, D), F32)] * 2,
        compiler_params=_cparams(("arbitrary",)),
        name="ft_stage_a",
    )(x3, mod, g, perm, wc, ma, twc, tws)
    x_new = pl.pallas_call(
        functools.partial(_ft_c_body, n, nj),
        grid=(n // nj,),
        in_specs=[yblk, yblk, _full((n, 2 * n)), _full((D, D)), xblk, _full((ADA_CHUNKS, D))],
        out_specs=xblk,
        out_shape=jax.ShapeDtypeStruct((n, n, D), F32),
        compiler_params=_cparams(("arbitrary",)),
        name="ft_stage_c",
    )(yr, yi, mc, w_f_b, x3, mod).reshape(t, D)
    xc_new = None
    if need_ctx:
        lc = xc.shape[0]
        cl, sl = _dft_mats(lc)
        ml = jnp.asarray(np.concatenate([cl, sl], axis=1) / math.sqrt(lc), dtype=F32).astype(BF16)
        xc_new = pl.pallas_call(
            _ft_ctx_body,
            grid=(1,),
            in_specs=[_full((lc, D)), _full((ADA_CHUNKS, D)), _full((1, D)), wcspec,
                      _full((lc, 2 * lc)), _full((D, D))],
            out_specs=_full((lc, D)),
            out_shape=jax.ShapeDtypeStruct((lc, D), F32),
            compiler_params=_cparams(("arbitrary",)),
            name="ft_ctx",
        )(xc, modc, g, wc, ml, w_f_b)
    return x_new, xc_new


def _router_body(x_ref, mod_ref, g_ref, r_ref, info_ref, w0_ref, w1_ref):
    h = _normmod(x_ref[...], g_ref[...], mod_ref[4:5, :], mod_ref[3:4, :])
    hh = h.astype(BF16)
    hl = (h - hh.astype(F32)).astype(BF16)
    r = r_ref[...]
    rh = r.astype(BF16)
    rl = (r - rh.astype(F32)).astype(BF16)
    logits = (jnp.dot(hh, rh, preferred_element_type=F32) + jnp.dot(hh, rl, preferred_element_type=F32)
              + jnp.dot(hl, rh, preferred_element_type=F32))
    lane = lax.broadcasted_iota(jnp.int32, logits.shape, 1)
    logits = jnp.where(lane < N_EXPERTS, logits, NEG_BIG)
    v0 = jnp.max(logits, axis=-1, keepdims=True)
    i0 = jnp.min(jnp.where(logits == v0, lane, LANES), axis=-1, keepdims=True)
    rest = jnp.where(lane == i0, NEG_BIG, logits)
    v1 = jnp.max(rest, axis=-1, keepdims=True)
    i1 = jnp.min(jnp.where(rest == v1, lane, LANES), axis=-1, keepdims=True)
    e = jnp.exp(v1 - v0)
    w0 = 1.0 / (1.0 + e)
    w1 = e / (1.0 + e)
    info_ref[...] = jnp.where(lane == 0, i0, jnp.where(lane == 1, i1, 0))
    w0_ref[...] = jnp.broadcast_to(w0, logits.shape)
    w1_ref[...] = jnp.broadcast_to(w1, logits.shape)


def moe_router(x, mod, g, router_pad, tm):
    t = x.shape[0]
    tm = min(tm, t)
    spec = pl.BlockSpec((tm, D), lambda i: (i, 0))
    lspec = pl.BlockSpec((tm, LANES), lambda i: (i, 0))
    return pl.pallas_call(
        _router_body,
        grid=(t // tm,),
        in_specs=[spec, _full((ADA_CHUNKS, D)), _full((1, D)), _full((D, LANES))],
        out_specs=[lspec, lspec, lspec],
        out_shape=[jax.ShapeDtypeStruct((t, LANES), jnp.int32),
                   jax.ShapeDtypeStruct((t, LANES), F32), jax.ShapeDtypeStruct((t, LANES), F32)],
        compiler_params=_cparams(("arbitrary",)),
        name="moe_router",
    )(x, mod, g, router_pad)


SCATTER_TOKENS = 256


def _row_scatter_body(nlat, nctx, didx_ref, g_ref, x_ref, mod_ref, *rest):
    if nctx:
        xc_ref, modc_ref, dst_ref, h_scr, zero_scr, sems = rest
    else:
        dst_ref, h_scr, zero_scr, sems = rest
    i = pl.program_id(0)
    nsteps = pl.num_programs(0)
    ts = SCATTER_TOKENS
    slot = i % 2

    def start_all(src_ref, src_is_zero_rows):
        def issue(grp, c):
            base = pl.multiple_of(grp * SUBLANES, SUBLANES)
            for r in range(SUBLANES):
                src = src_ref.at[pl.ds(r if src_is_zero_rows else base + r, 1), :]
                for half in range(2):
                    d = didx_ref[0, 0, base + r + half * ts]
                    pltpu.make_async_copy(src, dst_ref.at[pl.ds(d, 1), :], sems.at[slot]).start(priority=half)
            return c
        lax.fori_loop(0, ts // SUBLANES, issue, 0)

    def wait_all(which):
        def drain(n, c):
            pltpu.make_async_copy(zero_scr.at[pl.ds(0, 1), :], dst_ref.at[pl.ds(0, 1), :], sems.at[which]).wait()
            return c
        lax.fori_loop(0, 2 * ts, drain, 0, unroll=8)

    @pl.when(i == 0)
    def _():
        zero_scr[...] = jnp.zeros_like(zero_scr)

    def stage_and_start(src_ref, m_ref):
        h_scr[slot] = _normmod(src_ref[...], g_ref[...], m_ref[4:5, :], m_ref[3:4, :])
        start_all(h_scr.at[slot], False)

    @pl.when(i < nlat)
    def _():
        stage_and_start(x_ref, mod_ref)

    if nctx:
        @pl.when(jnp.logical_and(i >= nlat, i < nlat + nctx))
        def _():
            stage_and_start(xc_ref, modc_ref)

    @pl.when(i >= nlat + nctx)
    def _():
        start_all(zero_scr, True)

    @pl.when(i > 0)
    def _():
        wait_all(1 - slot)

    @pl.when(i == nsteps - 1)
    def _():
        wait_all(slot)


def row_scatter(x, xc, mod, modc, g, d0, d1, pad_pos):
    ts = SCATTER_TOKENS
    nlat = x.shape[0] // ts
    nctx = 0 if xc is None else 1
    assert xc is None or xc.shape[0] == ts
    ntok = nlat + nctx
    npad = pad_pos.shape[0] // (2 * ts)
    didx = jnp.concatenate([jnp.concatenate([d0.reshape(ntok, 1, ts), d1.reshape(ntok, 1, ts)], axis=2),
                            pad_pos.reshape(npad, 1, 2 * ts)], axis=0)
    in_specs = [pl.BlockSpec((1, 1, 2 * ts), lambda i: (i, 0, 0), memory_space=pltpu.SMEM),
                _full((1, D)),
                pl.BlockSpec((ts, D), lambda i: (jnp.minimum(i, nlat - 1), 0)),
                _full((ADA_CHUNKS, D))]
    args = [didx, g, x, mod]
    if nctx:
        in_specs += [_full((ts, D)), _full((ADA_CHUNKS, D))]
        args += [xc, modc]
    return pl.pallas_call(
        functools.partial(_row_scatter_body, nlat, nctx),
        grid=(ntok + npad,),
        in_specs=in_specs,
        out_specs=pl.BlockSpec(memory_space=pl.ANY),
        out_shape=jax.ShapeDtypeStruct((2 * ntok * ts + pad_pos.shape[0], D), F32),
        scratch_shapes=[pltpu.VMEM((2, ts, D), F32), pltpu.VMEM((SUBLANES, D), F32), pltpu.SemaphoreType.DMA((2,))],
        compiler_params=_cparams(("arbitrary",)),
        name="moe_row_scatter",
    )(*args)


def _moe_ffn_body(te_ref, tv_ref, xg_ref, wg_ref, wu_ref, wd_ref, o_ref):
    i = pl.program_id(0)

    @pl.when(tv_ref[i] > 0)
    def _():
        o_ref[...] = _swiglu_chunks(xg_ref[...].astype(BF16), wg_ref, wu_ref, wd_ref)

    @pl.when(tv_ref[i] == 0)
    def _():
        o_ref[...] = jnp.zeros_like(o_ref)


def moe_ffn(xg, tile_e, tile_v, w_gu, w_down, li, tm):
    p = xg.shape[0]
    grid_spec = pltpu.PrefetchScalarGridSpec(
        num_scalar_prefetch=2,
        grid=(p // tm,),
        in_specs=[
            pl.BlockSpec((tm, D), lambda i, te, tv: (i, 0)),
            pl.BlockSpec((None, None, D, D_FF), lambda i, te, tv: (li, te[i], 0, 0)),
            pl.BlockSpec((None, None, D, D_FF), lambda i, te, tv: (li, te[i], 0, 1)),
            pl.BlockSpec((None, None, D_FF, D), lambda i, te, tv: (li, te[i], 0, 0)),
        ],
        out_specs=pl.BlockSpec((tm, D), lambda i, te, tv: (i, 0)),
    )
    return pl.pallas_call(
        _moe_ffn_body,
        grid_spec=grid_spec,
        out_shape=jax.ShapeDtypeStruct((p, D), F32),
        compiler_params=pltpu.CompilerParams(dimension_semantics=("arbitrary",), vmem_limit_bytes=MOE_VMEM_LIMIT),
        name="moe_ffn",
    )(tile_e, tile_v, xg, w_gu, w_gu, w_down)


def _combine_body(tt, d0_ref, d1_ref, d0n_ref, d1n_ref, yp_ref, x_ref, mod_ref, w0_ref, w1_ref, o_ref,
                  a_scr, b_scr, sems):
    i = pl.program_id(0)
    nsteps = pl.num_programs(0)
    slot = i % 2

    def start_all(i0_ref, i1_ref, which):
        def issue(grp, c):
            base = pl.multiple_of(grp * SUBLANES, SUBLANES)
            for r in range(SUBLANES):
                n = base + r
                pltpu.make_async_copy(yp_ref.at[pl.ds(i0_ref[0, 0, n], 1), :], a_scr.at[which, pl.ds(n, 1), :],
                                      sems.at[which]).start(priority=0)
                pltpu.make_async_copy(yp_ref.at[pl.ds(i1_ref[0, 0, n], 1), :], b_scr.at[which, pl.ds(n, 1), :],
                                      sems.at[which]).start(priority=1)
            return c
        lax.fori_loop(0, tt // SUBLANES, issue, 0)

    @pl.when(i == 0)
    def _():
        start_all(d0_ref, d1_ref, 0)

    @pl.when(i + 1 < nsteps)
    def _():
        start_all(d0n_ref, d1n_ref, 1 - slot)

    def drain(n, c):
        pltpu.make_async_copy(yp_ref.at[pl.ds(0, 1), :], a_scr.at[slot, pl.ds(0, 1), :], sems.at[slot]).wait()
        pltpu.make_async_copy(yp_ref.at[pl.ds(0, 1), :], b_scr.at[slot, pl.ds(0, 1), :], sems.at[slot]).wait()
        return c
    lax.fori_loop(0, tt, drain, 0, unroll=8)
    w0 = _lane_tile(w0_ref[...])
    w1 = _lane_tile(w1_ref[...])
    o_ref[...] = x_ref[...] + mod_ref[5:6, :] * (w0 * a_scr[slot] + w1 * b_scr[slot])


def moe_combine(yp, d0, d1, x, mod, w0b, w1b, tt):
    t = x.shape[0]
    nt = t // tt
    ispec = pl.BlockSpec((1, 1, tt), lambda i: (i, 0, 0), memory_space=pltpu.SMEM)
    nspec = pl.BlockSpec((1, 1, tt), lambda i: (jnp.minimum(i + 1, nt - 1), 0, 0), memory_space=pltpu.SMEM)
    spec = pl.BlockSpec((tt, D), lambda i: (i, 0))
    lspec = pl.BlockSpec((tt, LANES), lambda i: (i, 0))
    d0r, d1r = d0.reshape(nt, 1, tt), d1.reshape(nt, 1, tt)
    return pl.pallas_call(
        functools.partial(_combine_body, tt),
        grid=(nt,),
        in_specs=[ispec, ispec, nspec, nspec, pl.BlockSpec(memory_space=pl.ANY), spec, _full((ADA_CHUNKS, D)),
                  lspec, lspec],
        out_specs=spec,
        out_shape=jax.ShapeDtypeStruct((t, D), F32),
        scratch_shapes=[pltpu.VMEM((2, tt, D), F32), pltpu.VMEM((2, tt, D), F32), pltpu.SemaphoreType.DMA((2,))],
        compiler_params=_cparams(("arbitrary",)),
        name="moe_combine",
    )(d0r, d1r, d0r, d1r, yp, x, mod, w0b, w1b)


def _route_plan(e0, e1, tm):
    t = e0.shape[0]
    n = 2 * t
    ex = jnp.arange(N_EXPERTS, dtype=jnp.int32)
    oh0 = (e0[:, None] == ex[None, :]).astype(jnp.int32)
    oh1 = (e1[:, None] == ex[None, :]).astype(jnp.int32)
    both = oh0 + oh1
    csum = jnp.cumsum(both, axis=0)
    before = csum - both
    counts = csum[-1]
    padded = ((counts + tm - 1) // tm) * tm
    pad_end = jnp.cumsum(padded)
    pad_off = pad_end - padded
    total = pad_end[-1]
    d0 = jnp.sum(oh0 * (before + pad_off[None, :]), axis=1)
    d1 = jnp.sum(oh1 * (before + oh0 + pad_off[None, :]), axis=1)
    gap = padded - counts
    tail_off = jnp.cumsum(tm - gap) - (tm - gap)
    r = jnp.arange(tm, dtype=jnp.int32)[None, :]
    pad_pos = jnp.where(r < gap[:, None], (pad_off + counts)[:, None] + r,
                        total + tail_off[:, None] + (r - gap[:, None])).reshape(-1)
    ntiles = (n + N_EXPERTS * tm) // tm
    tstart = jnp.arange(ntiles, dtype=jnp.int32) * tm
    tile_v = (tstart < total).astype(jnp.int32)
    tile_e = jnp.sum((jnp.minimum(tstart, total - 1)[:, None] >= pad_end[None, :]).astype(jnp.int32), axis=1)
    return (d0.astype(jnp.int32), d1.astype(jnp.int32), pad_pos.astype(jnp.int32), tile_e.astype(jnp.int32), tile_v)


def moe_layer(x, xc, mod, modc, g, router, w_gu_b, w_down_b, li, need_ctx, tm=ROW_TILE):
    router_pad = jnp.pad(router, ((0, 0), (0, LANES - N_EXPERTS)))
    s = x.shape[0]
    info, w0b, w1b = moe_router(x, mod, g, router_pad, LIGHT_TILE)
    if need_ctx:
        sc = xc.shape[0]
        infoc, w0c, w1c = moe_router(xc, modc, g, router_pad, sc)
        e0 = jnp.concatenate([info[:, 0], infoc[:, 0]])
        e1 = jnp.concatenate([info[:, 1], infoc[:, 1]])
    else:
        e0, e1 = info[:, 0], info[:, 1]
    d0, d1, pad_pos, tile_e, tile_v = _route_plan(e0, e1, tm)
    xg = row_scatter(x, xc if need_ctx else None, mod, modc, g, d0, d1, pad_pos)
    yp = moe_ffn(xg, tile_e, tile_v, w_gu_b, w_down_b, li, tm)
    x_new = moe_combine(yp, d0[:s], d1[:s], x, mod, w0b, w1b, ROW_TILE)
    xc_new = None
    if need_ctx:
        xc_new = moe_combine(yp, d0[s:], d1[s:], xc, modc, w0c, w1c, sc)
    return x_new, xc_new


def kernel(x, c, ctx, c_ctx, ada_w, ada_b, norm_g, rg_w_in, rg_conv_w, rg_conv_b, rg_wa, rg_ba, rg_wi, rg_bi,
           rg_lambda, rg_w_out, na_w_qkv, na_q_g, na_k_g, na_rpb, na_w_o, ft_w_out, ffn_w_gu, ffn_w_down,
           moe_router, moe_w_gu, moe_w_down):
    depth = ada_w.shape[0]
    assert x.shape[0] == 1 and x.shape[2] == D
    xs = x[0]
    xc = ctx[0]
    mods = ada_modulation(c, c_ctx, ada_w, ada_b)
    ffn_gu_b, ffn_dn_b = ffn_w_gu.astype(BF16), ffn_w_down.astype(BF16)
    moe_gu_b = moe_dn_b = None
    mix_idx = [0] * N_MIXERS
    dense_idx = 0
    moe_idx = 0
    for layer in range(depth):
        need_ctx = layer != depth - 1
        mod, modc = mods[layer, 0], mods[layer, 1]
        g0 = norm_g[layer, 0][None]
        g1 = norm_g[layer, 1][None]
        kind = layer % N_MIXERS
        j = mix_idx[kind]
        mix_idx[kind] += 1
        if kind == 0:
            xs, xcn = rglru_layer(xs, xc, mod, modc, g0, rg_w_in[j], rg_conv_w[j], rg_conv_b[j], rg_wa[j], rg_wi[j],
                                  rg_ba[j], rg_bi[j], rg_lambda[j], rg_w_out[j], need_ctx)
        elif kind == 1:
            side = ()
            if moe_gu_b is None:
                side = (moe_w_gu.reshape(-1, moe_w_gu.shape[-1]), moe_w_down.reshape(-1, moe_w_down.shape[-1]))
            xs, xcn, side_b = na_layer(xs, xc, mod, modc, g0, na_w_qkv[j], na_q_g[j], na_k_g[j], na_rpb[j],
                                       na_w_o[j], need_ctx, side)
            if side_b:
                moe_gu_b, moe_dn_b = side_b[0].reshape(moe_w_gu.shape), side_b[1].reshape(moe_w_down.shape)
        else:
            xs, xcn = fourier_layer(xs, xc, mod, modc, g0, ft_w_out[j], need_ctx)
        if need_ctx:
            xc = xcn
        if layer % 2 == 0:
            if need_ctx:
                xc = ffn_dense(xc, modc, g1, ffn_gu_b, ffn_dn_b, dense_idx, xc.shape[0])
            xs = ffn_dense(xs, mod, g1, ffn_gu_b, ffn_dn_b, dense_idx, ROW_TILE)
            dense_idx += 1
        else:
            if moe_gu_b is None:
                moe_gu_b, moe_dn_b = moe_w_gu.astype(BF16), moe_w_down.astype(BF16)
            xs, xcn = moe_layer(xs, xc, mod, modc, g1, moe_router[moe_idx], moe_gu_b, moe_dn_b, moe_idx, need_ctx)
            moe_idx += 1
            if need_ctx:
                xc = xcn
    return xs[None]
```

```python
import functools
import math

import numpy as np
import jax
import jax.numpy as jnp
from jax import lax
from jax.experimental import pallas as pl
from jax.experimental.pallas import tpu as pltpu

F32 = jnp.float32
BF16 = jnp.bfloat16

D = 1024
D_FF = 3584
N_EXPERTS = 8
GRID_W = 64
NA_HEADS = 16
NA_HEAD_DIM = 64
NA_ROWS = 8
NA_COLS = 16
FT_GROUP_W = 256
RG_BLOCK_W = 256
RMS_EPS = 1e-6
LRU_C = 8.0
N_MIXERS = 3
ADA_CHUNKS = 6

LANES = 128
SUBLANES = 8
BF16_ROWS = 16
VMEM_LIMIT = 56 * 1024 * 1024
MOE_VMEM_LIMIT = 60 * 1024 * 1024

ROW_TILE = 512
WIDE_TILE = 1024
LIGHT_TILE = 2048
NEG_BIG = -1e30
LOG2E = math.log2(math.e)


def _cparams(sem):
    return pltpu.CompilerParams(dimension_semantics=sem, vmem_limit_bytes=VMEM_LIMIT)


def _full(shape):
    nd = len(shape)
    return pl.BlockSpec(shape, lambda *_: (0,) * nd)


def _normmod(x, g, scale, shift):
    ms = jnp.mean(x * x, axis=-1, keepdims=True)
    y = x * lax.rsqrt(ms + RMS_EPS)
    return (y * g) * (1.0 + scale) + shift


def _lane_tile(v):
    return jnp.concatenate([v] * (D // LANES), axis=1)


def _sigmoid(v):
    return 1.0 / (1.0 + jnp.exp(-v))


def _gelu_tanh(v):
    c = math.sqrt(2.0 / math.pi)
    return v * (0.5 * (1.0 + jnp.tanh(c * (v + 0.044715 * (v * v * v)))))


def _ada_body(cin_ref, w_ref, b_ref, o_ref):
    v = cin_ref[...]
    s = v * _sigmoid(v)
    w = w_ref[0]
    r0 = jnp.sum(s[:, 0:1] * w, axis=0, keepdims=True)
    r1 = jnp.sum(s[:, 1:2] * w, axis=0, keepdims=True)
    o_ref[0] = jnp.concatenate([r0, r1], axis=0) + b_ref[0]


def ada_modulation(c, c_ctx, ada_w, ada_b):
    depth = ada_w.shape[0]
    n = ada_w.shape[2]
    nc = n // 4
    cin = jnp.stack([c[0], c_ctx], axis=1)
    out = pl.pallas_call(
        _ada_body,
        grid=(depth, n // nc),
        in_specs=[
            pl.BlockSpec((D, 2), lambda l, j: (0, 0)),
            pl.BlockSpec((1, D, nc), lambda l, j: (l, 0, j)),
            pl.BlockSpec((1, 1, nc), lambda l, j: (l, 0, j)),
        ],
        out_specs=pl.BlockSpec((1, 2, nc), lambda l, j: (l, 0, j)),
        out_shape=jax.ShapeDtypeStruct((depth, 2, n), F32),
        compiler_params=_cparams(("arbitrary", "arbitrary")),
        name="ada_mod",
    )(cin, ada_w, ada_b.reshape(depth, 1, n))
    return out.reshape(depth, 2, ADA_CHUNKS, D)


FFN_CHUNK = 512


def _swiglu_chunks(h, wg_ref, wu_ref, wd_ref):
    acc = None
    for c in range(D_FF // FFN_CHUNK):
        sl = slice(c * FFN_CHUNK, (c + 1) * FFN_CHUNK)
        gg = jnp.dot(h, wg_ref[:, sl], preferred_element_type=F32)
        uu = jnp.dot(h, wu_ref[:, sl], preferred_element_type=F32)
        a = ((gg * _sigmoid(gg)) * uu).astype(BF16)
        part = jnp.dot(a, wd_ref[sl, :], preferred_element_type=F32)
        acc = part if acc is None else acc + part
    return acc


def _ffn_body(x_ref, mod_ref, g_ref, wg_ref, wu_ref, wd_ref, o_ref):
    x = x_ref[...]
    h = _normmod(x, g_ref[...], mod_ref[4:5, :], mod_ref[3:4, :]).astype(BF16)
    o_ref[...] = x + mod_ref[5:6, :] * _swiglu_chunks(h, wg_ref, wu_ref, wd_ref)


def ffn_dense(x, mod, g, w_gu, w_down, li, tm):
    t = x.shape[0]
    once = pl.Buffered(1)
    return pl.pallas_call(
        _ffn_body,
        grid=(t // tm,),
        in_specs=[
            pl.BlockSpec((tm, D), lambda i: (i, 0)),
            _full((ADA_CHUNKS, D)),
            _full((1, D)),
            pl.BlockSpec((None, D, D_FF), lambda i: (li, 0, 0), pipeline_mode=once),
            pl.BlockSpec((None, D, D_FF), lambda i: (li, 0, 1), pipeline_mode=once),
            pl.BlockSpec((None, D_FF, D), lambda i: (li, 0, 0), pipeline_mode=once),
        ],
        out_specs=pl.BlockSpec((tm, D), lambda i: (i, 0)),
        out_shape=jax.ShapeDtypeStruct((t, D), F32),
        compiler_params=_cparams(("arbitrary",)),
        name="ffn_dense",
    )(x, mod, g, w_gu, w_gu, w_down)


def _proj_body(gate_row, a_ref, w_ref, x_ref, mod_ref, o_ref):
    y = jnp.dot(a_ref[...], w_ref[...], preferred_element_type=F32)
    o_ref[...] = x_ref[...] + mod_ref[gate_row:gate_row + 1, :] * y


def proj_residual(a, w, x, mod, gate_row, tm):
    t, k = a.shape
    tm = min(tm, t)
    return pl.pallas_call(
        functools.partial(_proj_body, gate_row),
        grid=(t // tm,),
        in_specs=[
            pl.BlockSpec((tm, k), lambda i: (i, 0)),
            _full((k, D)),
            pl.BlockSpec((tm, D), lambda i: (i, 0)),
            _full((ADA_CHUNKS, D)),
        ],
        out_specs=pl.BlockSpec((tm, D), lambda i: (i, 0)),
        out_shape=jax.ShapeDtypeStruct((t, D), F32),
        compiler_params=_cparams(("arbitrary",)),
        name="proj_residual",
    )(a, w, x, mod)


HALO = SUBLANES
RG_IN_PIECE = 128


def _rg_in_body(tm, xp_ref, x_ref, xn_ref, mod_ref, g_ref, w_ref, cw_ref, cb_ref, xc_ref, gg_ref):
    i = pl.program_id(0)
    last = pl.num_programs(0) - 1
    xa = jnp.concatenate([xp_ref[...], x_ref[...], xn_ref[...]], axis=0)
    npiece = tm // RG_IN_PIECE
    bounds = [0] + [2 * HALO + RG_IN_PIECE * (k + 1) for k in range(npiece - 1)] + [tm + 2 * HALO]
    zs = []
    for k in range(npiece):
        hk = _normmod(xa[bounds[k]:bounds[k + 1]], g_ref[...], mod_ref[1:2, :], mod_ref[0:1, :]).astype(BF16)
        zs.append(jnp.dot(hk, w_ref[...], preferred_element_type=F32))
    z = jnp.concatenate(zs, axis=0)
    row = lax.broadcasted_iota(jnp.int32, (tm + 2 * HALO, 1), 0)
    valid = jnp.logical_and(jnp.logical_or(row >= HALO, i > 0),
                            jnp.logical_or(row < tm + HALO, i < last))
    xz = jnp.where(valid, z[:, :D], 0.0)
    y = cb_ref[...] + cw_ref[2:3, :] * xz[HALO:HALO + tm]
    y = y + cw_ref[0:1, :] * xz[HALO - 2:HALO - 2 + tm]
    y = y + cw_ref[1:2, :] * xz[HALO - 1:HALO - 1 + tm]
    y = y + cw_ref[3:4, :] * xz[HALO + 1:HALO + 1 + tm]
    xc_ref[...] = y
    gg_ref[...] = _gelu_tanh(z[HALO:HALO + tm, D:]).astype(BF16)


def rg_in(x, mod, g, w_in, conv_w, conv_b, tm):
    t = x.shape[0]
    nb = tm // HALO
    nblk = t // HALO
    return pl.pallas_call(
        functools.partial(_rg_in_body, tm),
        grid=(t // tm,),
        in_specs=[
            pl.BlockSpec((HALO, D), lambda i: (jnp.maximum(i * nb - 1, 0), 0)),
            pl.BlockSpec((tm, D), lambda i: (i, 0)),
            pl.BlockSpec((HALO, D), lambda i: (jnp.minimum((i + 1) * nb, nblk - 1), 0)),
            _full((ADA_CHUNKS, D)),
            _full((1, D)),
            _full((D, 2 * D)),
            _full((4, D)),
            _full((1, D)),
        ],
        out_specs=[pl.BlockSpec((tm, D), lambda i: (i, 0)), pl.BlockSpec((tm, D), lambda i: (i, 0))],
        out_shape=[jax.ShapeDtypeStruct((t, D), F32), jax.ShapeDtypeStruct((t, D), BF16)],
        compiler_params=_cparams(("arbitrary",)),
        name="rg_in",
    )(x, x, x, mod, g, w_in, conv_w, conv_b)


def _rg_gates(xc, wa_ref, wi_ref, ba, bi, lam):
    xb = xc.astype(BF16)
    nblk = D // RG_BLOCK_W
    r = jnp.concatenate([jnp.dot(xb[:, n * RG_BLOCK_W:(n + 1) * RG_BLOCK_W], wa_ref[n],
                                 preferred_element_type=F32) for n in range(nblk)], axis=1)
    ig = jnp.concatenate([jnp.dot(xb[:, n * RG_BLOCK_W:(n + 1) * RG_BLOCK_W], wi_ref[n],
                                  preferred_element_type=F32) for n in range(nblk)], axis=1)
    t_r = jnp.tanh(r + 0.5 * ba)
    t_i = jnp.tanh(ig + 0.5 * bi)
    nl = -lam
    softplus = jnp.maximum(nl, 0.0) + jnp.log1p(jnp.exp(-jnp.abs(nl)))
    half_c = (-0.5 * LRU_C) * softplus
    log_a = half_c + half_c * t_r
    a = jnp.exp(log_a)
    xh = 0.5 * xc
    b = jnp.sqrt(1.0 - a * a) * (xh + xh * t_i)
    return a, b


def _rg_scan_body(reverse, epilogue, emit_h, nside, tc, *refs):
    xc_ref, wa_ref, wi_ref, ba_ref, bi_ref, lam_ref, h0_ref = refs[:7]
    refs = refs[7:]
    if epilogue:
        hf_ref, gg_ref, wo_ref, x_ref, mod_ref = refs[:5]
        refs = refs[5:]
    side_in, refs = refs[:nside], refs[nside:]
    if emit_h:
        h_ref = refs[0]
        refs = refs[1:]
    if epilogue:
        o_ref = refs[0]
        refs = refs[1:]
    side_out, refs = refs[:nside], refs[nside:]
    a_scr, b_scr, h_scr, carry_scr = refs
    c = pl.program_id(0)
    for src, dst in zip(side_in, side_out):
        dst[...] = src[...].astype(BF16)

    @pl.when(c == 0)
    def _():
        carry_scr[...] = jnp.broadcast_to(h0_ref[...], (SUBLANES, D))

    a, b = _rg_gates(xc_ref[...], wa_ref, wi_ref, ba_ref[...], bi_ref[...], lam_ref[...])
    a_scr[...] = a
    b_scr[...] = b
    nblk = tc // SUBLANES
    row = lax.broadcasted_iota(jnp.int32, (SUBLANES, D), 0)
    first = (row == SUBLANES - 1) if reverse else (row == 0)

    def block(n, carry):
        blk = (nblk - 1 - n) if reverse else n
        off = pl.multiple_of(blk * SUBLANES, SUBLANES)
        av = a_scr[pl.ds(off, SUBLANES), :]
        bv = b_scr[pl.ds(off, SUBLANES), :]
        bv = jnp.where(first, av * carry + bv, bv)
        av = jnp.where(first, 0.0, av)
        for k in (1, 2, 4):
            shift = (SUBLANES - k) if reverse else k
            bv = av * pltpu.roll(bv, shift, 0) + bv
            if k != 4:
                av = av * pltpu.roll(av, shift, 0)
        h_scr[pl.ds(off, SUBLANES), :] = bv
        edge = bv[0:1, :] if reverse else bv[SUBLANES - 1:SUBLANES, :]
        return jnp.broadcast_to(edge, (SUBLANES, D))

    carry_scr[...] = lax.fori_loop(0, nblk, block, carry_scr[...], unroll=2)

    if emit_h:
        h_ref[...] = h_scr[...].astype(h_ref.dtype)
    if epilogue:
        y = ((hf_ref[...].astype(F32) + h_scr[...]) * gg_ref[...].astype(F32)).astype(BF16)
        o_ref[...] = x_ref[...] + mod_ref[2:3, :] * jnp.dot(y, wo_ref[...], preferred_element_type=F32)


def rg_scan(xconv, wa, wi, ba, bi, lam, h0, tc, reverse, epi=None, h_dtype=F32, side=()):
    t = xconv.shape[0]
    nchunks = t // tc
    side_specs = []
    for a in side:
        assert a.shape[0] % (nchunks * BF16_ROWS) == 0
        side_specs.append(pl.BlockSpec((a.shape[0] // nchunks, a.shape[1]), lambda c: (c, 0)))
    idx = (lambda c: (nchunks - 1 - c, 0)) if reverse else (lambda c: (c, 0))
    nb = D // RG_BLOCK_W
    blk = pl.BlockSpec((tc, D), idx)
    in_specs = [
        blk,
        _full((nb, RG_BLOCK_W, RG_BLOCK_W)),
        _full((nb, RG_BLOCK_W, RG_BLOCK_W)),
        _full((1, D)), _full((1, D)), _full((1, D)), _full((1, D)),
    ]
    args = [xconv, wa, wi, ba, bi, lam, h0]
    out_specs = []
    out_shape = []
    if epi is not None:
        hf, gg, w_out, x, mod = epi
        in_specs += [blk, blk, _full((D, D)), blk, _full((ADA_CHUNKS, D))]
        args += [hf, gg, w_out, x, mod]
    in_specs += side_specs
    args += list(side)
    if h_dtype is not None:
        out_specs.append(blk)
        out_shape.append(jax.ShapeDtypeStruct((t, D), h_dtype))
    if epi is not None:
        out_specs.append(blk)
        out_shape.append(jax.ShapeDtypeStruct((t, D), F32))
    out_specs += side_specs
    out_shape += [jax.ShapeDtypeStruct(a.shape, BF16) for a in side]
    return pl.pallas_call(
        functools.partial(_rg_scan_body, reverse, epi is not None, h_dtype is not None, len(side), tc),
        grid=(nchunks,),
        in_specs=in_specs,
        out_specs=out_specs,
        out_shape=out_shape,
        scratch_shapes=[pltpu.VMEM((tc, D), F32), pltpu.VMEM((tc, D), F32), pltpu.VMEM((tc, D), F32),
                        pltpu.VMEM((SUBLANES, D), F32)],
        compiler_params=_cparams(("arbitrary",)),
        name="rg_scan_bwd" if reverse else "rg_scan_fwd",
    )(*args)


def rglru_layer(x, xc, mod, modc, g, w_in, conv_w, conv_b, wa, wi, ba, bi, lam, w_out, need_ctx, side=()):
    w_in_b = w_in.astype(BF16)
    wa_b = (0.5 * wa).astype(BF16)
    wi_b = (0.5 * wi).astype(BF16)
    w_out_b = w_out.astype(BF16)
    cb = conv_b[None]
    tcx = xc.shape[0]
    xcl, ggl = rg_in(x, mod, g, w_in_b, conv_w, cb, WIDE_TILE)
    xcc, ggc = rg_in(xc, modc, g, w_in_b, conv_w, cb, tcx)
    zeros = jnp.zeros((1, D), F32)
    p = lambda d: (wa_b[d], wi_b[d], ba[d][None], bi[d][None], lam[d][None])
    (hcf,) = rg_scan(xcc, *p(0), zeros, tcx, False)
    hlf, *side_b = rg_scan(xcl, *p(0), hcf[tcx - 1:tcx], WIDE_TILE, False, h_dtype=BF16, side=side)
    if need_ctx:
        hcb, xc_new = rg_scan(xcc, *p(1), zeros, tcx, True, epi=(hcf, ggc, w_out_b, xc, modc))
    else:
        (hcb,) = rg_scan(xcc, *p(1), zeros, tcx, True)
        xc_new = None
    (x_new,) = rg_scan(xcl, *p(1), hcb[0:1], WIDE_TILE, True, epi=(hlf, ggl, w_out_b, x, mod), h_dtype=None)
    return x_new, xc_new, side_b


def _qkv_body(x_ref, mod_ref, g_ref, w_ref, gm_ref, qg_ref, kg_ref, q_ref, k_ref, v_ref):
    h = _normmod(x_ref[...], g_ref[...], mod_ref[1:2, :], mod_ref[0:1, :]).astype(BF16)
    z = jnp.dot(h, w_ref[...], preferred_element_type=F32)

    def headnorm(v, gain):
        ms = jnp.dot((v * v).astype(BF16), gm_ref[...], preferred_element_type=F32)
        return (v * lax.rsqrt(ms + RMS_EPS)) * gain

    q_ref[...] = headnorm(z[:, :D], qg_ref[...]).astype(BF16)
    k_ref[...] = headnorm(z[:, D:2 * D], kg_ref[...]).astype(BF16)
    v_ref[...] = z[:, 2 * D:].astype(BF16)


def qkv_proj(x, mod, g, w_qkv, gmean, qg, kg, tm):
    t = x.shape[0]
    spec = pl.BlockSpec((tm, D), lambda i: (i, 0))
    return pl.pallas_call(
        _qkv_body,
        grid=(t // tm,),
        in_specs=[spec, _full((ADA_CHUNKS, D)), _full((1, D)), _full((D, 3 * D)), _full((D, D)),
                  _full((1, D)), _full((1, D))],
        out_specs=[spec, spec, spec],
        out_shape=[jax.ShapeDtypeStruct((t, D), BF16)] * 3,
        compiler_params=_cparams(("arbitrary",)),
        name="qkv_proj",
    )(x, mod, g, w_qkv, gmean, qg, kg)


def _attend_pair(q2, keys, vals, biases):
    m_rows = q2.shape[0]
    lane = lax.broadcasted_iota(jnp.int32, q2.shape, 1)
    zero = jnp.zeros_like(q2)
    qs = jnp.concatenate([jnp.where(lane < NA_HEAD_DIM, q2, zero), jnp.where(lane >= NA_HEAD_DIM, q2, zero)], axis=0)
    ss = []
    for kseg, bseg in zip(keys, biases):
        s = lax.dot_general(qs, kseg, (((1,), (1,)), ((), ())), preferred_element_type=F32)
        if bseg is not None:
            s = s + jnp.concatenate([bseg[0], bseg[1]], axis=0)
        ss.append(s)
    m = ss[0].max(axis=-1, keepdims=True)
    for s in ss[1:]:
        m = jnp.maximum(m, s.max(axis=-1, keepdims=True))
    acc = None
    for s, vseg in zip(ss, vals):
        p = jnp.exp2(s - m)
        vaug = jnp.concatenate([vseg, jnp.ones_like(vseg)], axis=1)
        o = jnp.dot(p.astype(BF16), vaug, preferred_element_type=F32)
        acc = o if acc is None else acc + o
    out = acc[:, :LANES] / acc[:, LANES:]
    return jnp.where(lane < NA_HEAD_DIM, out[:m_rows], out[m_rows:])


NA_QROWS = 2
NA_UNION = NA_ROWS + NA_QROWS - 1


def _na_body(nside, var_ref, q_ref, kl_ref, vl_ref, kc_ref, vc_ref, bias_ref, *refs):
    side_in, o_ref, side_out = refs[:nside], refs[nside], refs[nside + 1:]
    npair = NA_HEADS // 2
    for pr in range(npair):
        sl = slice(pr * LANES, (pr + 1) * LANES)
        o_ref[:, sl] = _attend_pair(
            q_ref[:, sl], [kl_ref[:, sl], kc_ref[:, sl]], [vl_ref[:, sl], vc_ref[:, sl]],
            [(bias_ref[0, 2 * pr], bias_ref[0, 2 * pr + 1]), None]).astype(BF16)
        for src, dst in zip(side_in, side_out):
            slab = -(-src.shape[0] // (npair * BF16_ROWS)) * BF16_ROWS
            lo = min(pr * slab, src.shape[0])
            hi = min(lo + slab, src.shape[0])
            if hi > lo:
                dst[lo:hi, :] = src[lo:hi, :].astype(BF16)


def _na_geometry(rows):
    steps = rows // NA_QROWS
    g = np.arange(steps)
    base = np.clip(NA_QROWS * g - NA_ROWS // 2, 0, rows - NA_UNION)
    r = NA_QROWS * g[:, None] + np.arange(NA_QROWS)[None, :]
    rs = np.clip(r - NA_ROWS // 2, 0, rows - NA_ROWS)
    key = np.concatenate([(base - NA_QROWS * g)[:, None], rs - r], axis=1)
    uniq, first, var = np.unique(key, axis=0, return_index=True, return_inverse=True)
    return base, var.reshape(-1).astype(np.int32), g[first]


def na_attention(q, k, v, kc, vc, bias_tab, var, side=()):
    t = q.shape[0]
    rows = t // GRID_W
    nctx = kc.shape[0]
    steps = rows // NA_QROWS
    side_specs = []
    for a in side:
        assert a.shape[0] % (steps * 16) == 0
        side_specs.append(pl.BlockSpec((a.shape[0] // steps, a.shape[1]), lambda g, var: (g, 0)))

    def kbase(g):
        return jnp.clip(NA_QROWS * g - NA_ROWS // 2, 0, rows - NA_UNION)

    qrows = NA_QROWS * GRID_W
    nloc = NA_UNION * GRID_W
    kspec = pl.BlockSpec((pl.Element(nloc), pl.Element(D)), lambda g, var: (kbase(g) * GRID_W, 0))
    grid_spec = pltpu.PrefetchScalarGridSpec(
        num_scalar_prefetch=1,
        grid=(steps,),
        in_specs=[pl.BlockSpec((qrows, D), lambda g, var: (g, 0)), kspec, kspec] + [
            pl.BlockSpec((nctx, D), lambda g, var: (0, 0)), pl.BlockSpec((nctx, D), lambda g, var: (0, 0)),
            pl.BlockSpec((1, NA_HEADS, qrows, nloc), lambda g, var: (var[g], 0, 0, 0)),
        ] + side_specs,
        out_specs=[pl.BlockSpec((qrows, D), lambda g, var: (g, 0))] + side_specs,
    )
    return pl.pallas_call(
        functools.partial(_na_body, len(side)),
        grid_spec=grid_spec,
        out_shape=[jax.ShapeDtypeStruct((t, D), BF16)] + [jax.ShapeDtypeStruct(a.shape, BF16) for a in side],
        compiler_params=_cparams(("arbitrary",)),
        name="na_attention",
    )(var, q, k, v, kc, vc, bias_tab, *side)


def _ctx_attn_body(q_ref, k_ref, v_ref, o_ref):
    for pr in range(NA_HEADS // 2):
        sl = slice(pr * LANES, (pr + 1) * LANES)
        o_ref[:, sl] = _attend_pair(q_ref[:, sl], [k_ref[:, sl]], [v_ref[:, sl]], [None]).astype(BF16)


def ctx_attention(q, k, v):
    t = q.shape[0]
    return pl.pallas_call(
        _ctx_attn_body,
        grid=(1,),
        in_specs=[_full((t, D))] * 3,
        out_specs=_full((t, D)),
        out_shape=jax.ShapeDtypeStruct((t, D), BF16),
        compiler_params=_cparams(("arbitrary",)),
        name="ctx_attention",
    )(q, k, v)


def _na_bias_table(rpb, rows):
    base, var, reps = _na_geometry(rows)
    cols = np.arange(GRID_W)
    cstart = np.clip(cols - NA_COLS // 2, 0, GRID_W - NA_COLS)
    kcol = np.arange(GRID_W)
    inwin = (kcol[None, :] >= cstart[:, None]) & (kcol[None, :] < cstart[:, None] + NA_COLS)
    r = NA_QROWS * reps[:, None] + np.arange(NA_QROWS)[None, :]
    rs = np.clip(r - NA_ROWS // 2, 0, rows - NA_ROWS)
    krow = base[reps][:, None] + np.arange(NA_UNION)[None, :]
    rvalid = (krow[:, None, :] >= rs[:, :, None]) & (krow[:, None, :] < rs[:, :, None] + NA_ROWS)
    ridx = np.clip(krow[:, None, :] - r[:, :, None] + (NA_ROWS - 1), 0, 2 * NA_ROWS - 2)
    nd = 2 * NA_COLS - 1
    w = jnp.pad(rpb.astype(F32), ((0, 0), (0, 0), (GRID_W - NA_COLS, 2 * GRID_W - (GRID_W - NA_COLS) - nd)))
    flat = jnp.tile(w, (1, 1, GRID_W))[:, :, :GRID_W * (2 * GRID_W - 1)]
    blk = flat.reshape(NA_HEADS, 2 * NA_ROWS - 1, GRID_W, 2 * GRID_W - 1)[..., GRID_W - 1:]
    blk = jnp.where(jnp.asarray(inwin)[None, None], blk, NEG_BIG)
    neg = jnp.full((NA_HEADS, GRID_W, GRID_W), NEG_BIG, F32)
    variants = []
    for v in range(len(reps)):
        strips = [jnp.concatenate([blk[:, ridx[v, a, j]] if rvalid[v, a, j] else neg for j in range(NA_UNION)], axis=2)
                  for a in range(NA_QROWS)]
        variants.append(jnp.concatenate(strips, axis=1))
    return jnp.stack(variants, axis=0), jnp.asarray(var)


def na_layer(x, xc, mod, modc, g, w_qkv, q_g, k_g, rpb, w_o, need_ctx, side=()):
    w_qkv_b = w_qkv.astype(BF16)
    w_o_b = w_o.astype(BF16)
    gm = np.kron(np.eye(NA_HEADS), np.full((NA_HEAD_DIM, NA_HEAD_DIM), 1.0 / NA_HEAD_DIM))
    gmean = jnp.asarray(gm, dtype=BF16)
    qg = jnp.tile(q_g, NA_HEADS)[None] * (NA_HEAD_DIM ** -0.5 * LOG2E)
    kg = jnp.tile(k_g, NA_HEADS)[None]
    q, k, v = qkv_proj(x, mod, g, w_qkv_b, gmean, qg, kg, WIDE_TILE)
    qc, kc, vc = qkv_proj(xc, modc, g, w_qkv_b, gmean, qg, kg, xc.shape[0])
    bias_tab, var = _na_bias_table(rpb * LOG2E, x.shape[0] // GRID_W)
    o, *side_b = na_attention(q, k, v, kc, vc, bias_tab, var, side)
    x_new = proj_residual(o, w_o_b, x, mod, 2, LIGHT_TILE)
    xc_new = None
    if need_ctx:
        oc = ctx_attention(qc, kc, vc)
        xc_new = proj_residual(oc, w_o_b, xc, modc, 2, xc.shape[0])
    return x_new, xc_new, side_b


def _dft_mats(n):
    ang = 2.0 * np.pi * np.outer(np.arange(n), np.arange(n)) / n
    return np.cos(ang), np.sin(ang)


def _channel_dft(h, wc):
    us = [jnp.dot(h[:, gi * FT_GROUP_W:(gi + 1) * FT_GROUP_W], wc, preferred_element_type=F32).astype(BF16)
          for gi in range(D // FT_GROUP_W)]
    return jnp.concatenate([u[:, :FT_GROUP_W] for u in us] + [u[:, FT_GROUP_W:] for u in us], axis=1)


def _ft_a_body(n, nj, x_ref, mod_ref, g_ref, perm_ref, wc_ref, ma_ref, tc_ref, ts_ref, yr_ref, yi_ref):
    h3 = _normmod(x_ref[...], g_ref[...], mod_ref[1:2, :], mod_ref[0:1, :])
    h = jnp.dot(perm_ref[...], h3.reshape(n * nj, D).astype(BF16), preferred_element_type=F32).astype(BF16)
    u = _channel_dft(h, wc_ref[...])
    for j in range(nj):
        uj = u[j * n:(j + 1) * n]
        y = jnp.dot(ma_ref[...], jnp.concatenate([uj[:, :D], uj[:, D:]], axis=0), preferred_element_type=F32)
        yr, yi = y[:n], y[n:]
        tc = _lane_tile(tc_ref[0, :, j * LANES:(j + 1) * LANES])
        ts = _lane_tile(ts_ref[0, :, j * LANES:(j + 1) * LANES])
        yr_ref[:, j, :] = yr * tc + yi * ts
        yi_ref[:, j, :] = yi * tc - yr * ts


def _ft_c_body(n, nj, yr_ref, yi_ref, mc_ref, wf_ref, x_ref, mod_ref, o_ref):
    fs = []
    for j in range(nj):
        ys = jnp.concatenate([yr_ref[j].astype(BF16), yi_ref[j].astype(BF16)], axis=0)
        fs.append(jnp.dot(mc_ref[...], ys, preferred_element_type=F32).astype(BF16))
    z = jnp.dot(jnp.concatenate(fs, axis=0), wf_ref[...], preferred_element_type=F32)
    gate = mod_ref[2:3, :]
    for j in range(nj):
        o_ref[:, j, :] = x_ref[:, j, :] + gate * z[j * n:(j + 1) * n]


def _ft_ctx_body(x_ref, mod_ref, g_ref, wc_ref, ml_ref, wf_ref, o_ref):
    x = x_ref[...]
    h = _normmod(x, g_ref[...], mod_ref[1:2, :], mod_ref[0:1, :]).astype(BF16)
    u = _channel_dft(h, wc_ref[...])
    us = jnp.concatenate([u[:, :D], u[:, D:]], axis=0)
    f = jnp.dot(ml_ref[...], us, preferred_element_type=F32).astype(BF16)
    o_ref[...] = x + mod_ref[2:3, :] * jnp.dot(f, wf_ref[...], preferred_element_type=F32)


def fourier_layer(x, xc, mod, modc, g, w_f, need_ctx):
    t = x.shape[0]
    n = math.isqrt(t)
    assert n * n == t and n % 16 == 0
    w_f_b = w_f.astype(BF16)
    cw, sw = _dft_mats(FT_GROUP_W)
    wc = jnp.asarray(np.concatenate([cw, -sw], axis=1) / math.sqrt(FT_GROUP_W), dtype=F32).astype(BF16)
    wcspec = _full((FT_GROUP_W, 2 * FT_GROUP_W))
    cn, sn = _dft_mats(n)
    ma = jnp.asarray(np.block([[cn, sn], [-sn, cn]]) / math.sqrt(n), dtype=F32).astype(BF16)
    mc = jnp.asarray(np.concatenate([cn, sn], axis=1) / math.sqrt(n), dtype=F32).astype(BF16)
    nj = 8
    ang = 2.0 * np.pi * np.outer(np.arange(n), np.arange(n)) / t
    def expand(tab):
        a = jnp.asarray(tab, dtype=F32).reshape(n // nj, nj, n).transpose(0, 2, 1)
        return jnp.repeat(a, LANES, axis=2)
    twc, tws = expand(np.cos(ang)), expand(np.sin(ang))
    xblk = pl.BlockSpec((n, nj, D), lambda b: (0, b, 0))
    yblk = pl.BlockSpec((nj, n, D), lambda b: (b, 0, 0))
    tblk = pl.BlockSpec((1, n, nj * LANES), lambda b: (b, 0, 0))
    x3 = x.reshape(n, n, D)
    src = (np.arange(n)[None, :] * nj + np.arange(nj)[:, None]).reshape(-1)
    perm = jnp.asarray(np.eye(n * nj)[src], dtype=BF16)
    yr, yi = pl.pallas_call(
        functools.partial(_ft_a_body, n, nj),
        grid=(n // nj,),
        in_specs=[xblk, _full((ADA_CHUNKS, D)), _full((1, D)), _full((n * nj, n * nj)), wcspec,
                  _full((2 * n, 2 * n)), tblk, tblk],
        out_specs=[xblk, xblk],
        out_shape=[jax.ShapeDtypeStruct((n, n, D), F32)] * 2,
        compiler_params=_cparams(("arbitrary",)),
        name="ft_stage_a",
    )(x3, mod, g, perm, wc, ma, twc, tws)
    x_new = pl.pallas_call(
        functools.partial(_ft_c_body, n, nj),
        grid=(n // nj,),
        in_specs=[yblk, yblk, _full((n, 2 * n)), _full((D, D)), xblk, _full((ADA_CHUNKS, D))],
        out_specs=xblk,
        out_shape=jax.ShapeDtypeStruct((n, n, D), F32),
        compiler_params=_cparams(("arbitrary",)),
        name="ft_stage_c",
    )(yr, yi, mc, w_f_b, x3, mod).reshape(t, D)
    xc_new = None
    if need_ctx:
        lc = xc.shape[0]
        cl, sl = _dft_mats(lc)
        ml = jnp.asarray(np.concatenate([cl, sl], axis=1) / math.sqrt(lc), dtype=F32).astype(BF16)
        xc_new = pl.pallas_call(
            _ft_ctx_body,
            grid=(1,),
            in_specs=[_full((lc, D)), _full((ADA_CHUNKS, D)), _full((1, D)), wcspec,
                      _full((lc, 2 * lc)), _full((D, D))],
            out_specs=_full((lc, D)),
            out_shape=jax.ShapeDtypeStruct((lc, D), F32),
            compiler_params=_cparams(("arbitrary",)),
            name="ft_ctx",
        )(xc, modc, g, wc, ml, w_f_b)
    return x_new, xc_new


def _router_body(x_ref, mod_ref, g_ref, r_ref, info_ref, w0_ref, w1_ref):
    h = _normmod(x_ref[...], g_ref[...], mod_ref[4:5, :], mod_ref[3:4, :])
    hh = h.astype(BF16)
    hl = (h - hh.astype(F32)).astype(BF16)
    r = r_ref[...]
    rh = r.astype(BF16)
    rl = (r - rh.astype(F32)).astype(BF16)
    logits = (jnp.dot(hh, rh, preferred_element_type=F32) + jnp.dot(hh, rl, preferred_element_type=F32)
              + jnp.dot(hl, rh, preferred_element_type=F32))
    lane = lax.broadcasted_iota(jnp.int32, logits.shape, 1)
    logits = jnp.where(lane < N_EXPERTS, logits, NEG_BIG)
    v0 = jnp.max(logits, axis=-1, keepdims=True)
    i0 = jnp.min(jnp.where(logits == v0, lane, LANES), axis=-1, keepdims=True)
    rest = jnp.where(lane == i0, NEG_BIG, logits)
    v1 = jnp.max(rest, axis=-1, keepdims=True)
    i1 = jnp.min(jnp.where(rest == v1, lane, LANES), axis=-1, keepdims=True)
    e = jnp.exp(v1 - v0)
    w0 = 1.0 / (1.0 + e)
    w1 = e / (1.0 + e)
    info_ref[...] = jnp.where(lane == 0, i0, jnp.where(lane == 1, i1, 0))
    w0_ref[...] = jnp.broadcast_to(w0, logits.shape)
    w1_ref[...] = jnp.broadcast_to(w1, logits.shape)


def moe_router(x, mod, g, router_pad, tm):
    t = x.shape[0]
    tm = min(tm, t)
    spec = pl.BlockSpec((tm, D), lambda i: (i, 0))
    lspec = pl.BlockSpec((tm, LANES), lambda i: (i, 0))
    return pl.pallas_call(
        _router_body,
        grid=(t // tm,),
        in_specs=[spec, _full((ADA_CHUNKS, D)), _full((1, D)), _full((D, LANES))],
        out_specs=[lspec, lspec, lspec],
        out_shape=[jax.ShapeDtypeStruct((t, LANES), jnp.int32),
                   jax.ShapeDtypeStruct((t, LANES), F32), jax.ShapeDtypeStruct((t, LANES), F32)],
        compiler_params=_cparams(("arbitrary",)),
        name="moe_router",
    )(x, mod, g, router_pad)


SCATTER_TOKENS = 256


def _row_scatter_body(nlat, nctx, didx_ref, g_ref, x_ref, mod_ref, *rest):
    if nctx:
        xc_ref, modc_ref, dst_ref, h_scr, zero_scr, sems = rest
    else:
        dst_ref, h_scr, zero_scr, sems = rest
    i = pl.program_id(0)
    nsteps = pl.num_programs(0)
    ts = SCATTER_TOKENS
    slot = i % 2

    def start_all(src_ref, src_is_zero_rows):
        def issue(grp, c):
            base = pl.multiple_of(grp * SUBLANES, SUBLANES)
            for r in range(SUBLANES):
                src = src_ref.at[pl.ds(r if src_is_zero_rows else base + r, 1), :]
                for half in range(2):
                    d = didx_ref[0, 0, base + r + half * ts]
                    pltpu.make_async_copy(src, dst_ref.at[pl.ds(d, 1), :], sems.at[slot]).start(priority=half)
            return c
        lax.fori_loop(0, ts // SUBLANES, issue, 0)

    def wait_all(which):
        def drain(n, c):
            pltpu.make_async_copy(zero_scr.at[pl.ds(0, 1), :], dst_ref.at[pl.ds(0, 1), :], sems.at[which]).wait()
            return c
        lax.fori_loop(0, 2 * ts, drain, 0, unroll=8)

    @pl.when(i == 0)
    def _():
        zero_scr[...] = jnp.zeros_like(zero_scr)

    def stage_and_start(src_ref, m_ref):
        h_scr[slot] = _normmod(src_ref[...], g_ref[...], m_ref[4:5, :], m_ref[3:4, :])
        start_all(h_scr.at[slot], False)

    @pl.when(i < nlat)
    def _():
        stage_and_start(x_ref, mod_ref)

    if nctx:
        @pl.when(jnp.logical_and(i >= nlat, i < nlat + nctx))
        def _():
            stage_and_start(xc_ref, modc_ref)

    @pl.when(i >= nlat + nctx)
    def _():
        start_all(zero_scr, True)

    @pl.when(i > 0)
    def _():
        wait_all(1 - slot)

    @pl.when(i == nsteps - 1)
    def _():
        wait_all(slot)


def row_scatter(x, xc, mod, modc, g, d0, d1, pad_pos):
    ts = SCATTER_TOKENS
    nlat = x.shape[0] // ts
    nctx = 0 if xc is None else 1
    assert xc is None or xc.shape[0] == ts
    ntok = nlat + nctx
    npad = pad_pos.shape[0] // (2 * ts)
    didx = jnp.concatenate([jnp.concatenate([d0.reshape(ntok, 1, ts), d1.reshape(ntok, 1, ts)], axis=2),
                            pad_pos.reshape(npad, 1, 2 * ts)], axis=0)
    in_specs = [pl.BlockSpec((1, 1, 2 * ts), lambda i: (i, 0, 0), memory_space=pltpu.SMEM),
                _full((1, D)),
                pl.BlockSpec((ts, D), lambda i: (jnp.minimum(i, nlat - 1), 0)),
                _full((ADA_CHUNKS, D))]
    args = [didx, g, x, mod]
    if nctx:
        in_specs += [_full((ts, D)), _full((ADA_CHUNKS, D))]
        args += [xc, modc]
    return pl.pallas_call(
        functools.partial(_row_scatter_body, nlat, nctx),
        grid=(ntok + npad,),
        in_specs=in_specs,
        out_specs=pl.BlockSpec(memory_space=pl.ANY),
        out_shape=jax.ShapeDtypeStruct((2 * ntok * ts + pad_pos.shape[0], D), F32),
        scratch_shapes=[pltpu.VMEM((2, ts, D), F32), pltpu.VMEM((SUBLANES, D), F32), pltpu.SemaphoreType.DMA((2,))],
        compiler_params=_cparams(("arbitrary",)),
        name="moe_row_scatter",
    )(*args)


def _moe_ffn_body(te_ref, tv_ref, xg_ref, wg_ref, wu_ref, wd_ref, o_ref):
    i = pl.program_id(0)

    @pl.when(tv_ref[i] > 0)
    def _():
        o_ref[...] = _swiglu_chunks(xg_ref[...].astype(BF16), wg_ref, wu_ref, wd_ref)

    @pl.when(tv_ref[i] == 0)
    def _():
        o_ref[...] = jnp.zeros_like(o_ref)


def moe_ffn(xg, tile_e, tile_v, w_gu, w_down, li, tm):
    p = xg.shape[0]
    grid_spec = pltpu.PrefetchScalarGridSpec(
        num_scalar_prefetch=2,
        grid=(p // tm,),
        in_specs=[
            pl.BlockSpec((tm, D), lambda i, te, tv: (i, 0)),
            pl.BlockSpec((None, None, D, D_FF), lambda i, te, tv: (li, te[i], 0, 0)),
            pl.BlockSpec((None, None, D, D_FF), lambda i, te, tv: (li, te[i], 0, 1)),
            pl.BlockSpec((None, None, D_FF, D), lambda i, te, tv: (li, te[i], 0, 0)),
        ],
        out_specs=pl.BlockSpec((tm, D), lambda i, te, tv: (i, 0)),
    )
    return pl.pallas_call(
        _moe_ffn_body,
        grid_spec=grid_spec,
        out_shape=jax.ShapeDtypeStruct((p, D), F32),
        compiler_params=pltpu.CompilerParams(dimension_semantics=("arbitrary",), vmem_limit_bytes=MOE_VMEM_LIMIT),
        name="moe_ffn",
    )(tile_e, tile_v, xg, w_gu, w_gu, w_down)


def _combine_body(tt, d0_ref, d1_ref, d0n_ref, d1n_ref, yp_ref, x_ref, mod_ref, w0_ref, w1_ref, o_ref,
                  a_scr, b_scr, sems):
    i = pl.program_id(0)
    nsteps = pl.num_programs(0)
    slot = i % 2

    def start_all(i0_ref, i1_ref, which):
        def issue(grp, c):
            base = pl.multiple_of(grp * SUBLANES, SUBLANES)
            for r in range(SUBLANES):
                n = base + r
                pltpu.make_async_copy(yp_ref.at[pl.ds(i0_ref[0, 0, n], 1), :], a_scr.at[which, pl.ds(n, 1), :],
                                      sems.at[which]).start(priority=0)
                pltpu.make_async_copy(yp_ref.at[pl.ds(i1_ref[0, 0, n], 1), :], b_scr.at[which, pl.ds(n, 1), :],
                                      sems.at[which]).start(priority=1)
            return c
        lax.fori_loop(0, tt // SUBLANES, issue, 0)

    @pl.when(i == 0)
    def _():
        start_all(d0_ref, d1_ref, 0)

    @pl.when(i + 1 < nsteps)
    def _():
        start_all(d0n_ref, d1n_ref, 1 - slot)

    def drain(n, c):
        pltpu.make_async_copy(yp_ref.at[pl.ds(0, 1), :], a_scr.at[slot, pl.ds(0, 1), :], sems.at[slot]).wait()
        pltpu.make_async_copy(yp_ref.at[pl.ds(0, 1), :], b_scr.at[slot, pl.ds(0, 1), :], sems.at[slot]).wait()
        return c
    lax.fori_loop(0, tt, drain, 0, unroll=8)
    w0 = _lane_tile(w0_ref[...])
    w1 = _lane_tile(w1_ref[...])
    o_ref[...] = x_ref[...] + mod_ref[5:6, :] * (w0 * a_scr[slot] + w1 * b_scr[slot])


def moe_combine(yp, d0, d1, x, mod, w0b, w1b, tt):
    t = x.shape[0]
    nt = t // tt
    ispec = pl.BlockSpec((1, 1, tt), lambda i: (i, 0, 0), memory_space=pltpu.SMEM)
    nspec = pl.BlockSpec((1, 1, tt), lambda i: (jnp.minimum(i + 1, nt - 1), 0, 0), memory_space=pltpu.SMEM)
    spec = pl.BlockSpec((tt, D), lambda i: (i, 0))
    lspec = pl.BlockSpec((tt, LANES), lambda i: (i, 0))
    d0r, d1r = d0.reshape(nt, 1, tt), d1.reshape(nt, 1, tt)
    return pl.pallas_call(
        functools.partial(_combine_body, tt),
        grid=(nt,),
        in_specs=[ispec, ispec, nspec, nspec, pl.BlockSpec(memory_space=pl.ANY), spec, _full((ADA_CHUNKS, D)),
                  lspec, lspec],
        out_specs=spec,
        out_shape=jax.ShapeDtypeStruct((t, D), F32),
        scratch_shapes=[pltpu.VMEM((2, tt, D), F32), pltpu.VMEM((2, tt, D), F32), pltpu.SemaphoreType.DMA((2,))],
        compiler_params=_cparams(("arbitrary",)),
        name="moe_combine",
    )(d0r, d1r, d0r, d1r, yp, x, mod, w0b, w1b)


def _route_plan(e0, e1, tm):
    t = e0.shape[0]
    n = 2 * t
    ex = jnp.arange(N_EXPERTS, dtype=jnp.int32)
    oh0 = (e0[:, None] == ex[None, :]).astype(jnp.int32)
    oh1 = (e1[:, None] == ex[None, :]).astype(jnp.int32)
    both = oh0 + oh1
    csum = jnp.cumsum(both, axis=0)
    before = csum - both
    counts = csum[-1]
    padded = ((counts + tm - 1) // tm) * tm
    pad_end = jnp.cumsum(padded)
    pad_off = pad_end - padded
    total = pad_end[-1]
    d0 = jnp.sum(oh0 * (before + pad_off[None, :]), axis=1)
    d1 = jnp.sum(oh1 * (before + oh0 + pad_off[None, :]), axis=1)
    gap = padded - counts
    tail_off = jnp.cumsum(tm - gap) - (tm - gap)
    r = jnp.arange(tm, dtype=jnp.int32)[None, :]
    pad_pos = jnp.where(r < gap[:, None], (pad_off + counts)[:, None] + r,
                        total + tail_off[:, None] + (r - gap[:, None])).reshape(-1)
    ntiles = (n + N_EXPERTS * tm) // tm
    tstart = jnp.arange(ntiles, dtype=jnp.int32) * tm
    tile_v = (tstart < total).astype(jnp.int32)
    tile_e = jnp.sum((jnp.minimum(tstart, total - 1)[:, None] >= pad_end[None, :]).astype(jnp.int32), axis=1)
    return (d0.astype(jnp.int32), d1.astype(jnp.int32), pad_pos.astype(jnp.int32), tile_e.astype(jnp.int32), tile_v)


def moe_layer(x, xc, mod, modc, g, router, w_gu_b, w_down_b, li, need_ctx, tm=ROW_TILE):
    router_pad = jnp.pad(router, ((0, 0), (0, LANES - N_EXPERTS)))
    s = x.shape[0]
    info, w0b, w1b = moe_router(x, mod, g, router_pad, LIGHT_TILE)
    if need_ctx:
        sc = xc.shape[0]
        infoc, w0c, w1c = moe_router(xc, modc, g, router_pad, sc)
        e0 = jnp.concatenate([info[:, 0], infoc[:, 0]])
        e1 = jnp.concatenate([info[:, 1], infoc[:, 1]])
    else:
        e0, e1 = info[:, 0], info[:, 1]
    d0, d1, pad_pos, tile_e, tile_v = _route_plan(e0, e1, tm)
    xg = row_scatter(x, xc if need_ctx else None, mod, modc, g, d0, d1, pad_pos)
    yp = moe_ffn(xg, tile_e, tile_v, w_gu_b, w_down_b, li, tm)
    x_new = moe_combine(yp, d0[:s], d1[:s], x, mod, w0b, w1b, ROW_TILE)
    xc_new = None
    if need_ctx:
        xc_new = moe_combine(yp, d0[s:], d1[s:], xc, modc, w0c, w1c, sc)
    return x_new, xc_new


def kernel(x, c, ctx, c_ctx, ada_w, ada_b, norm_g, rg_w_in, rg_conv_w, rg_conv_b, rg_wa, rg_ba, rg_wi, rg_bi,
           rg_lambda, rg_w_out, na_w_qkv, na_q_g, na_k_g, na_rpb, na_w_o, ft_w_out, ffn_w_gu, ffn_w_down,
           moe_router, moe_w_gu, moe_w_down):
    depth = ada_w.shape[0]
    assert x.shape[0] == 1 and x.shape[2] == D
    xs = x[0]
    xc = ctx[0]
    mods = ada_modulation(c, c_ctx, ada_w, ada_b)
    ffn_gu_b = ffn_dn_b = moe_gu_b = moe_dn_b = None
    mix_idx = [0] * N_MIXERS
    dense_idx = 0
    moe_idx = 0
    for layer in range(depth):
        need_ctx = layer != depth - 1
        mod, modc = mods[layer, 0], mods[layer, 1]
        g0 = norm_g[layer, 0][None]
        g1 = norm_g[layer, 1][None]
        kind = layer % N_MIXERS
        j = mix_idx[kind]
        mix_idx[kind] += 1
        if kind == 0:
            side = ()
            if ffn_gu_b is None:
                side = (ffn_w_gu.reshape(-1, ffn_w_gu.shape[-1]), ffn_w_down.reshape(-1, ffn_w_down.shape[-1]))
            xs, xcn, side_b = rglru_layer(xs, xc, mod, modc, g0, rg_w_in[j], rg_conv_w[j], rg_conv_b[j], rg_wa[j],
                                          rg_wi[j], rg_ba[j], rg_bi[j], rg_lambda[j], rg_w_out[j], need_ctx, side)
            if side_b:
                ffn_gu_b, ffn_dn_b = side_b[0].reshape(ffn_w_gu.shape), side_b[1].reshape(ffn_w_down.shape)
        elif kind == 1:
            side = ()
            if moe_gu_b is None:
                side = (moe_w_gu.reshape(-1, moe_w_gu.shape[-1]), moe_w_down.reshape(-1, moe_w_down.shape[-1]))
            xs, xcn, side_b = na_layer(xs, xc, mod, modc, g0, na_w_qkv[j], na_q_g[j], na_k_g[j], na_rpb[j],
                                       na_w_o[j], need_ctx, side)
            if side_b:
                moe_gu_b, moe_dn_b = side_b[0].reshape(moe_w_gu.shape), side_b[1].reshape(moe_w_down.shape)
        else:
            xs, xcn = fourier_layer(xs, xc, mod, modc, g0, ft_w_out[j], need_ctx)
        if need_ctx:
            xc = xcn
        if layer % 2 == 0:
            if ffn_gu_b is None:
                ffn_gu_b, ffn_dn_b = ffn_w_gu.astype(BF16), ffn_w_down.astype(BF16)
            if need_ctx:
                xc = ffn_dense(xc, modc, g1, ffn_gu_b, ffn_dn_b, dense_idx, xc.shape[0])
            xs = ffn_dense(xs, mod, g1, ffn_gu_b, ffn_dn_b, dense_idx, ROW_TILE)
            dense_idx += 1
        else:
            if moe_gu_b is None:
                moe_gu_b, moe_dn_b = moe_w_gu.astype(BF16), moe_w_down.astype(BF16)
            xs, xcn = moe_layer(xs, xc, mod, modc, g1, moe_router[moe_idx], moe_gu_b, moe_dn_b, moe_idx, need_ctx)
            moe_idx += 1
            if need_ctx:
                xc = xcn
    return xs[None]
```

```python
import functools
import math

import numpy as np
import jax
import jax.numpy as jnp
from jax import lax
from jax.experimental import pallas as pl
from jax.experimental.pallas import tpu as pltpu

F32 = jnp.float32
BF16 = jnp.bfloat16

D = 1024
D_FF = 3584
N_EXPERTS = 8
GRID_W = 64
NA_HEADS = 16
NA_HEAD_DIM = 64
NA_ROWS = 8
NA_COLS = 16
FT_GROUP_W = 256
RG_BLOCK_W = 256
RMS_EPS = 1e-6
LRU_C = 8.0
N_MIXERS = 3
ADA_CHUNKS = 6

LANES = 128
SUBLANES = 8
BF16_ROWS = 16
VMEM_LIMIT = 56 * 1024 * 1024
MOE_VMEM_LIMIT = 60 * 1024 * 1024

ROW_TILE = 512
WIDE_TILE = 1024
LIGHT_TILE = 2048
NEG_BIG = -1e30
LOG2E = math.log2(math.e)


def _cparams(sem):
    return pltpu.CompilerParams(dimension_semantics=sem, vmem_limit_bytes=VMEM_LIMIT)


def _full(shape):
    nd = len(shape)
    return pl.BlockSpec(shape, lambda *_: (0,) * nd)


def _normmod(x, g, scale, shift):
    ms = jnp.mean(x * x, axis=-1, keepdims=True)
    y = x * lax.rsqrt(ms + RMS_EPS)
    return (y * g) * (1.0 + scale) + shift


def _lane_tile(v):
    return jnp.concatenate([v] * (D // LANES), axis=1)


def _sigmoid(v):
    return 1.0 / (1.0 + jnp.exp(-v))


def _gelu_tanh(v):
    c = math.sqrt(2.0 / math.pi)
    return v * (0.5 * (1.0 + jnp.tanh(c * (v + 0.044715 * (v * v * v)))))


def _ada_body(cin_ref, w_ref, b_ref, o_ref):
    v = cin_ref[...]
    s = v * _sigmoid(v)
    w = w_ref[0]
    r0 = jnp.sum(s[:, 0:1] * w, axis=0, keepdims=True)
    r1 = jnp.sum(s[:, 1:2] * w, axis=0, keepdims=True)
    o_ref[0] = jnp.concatenate([r0, r1], axis=0) + b_ref[0]


def ada_modulation(c, c_ctx, ada_w, ada_b):
    depth = ada_w.shape[0]
    n = ada_w.shape[2]
    nc = n // 4
    cin = jnp.stack([c[0], c_ctx], axis=1)
    out = pl.pallas_call(
        _ada_body,
        grid=(depth, n // nc),
        in_specs=[
            pl.BlockSpec((D, 2), lambda l, j: (0, 0)),
            pl.BlockSpec((1, D, nc), lambda l, j: (l, 0, j)),
            pl.BlockSpec((1, 1, nc), lambda l, j: (l, 0, j)),
        ],
        out_specs=pl.BlockSpec((1, 2, nc), lambda l, j: (l, 0, j)),
        out_shape=jax.ShapeDtypeStruct((depth, 2, n), F32),
        compiler_params=_cparams(("arbitrary", "arbitrary")),
        name="ada_mod",
    )(cin, ada_w, ada_b.reshape(depth, 1, n))
    return out.reshape(depth, 2, ADA_CHUNKS, D)


FFN_CHUNK = 512


def _swiglu_chunks(h, wg_ref, wu_ref, wd_ref):
    acc = None
    for c in range(D_FF // FFN_CHUNK):
        sl = slice(c * FFN_CHUNK, (c + 1) * FFN_CHUNK)
        gg = jnp.dot(h, wg_ref[:, sl], preferred_element_type=F32)
        uu = jnp.dot(h, wu_ref[:, sl], preferred_element_type=F32)
        a = ((gg * _sigmoid(gg)) * uu).astype(BF16)
        part = jnp.dot(a, wd_ref[sl, :], preferred_element_type=F32)
        acc = part if acc is None else acc + part
    return acc


def _ffn_body(x_ref, mod_ref, g_ref, wg_ref, wu_ref, wd_ref, o_ref):
    x = x_ref[...]
    h = _normmod(x, g_ref[...], mod_ref[4:5, :], mod_ref[3:4, :]).astype(BF16)
    o_ref[...] = x + mod_ref[5:6, :] * _swiglu_chunks(h, wg_ref, wu_ref, wd_ref)


def ffn_dense(x, mod, g, w_gu, w_down, li, tm):
    t = x.shape[0]
    once = pl.Buffered(1)
    return pl.pallas_call(
        _ffn_body,
        grid=(t // tm,),
        in_specs=[
            pl.BlockSpec((tm, D), lambda i: (i, 0)),
            _full((ADA_CHUNKS, D)),
            _full((1, D)),
            pl.BlockSpec((None, D, D_FF), lambda i: (li, 0, 0), pipeline_mode=once),
            pl.BlockSpec((None, D, D_FF), lambda i: (li, 0, 1), pipeline_mode=once),
            pl.BlockSpec((None, D_FF, D), lambda i: (li, 0, 0), pipeline_mode=once),
        ],
        out_specs=pl.BlockSpec((tm, D), lambda i: (i, 0)),
        out_shape=jax.ShapeDtypeStruct((t, D), F32),
        compiler_params=_cparams(("arbitrary",)),
        name="ffn_dense",
    )(x, mod, g, w_gu, w_gu, w_down)


def _proj_body(gate_row, a_ref, w_ref, x_ref, mod_ref, o_ref):
    y = jnp.dot(a_ref[...], w_ref[...], preferred_element_type=F32)
    o_ref[...] = x_ref[...] + mod_ref[gate_row:gate_row + 1, :] * y


def proj_residual(a, w, x, mod, gate_row, tm):
    t, k = a.shape
    tm = min(tm, t)
    return pl.pallas_call(
        functools.partial(_proj_body, gate_row),
        grid=(t // tm,),
        in_specs=[
            pl.BlockSpec((tm, k), lambda i: (i, 0)),
            _full((k, D)),
            pl.BlockSpec((tm, D), lambda i: (i, 0)),
            _full((ADA_CHUNKS, D)),
        ],
        out_specs=pl.BlockSpec((tm, D), lambda i: (i, 0)),
        out_shape=jax.ShapeDtypeStruct((t, D), F32),
        compiler_params=_cparams(("arbitrary",)),
        name="proj_residual",
    )(a, w, x, mod)


HALO = SUBLANES
RG_IN_PIECE = 128


def _rg_in_body(tm, xp_ref, x_ref, xn_ref, mod_ref, g_ref, w_ref, cw_ref, cb_ref, xc_ref, gg_ref):
    i = pl.program_id(0)
    last = pl.num_programs(0) - 1
    xa = jnp.concatenate([xp_ref[...], x_ref[...], xn_ref[...]], axis=0)
    npiece = tm // RG_IN_PIECE
    bounds = [0] + [2 * HALO + RG_IN_PIECE * (k + 1) for k in range(npiece - 1)] + [tm + 2 * HALO]
    zs = []
    for k in range(npiece):
        hk = _normmod(xa[bounds[k]:bounds[k + 1]], g_ref[...], mod_ref[1:2, :], mod_ref[0:1, :]).astype(BF16)
        zs.append(jnp.dot(hk, w_ref[...], preferred_element_type=F32))
    z = jnp.concatenate(zs, axis=0)
    row = lax.broadcasted_iota(jnp.int32, (tm + 2 * HALO, 1), 0)
    valid = jnp.logical_and(jnp.logical_or(row >= HALO, i > 0),
                            jnp.logical_or(row < tm + HALO, i < last))
    xz = jnp.where(valid, z[:, :D], 0.0)
    y = cb_ref[...] + cw_ref[2:3, :] * xz[HALO:HALO + tm]
    y = y + cw_ref[0:1, :] * xz[HALO - 2:HALO - 2 + tm]
    y = y + cw_ref[1:2, :] * xz[HALO - 1:HALO - 1 + tm]
    y = y + cw_ref[3:4, :] * xz[HALO + 1:HALO + 1 + tm]
    xc_ref[...] = y
    gg_ref[...] = _gelu_tanh(z[HALO:HALO + tm, D:]).astype(BF16)


def rg_in(x, mod, g, w_in, conv_w, conv_b, tm):
    t = x.shape[0]
    nb = tm // HALO
    nblk = t // HALO
    return pl.pallas_call(
        functools.partial(_rg_in_body, tm),
        grid=(t // tm,),
        in_specs=[
            pl.BlockSpec((HALO, D), lambda i: (jnp.maximum(i * nb - 1, 0), 0)),
            pl.BlockSpec((tm, D), lambda i: (i, 0)),
            pl.BlockSpec((HALO, D), lambda i: (jnp.minimum((i + 1) * nb, nblk - 1), 0)),
            _full((ADA_CHUNKS, D)),
            _full((1, D)),
            _full((D, 2 * D)),
            _full((4, D)),
            _full((1, D)),
        ],
        out_specs=[pl.BlockSpec((tm, D), lambda i: (i, 0)), pl.BlockSpec((tm, D), lambda i: (i, 0))],
        out_shape=[jax.ShapeDtypeStruct((t, D), F32), jax.ShapeDtypeStruct((t, D), BF16)],
        compiler_params=_cparams(("arbitrary",)),
        name="rg_in",
    )(x, x, x, mod, g, w_in, conv_w, conv_b)


def _rg_gates(xc, wa_ref, wi_ref, ba, bi, lam):
    xb = xc.astype(BF16)
    nblk = D // RG_BLOCK_W
    r = jnp.concatenate([jnp.dot(xb[:, n * RG_BLOCK_W:(n + 1) * RG_BLOCK_W], wa_ref[n],
                                 preferred_element_type=F32) for n in range(nblk)], axis=1)
    ig = jnp.concatenate([jnp.dot(xb[:, n * RG_BLOCK_W:(n + 1) * RG_BLOCK_W], wi_ref[n],
                                  preferred_element_type=F32) for n in range(nblk)], axis=1)
    t_r = jnp.tanh(r + 0.5 * ba)
    t_i = jnp.tanh(ig + 0.5 * bi)
    nl = -lam
    softplus = jnp.maximum(nl, 0.0) + jnp.log1p(jnp.exp(-jnp.abs(nl)))
    half_c = (-0.5 * LRU_C) * softplus
    log_a = half_c + half_c * t_r
    a = jnp.exp(log_a)
    xh = 0.5 * xc
    b = jnp.sqrt(1.0 - a * a) * (xh + xh * t_i)
    return a, b


def _rg_scan_body(reverse, epilogue, emit_h, nside, tc, *refs):
    xc_ref, wa_ref, wi_ref, ba_ref, bi_ref, lam_ref, h0_ref = refs[:7]
    refs = refs[7:]
    if epilogue:
        hf_ref, gg_ref, wo_ref, x_ref, mod_ref = refs[:5]
        refs = refs[5:]
    side_in, refs = refs[:nside], refs[nside:]
    if emit_h:
        h_ref = refs[0]
        refs = refs[1:]
    if epilogue:
        o_ref = refs[0]
        refs = refs[1:]
    side_out, refs = refs[:nside], refs[nside:]
    a_scr, b_scr, h_scr, carry_scr = refs
    c = pl.program_id(0)
    for src, dst in zip(side_in, side_out):
        dst[...] = src[...].astype(BF16)

    @pl.when(c == 0)
    def _():
        carry_scr[...] = jnp.broadcast_to(h0_ref[...], (SUBLANES, D))

    a, b = _rg_gates(xc_ref[...], wa_ref, wi_ref, ba_ref[...], bi_ref[...], lam_ref[...])
    a_scr[...] = a
    b_scr[...] = b
    nblk = tc // SUBLANES
    row = lax.broadcasted_iota(jnp.int32, (SUBLANES, D), 0)
    first = (row == SUBLANES - 1) if reverse else (row == 0)

    def block(n, carry):
        blk = (nblk - 1 - n) if reverse else n
        off = pl.multiple_of(blk * SUBLANES, SUBLANES)
        av = a_scr[pl.ds(off, SUBLANES), :]
        bv = b_scr[pl.ds(off, SUBLANES), :]
        bv = jnp.where(first, av * carry + bv, bv)
        av = jnp.where(first, 0.0, av)
        for k in (1, 2, 4):
            shift = (SUBLANES - k) if reverse else k
            bv = av * pltpu.roll(bv, shift, 0) + bv
            if k != 4:
                av = av * pltpu.roll(av, shift, 0)
        h_scr[pl.ds(off, SUBLANES), :] = bv
        edge = bv[0:1, :] if reverse else bv[SUBLANES - 1:SUBLANES, :]
        return jnp.broadcast_to(edge, (SUBLANES, D))

    carry_scr[...] = lax.fori_loop(0, nblk, block, carry_scr[...], unroll=2)

    if emit_h:
        h_ref[...] = h_scr[...].astype(h_ref.dtype)
    if epilogue:
        y = ((hf_ref[...].astype(F32) + h_scr[...]) * gg_ref[...].astype(F32)).astype(BF16)
        o_ref[...] = x_ref[...] + mod_ref[2:3, :] * jnp.dot(y, wo_ref[...], preferred_element_type=F32)


def rg_scan(xconv, wa, wi, ba, bi, lam, h0, tc, reverse, epi=None, h_dtype=F32, side=()):
    t = xconv.shape[0]
    nchunks = t // tc
    side_specs = []
    for a in side:
        assert a.shape[0] % (nchunks * BF16_ROWS) == 0
        side_specs.append(pl.BlockSpec((a.shape[0] // nchunks, a.shape[1]), lambda c: (c, 0)))
    idx = (lambda c: (nchunks - 1 - c, 0)) if reverse else (lambda c: (c, 0))
    nb = D // RG_BLOCK_W
    blk = pl.BlockSpec((tc, D), idx)
    in_specs = [
        blk,
        _full((nb, RG_BLOCK_W, RG_BLOCK_W)),
        _full((nb, RG_BLOCK_W, RG_BLOCK_W)),
        _full((1, D)), _full((1, D)), _full((1, D)), _full((1, D)),
    ]
    args = [xconv, wa, wi, ba, bi, lam, h0]
    out_specs = []
    out_shape = []
    if epi is not None:
        hf, gg, w_out, x, mod = epi
        in_specs += [blk, blk, _full((D, D)), blk, _full((ADA_CHUNKS, D))]
        args += [hf, gg, w_out, x, mod]
    in_specs += side_specs
    args += list(side)
    if h_dtype is not None:
        out_specs.append(blk)
        out_shape.append(jax.ShapeDtypeStruct((t, D), h_dtype))
    if epi is not None:
        out_specs.append(blk)
        out_shape.append(jax.ShapeDtypeStruct((t, D), F32))
    out_specs += side_specs
    out_shape += [jax.ShapeDtypeStruct(a.shape, BF16) for a in side]
    return pl.pallas_call(
        functools.partial(_rg_scan_body, reverse, epi is not None, h_dtype is not None, len(side), tc),
        grid=(nchunks,),
        in_specs=in_specs,
        out_specs=out_specs,
        out_shape=out_shape,
        scratch_shapes=[pltpu.VMEM((tc, D), F32), pltpu.VMEM((tc, D), F32), pltpu.VMEM((tc, D), F32),
                        pltpu.VMEM((SUBLANES, D), F32)],
        compiler_params=_cparams(("arbitrary",)),
        name="rg_scan_bwd" if reverse else "rg_scan_fwd",
    )(*args)


def rglru_layer(x, xc, mod, modc, g, w_in, conv_w, conv_b, wa, wi, ba, bi, lam, w_out, need_ctx, side=()):
    w_in_b = w_in.astype(BF16)
    wa_b = (0.5 * wa).astype(BF16)
    wi_b = (0.5 * wi).astype(BF16)
    w_out_b = w_out.astype(BF16)
    cb = conv_b[None]
    tcx = xc.shape[0]
    xcl, ggl = rg_in(x, mod, g, w_in_b, conv_w, cb, WIDE_TILE)
    xcc, ggc = rg_in(xc, modc, g, w_in_b, conv_w, cb, tcx)
    zeros = jnp.zeros((1, D), F32)
    p = lambda d: (wa_b[d], wi_b[d], ba[d][None], bi[d][None], lam[d][None])
    (hcf,) = rg_scan(xcc, *p(0), zeros, tcx, False)
    hlf, *side_b = rg_scan(xcl, *p(0), hcf[tcx - 1:tcx], WIDE_TILE, False, h_dtype=BF16, side=side)
    if need_ctx:
        hcb, xc_new = rg_scan(xcc, *p(1), zeros, tcx, True, epi=(hcf, ggc, w_out_b, xc, modc))
    else:
        (hcb,) = rg_scan(xcc, *p(1), zeros, tcx, True)
        xc_new = None
    (x_new,) = rg_scan(xcl, *p(1), hcb[0:1], WIDE_TILE, True, epi=(hlf, ggl, w_out_b, x, mod), h_dtype=None)
    return x_new, xc_new, side_b


def _qkv_body(x_ref, mod_ref, g_ref, w_ref, gm_ref, qg_ref, kg_ref, q_ref, k_ref, v_ref):
    h = _normmod(x_ref[...], g_ref[...], mod_ref[1:2, :], mod_ref[0:1, :]).astype(BF16)
    z = jnp.dot(h, w_ref[...], preferred_element_type=F32)

    def headnorm(v, gain):
        ms = jnp.dot((v * v).astype(BF16), gm_ref[...], preferred_element_type=F32)
        return (v * lax.rsqrt(ms + RMS_EPS)) * gain

    q_ref[...] = headnorm(z[:, :D], qg_ref[...]).astype(BF16)
    k_ref[...] = headnorm(z[:, D:2 * D], kg_ref[...]).astype(BF16)
    v_ref[...] = z[:, 2 * D:].astype(BF16)


def qkv_proj(x, mod, g, w_qkv, gmean, qg, kg, tm):
    t = x.shape[0]
    spec = pl.BlockSpec((tm, D), lambda i: (i, 0))
    return pl.pallas_call(
        _qkv_body,
        grid=(t // tm,),
        in_specs=[spec, _full((ADA_CHUNKS, D)), _full((1, D)), _full((D, 3 * D)), _full((D, D)),
                  _full((1, D)), _full((1, D))],
        out_specs=[spec, spec, spec],
        out_shape=[jax.ShapeDtypeStruct((t, D), BF16)] * 3,
        compiler_params=_cparams(("arbitrary",)),
        name="qkv_proj",
    )(x, mod, g, w_qkv, gmean, qg, kg)


def _attend_pair(q2, keys, vals, biases):
    m_rows = q2.shape[0]
    lane = lax.broadcasted_iota(jnp.int32, q2.shape, 1)
    zero = jnp.zeros_like(q2)
    qs = jnp.concatenate([jnp.where(lane < NA_HEAD_DIM, q2, zero), jnp.where(lane >= NA_HEAD_DIM, q2, zero)], axis=0)
    ss = []
    for kseg, bseg in zip(keys, biases):
        s = lax.dot_general(qs, kseg, (((1,), (1,)), ((), ())), preferred_element_type=F32)
        if bseg is not None:
            s = s + jnp.concatenate([bseg[0], bseg[1]], axis=0)
        ss.append(s)
    m = ss[0].max(axis=-1, keepdims=True)
    for s in ss[1:]:
        m = jnp.maximum(m, s.max(axis=-1, keepdims=True))
    acc = None
    for s, vseg in zip(ss, vals):
        p = jnp.exp2(s - m)
        vaug = jnp.concatenate([vseg, jnp.ones_like(vseg)], axis=1)
        o = jnp.dot(p.astype(BF16), vaug, preferred_element_type=F32)
        acc = o if acc is None else acc + o
    out = acc[:, :LANES] / acc[:, LANES:]
    return jnp.where(lane < NA_HEAD_DIM, out[:m_rows], out[m_rows:])


NA_QROWS = 2
NA_UNION = NA_ROWS + NA_QROWS - 1


def _na_body(nside, var_ref, q_ref, kl_ref, vl_ref, kc_ref, vc_ref, bias_ref, *refs):
    side_in, o_ref, side_out = refs[:nside], refs[nside], refs[nside + 1:]
    npair = NA_HEADS // 2
    for pr in range(npair):
        sl = slice(pr * LANES, (pr + 1) * LANES)
        o_ref[:, sl] = _attend_pair(
            q_ref[:, sl], [kl_ref[:, sl], kc_ref[:, sl]], [vl_ref[:, sl], vc_ref[:, sl]],
            [(bias_ref[0, 2 * pr], bias_ref[0, 2 * pr + 1]), None]).astype(BF16)
        for src, dst in zip(side_in, side_out):
            slab = -(-src.shape[0] // (npair * BF16_ROWS)) * BF16_ROWS
            lo = min(pr * slab, src.shape[0])
            hi = min(lo + slab, src.shape[0])
            if hi > lo:
                dst[lo:hi, :] = src[lo:hi, :].astype(BF16)


def _na_geometry(rows):
    steps = rows // NA_QROWS
    g = np.arange(steps)
    base = np.clip(NA_QROWS * g - NA_ROWS // 2, 0, rows - NA_UNION)
    r = NA_QROWS * g[:, None] + np.arange(NA_QROWS)[None, :]
    rs = np.clip(r - NA_ROWS // 2, 0, rows - NA_ROWS)
    key = np.concatenate([(base - NA_QROWS * g)[:, None], rs - r], axis=1)
    uniq, first, var = np.unique(key, axis=0, return_index=True, return_inverse=True)
    return base, var.reshape(-1).astype(np.int32), g[first]


def na_attention(q, k, v, kc, vc, bias_tab, var, side=()):
    t = q.shape[0]
    rows = t // GRID_W
    nctx = kc.shape[0]
    steps = rows // NA_QROWS
    side_specs = []
    for a in side:
        assert a.shape[0] % (steps * 16) == 0
        side_specs.append(pl.BlockSpec((a.shape[0] // steps, a.shape[1]), lambda g, var: (g, 0)))

    def kbase(g):
        return jnp.clip(NA_QROWS * g - NA_ROWS // 2, 0, rows - NA_UNION)

    qrows = NA_QROWS * GRID_W
    nloc = NA_UNION * GRID_W
    kspec = pl.BlockSpec((pl.Element(nloc), pl.Element(D)), lambda g, var: (kbase(g) * GRID_W, 0))
    grid_spec = pltpu.PrefetchScalarGridSpec(
        num_scalar_prefetch=1,
        grid=(steps,),
        in_specs=[pl.BlockSpec((qrows, D), lambda g, var: (g, 0)), kspec, kspec] + [
            pl.BlockSpec((nctx, D), lambda g, var: (0, 0)), pl.BlockSpec((nctx, D), lambda g, var: (0, 0)),
            pl.BlockSpec((1, NA_HEADS, qrows, nloc), lambda g, var: (var[g], 0, 0, 0)),
        ] + side_specs,
        out_specs=[pl.BlockSpec((qrows, D), lambda g, var: (g, 0))] + side_specs,
    )
    return pl.pallas_call(
        functools.partial(_na_body, len(side)),
        grid_spec=grid_spec,
        out_shape=[jax.ShapeDtypeStruct((t, D), BF16)] + [jax.ShapeDtypeStruct(a.shape, BF16) for a in side],
        compiler_params=_cparams(("arbitrary",)),
        name="na_attention",
    )(var, q, k, v, kc, vc, bias_tab, *side)


def _ctx_attn_body(q_ref, k_ref, v_ref, o_ref):
    for pr in range(NA_HEADS // 2):
        sl = slice(pr * LANES, (pr + 1) * LANES)
        o_ref[:, sl] = _attend_pair(q_ref[:, sl], [k_ref[:, sl]], [v_ref[:, sl]], [None]).astype(BF16)


def ctx_attention(q, k, v):
    t = q.shape[0]
    return pl.pallas_call(
        _ctx_attn_body,
        grid=(1,),
        in_specs=[_full((t, D))] * 3,
        out_specs=_full((t, D)),
        out_shape=jax.ShapeDtypeStruct((t, D), BF16),
        compiler_params=_cparams(("arbitrary",)),
        name="ctx_attention",
    )(q, k, v)


def _na_bias_table(rpb, rows):
    base, var, reps = _na_geometry(rows)
    cols = np.arange(GRID_W)
    cstart = np.clip(cols - NA_COLS // 2, 0, GRID_W - NA_COLS)
    kcol = np.arange(GRID_W)
    inwin = (kcol[None, :] >= cstart[:, None]) & (kcol[None, :] < cstart[:, None] + NA_COLS)
    r = NA_QROWS * reps[:, None] + np.arange(NA_QROWS)[None, :]
    rs = np.clip(r - NA_ROWS // 2, 0, rows - NA_ROWS)
    krow = base[reps][:, None] + np.arange(NA_UNION)[None, :]
    rvalid = (krow[:, None, :] >= rs[:, :, None]) & (krow[:, None, :] < rs[:, :, None] + NA_ROWS)
    ridx = np.clip(krow[:, None, :] - r[:, :, None] + (NA_ROWS - 1), 0, 2 * NA_ROWS - 2)
    nd = 2 * NA_COLS - 1
    w = jnp.pad(rpb.astype(F32), ((0, 0), (0, 0), (GRID_W - NA_COLS, 2 * GRID_W - (GRID_W - NA_COLS) - nd)))
    flat = jnp.tile(w, (1, 1, GRID_W))[:, :, :GRID_W * (2 * GRID_W - 1)]
    blk = flat.reshape(NA_HEADS, 2 * NA_ROWS - 1, GRID_W, 2 * GRID_W - 1)[..., GRID_W - 1:]
    blk = jnp.where(jnp.asarray(inwin)[None, None], blk, NEG_BIG)
    neg = jnp.full((NA_HEADS, GRID_W, GRID_W), NEG_BIG, F32)
    variants = []
    for v in range(len(reps)):
        strips = [jnp.concatenate([blk[:, ridx[v, a, j]] if rvalid[v, a, j] else neg for j in range(NA_UNION)], axis=2)
                  for a in range(NA_QROWS)]
        variants.append(jnp.concatenate(strips, axis=1))
    return jnp.stack(variants, axis=0), jnp.asarray(var)


def na_layer(x, xc, mod, modc, g, w_qkv, q_g, k_g, rpb, w_o, need_ctx, side=()):
    w_qkv_b = w_qkv.astype(BF16)
    w_o_b = w_o.astype(BF16)
    gm = np.kron(np.eye(NA_HEADS), np.full((NA_HEAD_DIM, NA_HEAD_DIM), 1.0 / NA_HEAD_DIM))
    gmean = jnp.asarray(gm, dtype=BF16)
    qg = jnp.tile(q_g, NA_HEADS)[None] * (NA_HEAD_DIM ** -0.5 * LOG2E)
    kg = jnp.tile(k_g, NA_HEADS)[None]
    q, k, v = qkv_proj(x, mod, g, w_qkv_b, gmean, qg, kg, WIDE_TILE)
    qc, kc, vc = qkv_proj(xc, modc, g, w_qkv_b, gmean, qg, kg, xc.shape[0])
    bias_tab, var = _na_bias_table(rpb * LOG2E, x.shape[0] // GRID_W)
    o, *side_b = na_attention(q, k, v, kc, vc, bias_tab, var, side)
    x_new = proj_residual(o, w_o_b, x, mod, 2, LIGHT_TILE)
    xc_new = None
    if need_ctx:
        oc = ctx_attention(qc, kc, vc)
        xc_new = proj_residual(oc, w_o_b, xc, modc, 2, xc.shape[0])
    return x_new, xc_new, side_b


def _dft_mats(n):
    ang = 2.0 * np.pi * np.outer(np.arange(n), np.arange(n)) / n
    return np.cos(ang), np.sin(ang)


def _channel_dft(h, wc):
    us = [jnp.dot(h[:, gi * FT_GROUP_W:(gi + 1) * FT_GROUP_W], wc, preferred_element_type=F32).astype(BF16)
          for gi in range(D // FT_GROUP_W)]
    return jnp.concatenate([u[:, :FT_GROUP_W] for u in us] + [u[:, FT_GROUP_W:] for u in us], axis=1)


def _ft_a_body(n, nj, x_ref, mod_ref, g_ref, perm_ref, wc_ref, ma_ref, yr_ref, yi_ref):
    h3 = _normmod(x_ref[...], g_ref[...], mod_ref[1:2, :], mod_ref[0:1, :])
    h = jnp.dot(perm_ref[...], h3.reshape(n * nj, D).astype(BF16), preferred_element_type=F32).astype(BF16)
    u = _channel_dft(h, wc_ref[...])
    for j in range(nj):
        uj = u[j * n:(j + 1) * n]
        y = jnp.dot(ma_ref[...], jnp.concatenate([uj[:, :D], uj[:, D:]], axis=0), preferred_element_type=F32)
        yr_ref[:, j, :] = y[:n]
        yi_ref[:, j, :] = y[n:]


def _ft_c_body(n, nj, yr_ref, yi_ref, mc_ref, wf_ref, x_ref, mod_ref, o_ref):
    fs = []
    for j in range(nj):
        ys = jnp.concatenate([yr_ref[j].astype(BF16), yi_ref[j].astype(BF16)], axis=0)
        fs.append(jnp.dot(mc_ref[j], ys, preferred_element_type=F32).astype(BF16))
    z = jnp.dot(jnp.concatenate(fs, axis=0), wf_ref[...], preferred_element_type=F32)
    gate = mod_ref[2:3, :]
    for j in range(nj):
        o_ref[:, j, :] = x_ref[:, j, :] + gate * z[j * n:(j + 1) * n]


def _ft_ctx_body(x_ref, mod_ref, g_ref, wc_ref, ml_ref, wf_ref, o_ref):
    x = x_ref[...]
    h = _normmod(x, g_ref[...], mod_ref[1:2, :], mod_ref[0:1, :]).astype(BF16)
    u = _channel_dft(h, wc_ref[...])
    us = jnp.concatenate([u[:, :D], u[:, D:]], axis=0)
    f = jnp.dot(ml_ref[...], us, preferred_element_type=F32).astype(BF16)
    o_ref[...] = x + mod_ref[2:3, :] * jnp.dot(f, wf_ref[...], preferred_element_type=F32)


def fourier_layer(x, xc, mod, modc, g, w_f, need_ctx):
    t = x.shape[0]
    n = math.isqrt(t)
    assert n * n == t and n % 16 == 0
    w_f_b = w_f.astype(BF16)
    cw, sw = _dft_mats(FT_GROUP_W)
    wc = jnp.asarray(np.concatenate([cw, -sw], axis=1) / math.sqrt(FT_GROUP_W), dtype=F32).astype(BF16)
    wcspec = _full((FT_GROUP_W, 2 * FT_GROUP_W))
    cn, sn = _dft_mats(n)
    ma = jnp.asarray(np.block([[cn, sn], [-sn, cn]]) / math.sqrt(n), dtype=F32).astype(BF16)
    nj = 8
    ang = 2.0 * np.pi * np.outer(np.arange(n), np.arange(n)) / t
    cnj, snj = jnp.asarray(cn, dtype=F32)[None], jnp.asarray(sn, dtype=F32)[None]
    tcj = jnp.asarray(np.cos(ang).T, dtype=F32)[:, None, :]
    tsj = jnp.asarray(np.sin(ang).T, dtype=F32)[:, None, :]
    mc = (jnp.concatenate([cnj * tcj - snj * tsj, cnj * tsj + snj * tcj], axis=2) / math.sqrt(n)).astype(BF16)
    xblk = pl.BlockSpec((n, nj, D), lambda b: (0, b, 0))
    yblk = pl.BlockSpec((nj, n, D), lambda b: (b, 0, 0))
    mblk = pl.BlockSpec((nj, n, 2 * n), lambda b: (b, 0, 0))
    x3 = x.reshape(n, n, D)
    src = (np.arange(n)[None, :] * nj + np.arange(nj)[:, None]).reshape(-1)
    perm = jnp.asarray(np.eye(n * nj)[src], dtype=BF16)
    yr, yi = pl.pallas_call(
        functools.partial(_ft_a_body, n, nj),
        grid=(n // nj,),
        in_specs=[xblk, _full((ADA_CHUNKS, D)), _full((1, D)), _full((n * nj, n * nj)), wcspec,
                  _full((2 * n, 2 * n))],
        out_specs=[xblk, xblk],
        out_shape=[jax.ShapeDtypeStruct((n, n, D), F32)] * 2,
        compiler_params=_cparams(("arbitrary",)),
        name="ft_stage_a",
    )(x3, mod, g, perm, wc, ma)
    x_new = pl.pallas_call(
        functools.partial(_ft_c_body, n, nj),
        grid=(n // nj,),
        in_specs=[yblk, yblk, mblk, _full((D, D)), xblk, _full((ADA_CHUNKS, D))],
        out_specs=xblk,
        out_shape=jax.ShapeDtypeStruct((n, n, D), F32),
        compiler_params=_cparams(("arbitrary",)),
        name="ft_stage_c",
    )(yr, yi, mc, w_f_b, x3, mod).reshape(t, D)
    xc_new = None
    if need_ctx:
        lc = xc.shape[0]
        cl, sl = _dft_mats(lc)
        ml = jnp.asarray(np.concatenate([cl, sl], axis=1) / math.sqrt(lc), dtype=F32).astype(BF16)
        xc_new = pl.pallas_call(
            _ft_ctx_body,
            grid=(1,),
            in_specs=[_full((lc, D)), _full((ADA_CHUNKS, D)), _full((1, D)), wcspec,
                      _full((lc, 2 * lc)), _full((D, D))],
            out_specs=_full((lc, D)),
            out_shape=jax.ShapeDtypeStruct((lc, D), F32),
            compiler_params=_cparams(("arbitrary",)),
            name="ft_ctx",
        )(xc, modc, g, wc, ml, w_f_b)
    return x_new, xc_new


def _router_body(x_ref, mod_ref, g_ref, r_ref, info_ref, w0_ref, w1_ref):
    h = _normmod(x_ref[...], g_ref[...], mod_ref[4:5, :], mod_ref[3:4, :])
    hh = h.astype(BF16)
    hl = (h - hh.astype(F32)).astype(BF16)
    r = r_ref[...]
    rh = r.astype(BF16)
    rl = (r - rh.astype(F32)).astype(BF16)
    logits = (jnp.dot(hh, rh, preferred_element_type=F32) + jnp.dot(hh, rl, preferred_element_type=F32)
              + jnp.dot(hl, rh, preferred_element_type=F32))
    lane = lax.broadcasted_iota(jnp.int32, logits.shape, 1)
    logits = jnp.where(lane < N_EXPERTS, logits, NEG_BIG)
    v0 = jnp.max(logits, axis=-1, keepdims=True)
    i0 = jnp.min(jnp.where(logits == v0, lane, LANES), axis=-1, keepdims=True)
    rest = jnp.where(lane == i0, NEG_BIG, logits)
    v1 = jnp.max(rest, axis=-1, keepdims=True)
    i1 = jnp.min(jnp.where(rest == v1, lane, LANES), axis=-1, keepdims=True)
    e = jnp.exp(v1 - v0)
    w0 = 1.0 / (1.0 + e)
    w1 = e / (1.0 + e)
    info_ref[...] = jnp.where(lane == 0, i0, jnp.where(lane == 1, i1, 0))
    w0_ref[...] = jnp.broadcast_to(w0, logits.shape)
    w1_ref[...] = jnp.broadcast_to(w1, logits.shape)


def moe_router(x, mod, g, router_pad, tm):
    t = x.shape[0]
    tm = min(tm, t)
    spec = pl.BlockSpec((tm, D), lambda i: (i, 0))
    lspec = pl.BlockSpec((tm, LANES), lambda i: (i, 0))
    return pl.pallas_call(
        _router_body,
        grid=(t // tm,),
        in_specs=[spec, _full((ADA_CHUNKS, D)), _full((1, D)), _full((D, LANES))],
        out_specs=[lspec, lspec, lspec],
        out_shape=[jax.ShapeDtypeStruct((t, LANES), jnp.int32),
                   jax.ShapeDtypeStruct((t, LANES), F32), jax.ShapeDtypeStruct((t, LANES), F32)],
        compiler_params=_cparams(("arbitrary",)),
        name="moe_router",
    )(x, mod, g, router_pad)


SCATTER_TOKENS = 256


def _row_scatter_body(nlat, nctx, didx_ref, g_ref, x_ref, mod_ref, *rest):
    if nctx:
        xc_ref, modc_ref, dst_ref, h_scr, zero_scr, sems = rest
    else:
        dst_ref, h_scr, zero_scr, sems = rest
    i = pl.program_id(0)
    nsteps = pl.num_programs(0)
    ts = SCATTER_TOKENS
    slot = i % 2

    def start_all(src_ref, src_is_zero_rows):
        def issue(grp, c):
            base = pl.multiple_of(grp * SUBLANES, SUBLANES)
            for r in range(SUBLANES):
                src = src_ref.at[pl.ds(r if src_is_zero_rows else base + r, 1), :]
                for half in range(2):
                    d = didx_ref[0, 0, base + r + half * ts]
                    pltpu.make_async_copy(src, dst_ref.at[pl.ds(d, 1), :], sems.at[slot]).start(priority=half)
            return c
        lax.fori_loop(0, ts // SUBLANES, issue, 0)

    def wait_all(which):
        def drain(n, c):
            pltpu.make_async_copy(zero_scr.at[pl.ds(0, 1), :], dst_ref.at[pl.ds(0, 1), :], sems.at[which]).wait()
            return c
        lax.fori_loop(0, 2 * ts, drain, 0, unroll=8)

    @pl.when(i == 0)
    def _():
        zero_scr[...] = jnp.zeros_like(zero_scr)

    def stage_and_start(src_ref, m_ref):
        h_scr[slot] = _normmod(src_ref[...], g_ref[...], m_ref[4:5, :], m_ref[3:4, :])
        start_all(h_scr.at[slot], False)

    @pl.when(i < nlat)
    def _():
        stage_and_start(x_ref, mod_ref)

    if nctx:
        @pl.when(jnp.logical_and(i >= nlat, i < nlat + nctx))
        def _():
            stage_and_start(xc_ref, modc_ref)

    @pl.when(i >= nlat + nctx)
    def _():
        start_all(zero_scr, True)

    @pl.when(i > 0)
    def _():
        wait_all(1 - slot)

    @pl.when(i == nsteps - 1)
    def _():
        wait_all(slot)


def row_scatter(x, xc, mod, modc, g, d0, d1, pad_pos):
    ts = SCATTER_TOKENS
    nlat = x.shape[0] // ts
    nctx = 0 if xc is None else 1
    assert xc is None or xc.shape[0] == ts
    ntok = nlat + nctx
    npad = pad_pos.shape[0] // (2 * ts)
    didx = jnp.concatenate([jnp.concatenate([d0.reshape(ntok, 1, ts), d1.reshape(ntok, 1, ts)], axis=2),
                            pad_pos.reshape(npad, 1, 2 * ts)], axis=0)
    in_specs = [pl.BlockSpec((1, 1, 2 * ts), lambda i: (i, 0, 0), memory_space=pltpu.SMEM),
                _full((1, D)),
                pl.BlockSpec((ts, D), lambda i: (jnp.minimum(i, nlat - 1), 0)),
                _full((ADA_CHUNKS, D))]
    args = [didx, g, x, mod]
    if nctx:
        in_specs += [_full((ts, D)), _full((ADA_CHUNKS, D))]
        args += [xc, modc]
    return pl.pallas_call(
        functools.partial(_row_scatter_body, nlat, nctx),
        grid=(ntok + npad,),
        in_specs=in_specs,
        out_specs=pl.BlockSpec(memory_space=pl.ANY),
        out_shape=jax.ShapeDtypeStruct((2 * ntok * ts + pad_pos.shape[0], D), F32),
        scratch_shapes=[pltpu.VMEM((2, ts, D), F32), pltpu.VMEM((SUBLANES, D), F32), pltpu.SemaphoreType.DMA((2,))],
        compiler_params=_cparams(("arbitrary",)),
        name="moe_row_scatter",
    )(*args)


def _moe_ffn_body(te_ref, tv_ref, xg_ref, wg_ref, wu_ref, wd_ref, o_ref):
    i = pl.program_id(0)

    @pl.when(tv_ref[i] > 0)
    def _():
        o_ref[...] = _swiglu_chunks(xg_ref[...].astype(BF16), wg_ref, wu_ref, wd_ref)

    @pl.when(tv_ref[i] == 0)
    def _():
        o_ref[...] = jnp.zeros_like(o_ref)


def moe_ffn(xg, tile_e, tile_v, w_gu, w_down, li, tm):
    p = xg.shape[0]
    grid_spec = pltpu.PrefetchScalarGridSpec(
        num_scalar_prefetch=2,
        grid=(p // tm,),
        in_specs=[
            pl.BlockSpec((tm, D), lambda i, te, tv: (i, 0)),
            pl.BlockSpec((None, None, D, D_FF), lambda i, te, tv: (li, te[i], 0, 0)),
            pl.BlockSpec((None, None, D, D_FF), lambda i, te, tv: (li, te[i], 0, 1)),
            pl.BlockSpec((None, None, D_FF, D), lambda i, te, tv: (li, te[i], 0, 0)),
        ],
        out_specs=pl.BlockSpec((tm, D), lambda i, te, tv: (i, 0)),
    )
    return pl.pallas_call(
        _moe_ffn_body,
        grid_spec=grid_spec,
        out_shape=jax.ShapeDtypeStruct((p, D), F32),
        compiler_params=pltpu.CompilerParams(dimension_semantics=("arbitrary",), vmem_limit_bytes=MOE_VMEM_LIMIT),
        name="moe_ffn",
    )(tile_e, tile_v, xg, w_gu, w_gu, w_down)


def _combine_body(tt, d0_ref, d1_ref, d0n_ref, d1n_ref, yp_ref, x_ref, mod_ref, w0_ref, w1_ref, o_ref,
                  a_scr, b_scr, sems):
    i = pl.program_id(0)
    nsteps = pl.num_programs(0)
    slot = i % 2

    def start_all(i0_ref, i1_ref, which):
        def issue(grp, c):
            base = pl.multiple_of(grp * SUBLANES, SUBLANES)
            for r in range(SUBLANES):
                n = base + r
                pltpu.make_async_copy(yp_ref.at[pl.ds(i0_ref[0, 0, n], 1), :], a_scr.at[which, pl.ds(n, 1), :],
                                      sems.at[which]).start(priority=0)
                pltpu.make_async_copy(yp_ref.at[pl.ds(i1_ref[0, 0, n], 1), :], b_scr.at[which, pl.ds(n, 1), :],
                                      sems.at[which]).start(priority=1)
            return c
        lax.fori_loop(0, tt // SUBLANES, issue, 0)

    @pl.when(i == 0)
    def _():
        start_all(d0_ref, d1_ref, 0)

    @pl.when(i + 1 < nsteps)
    def _():
        start_all(d0n_ref, d1n_ref, 1 - slot)

    def drain(n, c):
        pltpu.make_async_copy(yp_ref.at[pl.ds(0, 1), :], a_scr.at[slot, pl.ds(0, 1), :], sems.at[slot]).wait()
        pltpu.make_async_copy(yp_ref.at[pl.ds(0, 1), :], b_scr.at[slot, pl.ds(0, 1), :], sems.at[slot]).wait()
        return c
    lax.fori_loop(0, tt, drain, 0, unroll=8)
    w0 = _lane_tile(w0_ref[...])
    w1 = _lane_tile(w1_ref[...])
    o_ref[...] = x_ref[...] + mod_ref[5:6, :] * (w0 * a_scr[slot] + w1 * b_scr[slot])


def moe_combine(yp, d0, d1, x, mod, w0b, w1b, tt):
    t = x.shape[0]
    nt = t // tt
    ispec = pl.BlockSpec((1, 1, tt), lambda i: (i, 0, 0), memory_space=pltpu.SMEM)
    nspec = pl.BlockSpec((1, 1, tt), lambda i: (jnp.minimum(i + 1, nt - 1), 0, 0), memory_space=pltpu.SMEM)
    spec = pl.BlockSpec((tt, D), lambda i: (i, 0))
    lspec = pl.BlockSpec((tt, LANES), lambda i: (i, 0))
    d0r, d1r = d0.reshape(nt, 1, tt), d1.reshape(nt, 1, tt)
    return pl.pallas_call(
        functools.partial(_combine_body, tt),
        grid=(nt,),
        in_specs=[ispec, ispec, nspec, nspec, pl.BlockSpec(memory_space=pl.ANY), spec, _full((ADA_CHUNKS, D)),
                  lspec, lspec],
        out_specs=spec,
        out_shape=jax.ShapeDtypeStruct((t, D), F32),
        scratch_shapes=[pltpu.VMEM((2, tt, D), F32), pltpu.VMEM((2, tt, D), F32), pltpu.SemaphoreType.DMA((2,))],
        compiler_params=_cparams(("arbitrary",)),
        name="moe_combine",
    )(d0r, d1r, d0r, d1r, yp, x, mod, w0b, w1b)


def _route_plan(e0, e1, tm):
    t = e0.shape[0]
    n = 2 * t
    ex = jnp.arange(N_EXPERTS, dtype=jnp.int32)
    oh0 = (e0[:, None] == ex[None, :]).astype(jnp.int32)
    oh1 = (e1[:, None] == ex[None, :]).astype(jnp.int32)
    both = oh0 + oh1
    csum = jnp.cumsum(both, axis=0)
    before = csum - both
    counts = csum[-1]
    padded = ((counts + tm - 1) // tm) * tm
    pad_end = jnp.cumsum(padded)
    pad_off = pad_end - padded
    total = pad_end[-1]
    d0 = jnp.sum(oh0 * (before + pad_off[None, :]), axis=1)
    d1 = jnp.sum(oh1 * (before + oh0 + pad_off[None, :]), axis=1)
    gap = padded - counts
    tail_off = jnp.cumsum(tm - gap) - (tm - gap)
    r = jnp.arange(tm, dtype=jnp.int32)[None, :]
    pad_pos = jnp.where(r < gap[:, None], (pad_off + counts)[:, None] + r,
                        total + tail_off[:, None] + (r - gap[:, None])).reshape(-1)
    ntiles = (n + N_EXPERTS * tm) // tm
    tstart = jnp.arange(ntiles, dtype=jnp.int32) * tm
    tile_v = (tstart < total).astype(jnp.int32)
    tile_e = jnp.sum((jnp.minimum(tstart, total - 1)[:, None] >= pad_end[None, :]).astype(jnp.int32), axis=1)
    return (d0.astype(jnp.int32), d1.astype(jnp.int32), pad_pos.astype(jnp.int32), tile_e.astype(jnp.int32), tile_v)


def moe_layer(x, xc, mod, modc, g, router, w_gu_b, w_down_b, li, need_ctx, tm=ROW_TILE):
    router_pad = jnp.pad(router, ((0, 0), (0, LANES - N_EXPERTS)))
    s = x.shape[0]
    info, w0b, w1b = moe_router(x, mod, g, router_pad, LIGHT_TILE)
    if need_ctx:
        sc = xc.shape[0]
        infoc, w0c, w1c = moe_router(xc, modc, g, router_pad, sc)
        e0 = jnp.concatenate([info[:, 0], infoc[:, 0]])
        e1 = jnp.concatenate([info[:, 1], infoc[:, 1]])
    else:
        e0, e1 = info[:, 0], info[:, 1]
    d0, d1, pad_pos, tile_e, tile_v = _route_plan(e0, e1, tm)
    xg = row_scatter(x, xc if need_ctx else None, mod, modc, g, d0, d1, pad_pos)
    yp = moe_ffn(xg, tile_e, tile_v, w_gu_b, w_down_b, li, tm)
    x_new = moe_combine(yp, d0[:s], d1[:s], x, mod, w0b, w1b, ROW_TILE)
    xc_new = None
    if need_ctx:
        xc_new = moe_combine(yp, d0[s:], d1[s:], xc, modc, w0c, w1c, sc)
    return x_new, xc_new


def kernel(x, c, ctx, c_ctx, ada_w, ada_b, norm_g, rg_w_in, rg_conv_w, rg_conv_b, rg_wa, rg_ba, rg_wi, rg_bi,
           rg_lambda, rg_w_out, na_w_qkv, na_q_g, na_k_g, na_rpb, na_w_o, ft_w_out, ffn_w_gu, ffn_w_down,
           moe_router, moe_w_gu, moe_w_down):
    depth = ada_w.shape[0]
    assert x.shape[0] == 1 and x.shape[2] == D
    xs = x[0]
    xc = ctx[0]
    mods = ada_modulation(c, c_ctx, ada_w, ada_b)
    ffn_gu_b = ffn_dn_b = moe_gu_b = moe_dn_b = None
    mix_idx = [0] * N_MIXERS
    dense_idx = 0
    moe_idx = 0
    for layer in range(depth):
        need_ctx = layer != depth - 1
        mod, modc = mods[layer, 0], mods[layer, 1]
        g0 = norm_g[layer, 0][None]
        g1 = norm_g[layer, 1][None]
        kind = layer % N_MIXERS
        j = mix_idx[kind]
        mix_idx[kind] += 1
        if kind == 0:
            side = ()
            if ffn_gu_b is None:
                side = (ffn_w_gu.reshape(-1, ffn_w_gu.shape[-1]), ffn_w_down.reshape(-1, ffn_w_down.shape[-1]))
            xs, xcn, side_b = rglru_layer(xs, xc, mod, modc, g0, rg_w_in[j], rg_conv_w[j], rg_conv_b[j], rg_wa[j],
                                          rg_wi[j], rg_ba[j], rg_bi[j], rg_lambda[j], rg_w_out[j], need_ctx, side)
            if side_b:
                ffn_gu_b, ffn_dn_b = side_b[0].reshape(ffn_w_gu.shape), side_b[1].reshape(ffn_w_down.shape)
        elif kind == 1:
            side = ()
            if moe_gu_b is None:
                side = (moe_w_gu.reshape(-1, moe_w_gu.shape[-1]), moe_w_down.reshape(-1, moe_w_down.shape[-1]))
            xs, xcn, side_b = na_layer(xs, xc, mod, modc, g0, na_w_qkv[j], na_q_g[j], na_k_g[j], na_rpb[j],
                                       na_w_o[j], need_ctx, side)
            if side_b:
                moe_gu_b, moe_dn_b = side_b[0].reshape(moe_w_gu.shape), side_b[1].reshape(moe_w_down.shape)
        else:
            xs, xcn = fourier_layer(xs, xc, mod, modc, g0, ft_w_out[j], need_ctx)
        if need_ctx:
            xc = xcn
        if layer % 2 == 0:
            if ffn_gu_b is None:
                ffn_gu_b, ffn_dn_b = ffn_w_gu.astype(BF16), ffn_w_down.astype(BF16)
            if need_ctx:
                xc = ffn_dense(xc, modc, g1, ffn_gu_b, ffn_dn_b, dense_idx, xc.shape[0])
            xs = ffn_dense(xs, mod, g1, ffn_gu_b, ffn_dn_b, dense_idx, ROW_TILE)
            dense_idx += 1
        else:
            if moe_gu_b is None:
                moe_gu_b, moe_dn_b = moe_w_gu.astype(BF16), moe_w_down.astype(BF16)
            xs, xcn = moe_layer(xs, xc, mod, modc, g1, moe_router[moe_idx], moe_gu_b, moe_dn_b, moe_idx, need_ctx)
            moe_idx += 1
            if need_ctx:
                xc = xcn
    return xs[None]
```

```python
import functools
import math

import numpy as np
import jax
import jax.numpy as jnp
from jax import lax
from jax.experimental import pallas as pl
from jax.experimental.pallas import tpu as pltpu

F32 = jnp.float32
BF16 = jnp.bfloat16

D = 1024
D_FF = 3584
N_EXPERTS = 8
GRID_W = 64
NA_HEADS = 16
NA_HEAD_DIM = 64
NA_ROWS = 8
NA_COLS = 16
FT_GROUP_W = 256
RG_BLOCK_W = 256
RMS_EPS = 1e-6
LRU_C = 8.0
N_MIXERS = 3
ADA_CHUNKS = 6

LANES = 128
SUBLANES = 8
BF16_ROWS = 16
VMEM_LIMIT = 56 * 1024 * 1024
MOE_VMEM_LIMIT = 60 * 1024 * 1024

ROW_TILE = 512
WIDE_TILE = 1024
LIGHT_TILE = 2048
NEG_BIG = -1e30
LOG2E = math.log2(math.e)


def _cparams(sem):
    return pltpu.CompilerParams(dimension_semantics=sem, vmem_limit_bytes=VMEM_LIMIT)


def _full(shape):
    nd = len(shape)
    return pl.BlockSpec(shape, lambda *_: (0,) * nd)


def _normmod(x, g, scale, shift):
    ms = jnp.mean(x * x, axis=-1, keepdims=True)
    y = x * lax.rsqrt(ms + RMS_EPS)
    return (y * g) * (1.0 + scale) + shift


def _lane_tile(v):
    return jnp.concatenate([v] * (D // LANES), axis=1)


def _sigmoid(v):
    return 1.0 / (1.0 + jnp.exp(-v))


def _gelu_tanh(v):
    c = math.sqrt(2.0 / math.pi)
    return v * (0.5 * (1.0 + jnp.tanh(c * (v + 0.044715 * (v * v * v)))))


def _ada_body(cin_ref, w_ref, b_ref, o_ref):
    v = cin_ref[...]
    s = v * _sigmoid(v)
    w = w_ref[0]
    r0 = jnp.sum(s[:, 0:1] * w, axis=0, keepdims=True)
    r1 = jnp.sum(s[:, 1:2] * w, axis=0, keepdims=True)
    o_ref[0] = jnp.concatenate([r0, r1], axis=0) + b_ref[0]


def ada_modulation(c, c_ctx, ada_w, ada_b):
    depth = ada_w.shape[0]
    n = ada_w.shape[2]
    nc = n // 4
    cin = jnp.stack([c[0], c_ctx], axis=1)
    out = pl.pallas_call(
        _ada_body,
        grid=(depth, n // nc),
        in_specs=[
            pl.BlockSpec((D, 2), lambda l, j: (0, 0)),
            pl.BlockSpec((1, D, nc), lambda l, j: (l, 0, j)),
            pl.BlockSpec((1, 1, nc), lambda l, j: (l, 0, j)),
        ],
        out_specs=pl.BlockSpec((1, 2, nc), lambda l, j: (l, 0, j)),
        out_shape=jax.ShapeDtypeStruct((depth, 2, n), F32),
        compiler_params=_cparams(("arbitrary", "arbitrary")),
        name="ada_mod",
    )(cin, ada_w, ada_b.reshape(depth, 1, n))
    return out.reshape(depth, 2, ADA_CHUNKS, D)


FFN_CHUNK = 512


def _swiglu_chunks(h, wg_ref, wu_ref, wd_ref):
    acc = None
    for c in range(D_FF // FFN_CHUNK):
        sl = slice(c * FFN_CHUNK, (c + 1) * FFN_CHUNK)
        gg = jnp.dot(h, wg_ref[:, sl], preferred_element_type=F32)
        uu = jnp.dot(h, wu_ref[:, sl], preferred_element_type=F32)
        a = ((gg * _sigmoid(gg)) * uu).astype(BF16)
        part = jnp.dot(a, wd_ref[sl, :], preferred_element_type=F32)
        acc = part if acc is None else acc + part
    return acc


def _ffn_body(x_ref, mod_ref, g_ref, wg_ref, wu_ref, wd_ref, o_ref):
    x = x_ref[...]
    h = _normmod(x, g_ref[...], mod_ref[4:5, :], mod_ref[3:4, :]).astype(BF16)
    o_ref[...] = x + mod_ref[5:6, :] * _swiglu_chunks(h, wg_ref, wu_ref, wd_ref)


def ffn_dense(x, mod, g, w_gu, w_down, li, tm):
    t = x.shape[0]
    once = pl.Buffered(1)
    return pl.pallas_call(
        _ffn_body,
        grid=(t // tm,),
        in_specs=[
            pl.BlockSpec((tm, D), lambda i: (i, 0)),
            _full((ADA_CHUNKS, D)),
            _full((1, D)),
            pl.BlockSpec((None, D, D_FF), lambda i: (li, 0, 0), pipeline_mode=once),
            pl.BlockSpec((None, D, D_FF), lambda i: (li, 0, 1), pipeline_mode=once),
            pl.BlockSpec((None, D_FF, D), lambda i: (li, 0, 0), pipeline_mode=once),
        ],
        out_specs=pl.BlockSpec((tm, D), lambda i: (i, 0)),
        out_shape=jax.ShapeDtypeStruct((t, D), F32),
        compiler_params=_cparams(("arbitrary",)),
        name="ffn_dense",
    )(x, mod, g, w_gu, w_gu, w_down)


def _proj_body(gate_row, a_ref, w_ref, x_ref, mod_ref, o_ref):
    y = jnp.dot(a_ref[...], w_ref[...], preferred_element_type=F32)
    o_ref[...] = x_ref[...] + mod_ref[gate_row:gate_row + 1, :] * y


def proj_residual(a, w, x, mod, gate_row, tm):
    t, k = a.shape
    tm = min(tm, t)
    return pl.pallas_call(
        functools.partial(_proj_body, gate_row),
        grid=(t // tm,),
        in_specs=[
            pl.BlockSpec((tm, k), lambda i: (i, 0)),
            _full((k, D)),
            pl.BlockSpec((tm, D), lambda i: (i, 0)),
            _full((ADA_CHUNKS, D)),
        ],
        out_specs=pl.BlockSpec((tm, D), lambda i: (i, 0)),
        out_shape=jax.ShapeDtypeStruct((t, D), F32),
        compiler_params=_cparams(("arbitrary",)),
        name="proj_residual",
    )(a, w, x, mod)


HALO = SUBLANES
RG_IN_PIECE = 128


def _rg_in_body(tm, xp_ref, x_ref, xn_ref, mod_ref, g_ref, w_ref, cw_ref, cb_ref, xc_ref, gg_ref):
    i = pl.program_id(0)
    last = pl.num_programs(0) - 1
    xa = jnp.concatenate([xp_ref[...], x_ref[...], xn_ref[...]], axis=0)
    npiece = tm // RG_IN_PIECE
    bounds = [0] + [2 * HALO + RG_IN_PIECE * (k + 1) for k in range(npiece - 1)] + [tm + 2 * HALO]
    zs = []
    for k in range(npiece):
        hk = _normmod(xa[bounds[k]:bounds[k + 1]], g_ref[...], mod_ref[1:2, :], mod_ref[0:1, :]).astype(BF16)
        zs.append(jnp.dot(hk, w_ref[...], preferred_element_type=F32))
    z = jnp.concatenate(zs, axis=0)
    row = lax.broadcasted_iota(jnp.int32, (tm + 2 * HALO, 1), 0)
    valid = jnp.logical_and(jnp.logical_or(row >= HALO, i > 0),
                            jnp.logical_or(row < tm + HALO, i < last))
    xz = jnp.where(valid, z[:, :D], 0.0)
    y = cb_ref[...] + cw_ref[2:3, :] * xz[HALO:HALO + tm]
    y = y + cw_ref[0:1, :] * xz[HALO - 2:HALO - 2 + tm]
    y = y + cw_ref[1:2, :] * xz[HALO - 1:HALO - 1 + tm]
    y = y + cw_ref[3:4, :] * xz[HALO + 1:HALO + 1 + tm]
    xc_ref[...] = y
    gg_ref[...] = _gelu_tanh(z[HALO:HALO + tm, D:]).astype(BF16)


def rg_in(x, mod, g, w_in, conv_w, conv_b, tm):
    t = x.shape[0]
    nb = tm // HALO
    nblk = t // HALO
    return pl.pallas_call(
        functools.partial(_rg_in_body, tm),
        grid=(t // tm,),
        in_specs=[
            pl.BlockSpec((HALO, D), lambda i: (jnp.maximum(i * nb - 1, 0), 0)),
            pl.BlockSpec((tm, D), lambda i: (i, 0)),
            pl.BlockSpec((HALO, D), lambda i: (jnp.minimum((i + 1) * nb, nblk - 1), 0)),
            _full((ADA_CHUNKS, D)),
            _full((1, D)),
            _full((D, 2 * D)),
            _full((4, D)),
            _full((1, D)),
        ],
        out_specs=[pl.BlockSpec((tm, D), lambda i: (i, 0)), pl.BlockSpec((tm, D), lambda i: (i, 0))],
        out_shape=[jax.ShapeDtypeStruct((t, D), F32), jax.ShapeDtypeStruct((t, D), BF16)],
        compiler_params=_cparams(("arbitrary",)),
        name="rg_in",
    )(x, x, x, mod, g, w_in, conv_w, conv_b)


def _rg_gates(xc, wa_ref, wi_ref, ba, bi, lam):
    xb = xc.astype(BF16)
    nblk = D // RG_BLOCK_W
    r = jnp.concatenate([jnp.dot(xb[:, n * RG_BLOCK_W:(n + 1) * RG_BLOCK_W], wa_ref[n],
                                 preferred_element_type=F32) for n in range(nblk)], axis=1)
    ig = jnp.concatenate([jnp.dot(xb[:, n * RG_BLOCK_W:(n + 1) * RG_BLOCK_W], wi_ref[n],
                                  preferred_element_type=F32) for n in range(nblk)], axis=1)
    t_r = jnp.tanh(r + 0.5 * ba)
    t_i = jnp.tanh(ig + 0.5 * bi)
    nl = -lam
    softplus = jnp.maximum(nl, 0.0) + jnp.log1p(jnp.exp(-jnp.abs(nl)))
    half_c = (-0.5 * LRU_C) * softplus
    log_a = half_c + half_c * t_r
    a = jnp.exp(log_a)
    xh = 0.5 * xc
    b = jnp.sqrt(1.0 - a * a) * (xh + xh * t_i)
    return a, b


def _rg_scan_body(reverse, epilogue, emit_h, nside, tc, *refs):
    xc_ref, wa_ref, wi_ref, ba_ref, bi_ref, lam_ref, h0_ref = refs[:7]
    refs = refs[7:]
    if epilogue:
        hf_ref, gg_ref, wo_ref, x_ref, mod_ref = refs[:5]
        refs = refs[5:]
    side_in, refs = refs[:nside], refs[nside:]
    if emit_h:
        h_ref = refs[0]
        refs = refs[1:]
    if epilogue:
        o_ref = refs[0]
        refs = refs[1:]
    side_out, refs = refs[:nside], refs[nside:]
    a_scr, b_scr, h_scr, carry_scr = refs
    c = pl.program_id(0)
    for src, dst in zip(side_in, side_out):
        dst[...] = src[...].astype(BF16)

    @pl.when(c == 0)
    def _():
        carry_scr[...] = jnp.broadcast_to(h0_ref[...], (SUBLANES, D))

    a, b = _rg_gates(xc_ref[...], wa_ref, wi_ref, ba_ref[...], bi_ref[...], lam_ref[...])
    a_scr[...] = a
    b_scr[...] = b
    nblk = tc // SUBLANES
    row = lax.broadcasted_iota(jnp.int32, (SUBLANES, D), 0)
    first = (row == SUBLANES - 1) if reverse else (row == 0)

    def block(n, carry):
        blk = (nblk - 1 - n) if reverse else n
        off = pl.multiple_of(blk * SUBLANES, SUBLANES)
        av = a_scr[pl.ds(off, SUBLANES), :]
        bv = b_scr[pl.ds(off, SUBLANES), :]
        bv = jnp.where(first, av * carry + bv, bv)
        av = jnp.where(first, 0.0, av)
        for k in (1, 2, 4):
            shift = (SUBLANES - k) if reverse else k
            bv = av * pltpu.roll(bv, shift, 0) + bv
            if k != 4:
                av = av * pltpu.roll(av, shift, 0)
        h_scr[pl.ds(off, SUBLANES), :] = bv
        edge = bv[0:1, :] if reverse else bv[SUBLANES - 1:SUBLANES, :]
        return jnp.broadcast_to(edge, (SUBLANES, D))

    carry_scr[...] = lax.fori_loop(0, nblk, block, carry_scr[...], unroll=2)

    if emit_h:
        h_ref[...] = h_scr[...].astype(h_ref.dtype)
    if epilogue:
        y = ((hf_ref[...].astype(F32) + h_scr[...]) * gg_ref[...].astype(F32)).astype(BF16)
        o_ref[...] = x_ref[...] + mod_ref[2:3, :] * jnp.dot(y, wo_ref[...], preferred_element_type=F32)


def rg_scan(xconv, wa, wi, ba, bi, lam, h0, tc, reverse, epi=None, h_dtype=F32, side=()):
    t = xconv.shape[0]
    nchunks = t // tc
    side_specs = []
    for a in side:
        assert a.shape[0] % (nchunks * BF16_ROWS) == 0
        side_specs.append(pl.BlockSpec((a.shape[0] // nchunks, a.shape[1]), lambda c: (c, 0)))
    idx = (lambda c: (nchunks - 1 - c, 0)) if reverse else (lambda c: (c, 0))
    nb = D // RG_BLOCK_W
    blk = pl.BlockSpec((tc, D), idx)
    in_specs = [
        blk,
        _full((nb, RG_BLOCK_W, RG_BLOCK_W)),
        _full((nb, RG_BLOCK_W, RG_BLOCK_W)),
        _full((1, D)), _full((1, D)), _full((1, D)), _full((1, D)),
    ]
    args = [xconv, wa, wi, ba, bi, lam, h0]
    out_specs = []
    out_shape = []
    if epi is not None:
        hf, gg, w_out, x, mod = epi
        in_specs += [blk, blk, _full((D, D)), blk, _full((ADA_CHUNKS, D))]
        args += [hf, gg, w_out, x, mod]
    in_specs += side_specs
    args += list(side)
    if h_dtype is not None:
        out_specs.append(blk)
        out_shape.append(jax.ShapeDtypeStruct((t, D), h_dtype))
    if epi is not None:
        out_specs.append(blk)
        out_shape.append(jax.ShapeDtypeStruct((t, D), F32))
    out_specs += side_specs
    out_shape += [jax.ShapeDtypeStruct(a.shape, BF16) for a in side]
    return pl.pallas_call(
        functools.partial(_rg_scan_body, reverse, epi is not None, h_dtype is not None, len(side), tc),
        grid=(nchunks,),
        in_specs=in_specs,
        out_specs=out_specs,
        out_shape=out_shape,
        scratch_shapes=[pltpu.VMEM((tc, D), F32), pltpu.VMEM((tc, D), F32), pltpu.VMEM((tc, D), F32),
                        pltpu.VMEM((SUBLANES, D), F32)],
        compiler_params=_cparams(("arbitrary",)),
        name="rg_scan_bwd" if reverse else "rg_scan_fwd",
    )(*args)


def rglru_layer(x, xc, mod, modc, g, w_in, conv_w, conv_b, wa, wi, ba, bi, lam, w_out, need_ctx, side=()):
    w_in_b = w_in.astype(BF16)
    wa_b = (0.5 * wa).astype(BF16)
    wi_b = (0.5 * wi).astype(BF16)
    w_out_b = w_out.astype(BF16)
    cb = conv_b[None]
    tcx = xc.shape[0]
    xcl, ggl = rg_in(x, mod, g, w_in_b, conv_w, cb, WIDE_TILE)
    xcc, ggc = rg_in(xc, modc, g, w_in_b, conv_w, cb, tcx)
    zeros = jnp.zeros((1, D), F32)
    p = lambda d: (wa_b[d], wi_b[d], ba[d][None], bi[d][None], lam[d][None])
    (hcf,) = rg_scan(xcc, *p(0), zeros, tcx, False)
    hlf, *side_b = rg_scan(xcl, *p(0), hcf[tcx - 1:tcx], WIDE_TILE, False, h_dtype=BF16, side=side)
    if need_ctx:
        hcb, xc_new = rg_scan(xcc, *p(1), zeros, tcx, True, epi=(hcf, ggc, w_out_b, xc, modc))
    else:
        (hcb,) = rg_scan(xcc, *p(1), zeros, tcx, True)
        xc_new = None
    (x_new,) = rg_scan(xcl, *p(1), hcb[0:1], WIDE_TILE, True, epi=(hlf, ggl, w_out_b, x, mod), h_dtype=None)
    return x_new, xc_new, side_b


def _qkv_body(x_ref, mod_ref, g_ref, w_ref, gm_ref, qg_ref, kg_ref, q_ref, k_ref, v_ref):
    h = _normmod(x_ref[...], g_ref[...], mod_ref[1:2, :], mod_ref[0:1, :]).astype(BF16)
    z = jnp.dot(h, w_ref[...], preferred_element_type=F32)

    def headnorm(v, gain):
        ms = jnp.dot((v * v).astype(BF16), gm_ref[...], preferred_element_type=F32)
        return (v * lax.rsqrt(ms + RMS_EPS)) * gain

    q_ref[...] = headnorm(z[:, :D], qg_ref[...]).astype(BF16)
    k_ref[...] = headnorm(z[:, D:2 * D], kg_ref[...]).astype(BF16)
    v_ref[...] = z[:, 2 * D:].astype(BF16)


def qkv_proj(x, mod, g, w_qkv, gmean, qg, kg, tm):
    t = x.shape[0]
    spec = pl.BlockSpec((tm, D), lambda i: (i, 0))
    return pl.pallas_call(
        _qkv_body,
        grid=(t // tm,),
        in_specs=[spec, _full((ADA_CHUNKS, D)), _full((1, D)), _full((D, 3 * D)), _full((D, D)),
                  _full((1, D)), _full((1, D))],
        out_specs=[spec, spec, spec],
        out_shape=[jax.ShapeDtypeStruct((t, D), BF16)] * 3,
        compiler_params=_cparams(("arbitrary",)),
        name="qkv_proj",
    )(x, mod, g, w_qkv, gmean, qg, kg)


def _attend_pair(q2, keys, vals, biases):
    m_rows = q2.shape[0]
    lane = lax.broadcasted_iota(jnp.int32, q2.shape, 1)
    zero = jnp.zeros_like(q2)
    qs = jnp.concatenate([jnp.where(lane < NA_HEAD_DIM, q2, zero), jnp.where(lane >= NA_HEAD_DIM, q2, zero)], axis=0)
    ss = []
    for kseg, bseg in zip(keys, biases):
        s = lax.dot_general(qs, kseg, (((1,), (1,)), ((), ())), preferred_element_type=F32)
        if bseg is not None:
            s = s + jnp.concatenate([bseg[0], bseg[1]], axis=0)
        ss.append(s)
    m = ss[0].max(axis=-1, keepdims=True)
    for s in ss[1:]:
        m = jnp.maximum(m, s.max(axis=-1, keepdims=True))
    acc = None
    for s, vseg in zip(ss, vals):
        p = jnp.exp2(s - m)
        vaug = jnp.concatenate([vseg, jnp.ones_like(vseg)], axis=1)
        o = jnp.dot(p.astype(BF16), vaug, preferred_element_type=F32)
        acc = o if acc is None else acc + o
    out = acc[:, :LANES] / acc[:, LANES:]
    return jnp.where(lane < NA_HEAD_DIM, out[:m_rows], out[m_rows:])


NA_QROWS = 2
NA_UNION = NA_ROWS + NA_QROWS - 1


def _na_body(nside, var_ref, q_ref, kl_ref, vl_ref, kc_ref, vc_ref, bias_ref, wo_ref, x_ref, mod_ref, *refs):
    side_in, xo_ref, side_out, o_scr = refs[:nside], refs[nside], refs[nside + 1:-1], refs[-1]
    npair = NA_HEADS // 2
    for pr in range(npair):
        sl = slice(pr * LANES, (pr + 1) * LANES)
        o_scr[:, sl] = _attend_pair(
            q_ref[:, sl], [kl_ref[:, sl], kc_ref[:, sl]], [vl_ref[:, sl], vc_ref[:, sl]],
            [(bias_ref[0, 2 * pr], bias_ref[0, 2 * pr + 1]), None]).astype(BF16)
        for src, dst in zip(side_in, side_out):
            slab = -(-src.shape[0] // (npair * BF16_ROWS)) * BF16_ROWS
            lo = min(pr * slab, src.shape[0])
            hi = min(lo + slab, src.shape[0])
            if hi > lo:
                dst[lo:hi, :] = src[lo:hi, :].astype(BF16)
    y = jnp.dot(o_scr[...], wo_ref[...], preferred_element_type=F32)
    xo_ref[...] = x_ref[...] + mod_ref[2:3, :] * y


def _na_geometry(rows):
    steps = rows // NA_QROWS
    g = np.arange(steps)
    base = np.clip(NA_QROWS * g - NA_ROWS // 2, 0, rows - NA_UNION)
    r = NA_QROWS * g[:, None] + np.arange(NA_QROWS)[None, :]
    rs = np.clip(r - NA_ROWS // 2, 0, rows - NA_ROWS)
    key = np.concatenate([(base - NA_QROWS * g)[:, None], rs - r], axis=1)
    uniq, first, var = np.unique(key, axis=0, return_index=True, return_inverse=True)
    return base, var.reshape(-1).astype(np.int32), g[first]


def na_attention(q, k, v, kc, vc, bias_tab, var, w_o, x, mod, side=()):
    t = q.shape[0]
    rows = t // GRID_W
    nctx = kc.shape[0]
    steps = rows // NA_QROWS
    side_specs = []
    for a in side:
        assert a.shape[0] % (steps * 16) == 0
        side_specs.append(pl.BlockSpec((a.shape[0] // steps, a.shape[1]), lambda g, var: (g, 0)))

    def kbase(g):
        return jnp.clip(NA_QROWS * g - NA_ROWS // 2, 0, rows - NA_UNION)

    qrows = NA_QROWS * GRID_W
    nloc = NA_UNION * GRID_W
    kspec = pl.BlockSpec((pl.Element(nloc), pl.Element(D)), lambda g, var: (kbase(g) * GRID_W, 0))
    grid_spec = pltpu.PrefetchScalarGridSpec(
        num_scalar_prefetch=1,
        grid=(steps,),
        in_specs=[pl.BlockSpec((qrows, D), lambda g, var: (g, 0)), kspec, kspec] + [
            pl.BlockSpec((nctx, D), lambda g, var: (0, 0)), pl.BlockSpec((nctx, D), lambda g, var: (0, 0)),
            pl.BlockSpec((1, NA_HEADS, qrows, nloc), lambda g, var: (var[g], 0, 0, 0)),
            pl.BlockSpec((D, D), lambda g, var: (0, 0)),
            pl.BlockSpec((qrows, D), lambda g, var: (g, 0)),
            pl.BlockSpec((ADA_CHUNKS, D), lambda g, var: (0, 0)),
        ] + side_specs,
        out_specs=[pl.BlockSpec((qrows, D), lambda g, var: (g, 0))] + side_specs,
        scratch_shapes=[pltpu.VMEM((qrows, D), BF16)],
    )
    return pl.pallas_call(
        functools.partial(_na_body, len(side)),
        grid_spec=grid_spec,
        out_shape=[jax.ShapeDtypeStruct((t, D), F32)] + [jax.ShapeDtypeStruct(a.shape, BF16) for a in side],
        compiler_params=_cparams(("arbitrary",)),
        name="na_attention",
    )(var, q, k, v, kc, vc, bias_tab, w_o, x, mod, *side)


def _ctx_attn_body(q_ref, k_ref, v_ref, o_ref):
    for pr in range(NA_HEADS // 2):
        sl = slice(pr * LANES, (pr + 1) * LANES)
        o_ref[:, sl] = _attend_pair(q_ref[:, sl], [k_ref[:, sl]], [v_ref[:, sl]], [None]).astype(BF16)


def ctx_attention(q, k, v):
    t = q.shape[0]
    return pl.pallas_call(
        _ctx_attn_body,
        grid=(1,),
        in_specs=[_full((t, D))] * 3,
        out_specs=_full((t, D)),
        out_shape=jax.ShapeDtypeStruct((t, D), BF16),
        compiler_params=_cparams(("arbitrary",)),
        name="ctx_attention",
    )(q, k, v)


def _na_bias_table(rpb, rows):
    base, var, reps = _na_geometry(rows)
    cols = np.arange(GRID_W)
    cstart = np.clip(cols - NA_COLS // 2, 0, GRID_W - NA_COLS)
    kcol = np.arange(GRID_W)
    inwin = (kcol[None, :] >= cstart[:, None]) & (kcol[None, :] < cstart[:, None] + NA_COLS)
    r = NA_QROWS * reps[:, None] + np.arange(NA_QROWS)[None, :]
    rs = np.clip(r - NA_ROWS // 2, 0, rows - NA_ROWS)
    krow = base[reps][:, None] + np.arange(NA_UNION)[None, :]
    rvalid = (krow[:, None, :] >= rs[:, :, None]) & (krow[:, None, :] < rs[:, :, None] + NA_ROWS)
    ridx = np.clip(krow[:, None, :] - r[:, :, None] + (NA_ROWS - 1), 0, 2 * NA_ROWS - 2)
    nd = 2 * NA_COLS - 1
    w = jnp.pad(rpb.astype(F32), ((0, 0), (0, 0), (GRID_W - NA_COLS, 2 * GRID_W - (GRID_W - NA_COLS) - nd)))
    flat = jnp.tile(w, (1, 1, GRID_W))[:, :, :GRID_W * (2 * GRID_W - 1)]
    blk = flat.reshape(NA_HEADS, 2 * NA_ROWS - 1, GRID_W, 2 * GRID_W - 1)[..., GRID_W - 1:]
    blk = jnp.where(jnp.asarray(inwin)[None, None], blk, NEG_BIG)
    neg = jnp.full((NA_HEADS, GRID_W, GRID_W), NEG_BIG, F32)
    variants = []
    for v in range(len(reps)):
        strips = [jnp.concatenate([blk[:, ridx[v, a, j]] if rvalid[v, a, j] else neg for j in range(NA_UNION)], axis=2)
                  for a in range(NA_QROWS)]
        variants.append(jnp.concatenate(strips, axis=1))
    return jnp.stack(variants, axis=0), jnp.asarray(var)


def na_layer(x, xc, mod, modc, g, w_qkv, q_g, k_g, rpb, w_o, need_ctx, side=()):
    w_qkv_b = w_qkv.astype(BF16)
    w_o_b = w_o.astype(BF16)
    gm = np.kron(np.eye(NA_HEADS), np.full((NA_HEAD_DIM, NA_HEAD_DIM), 1.0 / NA_HEAD_DIM))
    gmean = jnp.asarray(gm, dtype=BF16)
    qg = jnp.tile(q_g, NA_HEADS)[None] * (NA_HEAD_DIM ** -0.5 * LOG2E)
    kg = jnp.tile(k_g, NA_HEADS)[None]
    q, k, v = qkv_proj(x, mod, g, w_qkv_b, gmean, qg, kg, WIDE_TILE)
    qc, kc, vc = qkv_proj(xc, modc, g, w_qkv_b, gmean, qg, kg, xc.shape[0])
    bias_tab, var = _na_bias_table(rpb * LOG2E, x.shape[0] // GRID_W)
    x_new, *side_b = na_attention(q, k, v, kc, vc, bias_tab, var, w_o_b, x, mod, side)
    xc_new = None
    if need_ctx:
        oc = ctx_attention(qc, kc, vc)
        xc_new = proj_residual(oc, w_o_b, xc, modc, 2, xc.shape[0])
    return x_new, xc_new, side_b


def _dft_mats(n):
    ang = 2.0 * np.pi * np.outer(np.arange(n), np.arange(n)) / n
    return np.cos(ang), np.sin(ang)


def _channel_dft(h, wc):
    us = [jnp.dot(h[:, gi * FT_GROUP_W:(gi + 1) * FT_GROUP_W], wc, preferred_element_type=F32).astype(BF16)
          for gi in range(D // FT_GROUP_W)]
    return jnp.concatenate([u[:, :FT_GROUP_W] for u in us] + [u[:, FT_GROUP_W:] for u in us], axis=1)


def _ft_a_body(n, nj, x_ref, mod_ref, g_ref, perm_ref, wc_ref, ma_ref, yr_ref, yi_ref):
    h3 = _normmod(x_ref[...], g_ref[...], mod_ref[1:2, :], mod_ref[0:1, :])
    h = jnp.dot(perm_ref[...], h3.reshape(n * nj, D).astype(BF16), preferred_element_type=F32).astype(BF16)
    u = _channel_dft(h, wc_ref[...])
    for j in range(nj):
        uj = u[j * n:(j + 1) * n]
        y = jnp.dot(ma_ref[...], jnp.concatenate([uj[:, :D], uj[:, D:]], axis=0), preferred_element_type=F32)
        yr_ref[:, j, :] = y[:n]
        yi_ref[:, j, :] = y[n:]


def _ft_c_body(n, nj, yr_ref, yi_ref, mc_ref, wf_ref, x_ref, mod_ref, o_ref):
    fs = []
    for j in range(nj):
        ys = jnp.concatenate([yr_ref[j].astype(BF16), yi_ref[j].astype(BF16)], axis=0)
        fs.append(jnp.dot(mc_ref[j], ys, preferred_element_type=F32).astype(BF16))
    z = jnp.dot(jnp.concatenate(fs, axis=0), wf_ref[...], preferred_element_type=F32)
    gate = mod_ref[2:3, :]
    for j in range(nj):
        o_ref[:, j, :] = x_ref[:, j, :] + gate * z[j * n:(j + 1) * n]


def _ft_ctx_body(x_ref, mod_ref, g_ref, wc_ref, ml_ref, wf_ref, o_ref):
    x = x_ref[...]
    h = _normmod(x, g_ref[...], mod_ref[1:2, :], mod_ref[0:1, :]).astype(BF16)
    u = _channel_dft(h, wc_ref[...])
    us = jnp.concatenate([u[:, :D], u[:, D:]], axis=0)
    f = jnp.dot(ml_ref[...], us, preferred_element_type=F32).astype(BF16)
    o_ref[...] = x + mod_ref[2:3, :] * jnp.dot(f, wf_ref[...], preferred_element_type=F32)


def fourier_layer(x, xc, mod, modc, g, w_f, need_ctx):
    t = x.shape[0]
    n = math.isqrt(t)
    assert n * n == t and n % 16 == 0
    w_f_b = w_f.astype(BF16)
    cw, sw = _dft_mats(FT_GROUP_W)
    wc = jnp.asarray(np.concatenate([cw, -sw], axis=1) / math.sqrt(FT_GROUP_W), dtype=F32).astype(BF16)
    wcspec = _full((FT_GROUP_W, 2 * FT_GROUP_W))
    cn, sn = _dft_mats(n)
    ma = jnp.asarray(np.block([[cn, sn], [-sn, cn]]) / math.sqrt(n), dtype=F32).astype(BF16)
    nj = 8
    ang = 2.0 * np.pi * np.outer(np.arange(n), np.arange(n)) / t
    cnj, snj = jnp.asarray(cn, dtype=F32)[None], jnp.asarray(sn, dtype=F32)[None]
    tcj = jnp.asarray(np.cos(ang).T, dtype=F32)[:, None, :]
    tsj = jnp.asarray(np.sin(ang).T, dtype=F32)[:, None, :]
    mc = (jnp.concatenate([cnj * tcj - snj * tsj, cnj * tsj + snj * tcj], axis=2) / math.sqrt(n)).astype(BF16)
    xblk = pl.BlockSpec((n, nj, D), lambda b: (0, b, 0))
    yblk = pl.BlockSpec((nj, n, D), lambda b: (b, 0, 0))
    mblk = pl.BlockSpec((nj, n, 2 * n), lambda b: (b, 0, 0))
    x3 = x.reshape(n, n, D)
    src = (np.arange(n)[None, :] * nj + np.arange(nj)[:, None]).reshape(-1)
    perm = jnp.asarray(np.eye(n * nj)[src], dtype=BF16)
    yr, yi = pl.pallas_call(
        functools.partial(_ft_a_body, n, nj),
        grid=(n // nj,),
        in_specs=[xblk, _full((ADA_CHUNKS, D)), _full((1, D)), _full((n * nj, n * nj)), wcspec,
                  _full((2 * n, 2 * n))],
        out_specs=[xblk, xblk],
        out_shape=[jax.ShapeDtypeStruct((n, n, D), F32)] * 2,
        compiler_params=_cparams(("arbitrary",)),
        name="ft_stage_a",
    )(x3, mod, g, perm, wc, ma)
    x_new = pl.pallas_call(
        functools.partial(_ft_c_body, n, nj),
        grid=(n // nj,),
        in_specs=[yblk, yblk, mblk, _full((D, D)), xblk, _full((ADA_CHUNKS, D))],
        out_specs=xblk,
        out_shape=jax.ShapeDtypeStruct((n, n, D), F32),
        compiler_params=_cparams(("arbitrary",)),
        name="ft_stage_c",
    )(yr, yi, mc, w_f_b, x3, mod).reshape(t, D)
    xc_new = None
    if need_ctx:
        lc = xc.shape[0]
        cl, sl = _dft_mats(lc)
        ml = jnp.asarray(np.concatenate([cl, sl], axis=1) / math.sqrt(lc), dtype=F32).astype(BF16)
        xc_new = pl.pallas_call(
            _ft_ctx_body,
            grid=(1,),
            in_specs=[_full((lc, D)), _full((ADA_CHUNKS, D)), _full((1, D)), wcspec,
                      _full((lc, 2 * lc)), _full((D, D))],
            out_specs=_full((lc, D)),
            out_shape=jax.ShapeDtypeStruct((lc, D), F32),
            compiler_params=_cparams(("arbitrary",)),
            name="ft_ctx",
        )(xc, modc, g, wc, ml, w_f_b)
    return x_new, xc_new


def _router_body(x_ref, mod_ref, g_ref, r_ref, info_ref, w0_ref, w1_ref):
    h = _normmod(x_ref[...], g_ref[...], mod_ref[4:5, :], mod_ref[3:4, :])
    hh = h.astype(BF16)
    hl = (h - hh.astype(F32)).astype(BF16)
    r = r_ref[...]
    rh = r.astype(BF16)
    rl = (r - rh.astype(F32)).astype(BF16)
    logits = (jnp.dot(hh, rh, preferred_element_type=F32) + jnp.dot(hh, rl, preferred_element_type=F32)
              + jnp.dot(hl, rh, preferred_element_type=F32))
    lane = lax.broadcasted_iota(jnp.int32, logits.shape, 1)
    logits = jnp.where(lane < N_EXPERTS, logits, NEG_BIG)
    v0 = jnp.max(logits, axis=-1, keepdims=True)
    i0 = jnp.min(jnp.where(logits == v0, lane, LANES), axis=-1, keepdims=True)
    rest = jnp.where(lane == i0, NEG_BIG, logits)
    v1 = jnp.max(rest, axis=-1, keepdims=True)
    i1 = jnp.min(jnp.where(rest == v1, lane, LANES), axis=-1, keepdims=True)
    e = jnp.exp(v1 - v0)
    w0 = 1.0 / (1.0 + e)
    w1 = e / (1.0 + e)
    info_ref[...] = jnp.where(lane == 0, i0, jnp.where(lane == 1, i1, 0))
    w0_ref[...] = jnp.broadcast_to(w0, logits.shape)
    w1_ref[...] = jnp.broadcast_to(w1, logits.shape)


def moe_router(x, mod, g, router_pad, tm):
    t = x.shape[0]
    tm = min(tm, t)
    spec = pl.BlockSpec((tm, D), lambda i: (i, 0))
    lspec = pl.BlockSpec((tm, LANES), lambda i: (i, 0))
    return pl.pallas_call(
        _router_body,
        grid=(t // tm,),
        in_specs=[spec, _full((ADA_CHUNKS, D)), _full((1, D)), _full((D, LANES))],
        out_specs=[lspec, lspec, lspec],
        out_shape=[jax.ShapeDtypeStruct((t, LANES), jnp.int32),
                   jax.ShapeDtypeStruct((t, LANES), F32), jax.ShapeDtypeStruct((t, LANES), F32)],
        compiler_params=_cparams(("arbitrary",)),
        name="moe_router",
    )(x, mod, g, router_pad)


SCATTER_TOKENS = 256


def _row_scatter_body(nlat, nctx, didx_ref, g_ref, x_ref, mod_ref, *rest):
    if nctx:
        xc_ref, modc_ref, dst_ref, h_scr, zero_scr, sems = rest
    else:
        dst_ref, h_scr, zero_scr, sems = rest
    i = pl.program_id(0)
    nsteps = pl.num_programs(0)
    ts = SCATTER_TOKENS
    slot = i % 2

    def start_all(src_ref, src_is_zero_rows):
        def issue(grp, c):
            base = pl.multiple_of(grp * SUBLANES, SUBLANES)
            for r in range(SUBLANES):
                src = src_ref.at[pl.ds(r if src_is_zero_rows else base + r, 1), :]
                for half in range(2):
                    d = didx_ref[0, 0, base + r + half * ts]
                    pltpu.make_async_copy(src, dst_ref.at[pl.ds(d, 1), :], sems.at[slot]).start(priority=half)
            return c
        lax.fori_loop(0, ts // SUBLANES, issue, 0)

    def wait_all(which):
        def drain(n, c):
            pltpu.make_async_copy(zero_scr.at[pl.ds(0, 1), :], dst_ref.at[pl.ds(0, 1), :], sems.at[which]).wait()
            return c
        lax.fori_loop(0, 2 * ts, drain, 0, unroll=8)

    @pl.when(i == 0)
    def _():
        zero_scr[...] = jnp.zeros_like(zero_scr)

    def stage_and_start(src_ref, m_ref):
        h_scr[slot] = _normmod(src_ref[...], g_ref[...], m_ref[4:5, :], m_ref[3:4, :])
        start_all(h_scr.at[slot], False)

    @pl.when(i < nlat)
    def _():
        stage_and_start(x_ref, mod_ref)

    if nctx:
        @pl.when(jnp.logical_and(i >= nlat, i < nlat + nctx))
        def _():
            stage_and_start(xc_ref, modc_ref)

    @pl.when(i >= nlat + nctx)
    def _():
        start_all(zero_scr, True)

    @pl.when(i > 0)
    def _():
        wait_all(1 - slot)

    @pl.when(i == nsteps - 1)
    def _():
        wait_all(slot)


def row_scatter(x, xc, mod, modc, g, d0, d1, pad_pos):
    ts = SCATTER_TOKENS
    nlat = x.shape[0] // ts
    nctx = 0 if xc is None else 1
    assert xc is None or xc.shape[0] == ts
    ntok = nlat + nctx
    npad = pad_pos.shape[0] // (2 * ts)
    didx = jnp.concatenate([jnp.concatenate([d0.reshape(ntok, 1, ts), d1.reshape(ntok, 1, ts)], axis=2),
                            pad_pos.reshape(npad, 1, 2 * ts)], axis=0)
    in_specs = [pl.BlockSpec((1, 1, 2 * ts), lambda i: (i, 0, 0), memory_space=pltpu.SMEM),
                _full((1, D)),
                pl.BlockSpec((ts, D), lambda i: (jnp.minimum(i, nlat - 1), 0)),
                _full((ADA_CHUNKS, D))]
    args = [didx, g, x, mod]
    if nctx:
        in_specs += [_full((ts, D)), _full((ADA_CHUNKS, D))]
        args += [xc, modc]
    return pl.pallas_call(
        functools.partial(_row_scatter_body, nlat, nctx),
        grid=(ntok + npad,),
        in_specs=in_specs,
        out_specs=pl.BlockSpec(memory_space=pl.ANY),
        out_shape=jax.ShapeDtypeStruct((2 * ntok * ts + pad_pos.shape[0], D), F32),
        scratch_shapes=[pltpu.VMEM((2, ts, D), F32), pltpu.VMEM((SUBLANES, D), F32), pltpu.SemaphoreType.DMA((2,))],
        compiler_params=_cparams(("arbitrary",)),
        name="moe_row_scatter",
    )(*args)


def _moe_ffn_body(te_ref, tv_ref, xg_ref, wg_ref, wu_ref, wd_ref, o_ref):
    i = pl.program_id(0)

    @pl.when(tv_ref[i] > 0)
    def _():
        o_ref[...] = _swiglu_chunks(xg_ref[...].astype(BF16), wg_ref, wu_ref, wd_ref)

    @pl.when(tv_ref[i] == 0)
    def _():
        o_ref[...] = jnp.zeros_like(o_ref)


def moe_ffn(xg, tile_e, tile_v, w_gu, w_down, li, tm):
    p = xg.shape[0]
    grid_spec = pltpu.PrefetchScalarGridSpec(
        num_scalar_prefetch=2,
        grid=(p // tm,),
        in_specs=[
            pl.BlockSpec((tm, D), lambda i, te, tv: (i, 0)),
            pl.BlockSpec((None, None, D, D_FF), lambda i, te, tv: (li, te[i], 0, 0)),
            pl.BlockSpec((None, None, D, D_FF), lambda i, te, tv: (li, te[i], 0, 1)),
            pl.BlockSpec((None, None, D_FF, D), lambda i, te, tv: (li, te[i], 0, 0)),
        ],
        out_specs=pl.BlockSpec((tm, D), lambda i, te, tv: (i, 0)),
    )
    return pl.pallas_call(
        _moe_ffn_body,
        grid_spec=grid_spec,
        out_shape=jax.ShapeDtypeStruct((p, D), F32),
        compiler_params=pltpu.CompilerParams(dimension_semantics=("arbitrary",), vmem_limit_bytes=MOE_VMEM_LIMIT),
        name="moe_ffn",
    )(tile_e, tile_v, xg, w_gu, w_gu, w_down)


def _combine_body(tt, d0_ref, d1_ref, d0n_ref, d1n_ref, yp_ref, x_ref, mod_ref, w0_ref, w1_ref, o_ref,
                  a_scr, b_scr, sems):
    i = pl.program_id(0)
    nsteps = pl.num_programs(0)
    slot = i % 2

    def start_all(i0_ref, i1_ref, which):
        def issue(grp, c):
            base = pl.multiple_of(grp * SUBLANES, SUBLANES)
            for r in range(SUBLANES):
                n = base + r
                pltpu.make_async_copy(yp_ref.at[pl.ds(i0_ref[0, 0, n], 1), :], a_scr.at[which, pl.ds(n, 1), :],
                                      sems.at[which]).start(priority=0)
                pltpu.make_async_copy(yp_ref.at[pl.ds(i1_ref[0, 0, n], 1), :], b_scr.at[which, pl.ds(n, 1), :],
                                      sems.at[which]).start(priority=1)
            return c
        lax.fori_loop(0, tt // SUBLANES, issue, 0)

    @pl.when(i == 0)
    def _():
        start_all(d0_ref, d1_ref, 0)

    @pl.when(i + 1 < nsteps)
    def _():
        start_all(d0n_ref, d1n_ref, 1 - slot)

    def drain(n, c):
        pltpu.make_async_copy(yp_ref.at[pl.ds(0, 1), :], a_scr.at[slot, pl.ds(0, 1), :], sems.at[slot]).wait()
        pltpu.make_async_copy(yp_ref.at[pl.ds(0, 1), :], b_scr.at[slot, pl.ds(0, 1), :], sems.at[slot]).wait()
        return c
    lax.fori_loop(0, tt, drain, 0, unroll=8)
    w0 = _lane_tile(w0_ref[...])
    w1 = _lane_tile(w1_ref[...])
    o_ref[...] = x_ref[...] + mod_ref[5:6, :] * (w0 * a_scr[slot] + w1 * b_scr[slot])


def moe_combine(yp, d0, d1, x, mod, w0b, w1b, tt):
    t = x.shape[0]
    nt = t // tt
    ispec = pl.BlockSpec((1, 1, tt), lambda i: (i, 0, 0), memory_space=pltpu.SMEM)
    nspec = pl.BlockSpec((1, 1, tt), lambda i: (jnp.minimum(i + 1, nt - 1), 0, 0), memory_space=pltpu.SMEM)
    spec = pl.BlockSpec((tt, D), lambda i: (i, 0))
    lspec = pl.BlockSpec((tt, LANES), lambda i: (i, 0))
    d0r, d1r = d0.reshape(nt, 1, tt), d1.reshape(nt, 1, tt)
    return pl.pallas_call(
        functools.partial(_combine_body, tt),
        grid=(nt,),
        in_specs=[ispec, ispec, nspec, nspec, pl.BlockSpec(memory_space=pl.ANY), spec, _full((ADA_CHUNKS, D)),
                  lspec, lspec],
        out_specs=spec,
        out_shape=jax.ShapeDtypeStruct((t, D), F32),
        scratch_shapes=[pltpu.VMEM((2, tt, D), F32), pltpu.VMEM((2, tt, D), F32), pltpu.SemaphoreType.DMA((2,))],
        compiler_params=_cparams(("arbitrary",)),
        name="moe_combine",
    )(d0r, d1r, d0r, d1r, yp, x, mod, w0b, w1b)


def _route_plan(e0, e1, tm):
    t = e0.shape[0]
    n = 2 * t
    ex = jnp.arange(N_EXPERTS, dtype=jnp.int32)
    oh0 = (e0[:, None] == ex[None, :]).astype(jnp.int32)
    oh1 = (e1[:, None] == ex[None, :]).astype(jnp.int32)
    both = oh0 + oh1
    csum = jnp.cumsum(both, axis=0)
    before = csum - both
    counts = csum[-1]
    padded = ((counts + tm - 1) // tm) * tm
    pad_end = jnp.cumsum(padded)
    pad_off = pad_end - padded
    total = pad_end[-1]
    d0 = jnp.sum(oh0 * (before + pad_off[None, :]), axis=1)
    d1 = jnp.sum(oh1 * (before + oh0 + pad_off[None, :]), axis=1)
    gap = padded - counts
    tail_off = jnp.cumsum(tm - gap) - (tm - gap)
    r = jnp.arange(tm, dtype=jnp.int32)[None, :]
    pad_pos = jnp.where(r < gap[:, None], (pad_off + counts)[:, None] + r,
                        total + tail_off[:, None] + (r - gap[:, None])).reshape(-1)
    ntiles = (n + N_EXPERTS * tm) // tm
    tstart = jnp.arange(ntiles, dtype=jnp.int32) * tm
    tile_v = (tstart < total).astype(jnp.int32)
    tile_e = jnp.sum((jnp.minimum(tstart, total - 1)[:, None] >= pad_end[None, :]).astype(jnp.int32), axis=1)
    return (d0.astype(jnp.int32), d1.astype(jnp.int32), pad_pos.astype(jnp.int32), tile_e.astype(jnp.int32), tile_v)


def moe_layer(x, xc, mod, modc, g, router, w_gu_b, w_down_b, li, need_ctx, tm=ROW_TILE):
    router_pad = jnp.pad(router, ((0, 0), (0, LANES - N_EXPERTS)))
    s = x.shape[0]
    info, w0b, w1b = moe_router(x, mod, g, router_pad, LIGHT_TILE)
    if need_ctx:
        sc = xc.shape[0]
        infoc, w0c, w1c = moe_router(xc, modc, g, router_pad, sc)
        e0 = jnp.concatenate([info[:, 0], infoc[:, 0]])
        e1 = jnp.concatenate([info[:, 1], infoc[:, 1]])
    else:
        e0, e1 = info[:, 0], info[:, 1]
    d0, d1, pad_pos, tile_e, tile_v = _route_plan(e0, e1, tm)
    xg = row_scatter(x, xc if need_ctx else None, mod, modc, g, d0, d1, pad_pos)
    yp = moe_ffn(xg, tile_e, tile_v, w_gu_b, w_down_b, li, tm)
    x_new = moe_combine(yp, d0[:s], d1[:s], x, mod, w0b, w1b, ROW_TILE)
    xc_new = None
    if need_ctx:
        xc_new = moe_combine(yp, d0[s:], d1[s:], xc, modc, w0c, w1c, sc)
    return x_new, xc_new


def kernel(x, c, ctx, c_ctx, ada_w, ada_b, norm_g, rg_w_in, rg_conv_w, rg_conv_b, rg_wa, rg_ba, rg_wi, rg_bi,
           rg_lambda, rg_w_out, na_w_qkv, na_q_g, na_k_g, na_rpb, na_w_o, ft_w_out, ffn_w_gu, ffn_w_down,
           moe_router, moe_w_gu, moe_w_down):
    depth = ada_w.shape[0]
    assert x.shape[0] == 1 and x.shape[2] == D
    xs = x[0]
    xc = ctx[0]
    mods = ada_modulation(c, c_ctx, ada_w, ada_b)
    ffn_gu_b = ffn_dn_b = moe_gu_b = moe_dn_b = None
    mix_idx = [0] * N_MIXERS
    dense_idx = 0
    moe_idx = 0
    for layer in range(depth):
        need_ctx = layer != depth - 1
        mod, modc = mods[layer, 0], mods[layer, 1]
        g0 = norm_g[layer, 0][None]
        g1 = norm_g[layer, 1][None]
        kind = layer % N_MIXERS
        j = mix_idx[kind]
        mix_idx[kind] += 1
        if kind == 0:
            side = ()
            if ffn_gu_b is None:
                side = (ffn_w_gu.reshape(-1, ffn_w_gu.shape[-1]), ffn_w_down.reshape(-1, ffn_w_down.shape[-1]))
            xs, xcn, side_b = rglru_layer(xs, xc, mod, modc, g0, rg_w_in[j], rg_conv_w[j], rg_conv_b[j], rg_wa[j],
                                          rg_wi[j], rg_ba[j], rg_bi[j], rg_lambda[j], rg_w_out[j], need_ctx, side)
            if side_b:
                ffn_gu_b, ffn_dn_b = side_b[0].reshape(ffn_w_gu.shape), side_b[1].reshape(ffn_w_down.shape)
        elif kind == 1:
            side = ()
            if moe_gu_b is None:
                side = (moe_w_gu.reshape(-1, moe_w_gu.shape[-1]), moe_w_down.reshape(-1, moe_w_down.shape[-1]))
            xs, xcn, side_b = na_layer(xs, xc, mod, modc, g0, na_w_qkv[j], na_q_g[j], na_k_g[j], na_rpb[j],
                                       na_w_o[j], need_ctx, side)
            if side_b:
                moe_gu_b, moe_dn_b = side_b[0].reshape(moe_w_gu.shape), side_b[1].reshape(moe_w_down.shape)
        else:
            xs, xcn = fourier_layer(xs, xc, mod, modc, g0, ft_w_out[j], need_ctx)
        if need_ctx:
            xc = xcn
        if layer % 2 == 0:
            if ffn_gu_b is None:
                ffn_gu_b, ffn_dn_b = ffn_w_gu.astype(BF16), ffn_w_down.astype(BF16)
            if need_ctx:
                xc = ffn_dense(xc, modc, g1, ffn_gu_b, ffn_dn_b, dense_idx, xc.shape[0])
            xs = ffn_dense(xs, mod, g1, ffn_gu_b, ffn_dn_b, dense_idx, ROW_TILE)
            dense_idx += 1
        else:
            if moe_gu_b is None:
                moe_gu_b, moe_dn_b = moe_w_gu.astype(BF16), moe_w_down.astype(BF16)
            xs, xcn = moe_layer(xs, xc, mod, modc, g1, moe_router[moe_idx], moe_gu_b, moe_dn_b, moe_idx, need_ctx)
            moe_idx += 1
            if need_ctx:
                xc = xcn
    return xs[None]
```

```python
import functools
import math

import numpy as np
import jax
import jax.numpy as jnp
from jax import lax
from jax.experimental import pallas as pl
from jax.experimental.pallas import tpu as pltpu

F32 = jnp.float32
BF16 = jnp.bfloat16

D = 1024
D_FF = 3584
N_EXPERTS = 8
GRID_W = 64
NA_HEADS = 16
NA_HEAD_DIM = 64
NA_ROWS = 8
NA_COLS = 16
FT_GROUP_W = 256
RG_BLOCK_W = 256
RMS_EPS = 1e-6
LRU_C = 8.0
N_MIXERS = 3
ADA_CHUNKS = 6

LANES = 128
SUBLANES = 8
BF16_ROWS = 16
VMEM_LIMIT = 56 * 1024 * 1024
MOE_VMEM_LIMIT = 60 * 1024 * 1024

ROW_TILE = 512
WIDE_TILE = 1024
LIGHT_TILE = 2048
NEG_BIG = -1e30
LOG2E = math.log2(math.e)


def _cparams(sem):
    return pltpu.CompilerParams(dimension_semantics=sem, vmem_limit_bytes=VMEM_LIMIT)


def _full(shape):
    nd = len(shape)
    return pl.BlockSpec(shape, lambda *_: (0,) * nd)


def _normmod(x, g, scale, shift):
    ms = jnp.mean(x * x, axis=-1, keepdims=True)
    y = x * lax.rsqrt(ms + RMS_EPS)
    return (y * g) * (1.0 + scale) + shift


def _lane_tile(v):
    return jnp.concatenate([v] * (D // LANES), axis=1)


def _sigmoid(v):
    return 1.0 / (1.0 + jnp.exp(-v))


def _gelu_tanh(v):
    c = math.sqrt(2.0 / math.pi)
    return v * (0.5 * (1.0 + jnp.tanh(c * (v + 0.044715 * (v * v * v)))))


def _ada_body(cin_ref, w_ref, b_ref, o_ref):
    v = cin_ref[...]
    s = v * _sigmoid(v)
    w = w_ref[0]
    r0 = jnp.sum(s[:, 0:1] * w, axis=0, keepdims=True)
    r1 = jnp.sum(s[:, 1:2] * w, axis=0, keepdims=True)
    o_ref[0] = jnp.concatenate([r0, r1], axis=0) + b_ref[0]


def ada_modulation(c, c_ctx, ada_w, ada_b):
    depth = ada_w.shape[0]
    n = ada_w.shape[2]
    nc = n // 4
    cin = jnp.stack([c[0], c_ctx], axis=1)
    out = pl.pallas_call(
        _ada_body,
        grid=(depth, n // nc),
        in_specs=[
            pl.BlockSpec((D, 2), lambda l, j: (0, 0)),
            pl.BlockSpec((1, D, nc), lambda l, j: (l, 0, j)),
            pl.BlockSpec((1, 1, nc), lambda l, j: (l, 0, j)),
        ],
        out_specs=pl.BlockSpec((1, 2, nc), lambda l, j: (l, 0, j)),
        out_shape=jax.ShapeDtypeStruct((depth, 2, n), F32),
        compiler_params=_cparams(("arbitrary", "arbitrary")),
        name="ada_mod",
    )(cin, ada_w, ada_b.reshape(depth, 1, n))
    return out.reshape(depth, 2, ADA_CHUNKS, D)


FFN_CHUNK = 512


def _swiglu_chunks(h, wg_ref, wu_ref, wd_ref):
    acc = None
    for c in range(D_FF // FFN_CHUNK):
        sl = slice(c * FFN_CHUNK, (c + 1) * FFN_CHUNK)
        gg = jnp.dot(h, wg_ref[:, sl], preferred_element_type=F32)
        uu = jnp.dot(h, wu_ref[:, sl], preferred_element_type=F32)
        a = ((gg * _sigmoid(gg)) * uu).astype(BF16)
        part = jnp.dot(a, wd_ref[sl, :], preferred_element_type=F32)
        acc = part if acc is None else acc + part
    return acc


def _ffn_body(x_ref, mod_ref, g_ref, wg_ref, wu_ref, wd_ref, o_ref):
    x = x_ref[...]
    h = _normmod(x, g_ref[...], mod_ref[4:5, :], mod_ref[3:4, :]).astype(BF16)
    o_ref[...] = x + mod_ref[5:6, :] * _swiglu_chunks(h, wg_ref, wu_ref, wd_ref)


def ffn_dense(x, mod, g, w_gu, w_down, li, tm):
    t = x.shape[0]
    once = pl.Buffered(1)
    return pl.pallas_call(
        _ffn_body,
        grid=(t // tm,),
        in_specs=[
            pl.BlockSpec((tm, D), lambda i: (i, 0)),
            _full((ADA_CHUNKS, D)),
            _full((1, D)),
            pl.BlockSpec((None, D, D_FF), lambda i: (li, 0, 0), pipeline_mode=once),
            pl.BlockSpec((None, D, D_FF), lambda i: (li, 0, 1), pipeline_mode=once),
            pl.BlockSpec((None, D_FF, D), lambda i: (li, 0, 0), pipeline_mode=once),
        ],
        out_specs=pl.BlockSpec((tm, D), lambda i: (i, 0)),
        out_shape=jax.ShapeDtypeStruct((t, D), F32),
        compiler_params=_cparams(("arbitrary",)),
        name="ffn_dense",
    )(x, mod, g, w_gu, w_gu, w_down)


def _proj_body(gate_row, a_ref, w_ref, x_ref, mod_ref, o_ref):
    y = jnp.dot(a_ref[...], w_ref[...], preferred_element_type=F32)
    o_ref[...] = x_ref[...] + mod_ref[gate_row:gate_row + 1, :] * y


def proj_residual(a, w, x, mod, gate_row, tm):
    t, k = a.shape
    tm = min(tm, t)
    return pl.pallas_call(
        functools.partial(_proj_body, gate_row),
        grid=(t // tm,),
        in_specs=[
            pl.BlockSpec((tm, k), lambda i: (i, 0)),
            _full((k, D)),
            pl.BlockSpec((tm, D), lambda i: (i, 0)),
            _full((ADA_CHUNKS, D)),
        ],
        out_specs=pl.BlockSpec((tm, D), lambda i: (i, 0)),
        out_shape=jax.ShapeDtypeStruct((t, D), F32),
        compiler_params=_cparams(("arbitrary",)),
        name="proj_residual",
    )(a, w, x, mod)


HALO = SUBLANES
RG_IN_PIECE = 128


def _rg_in_body(tm, xp_ref, x_ref, xn_ref, mod_ref, g_ref, w_ref, cw_ref, cb_ref, xc_ref, gg_ref):
    i = pl.program_id(0)
    last = pl.num_programs(0) - 1
    xa = jnp.concatenate([xp_ref[...], x_ref[...], xn_ref[...]], axis=0)
    npiece = tm // RG_IN_PIECE
    bounds = [0] + [2 * HALO + RG_IN_PIECE * (k + 1) for k in range(npiece - 1)] + [tm + 2 * HALO]
    zs = []
    for k in range(npiece):
        hk = _normmod(xa[bounds[k]:bounds[k + 1]], g_ref[...], mod_ref[1:2, :], mod_ref[0:1, :]).astype(BF16)
        zs.append(jnp.dot(hk, w_ref[...], preferred_element_type=F32))
    z = jnp.concatenate(zs, axis=0)
    row = lax.broadcasted_iota(jnp.int32, (tm + 2 * HALO, 1), 0)
    valid = jnp.logical_and(jnp.logical_or(row >= HALO, i > 0),
                            jnp.logical_or(row < tm + HALO, i < last))
    xz = jnp.where(valid, z[:, :D], 0.0)
    y = cb_ref[...] + cw_ref[2:3, :] * xz[HALO:HALO + tm]
    y = y + cw_ref[0:1, :] * xz[HALO - 2:HALO - 2 + tm]
    y = y + cw_ref[1:2, :] * xz[HALO - 1:HALO - 1 + tm]
    y = y + cw_ref[3:4, :] * xz[HALO + 1:HALO + 1 + tm]
    xc_ref[...] = y
    gg_ref[...] = _gelu_tanh(z[HALO:HALO + tm, D:]).astype(BF16)


def rg_in(x, mod, g, w_in, conv_w, conv_b, tm):
    t = x.shape[0]
    nb = tm // HALO
    nblk = t // HALO
    return pl.pallas_call(
        functools.partial(_rg_in_body, tm),
        grid=(t // tm,),
        in_specs=[
            pl.BlockSpec((HALO, D), lambda i: (jnp.maximum(i * nb - 1, 0), 0)),
            pl.BlockSpec((tm, D), lambda i: (i, 0)),
            pl.BlockSpec((HALO, D), lambda i: (jnp.minimum((i + 1) * nb, nblk - 1), 0)),
            _full((ADA_CHUNKS, D)),
            _full((1, D)),
            _full((D, 2 * D)),
            _full((4, D)),
            _full((1, D)),
        ],
        out_specs=[pl.BlockSpec((tm, D), lambda i: (i, 0)), pl.BlockSpec((tm, D), lambda i: (i, 0))],
        out_shape=[jax.ShapeDtypeStruct((t, D), F32), jax.ShapeDtypeStruct((t, D), BF16)],
        compiler_params=_cparams(("arbitrary",)),
        name="rg_in",
    )(x, x, x, mod, g, w_in, conv_w, conv_b)


def _rg_gates(xc, wa_ref, wi_ref, ba, bi, lam):
    xb = xc.astype(BF16)
    nblk = D // RG_BLOCK_W
    r = jnp.concatenate([jnp.dot(xb[:, n * RG_BLOCK_W:(n + 1) * RG_BLOCK_W], wa_ref[n],
                                 preferred_element_type=F32) for n in range(nblk)], axis=1)
    ig = jnp.concatenate([jnp.dot(xb[:, n * RG_BLOCK_W:(n + 1) * RG_BLOCK_W], wi_ref[n],
                                  preferred_element_type=F32) for n in range(nblk)], axis=1)
    t_r = jnp.tanh(r + 0.5 * ba)
    t_i = jnp.tanh(ig + 0.5 * bi)
    nl = -lam
    softplus = jnp.maximum(nl, 0.0) + jnp.log1p(jnp.exp(-jnp.abs(nl)))
    half_c = (-0.5 * LRU_C) * softplus
    log_a = half_c + half_c * t_r
    a = jnp.exp(log_a)
    xh = 0.5 * xc
    b = jnp.sqrt(1.0 - a * a) * (xh + xh * t_i)
    return a, b


def _rg_scan_body(reverse, epilogue, emit_h, nside, tc, *refs):
    xc_ref, wa_ref, wi_ref, ba_ref, bi_ref, lam_ref, h0_ref = refs[:7]
    refs = refs[7:]
    if epilogue:
        hf_ref, gg_ref, wo_ref, x_ref, mod_ref = refs[:5]
        refs = refs[5:]
    side_in, refs = refs[:nside], refs[nside:]
    if emit_h:
        h_ref = refs[0]
        refs = refs[1:]
    if epilogue:
        o_ref = refs[0]
        refs = refs[1:]
    side_out, refs = refs[:nside], refs[nside:]
    a_scr, b_scr, h_scr, carry_scr = refs
    c = pl.program_id(0)
    for src, dst in zip(side_in, side_out):
        dst[...] = src[...].astype(BF16)

    @pl.when(c == 0)
    def _():
        carry_scr[...] = jnp.broadcast_to(h0_ref[...], (SUBLANES, D))

    a, b = _rg_gates(xc_ref[...], wa_ref, wi_ref, ba_ref[...], bi_ref[...], lam_ref[...])
    a_scr[...] = a
    b_scr[...] = b
    nblk = tc // SUBLANES
    row = lax.broadcasted_iota(jnp.int32, (SUBLANES, D), 0)
    first = (row == SUBLANES - 1) if reverse else (row == 0)

    def block(n, carry):
        blk = (nblk - 1 - n) if reverse else n
        off = pl.multiple_of(blk * SUBLANES, SUBLANES)
        av = a_scr[pl.ds(off, SUBLANES), :]
        bv = b_scr[pl.ds(off, SUBLANES), :]
        bv = jnp.where(first, av * carry + bv, bv)
        av = jnp.where(first, 0.0, av)
        for k in (1, 2, 4):
            shift = (SUBLANES - k) if reverse else k
            bv = av * pltpu.roll(bv, shift, 0) + bv
            if k != 4:
                av = av * pltpu.roll(av, shift, 0)
        h_scr[pl.ds(off, SUBLANES), :] = bv
        edge = bv[0:1, :] if reverse else bv[SUBLANES - 1:SUBLANES, :]
        return jnp.broadcast_to(edge, (SUBLANES, D))

    carry_scr[...] = lax.fori_loop(0, nblk, block, carry_scr[...], unroll=2)

    if emit_h:
        h_ref[...] = h_scr[...].astype(h_ref.dtype)
    if epilogue:
        y = ((hf_ref[...].astype(F32) + h_scr[...]) * gg_ref[...].astype(F32)).astype(BF16)
        o_ref[...] = x_ref[...] + mod_ref[2:3, :] * jnp.dot(y, wo_ref[...], preferred_element_type=F32)


def rg_scan(xconv, wa, wi, ba, bi, lam, h0, tc, reverse, epi=None, h_dtype=F32, side=()):
    t = xconv.shape[0]
    nchunks = t // tc
    side_specs = []
    for a in side:
        assert a.shape[0] % (nchunks * BF16_ROWS) == 0
        side_specs.append(pl.BlockSpec((a.shape[0] // nchunks, a.shape[1]), lambda c: (c, 0)))
    idx = (lambda c: (nchunks - 1 - c, 0)) if reverse else (lambda c: (c, 0))
    nb = D // RG_BLOCK_W
    blk = pl.BlockSpec((tc, D), idx)
    in_specs = [
        blk,
        _full((nb, RG_BLOCK_W, RG_BLOCK_W)),
        _full((nb, RG_BLOCK_W, RG_BLOCK_W)),
        _full((1, D)), _full((1, D)), _full((1, D)), _full((1, D)),
    ]
    args = [xconv, wa, wi, ba, bi, lam, h0]
    out_specs = []
    out_shape = []
    if epi is not None:
        hf, gg, w_out, x, mod = epi
        in_specs += [blk, blk, _full((D, D)), blk, _full((ADA_CHUNKS, D))]
        args += [hf, gg, w_out, x, mod]
    in_specs += side_specs
    args += list(side)
    if h_dtype is not None:
        out_specs.append(blk)
        out_shape.append(jax.ShapeDtypeStruct((t, D), h_dtype))
    if epi is not None:
        out_specs.append(blk)
        out_shape.append(jax.ShapeDtypeStruct((t, D), F32))
    out_specs += side_specs
    out_shape += [jax.ShapeDtypeStruct(a.shape, BF16) for a in side]
    return pl.pallas_call(
        functools.partial(_rg_scan_body, reverse, epi is not None, h_dtype is not None, len(side), tc),
        grid=(nchunks,),
        in_specs=in_specs,
        out_specs=out_specs,
        out_shape=out_shape,
        scratch_shapes=[pltpu.VMEM((tc, D), F32), pltpu.VMEM((tc, D), F32), pltpu.VMEM((tc, D), F32),
                        pltpu.VMEM((SUBLANES, D), F32)],
        compiler_params=_cparams(("arbitrary",)),
        name="rg_scan_bwd" if reverse else "rg_scan_fwd",
    )(*args)


def rglru_layer(x, xc, mod, modc, g, w_in, conv_w, conv_b, wa, wi, ba, bi, lam, w_out, need_ctx, side=()):
    w_in_b = w_in.astype(BF16)
    wa_b = (0.5 * wa).astype(BF16)
    wi_b = (0.5 * wi).astype(BF16)
    w_out_b = w_out.astype(BF16)
    cb = conv_b[None]
    tcx = xc.shape[0]
    xcl, ggl = rg_in(x, mod, g, w_in_b, conv_w, cb, WIDE_TILE)
    xcc, ggc = rg_in(xc, modc, g, w_in_b, conv_w, cb, tcx)
    zeros = jnp.zeros((1, D), F32)
    p = lambda d: (wa_b[d], wi_b[d], ba[d][None], bi[d][None], lam[d][None])
    (hcf,) = rg_scan(xcc, *p(0), zeros, tcx, False)
    hlf, *side_b = rg_scan(xcl, *p(0), hcf[tcx - 1:tcx], WIDE_TILE, False, h_dtype=BF16, side=side)
    if need_ctx:
        hcb, xc_new = rg_scan(xcc, *p(1), zeros, tcx, True, epi=(hcf, ggc, w_out_b, xc, modc))
    else:
        (hcb,) = rg_scan(xcc, *p(1), zeros, tcx, True)
        xc_new = None
    (x_new,) = rg_scan(xcl, *p(1), hcb[0:1], WIDE_TILE, True, epi=(hlf, ggl, w_out_b, x, mod), h_dtype=None)
    return x_new, xc_new, side_b


def _qkv_body(x_ref, mod_ref, g_ref, w_ref, gm_ref, qg_ref, kg_ref, q_ref, k_ref, v_ref):
    h = _normmod(x_ref[...], g_ref[...], mod_ref[1:2, :], mod_ref[0:1, :]).astype(BF16)
    z = jnp.dot(h, w_ref[...], preferred_element_type=F32)

    def headnorm(v, gain):
        ms = jnp.dot((v * v).astype(BF16), gm_ref[...], preferred_element_type=F32)
        return (v * lax.rsqrt(ms + RMS_EPS)) * gain

    q_ref[...] = headnorm(z[:, :D], qg_ref[...]).astype(BF16)
    k_ref[...] = headnorm(z[:, D:2 * D], kg_ref[...]).astype(BF16)
    v_ref[...] = z[:, 2 * D:].astype(BF16)


def qkv_proj(x, mod, g, w_qkv, gmean, qg, kg, tm):
    t = x.shape[0]
    spec = pl.BlockSpec((tm, D), lambda i: (i, 0))
    return pl.pallas_call(
        _qkv_body,
        grid=(t // tm,),
        in_specs=[spec, _full((ADA_CHUNKS, D)), _full((1, D)), _full((D, 3 * D)), _full((D, D)),
                  _full((1, D)), _full((1, D))],
        out_specs=[spec, spec, spec],
        out_shape=[jax.ShapeDtypeStruct((t, D), BF16)] * 3,
        compiler_params=_cparams(("arbitrary",)),
        name="qkv_proj",
    )(x, mod, g, w_qkv, gmean, qg, kg)


def _attend_pair(q2, keys, vals, biases):
    m_rows = q2.shape[0]
    lane = lax.broadcasted_iota(jnp.int32, q2.shape, 1)
    zero = jnp.zeros_like(q2)
    qs = jnp.concatenate([jnp.where(lane < NA_HEAD_DIM, q2, zero), jnp.where(lane >= NA_HEAD_DIM, q2, zero)], axis=0)
    ss = []
    for kseg, bseg in zip(keys, biases):
        s = lax.dot_general(qs, kseg, (((1,), (1,)), ((), ())), preferred_element_type=F32)
        if bseg is not None:
            s = s + jnp.concatenate([bseg[0], bseg[1]], axis=0)
        ss.append(s)
    m = ss[0].max(axis=-1, keepdims=True)
    for s in ss[1:]:
        m = jnp.maximum(m, s.max(axis=-1, keepdims=True))
    acc = None
    for s, vseg in zip(ss, vals):
        p = jnp.exp2(s - m)
        vaug = jnp.concatenate([vseg, jnp.ones_like(vseg)], axis=1)
        o = jnp.dot(p.astype(BF16), vaug, preferred_element_type=F32)
        acc = o if acc is None else acc + o
    out = acc[:, :LANES] / acc[:, LANES:]
    return jnp.where(lane < NA_HEAD_DIM, out[:m_rows], out[m_rows:])


NA_QROWS = 2
NA_UNION = NA_ROWS + NA_QROWS - 1


def _na_body(nside, var_ref, q_ref, kl_ref, vl_ref, kc_ref, vc_ref, bias_ref, wo_ref, x_ref, mod_ref, *refs):
    side_in, xo_ref, side_out, o_scr = refs[:nside], refs[nside], refs[nside + 1:-1], refs[-1]
    npair = NA_HEADS // 2
    for pr in range(npair):
        sl = slice(pr * LANES, (pr + 1) * LANES)
        o_scr[:, sl] = _attend_pair(
            q_ref[:, sl], [kl_ref[:, sl], kc_ref[:, sl]], [vl_ref[:, sl], vc_ref[:, sl]],
            [(bias_ref[0, 2 * pr], bias_ref[0, 2 * pr + 1]), None]).astype(BF16)
        for src, dst in zip(side_in, side_out):
            slab = -(-src.shape[0] // (npair * BF16_ROWS)) * BF16_ROWS
            lo = min(pr * slab, src.shape[0])
            hi = min(lo + slab, src.shape[0])
            if hi > lo:
                dst[lo:hi, :] = src[lo:hi, :].astype(BF16)
    y = jnp.dot(o_scr[...], wo_ref[...], preferred_element_type=F32)
    xo_ref[...] = x_ref[...] + mod_ref[2:3, :] * y


def _na_geometry(rows):
    steps = rows // NA_QROWS
    g = np.arange(steps)
    base = np.clip(NA_QROWS * g - NA_ROWS // 2, 0, rows - NA_UNION)
    r = NA_QROWS * g[:, None] + np.arange(NA_QROWS)[None, :]
    rs = np.clip(r - NA_ROWS // 2, 0, rows - NA_ROWS)
    key = np.concatenate([(base - NA_QROWS * g)[:, None], rs - r], axis=1)
    uniq, first, var = np.unique(key, axis=0, return_index=True, return_inverse=True)
    return base, var.reshape(-1).astype(np.int32), g[first]


def na_attention(q, k, v, kc, vc, bias_tab, var, w_o, x, mod, side=()):
    t = q.shape[0]
    rows = t // GRID_W
    nctx = kc.shape[0]
    steps = rows // NA_QROWS
    side_specs = []
    for a in side:
        assert a.shape[0] % (steps * 16) == 0
        side_specs.append(pl.BlockSpec((a.shape[0] // steps, a.shape[1]), lambda g, var: (g, 0)))

    def kbase(g):
        return jnp.clip(NA_QROWS * g - NA_ROWS // 2, 0, rows - NA_UNION)

    qrows = NA_QROWS * GRID_W
    nloc = NA_UNION * GRID_W
    kspec = pl.BlockSpec((pl.Element(nloc), pl.Element(D)), lambda g, var: (kbase(g) * GRID_W, 0))
    grid_spec = pltpu.PrefetchScalarGridSpec(
        num_scalar_prefetch=1,
        grid=(steps,),
        in_specs=[pl.BlockSpec((qrows, D), lambda g, var: (g, 0)), kspec, kspec] + [
            pl.BlockSpec((nctx, D), lambda g, var: (0, 0)), pl.BlockSpec((nctx, D), lambda g, var: (0, 0)),
            pl.BlockSpec((1, NA_HEADS, qrows, nloc), lambda g, var: (var[g], 0, 0, 0)),
            pl.BlockSpec((D, D), lambda g, var: (0, 0)),
            pl.BlockSpec((qrows, D), lambda g, var: (g, 0)),
            pl.BlockSpec((ADA_CHUNKS, D), lambda g, var: (0, 0)),
        ] + side_specs,
        out_specs=[pl.BlockSpec((qrows, D), lambda g, var: (g, 0))] + side_specs,
        scratch_shapes=[pltpu.VMEM((qrows, D), BF16)],
    )
    return pl.pallas_call(
        functools.partial(_na_body, len(side)),
        grid_spec=grid_spec,
        out_shape=[jax.ShapeDtypeStruct((t, D), F32)] + [jax.ShapeDtypeStruct(a.shape, BF16) for a in side],
        compiler_params=_cparams(("arbitrary",)),
        name="na_attention",
    )(var, q, k, v, kc, vc, bias_tab, w_o, x, mod, *side)


def _ctx_attn_body(q_ref, k_ref, v_ref, o_ref):
    for pr in range(NA_HEADS // 2):
        sl = slice(pr * LANES, (pr + 1) * LANES)
        o_ref[:, sl] = _attend_pair(q_ref[:, sl], [k_ref[:, sl]], [v_ref[:, sl]], [None]).astype(BF16)


def ctx_attention(q, k, v):
    t = q.shape[0]
    return pl.pallas_call(
        _ctx_attn_body,
        grid=(1,),
        in_specs=[_full((t, D))] * 3,
        out_specs=_full((t, D)),
        out_shape=jax.ShapeDtypeStruct((t, D), BF16),
        compiler_params=_cparams(("arbitrary",)),
        name="ctx_attention",
    )(q, k, v)


def _na_bias_table(rpb, rows):
    base, var, reps = _na_geometry(rows)
    cols = np.arange(GRID_W)
    cstart = np.clip(cols - NA_COLS // 2, 0, GRID_W - NA_COLS)
    kcol = np.arange(GRID_W)
    inwin = (kcol[None, :] >= cstart[:, None]) & (kcol[None, :] < cstart[:, None] + NA_COLS)
    r = NA_QROWS * reps[:, None] + np.arange(NA_QROWS)[None, :]
    rs = np.clip(r - NA_ROWS // 2, 0, rows - NA_ROWS)
    krow = base[reps][:, None] + np.arange(NA_UNION)[None, :]
    rvalid = (krow[:, None, :] >= rs[:, :, None]) & (krow[:, None, :] < rs[:, :, None] + NA_ROWS)
    ridx = np.clip(krow[:, None, :] - r[:, :, None] + (NA_ROWS - 1), 0, 2 * NA_ROWS - 2)
    nd = 2 * NA_COLS - 1
    w = jnp.pad(rpb.astype(F32), ((0, 0), (0, 0), (GRID_W - NA_COLS, 2 * GRID_W - (GRID_W - NA_COLS) - nd)))
    flat = jnp.tile(w, (1, 1, GRID_W))[:, :, :GRID_W * (2 * GRID_W - 1)]
    blk = flat.reshape(NA_HEADS, 2 * NA_ROWS - 1, GRID_W, 2 * GRID_W - 1)[..., GRID_W - 1:]
    blk = jnp.where(jnp.asarray(inwin)[None, None], blk, NEG_BIG)
    neg = jnp.full((NA_HEADS, GRID_W, GRID_W), NEG_BIG, F32)
    variants = []
    for v in range(len(reps)):
        strips = [jnp.concatenate([blk[:, ridx[v, a, j]] if rvalid[v, a, j] else neg for j in range(NA_UNION)], axis=2)
                  for a in range(NA_QROWS)]
        variants.append(jnp.concatenate(strips, axis=1))
    return jnp.stack(variants, axis=0), jnp.asarray(var)


def na_layer(x, xc, mod, modc, g, w_qkv, q_g, k_g, rpb, w_o, need_ctx, side=()):
    w_qkv_b = w_qkv.astype(BF16)
    w_o_b = w_o.astype(BF16)
    gm = np.kron(np.eye(NA_HEADS), np.full((NA_HEAD_DIM, NA_HEAD_DIM), 1.0 / NA_HEAD_DIM))
    gmean = jnp.asarray(gm, dtype=BF16)
    qg = jnp.tile(q_g, NA_HEADS)[None] * (NA_HEAD_DIM ** -0.5 * LOG2E)
    kg = jnp.tile(k_g, NA_HEADS)[None]
    q, k, v = qkv_proj(x, mod, g, w_qkv_b, gmean, qg, kg, WIDE_TILE)
    qc, kc, vc = qkv_proj(xc, modc, g, w_qkv_b, gmean, qg, kg, xc.shape[0])
    bias_tab, var = _na_bias_table(rpb * LOG2E, x.shape[0] // GRID_W)
    x_new, *side_b = na_attention(q, k, v, kc, vc, bias_tab, var, w_o_b, x, mod, side)
    xc_new = None
    if need_ctx:
        oc = ctx_attention(qc, kc, vc)
        xc_new = proj_residual(oc, w_o_b, xc, modc, 2, xc.shape[0])
    return x_new, xc_new, side_b


def _dft_mats(n):
    ang = 2.0 * np.pi * np.outer(np.arange(n), np.arange(n)) / n
    return np.cos(ang), np.sin(ang)


def _channel_dft(h, wc):
    us = [jnp.dot(h[:, gi * FT_GROUP_W:(gi + 1) * FT_GROUP_W], wc, preferred_element_type=F32).astype(BF16)
          for gi in range(D // FT_GROUP_W)]
    return jnp.concatenate([u[:, :FT_GROUP_W] for u in us] + [u[:, FT_GROUP_W:] for u in us], axis=1)


def _ft_a_body(n, nj, x_ref, mod_ref, g_ref, perm_ref, wc_ref, ma_ref, yr_ref, yi_ref):
    h3 = _normmod(x_ref[...], g_ref[...], mod_ref[1:2, :], mod_ref[0:1, :])
    h = jnp.dot(perm_ref[...], h3.reshape(n * nj, D).astype(BF16), preferred_element_type=F32).astype(BF16)
    u = _channel_dft(h, wc_ref[...])
    for j in range(nj):
        uj = u[j * n:(j + 1) * n]
        y = jnp.dot(ma_ref[...], jnp.concatenate([uj[:, :D], uj[:, D:]], axis=0), preferred_element_type=F32)
        yr_ref[:, j, :] = y[:n]
        yi_ref[:, j, :] = y[n:]


def _ft_c_body(n, nj, yr_ref, yi_ref, mc_ref, wf_ref, x_ref, mod_ref, o_ref):
    fs = []
    for j in range(nj):
        ys = jnp.concatenate([yr_ref[j].astype(BF16), yi_ref[j].astype(BF16)], axis=0)
        fs.append(jnp.dot(mc_ref[j], ys, preferred_element_type=F32).astype(BF16))
    z = jnp.dot(jnp.concatenate(fs, axis=0), wf_ref[...], preferred_element_type=F32)
    gate = mod_ref[2:3, :]
    for j in range(nj):
        o_ref[:, j, :] = x_ref[:, j, :] + gate * z[j * n:(j + 1) * n]


def _ft_ctx_body(x_ref, mod_ref, g_ref, wc_ref, ml_ref, wf_ref, o_ref):
    x = x_ref[...]
    h = _normmod(x, g_ref[...], mod_ref[1:2, :], mod_ref[0:1, :]).astype(BF16)
    u = _channel_dft(h, wc_ref[...])
    us = jnp.concatenate([u[:, :D], u[:, D:]], axis=0)
    f = jnp.dot(ml_ref[...], us, preferred_element_type=F32).astype(BF16)
    o_ref[...] = x + mod_ref[2:3, :] * jnp.dot(f, wf_ref[...], preferred_element_type=F32)


def fourier_layer(x, xc, mod, modc, g, w_f, need_ctx):
    t = x.shape[0]
    n = math.isqrt(t)
    assert n * n == t and n % 16 == 0
    w_f_b = w_f.astype(BF16)
    cw, sw = _dft_mats(FT_GROUP_W)
    wc = jnp.asarray(np.concatenate([cw, -sw], axis=1) / math.sqrt(FT_GROUP_W), dtype=F32).astype(BF16)
    wcspec = _full((FT_GROUP_W, 2 * FT_GROUP_W))
    cn, sn = _dft_mats(n)
    ma = jnp.asarray(np.block([[cn, sn], [-sn, cn]]) / math.sqrt(n), dtype=F32).astype(BF16)
    nj = 8
    ang = 2.0 * np.pi * np.outer(np.arange(n), np.arange(n)) / t
    cnj, snj = jnp.asarray(cn, dtype=F32)[None], jnp.asarray(sn, dtype=F32)[None]
    tcj = jnp.asarray(np.cos(ang).T, dtype=F32)[:, None, :]
    tsj = jnp.asarray(np.sin(ang).T, dtype=F32)[:, None, :]
    mc = (jnp.concatenate([cnj * tcj - snj * tsj, cnj * tsj + snj * tcj], axis=2) / math.sqrt(n)).astype(BF16)
    xblk = pl.BlockSpec((n, nj, D), lambda b: (0, b, 0))
    yblk = pl.BlockSpec((nj, n, D), lambda b: (b, 0, 0))
    mblk = pl.BlockSpec((nj, n, 2 * n), lambda b: (b, 0, 0))
    x3 = x.reshape(n, n, D)
    src = (np.arange(n)[None, :] * nj + np.arange(nj)[:, None]).reshape(-1)
    perm = jnp.asarray(np.eye(n * nj)[src], dtype=BF16)
    yr, yi = pl.pallas_call(
        functools.partial(_ft_a_body, n, nj),
        grid=(n // nj,),
        in_specs=[xblk, _full((ADA_CHUNKS, D)), _full((1, D)), _full((n * nj, n * nj)), wcspec,
                  _full((2 * n, 2 * n))],
        out_specs=[xblk, xblk],
        out_shape=[jax.ShapeDtypeStruct((n, n, D), F32)] * 2,
        compiler_params=_cparams(("arbitrary",)),
        name="ft_stage_a",
    )(x3, mod, g, perm, wc, ma)
    x_new = pl.pallas_call(
        functools.partial(_ft_c_body, n, nj),
        grid=(n // nj,),
        in_specs=[yblk, yblk, mblk, _full((D, D)), xblk, _full((ADA_CHUNKS, D))],
        out_specs=xblk,
        out_shape=jax.ShapeDtypeStruct((n, n, D), F32),
        compiler_params=_cparams(("arbitrary",)),
        name="ft_stage_c",
    )(yr, yi, mc, w_f_b, x3, mod).reshape(t, D)
    xc_new = None
    if need_ctx:
        lc = xc.shape[0]
        cl, sl = _dft_mats(lc)
        ml = jnp.asarray(np.concatenate([cl, sl], axis=1) / math.sqrt(lc), dtype=F32).astype(BF16)
        xc_new = pl.pallas_call(
            _ft_ctx_body,
            grid=(1,),
            in_specs=[_full((lc, D)), _full((ADA_CHUNKS, D)), _full((1, D)), wcspec,
                      _full((lc, 2 * lc)), _full((D, D))],
            out_specs=_full((lc, D)),
            out_shape=jax.ShapeDtypeStruct((lc, D), F32),
            compiler_params=_cparams(("arbitrary",)),
            name="ft_ctx",
        )(xc, modc, g, wc, ml, w_f_b)
    return x_new, xc_new


def _router_body(x_ref, mod_ref, g_ref, r_ref, info_ref, w0_ref, w1_ref):
    h = _normmod(x_ref[...], g_ref[...], mod_ref[4:5, :], mod_ref[3:4, :])
    hh = h.astype(BF16)
    hl = (h - hh.astype(F32)).astype(BF16)
    r = r_ref[...]
    rh = r.astype(BF16)
    rl = (r - rh.astype(F32)).astype(BF16)
    logits = (jnp.dot(hh, rh, preferred_element_type=F32) + jnp.dot(hh, rl, preferred_element_type=F32)
              + jnp.dot(hl, rh, preferred_element_type=F32))
    lane = lax.broadcasted_iota(jnp.int32, logits.shape, 1)
    logits = jnp.where(lane < N_EXPERTS, logits, NEG_BIG)
    v0 = jnp.max(logits, axis=-1, keepdims=True)
    i0 = jnp.min(jnp.where(logits == v0, lane, LANES), axis=-1, keepdims=True)
    rest = jnp.where(lane == i0, NEG_BIG, logits)
    v1 = jnp.max(rest, axis=-1, keepdims=True)
    i1 = jnp.min(jnp.where(rest == v1, lane, LANES), axis=-1, keepdims=True)
    e = jnp.exp(v1 - v0)
    w0 = 1.0 / (1.0 + e)
    w1 = e / (1.0 + e)
    info_ref[...] = jnp.where(lane == 0, i0, jnp.where(lane == 1, i1, 0))
    w0_ref[...] = jnp.broadcast_to(w0, logits.shape)
    w1_ref[...] = jnp.broadcast_to(w1, logits.shape)


def moe_router(x, mod, g, router_pad, tm):
    t = x.shape[0]
    tm = min(tm, t)
    spec = pl.BlockSpec((tm, D), lambda i: (i, 0))
    lspec = pl.BlockSpec((tm, LANES), lambda i: (i, 0))
    return pl.pallas_call(
        _router_body,
        grid=(t // tm,),
        in_specs=[spec, _full((ADA_CHUNKS, D)), _full((1, D)), _full((D, LANES))],
        out_specs=[lspec, lspec, lspec],
        out_shape=[jax.ShapeDtypeStruct((t, LANES), jnp.int32),
                   jax.ShapeDtypeStruct((t, LANES), F32), jax.ShapeDtypeStruct((t, LANES), F32)],
        compiler_params=_cparams(("arbitrary",)),
        name="moe_router",
    )(x, mod, g, router_pad)


SCATTER_TOKENS = 256


def _row_scatter_body(nlat, nctx, didx_ref, g_ref, x_ref, mod_ref, *rest):
    if nctx:
        xc_ref, modc_ref, dst_ref, h_scr, zero_scr, sems = rest
    else:
        dst_ref, h_scr, zero_scr, sems = rest
    i = pl.program_id(0)
    nsteps = pl.num_programs(0)
    ts = SCATTER_TOKENS
    slot = i % 2

    def start_all(src_ref, src_is_zero_rows):
        def issue(grp, c):
            base = pl.multiple_of(grp * SUBLANES, SUBLANES)
            for r in range(SUBLANES):
                src = src_ref.at[pl.ds(r if src_is_zero_rows else base + r, 1), :]
                for half in range(2):
                    d = didx_ref[0, 0, base + r + half * ts]
                    pltpu.make_async_copy(src, dst_ref.at[pl.ds(d, 1), :], sems.at[slot]).start(priority=half)
            return c
        lax.fori_loop(0, ts // SUBLANES, issue, 0)

    def wait_all(which):
        def drain(n, c):
            pltpu.make_async_copy(zero_scr.at[pl.ds(0, 1), :], dst_ref.at[pl.ds(0, 1), :], sems.at[which]).wait()
            return c
        lax.fori_loop(0, 2 * ts, drain, 0, unroll=8)

    @pl.when(i == 0)
    def _():
        zero_scr[...] = jnp.zeros_like(zero_scr)

    def stage_and_start(src_ref, m_ref):
        h_scr[slot] = _normmod(src_ref[...], g_ref[...], m_ref[4:5, :], m_ref[3:4, :])
        start_all(h_scr.at[slot], False)

    @pl.when(i < nlat)
    def _():
        stage_and_start(x_ref, mod_ref)

    if nctx:
        @pl.when(jnp.logical_and(i >= nlat, i < nlat + nctx))
        def _():
            stage_and_start(xc_ref, modc_ref)

    @pl.when(i >= nlat + nctx)
    def _():
        start_all(zero_scr, True)

    @pl.when(i > 0)
    def _():
        wait_all(1 - slot)

    @pl.when(i == nsteps - 1)
    def _():
        wait_all(slot)


def row_scatter(x, xc, mod, modc, g, d0, d1, pad_pos):
    ts = SCATTER_TOKENS
    nlat = x.shape[0] // ts
    nctx = 0 if xc is None else 1
    assert xc is None or xc.shape[0] == ts
    ntok = nlat + nctx
    npad = pad_pos.shape[0] // (2 * ts)
    didx = jnp.concatenate([jnp.concatenate([d0.reshape(ntok, 1, ts), d1.reshape(ntok, 1, ts)], axis=2),
                            pad_pos.reshape(npad, 1, 2 * ts)], axis=0)
    in_specs = [pl.BlockSpec((1, 1, 2 * ts), lambda i: (i, 0, 0), memory_space=pltpu.SMEM),
                _full((1, D)),
                pl.BlockSpec((ts, D), lambda i: (jnp.minimum(i, nlat - 1), 0)),
                _full((ADA_CHUNKS, D))]
    args = [didx, g, x, mod]
    if nctx:
        in_specs += [_full((ts, D)), _full((ADA_CHUNKS, D))]
        args += [xc, modc]
    return pl.pallas_call(
        functools.partial(_row_scatter_body, nlat, nctx),
        grid=(ntok + npad,),
        in_specs=in_specs,
        out_specs=pl.BlockSpec(memory_space=pl.ANY),
        out_shape=jax.ShapeDtypeStruct((2 * ntok * ts + pad_pos.shape[0], D), F32),
        scratch_shapes=[pltpu.VMEM((2, ts, D), F32), pltpu.VMEM((SUBLANES, D), F32), pltpu.SemaphoreType.DMA((2,))],
        compiler_params=_cparams(("arbitrary",)),
        name="moe_row_scatter",
    )(*args)


def _moe_ffn_body(te_ref, tv_ref, xg_ref, wg_ref, wu_ref, wd_ref, o_ref):
    i = pl.program_id(0)

    @pl.when(tv_ref[i] > 0)
    def _():
        o_ref[...] = _swiglu_chunks(xg_ref[...].astype(BF16), wg_ref, wu_ref, wd_ref)

    @pl.when(tv_ref[i] == 0)
    def _():
        o_ref[...] = jnp.zeros_like(o_ref)


def moe_ffn(xg, tile_e, tile_v, w_gu, w_down, li, tm):
    p = xg.shape[0]
    grid_spec = pltpu.PrefetchScalarGridSpec(
        num_scalar_prefetch=2,
        grid=(p // tm,),
        in_specs=[
            pl.BlockSpec((tm, D), lambda i, te, tv: (i, 0)),
            pl.BlockSpec((None, None, D, D_FF), lambda i, te, tv: (li, te[i], 0, 0)),
            pl.BlockSpec((None, None, D, D_FF), lambda i, te, tv: (li, te[i], 0, 1)),
            pl.BlockSpec((None, None, D_FF, D), lambda i, te, tv: (li, te[i], 0, 0)),
        ],
        out_specs=pl.BlockSpec((tm, D), lambda i, te, tv: (i, 0)),
    )
    return pl.pallas_call(
        _moe_ffn_body,
        grid_spec=grid_spec,
        out_shape=jax.ShapeDtypeStruct((p, D), F32),
        compiler_params=pltpu.CompilerParams(dimension_semantics=("arbitrary",), vmem_limit_bytes=MOE_VMEM_LIMIT),
        name="moe_ffn",
    )(tile_e, tile_v, xg, w_gu, w_gu, w_down)


def _combine_body(tt, d0_ref, d1_ref, d0n_ref, d1n_ref, yp_ref, x_ref, mod_ref, w0_ref, w1_ref, o_ref,
                  a_scr, b_scr, sems):
    i = pl.program_id(0)
    nsteps = pl.num_programs(0)
    slot = i % 2

    def start_all(i0_ref, i1_ref, which):
        def issue(grp, c):
            base = pl.multiple_of(grp * SUBLANES, SUBLANES)
            for r in range(SUBLANES):
                n = base + r
                pltpu.make_async_copy(yp_ref.at[pl.ds(i0_ref[0, 0, n], 1), :], a_scr.at[which, pl.ds(n, 1), :],
                                      sems.at[which]).start(priority=0)
                pltpu.make_async_copy(yp_ref.at[pl.ds(i1_ref[0, 0, n], 1), :], b_scr.at[which, pl.ds(n, 1), :],
                                      sems.at[which]).start(priority=1)
            return c
        lax.fori_loop(0, tt // SUBLANES, issue, 0)

    @pl.when(i == 0)
    def _():
        start_all(d0_ref, d1_ref, 0)

    @pl.when(i + 1 < nsteps)
    def _():
        start_all(d0n_ref, d1n_ref, 1 - slot)

    def drain(n, c):
        pltpu.make_async_copy(yp_ref.at[pl.ds(0, 1), :], a_scr.at[slot, pl.ds(0, 1), :], sems.at[slot]).wait()
        pltpu.make_async_copy(yp_ref.at[pl.ds(0, 1), :], b_scr.at[slot, pl.ds(0, 1), :], sems.at[slot]).wait()
        return c
    lax.fori_loop(0, tt, drain, 0, unroll=8)
    w0 = _lane_tile(w0_ref[...])
    w1 = _lane_tile(w1_ref[...])
    o_ref[...] = x_ref[...] + mod_ref[5:6, :] * (w0 * a_scr[slot] + w1 * b_scr[slot])


def moe_combine(yp, d0, d1, x, mod, w0b, w1b, tt):
    t = x.shape[0]
    nt = t // tt
    ispec = pl.BlockSpec((1, 1, tt), lambda i: (i, 0, 0), memory_space=pltpu.SMEM)
    nspec = pl.BlockSpec((1, 1, tt), lambda i: (jnp.minimum(i + 1, nt - 1), 0, 0), memory_space=pltpu.SMEM)
    spec = pl.BlockSpec((tt, D), lambda i: (i, 0))
    lspec = pl.BlockSpec((tt, LANES), lambda i: (i, 0))
    d0r, d1r = d0.reshape(nt, 1, tt), d1.reshape(nt, 1, tt)
    return pl.pallas_call(
        functools.partial(_combine_body, tt),
        grid=(nt,),
        in_specs=[ispec, ispec, nspec, nspec, pl.BlockSpec(memory_space=pl.ANY), spec, _full((ADA_CHUNKS, D)),
                  lspec, lspec],
        out_specs=spec,
        out_shape=jax.ShapeDtypeStruct((t, D), F32),
        scratch_shapes=[pltpu.VMEM((2, tt, D), F32), pltpu.VMEM((2, tt, D), F32), pltpu.SemaphoreType.DMA((2,))],
        compiler_params=_cparams(("arbitrary",)),
        name="moe_combine",
    )(d0r, d1r, d0r, d1r, yp, x, mod, w0b, w1b)


def _route_plan(e0, e1, tm):
    t = e0.shape[0]
    n = 2 * t
    ex = jnp.arange(N_EXPERTS, dtype=jnp.int32)
    oh0 = (e0[:, None] == ex[None, :]).astype(jnp.int32)
    oh1 = (e1[:, None] == ex[None, :]).astype(jnp.int32)
    both = oh0 + oh1
    csum = jnp.cumsum(both, axis=0)
    before = csum - both
    counts = csum[-1]
    padded = ((counts + tm - 1) // tm) * tm
    pad_end = jnp.cumsum(padded)
    pad_off = pad_end - padded
    total = pad_end[-1]
    d0 = jnp.sum(oh0 * (before + pad_off[None, :]), axis=1)
    d1 = jnp.sum(oh1 * (before + oh0 + pad_off[None, :]), axis=1)
    gap = padded - counts
    tail_off = jnp.cumsum(tm - gap) - (tm - gap)
    r = jnp.arange(tm, dtype=jnp.int32)[None, :]
    pad_pos = jnp.where(r < gap[:, None], (pad_off + counts)[:, None] + r,
                        total + tail_off[:, None] + (r - gap[:, None])).reshape(-1)
    ntiles = (n + N_EXPERTS * tm) // tm
    tstart = jnp.arange(ntiles, dtype=jnp.int32) * tm
    tile_v = (tstart < total).astype(jnp.int32)
    tile_e = jnp.sum((jnp.minimum(tstart, total - 1)[:, None] >= pad_end[None, :]).astype(jnp.int32), axis=1)
    return (d0.astype(jnp.int32), d1.astype(jnp.int32), pad_pos.astype(jnp.int32), tile_e.astype(jnp.int32), tile_v)


def moe_layer(x, xc, mod, modc, g, router, w_gu_b, w_down_b, li, need_ctx, tm=ROW_TILE):
    router_pad = jnp.pad(router, ((0, 0), (0, LANES - N_EXPERTS)))
    s = x.shape[0]
    info, w0b, w1b = moe_router(x, mod, g, router_pad, LIGHT_TILE)
    if need_ctx:
        sc = xc.shape[0]
        infoc, w0c, w1c = moe_router(xc, modc, g, router_pad, sc)
        e0 = jnp.concatenate([info[:, 0], infoc[:, 0]])
        e1 = jnp.concatenate([info[:, 1], infoc[:, 1]])
    else:
        e0, e1 = info[:, 0], info[:, 1]
    d0, d1, pad_pos, tile_e, tile_v = _route_plan(e0, e1, tm)
    xg = row_scatter(x, xc if need_ctx else None, mod, modc, g, d0, d1, pad_pos)
    yp = moe_ffn(xg, tile_e, tile_v, w_gu_b, w_down_b, li, tm)
    x_new = moe_combine(yp, d0[:s], d1[:s], x, mod, w0b, w1b, ROW_TILE)
    xc_new = None
    if need_ctx:
        xc_new = moe_combine(yp, d0[s:], d1[s:], xc, modc, w0c, w1c, sc)
    return x_new, xc_new


def kernel(x, c, ctx, c_ctx, ada_w, ada_b, norm_g, rg_w_in, rg_conv_w, rg_conv_b, rg_wa, rg_ba, rg_wi, rg_bi,
           rg_lambda, rg_w_out, na_w_qkv, na_q_g, na_k_g, na_rpb, na_w_o, ft_w_out, ffn_w_gu, ffn_w_down,
           moe_router, moe_w_gu, moe_w_down):
    depth = ada_w.shape[0]
    assert x.shape[0] == 1 and x.shape[2] == D
    xs = x[0]
    xc = ctx[0]
    mods = ada_modulation(c, c_ctx, ada_w, ada_b)
    ffn_gu_b = ffn_dn_b = moe_gu_b = moe_dn_b = None
    mix_idx = [0] * N_MIXERS
    dense_idx = 0
    moe_idx = 0
    for layer in range(depth):
        need_ctx = layer != depth - 1
        mod, modc = mods[layer, 0], mods[layer, 1]
        g0 = norm_g[layer, 0][None]
        g1 = norm_g[layer, 1][None]
        kind = layer % N_MIXERS
        j = mix_idx[kind]
        mix_idx[kind] += 1
        if kind == 0:
            side = ()
            if ffn_gu_b is None:
                later = (ffn_w_gu, ffn_w_down, na_w_qkv, na_w_o, ft_w_out)
                side = tuple(a.reshape(-1, a.shape[-1]) for a in later)
            xs, xcn, side_b = rglru_layer(xs, xc, mod, modc, g0, rg_w_in[j], rg_conv_w[j], rg_conv_b[j], rg_wa[j],
                                          rg_wi[j], rg_ba[j], rg_bi[j], rg_lambda[j], rg_w_out[j], need_ctx, side)
            if side_b:
                (ffn_gu_b, ffn_dn_b, na_w_qkv, na_w_o, ft_w_out) = (
                    b.reshape(a.shape) for a, b in zip(later, side_b))
        elif kind == 1:
            side = ()
            if moe_gu_b is None:
                side = (moe_w_gu.reshape(-1, moe_w_gu.shape[-1]), moe_w_down.reshape(-1, moe_w_down.shape[-1]))
            xs, xcn, side_b = na_layer(xs, xc, mod, modc, g0, na_w_qkv[j], na_q_g[j], na_k_g[j], na_rpb[j],
                                       na_w_o[j], need_ctx, side)
            if side_b:
                moe_gu_b, moe_dn_b = side_b[0].reshape(moe_w_gu.shape), side_b[1].reshape(moe_w_down.shape)
        else:
            xs, xcn = fourier_layer(xs, xc, mod, modc, g0, ft_w_out[j], need_ctx)
        if need_ctx:
            xc = xcn
        if layer % 2 == 0:
            if ffn_gu_b is None:
                ffn_gu_b, ffn_dn_b = ffn_w_gu.astype(BF16), ffn_w_down.astype(BF16)
            if need_ctx:
                xc = ffn_dense(xc, modc, g1, ffn_gu_b, ffn_dn_b, dense_idx, xc.shape[0])
            xs = ffn_dense(xs, mod, g1, ffn_gu_b, ffn_dn_b, dense_idx, ROW_TILE)
            dense_idx += 1
        else:
            if moe_gu_b is None:
                moe_gu_b, moe_dn_b = moe_w_gu.astype(BF16), moe_w_down.astype(BF16)
            xs, xcn = moe_layer(xs, xc, mod, modc, g1, moe_router[moe_idx], moe_gu_b, moe_dn_b, moe_idx, need_ctx)
            moe_idx += 1
            if need_ctx:
                xc = xcn
    return xs[None]
```
